```python
import jax, jax.numpy as jnp
from jax import lax
import numpy as np

D_MODEL = 2048
BATCH = 4
SEQ = 2048
DEPTH = 2
DEC_BATCH = 128
DEC_SEQ = 4
PAST_LEN = 16384
PAGE_SIZE = 128

G_RWKV = D_MODEL // 2
G_CONV = D_MODEL - G_RWKV
HEAD_SIZE = 64
N_HEADS = G_RWKV // HEAD_SIZE
CONV_GROUP = 64
N_CONV_GROUPS = G_CONV // CONV_GROUP
CONV_W = 3
D_FF = ((8 * D_MODEL // 3 + 255) // 256) * 256
LORA_DECAY = max(32, int(round(1.8 * D_MODEL ** 0.5 / 32)) * 32)
LORA_ICLR = max(32, int(round(1.8 * D_MODEL ** 0.5 / 32)) * 32)
LORA_MV = max(32, int(round(1.3 * D_MODEL ** 0.5 / 32)) * 32)
LORA_GATE = max(32, int(round(0.6 * D_MODEL ** 0.8 / 32)) * 32)
RMS_EPS = 1e-6
GN_EPS = 64e-5

kernel_name = 'hymba_rwkv7_shortconv_convffn_adaln_step'


def rmsnorm(x, g):
    xf = x.astype(jnp.float32)
    y = xf * lax.rsqrt(jnp.mean(xf * xf, axis=-1, keepdims=True) + RMS_EPS)
    return (y * g.astype(jnp.float32)).astype(x.dtype)


def causal_dwconv(buf, x, w):
    T = x.shape[1]
    full = jnp.concatenate([buf.astype(x.dtype), x], axis=1)
    y = full[:, 0:T] * w[0]
    for j in range(1, CONV_W):
        y = y + full[:, j:j + T] * w[j]
    return y, full[:, full.shape[1] - (CONV_W - 1):]


def wkv7_scan(S0, r, w, k, v, kk, b):
    def step(S, inp):
        r_t, w_t, k_t, v_t, kk_t, b_t = inp
        sa = jnp.einsum('bhij,bhj->bhi', S, -kk_t)
        S = S * w_t[:, :, None, :] + sa[..., None] * b_t[:, :, None, :] + v_t[..., None] * k_t[:, :, None, :]
        y = jnp.einsum('bhij,bhj->bhi', S, r_t)
        return S, y
    xs = tuple(jnp.moveaxis(t, 1, 0) for t in (r, w, k, v, kk, b))
    S, ys = lax.scan(step, S0, xs)
    return jnp.moveaxis(ys, 0, 1), S


def run_trunk(x, c, st_wkv, st_shift, st_conv, st_ffn, p):
    B, T, _ = x.shape
    f32 = jnp.float32
    new_wkv, new_shift, new_conv, new_ffn = [], [], [], []
    v_first = None
    for l in range(DEPTH):
        mod = jax.nn.silu(c) @ p['ada_w'][l] + p['ada_b'][l]
        sh1, sc1, ga1, sh2, sc2, ga2 = jnp.split(mod[:, None, :], 6, axis=-1)

        h = rmsnorm(x, p['norm_g'][l, 0]) * (1 + sc1) + sh1
        h_last = st_shift[l].astype(h.dtype)
        xx = jnp.concatenate([h_last[:, None], h[:, :-1]], axis=1) - h
        w_in = p['w_in'][l]
        proj = h @ w_in
        p_rkv, p_conv = proj[..., :3 * G_RWKV], proj[..., 3 * G_RWKV:]
        p_rkv_prev = jnp.concatenate([(h_last @ w_in[:, :3 * G_RWKV])[:, None], p_rkv[:, :-1]], axis=1)
        rkv = p_rkv + (p_rkv_prev - p_rkv) * p['mu_rkv'][l].reshape(-1)
        r, k, v = jnp.split(rkv.astype(f32), 3, axis=-1)

        mu = p['mu_x'][l]
        xw, xa, xg, xv = (h + xx * mu[i] for i in range(4))
        w_log = -jax.nn.softplus(-(p['decay_w0'][l] + jnp.tanh(xw @ p['decay_lora1'][l]) @ p['decay_lora2'][l]).astype(f32)) - 0.5
        decay = jnp.exp(-jnp.exp(w_log))
        a = jax.nn.sigmoid((p['iclr_a0'][l] + (xa @ p['iclr_lora1'][l]) @ p['iclr_lora2'][l]).astype(f32))
        g = jax.nn.sigmoid(xg @ p['gate_lora1'][l]) @ p['gate_lora2'][l]
        if v_first is None:
            v_first = v
        else:
            nu = jax.nn.sigmoid((p['vres_v0'][l - 1] + (xv @ p['vres_lora1'][l - 1]) @ p['vres_lora2'][l - 1]).astype(f32))
            v = v + (v_first - v) * nu
        kk = (k * p['k_k'][l]).reshape(B, T, N_HEADS, HEAD_SIZE)
        kk = kk / jnp.maximum(jnp.sqrt(jnp.sum(kk * kk, axis=-1, keepdims=True)), 1e-12)
        k = k * (1 + (a - 1) * p['k_a'][l])
        r_h, w_h, k_h, v_h, a_h = (t.reshape(B, T, N_HEADS, HEAD_SIZE) for t in (r, decay, k, v, a))
        y, S = wkv7_scan(st_wkv[l].astype(f32), r_h, w_h, k_h, v_h, kk, kk * a_h)
        mean = jnp.mean(y, axis=-1, keepdims=True)
        var = jnp.mean(jnp.square(y - mean), axis=-1, keepdims=True)
        yn = ((y - mean) * lax.rsqrt(var + GN_EPS)).reshape(B, T, G_RWKV) * p['ln_x_w'][l] + p['ln_x_b'][l]
        bonus = (jnp.sum(r_h * k_h * p['r_k'][l], axis=-1, keepdims=True) * v_h).reshape(B, T, G_RWKV)
        o_rwkv = ((yn + bonus) * g).astype(x.dtype)

        bg, cg, hc = jnp.split(p_conv, 3, axis=-1)
        zc, conv_buf = causal_dwconv(st_conv[l], cg * hc, p['conv_w'][l])
        o_conv = bg * zc

        x = x + ga1 * (jnp.concatenate([o_rwkv, o_conv], axis=-1) @ p['w_out'][l])

        h2 = rmsnorm(x, p['norm_g'][l, 1]) * (1 + sc2) + sh2
        u, ffn_buf = causal_dwconv(st_ffn[l], h2 @ p['ffn_up'][l], p['ffn_conv'][l])
        ua, ub = jnp.split(u, 2, axis=-1)
        x = x + ga2 * ((jax.nn.silu(ua) * ub) @ p['ffn_down'][l])

        new_wkv.append(S.astype(st_wkv.dtype))
        new_shift.append(h[:, -1].astype(st_shift.dtype))
        new_conv.append(conv_buf.astype(st_conv.dtype))
        new_ffn.append(ffn_buf.astype(st_ffn.dtype))
    return (rmsnorm(x, p['final_norm_g']), jnp.stack(new_wkv), jnp.stack(new_shift),
            jnp.stack(new_conv), jnp.stack(new_ffn))


def setup_inputs(seed: int = 0) -> dict:
    key = jax.random.key(seed)
    keys = jax.random.split(key, 40)
    counter = [0]

    def nxt():
        kk = keys[counter[0]]
        counter[0] += 1
        return kk

    def nrm(shape, s=1.0):
        return jax.random.normal(nxt(), shape, jnp.float32) * s

    def uni(shape, lo=0.0, hi=1.0):
        return jax.random.uniform(nxt(), shape, jnp.float32, lo, hi)

    D, G, GC, F = D_MODEL, G_RWKV, G_CONV, D_FF
    return {
        'x_prompt': nrm((BATCH, SEQ, D)),
        'x_sample': nrm((DEC_BATCH, DEC_SEQ, D)),
        'c_prompt': nrm((BATCH, D)),
        'c_sample': nrm((DEC_BATCH, D)),
        'state_wkv': nrm((DEPTH, DEC_BATCH, N_HEADS, HEAD_SIZE, HEAD_SIZE), 0.5),
        'state_shift': nrm((DEPTH, DEC_BATCH, D)),
        'state_conv': nrm((DEPTH, DEC_BATCH, CONV_W - 1, GC)),
        'state_ffn': nrm((DEPTH, DEC_BATCH, CONV_W - 1, 2 * F)),
        'ada_w': nrm((DEPTH, D, 6 * D), 0.5 * D ** -0.5),
        'ada_b': nrm((DEPTH, 6 * D), 0.02),
        'norm_g': 1.0 + nrm((DEPTH, 2, D), 0.05),
        'final_norm_g': 1.0 + nrm((D,), 0.05),
        'w_in': nrm((DEPTH, D, 3 * G + 3 * GC), D ** -0.5),
        'mu_x': uni((DEPTH, 4, D)),
        'mu_rkv': uni((DEPTH, 3, G)),
        'decay_w0': uni((DEPTH, G), -6.0, -1.0),
        'decay_lora1': nrm((DEPTH, D, LORA_DECAY), D ** -0.5),
        'decay_lora2': nrm((DEPTH, LORA_DECAY, G), 0.5 * LORA_DECAY ** -0.5),
        'iclr_a0': nrm((DEPTH, G), 0.1),
        'iclr_lora1': nrm((DEPTH, D, LORA_ICLR), D ** -0.5),
        'iclr_lora2': nrm((DEPTH, LORA_ICLR, G), 0.5 * LORA_ICLR ** -0.5),
        'gate_lora1': nrm((DEPTH, D, LORA_GATE), D ** -0.5),
        'gate_lora2': nrm((DEPTH, LORA_GATE, G), LORA_GATE ** -0.5),
        'vres_v0': nrm((DEPTH - 1, G), 0.1),
        'vres_lora1': nrm((DEPTH - 1, D, LORA_MV), D ** -0.5),
        'vres_lora2': nrm((DEPTH - 1, LORA_MV, G), 0.5 * LORA_MV ** -0.5),
        'k_k': 0.85 + nrm((DEPTH, G), 0.05),
        'k_a': 1.0 + nrm((DEPTH, G), 0.05),
        'r_k': nrm((DEPTH, N_HEADS, HEAD_SIZE), 0.1),
        'ln_x_w': 1.0 + nrm((DEPTH, G), 0.05),
        'ln_x_b': nrm((DEPTH, G), 0.02),
        'conv_w': nrm((DEPTH, CONV_W, GC), CONV_W ** -0.5),
        'w_out': nrm((DEPTH, D, D), D ** -0.5),
        'ffn_up': nrm((DEPTH, D, 2 * F), D ** -0.5),
        'ffn_conv': nrm((DEPTH, CONV_W, 2 * F), CONV_W ** -0.5),
        'ffn_down': nrm((DEPTH, F, D), F ** -0.5),
    }


def reference(x_prompt, x_sample, c_prompt, c_sample, state_wkv, state_shift, state_conv, state_ffn,
              ada_w, ada_b, norm_g, final_norm_g, w_in, mu_x, mu_rkv, decay_w0, decay_lora1, decay_lora2,
              iclr_a0, iclr_lora1, iclr_lora2, gate_lora1, gate_lora2, vres_v0, vres_lora1, vres_lora2,
              k_k, k_a, r_k, ln_x_w, ln_x_b, conv_w, w_out, ffn_up, ffn_conv, ffn_down):
    p = dict(ada_w=ada_w, ada_b=ada_b, norm_g=norm_g, final_norm_g=final_norm_g, w_in=w_in, mu_x=mu_x,
             mu_rkv=mu_rkv, decay_w0=decay_w0, decay_lora1=decay_lora1, decay_lora2=decay_lora2,
             iclr_a0=iclr_a0, iclr_lora1=iclr_lora1, iclr_lora2=iclr_lora2, gate_lora1=gate_lora1,
             gate_lora2=gate_lora2, vres_v0=vres_v0, vres_lora1=vres_lora1, vres_lora2=vres_lora2,
             k_k=k_k, k_a=k_a, r_k=r_k, ln_x_w=ln_x_w, ln_x_b=ln_x_b, conv_w=conv_w, w_out=w_out,
             ffn_up=ffn_up, ffn_conv=ffn_conv, ffn_down=ffn_down)
    bp = x_prompt.shape[0]
    dt = x_prompt.dtype
    z_wkv = jnp.zeros((DEPTH, bp) + state_wkv.shape[2:], dt)
    z_shift = jnp.zeros((DEPTH, bp) + state_shift.shape[2:], dt)
    z_conv = jnp.zeros((DEPTH, bp) + state_conv.shape[2:], dt)
    z_ffn = jnp.zeros((DEPTH, bp) + state_ffn.shape[2:], dt)
    y_prompt, wkv_p, shift_p, conv_p, ffn_p = run_trunk(x_prompt, c_prompt, z_wkv, z_shift, z_conv, z_ffn, p)
    y_sample, wkv_s, shift_s, conv_s, ffn_s = run_trunk(x_sample, c_sample, state_wkv, state_shift,
                                                        state_conv, state_ffn, p)
    return (y_prompt, y_sample, wkv_p, shift_p, conv_p, ffn_p, wkv_s, shift_s, conv_s, ffn_s)
```

```python
import functools

import jax
import jax.numpy as jnp
from jax import lax
from jax.experimental import pallas as pl
from jax.experimental.pallas import tpu as pltpu

_F32 = jnp.float32
_BF16 = jnp.bfloat16

_HEAD = 64
_PAIR = 2 * _HEAD
_SEG = 256
_RMS_EPS = 1e-6
_GN_EPS = 64e-5
_VMEM_LIMIT_BYTES = 56 * 1024 * 1024
_SUBLANES = 8


def _params(*sem):
    return pltpu.CompilerParams(dimension_semantics=sem, vmem_limit_bytes=_VMEM_LIMIT_BYTES)


def _dot(a, b):
    return jnp.dot(a.astype(_BF16), b.astype(_BF16), preferred_element_type=_F32)


def _dot_nt(a, b):
    return lax.dot_general(a.astype(_BF16), b.astype(_BF16), (((1,), (1,)), ((), ())),
                           preferred_element_type=_F32)


def _dot_tn(a, b):
    return lax.dot_general(a.astype(_BF16), b.astype(_BF16), (((0,), (0,)), ((), ())),
                           preferred_element_type=_F32)


def _split2(x):
    hi = x.astype(_BF16)
    lo = (x - hi.astype(_F32)).astype(_BF16)
    return hi, lo


def _split3(x):
    hi = x.astype(_BF16)
    r1 = x - hi.astype(_F32)
    mid = r1.astype(_BF16)
    lo = (r1 - mid.astype(_F32)).astype(_BF16)
    return hi, mid, lo


def _dot_exact_rhs(x, m_bf16):
    hi, lo = _split2(x)
    return (jnp.dot(hi, m_bf16, preferred_element_type=_F32)
            + jnp.dot(lo, m_bf16, preferred_element_type=_F32))


def _segment_ones(n):
    r = lax.broadcasted_iota(jnp.int32, (n, n), 0) // _HEAD
    c = lax.broadcasted_iota(jnp.int32, (n, n), 1) // _HEAD
    return (r == c).astype(_BF16)


def _head_sums(x, seg_ones):
    n = seg_ones.shape[0]
    parts = [_dot_exact_rhs(x[:, q * n:(q + 1) * n], seg_ones) for q in range(x.shape[1] // n)]
    return parts[0] if len(parts) == 1 else jnp.concatenate(parts, axis=1)


def _tile_rows(m, reps):
    return m if reps == 1 else jnp.concatenate([m] * reps, axis=0)


def _shift_rows_prompt(x, carry8, first, rows):
    l1 = jnp.where(first, 0.0, carry8[_SUBLANES - 1:_SUBLANES, :])
    l2 = jnp.where(first, 0.0, carry8[_SUBLANES - 2:_SUBLANES - 1, :])
    x1 = jnp.where(rows == 0, l1, pltpu.roll(x, 1, 0))
    x2 = jnp.where(rows == 0, l2, jnp.where(rows == 1, l1, pltpu.roll(x, 2, 0)))
    return x1, x2


def _mod_row(ref, i, tps, db):
    return ref[0, 0, pl.ds(db + i // tps, 1), :]


def _mod_tile(ref, db, ts):
    return _tile_rows(ref[0, 0, 0:db, :], ts)


def _mod_kernel(c_ref, w_ref, b_ref, o_ref):
    c = c_ref[...]
    s_hi, s_lo = _split2(c * jax.nn.sigmoid(c))
    w_hi, w_lo = _split2(w_ref[0])
    acc = (jnp.dot(s_hi, w_hi, preferred_element_type=_F32)
           + jnp.dot(s_hi, w_lo, preferred_element_type=_F32)
           + jnp.dot(s_lo, w_hi, preferred_element_type=_F32))
    o_ref[0, 0] = acc + b_ref[0]


def _adaln_mod(c_all, ada_w, ada_b):
    depth, d, six_d = ada_w.shape
    bc = c_all.shape[0]
    tn = 1024
    per = d // tn
    return pl.pallas_call(
        _mod_kernel,
        grid=(depth, six_d // tn),
        in_specs=[pl.BlockSpec((bc, d), lambda l, n: (0, 0)),
                  pl.BlockSpec((1, d, tn), lambda l, n: (l, 0, n)),
                  pl.BlockSpec((1, 1, tn), lambda l, n: (l, 0, n))],
        out_specs=pl.BlockSpec((1, 1, bc, tn), lambda l, n: (l, n // per, 0, n % per)),
        out_shape=jax.ShapeDtypeStruct((depth, 6, bc, d), _F32),
        compiler_params=_params("arbitrary", "arbitrary"),
    )(c_all, ada_w, ada_b.reshape(depth, 1, six_d))


def _norm_lora_kernel(*refs, n_pt, tps, db, has_v):
    if has_v:
        (x_ref, sh_ref, sc_ref, g_ref, mu_ref, hl_ref, d1_ref, i1_ref, g1_ref, v1_ref,
         h_ref, aw_ref, aa_ref, ag_ref, av_ref, hp_ref, hs_ref, carry_ref, hcur_ref, hprev_ref) = refs
    else:
        (x_ref, sh_ref, sc_ref, g_ref, mu_ref, hl_ref, d1_ref, i1_ref, g1_ref,
         h_ref, aw_ref, aa_ref, ag_ref, hp_ref, hs_ref, carry_ref, hcur_ref, hprev_ref) = refs
    i = pl.program_id(0)
    tm = x_ref.shape[0]
    ts = tm // db
    x = x_ref[...]
    xn = x * lax.rsqrt(jnp.mean(x * x, axis=-1, keepdims=True) + _RMS_EPS) * g_ref[0, 0]

    @pl.when(i < n_pt)
    def _prompt():
        h = xn * (1.0 + _mod_row(sc_ref, i, tps, db)) + _mod_row(sh_ref, i, tps, db)
        first = (i % tps) == 0
        rows = lax.broadcasted_iota(jnp.int32, (tm, 1), 0)
        prev_row = jnp.where(first, 0.0, carry_ref[_SUBLANES - 1:_SUBLANES, :])
        hprev_ref[...] = jnp.where(rows == 0, prev_row, pltpu.roll(h, 1, 0))
        hcur_ref[...] = h
        carry_ref[...] = h[tm - _SUBLANES:, :]
        hp_ref[0] = h[tm - _SUBLANES:, :]

    @pl.when(i >= n_pt)
    def _decode():
        h = xn * (1.0 + _mod_tile(sc_ref, db, ts)) + _mod_tile(sh_ref, db, ts)
        hprev_ref[...] = jnp.concatenate([hl_ref[0], h[:tm - db, :]], axis=0)
        hcur_ref[...] = h
        hs_ref[...] = h[tm - db:, :]

    h = hcur_ref[...]
    xx = hprev_ref[...] - h
    mu = mu_ref[0]
    h_ref[...] = h.astype(_BF16)
    aw_ref[...] = jnp.tanh(_dot(h + xx * mu[0:1], d1_ref[0]))
    aa_ref[...] = _dot(h + xx * mu[1:2], i1_ref[0])
    ag_ref[...] = jax.nn.sigmoid(_dot(h + xx * mu[2:3], g1_ref[0]))
    if has_v:
        av_ref[...] = _dot(h + xx * mu[3:4], v1_ref[0])


def _norm_lora(l, x, mod, norm_g4, mu_x, state_shift, decay_lora1, iclr_lora1, gate_lora1, vres_lora1,
               *, tm, n_pt, tps, db, bp):
    rows, d = x.shape
    has_v = l > 0
    bc = mod.shape[2]
    ld, la, lg = decay_lora1.shape[2], iclr_lora1.shape[2], gate_lora1.shape[2]
    full = lambda *shape: pl.BlockSpec(shape, lambda i: (l,) + (0,) * (len(shape) - 1))
    in_specs = [pl.BlockSpec((tm, d), lambda i: (i, 0)),
                pl.BlockSpec((1, 1, bc, d), lambda i: (l, 0, 0, 0)),
                pl.BlockSpec((1, 1, bc, d), lambda i: (l, 1, 0, 0)),
                pl.BlockSpec((1, 1, 1, d), lambda i: (l, 0, 0, 0)),
                full(1, 4, d), full(1, db, d), full(1, d, ld), full(1, d, la), full(1, d, lg)]
    args = [x, mod, mod, norm_g4, mu_x, state_shift, decay_lora1, iclr_lora1, gate_lora1]
    row_out = lambda n, dt: (pl.BlockSpec((tm, n), lambda i: (i, 0)), jax.ShapeDtypeStruct((rows, n), dt))
    outs = [row_out(d, _BF16), row_out(ld, _F32), row_out(la, _F32), row_out(lg, _F32)]
    if has_v:
        lv = vres_lora1.shape[2]
        in_specs.append(pl.BlockSpec((1, d, lv), lambda i: (l - 1, 0, 0)))
        args.append(vres_lora1)
        outs.append(row_out(lv, _F32))
    outs.append((pl.BlockSpec((1, _SUBLANES, d), lambda i: (jnp.minimum(i // tps, bp - 1), 0, 0)),
                 jax.ShapeDtypeStruct((bp, _SUBLANES, d), _F32)))
    outs.append((pl.BlockSpec((db, d), lambda i: (0, 0)), jax.ShapeDtypeStruct((db, d), _F32)))
    return pl.pallas_call(
        functools.partial(_norm_lora_kernel, n_pt=n_pt, tps=tps, db=db, has_v=has_v),
        grid=(rows // tm,),
        in_specs=in_specs,
        out_specs=[o[0] for o in outs],
        out_shape=[o[1] for o in outs],
        scratch_shapes=[pltpu.VMEM((_SUBLANES, d), _F32), pltpu.VMEM((tm, d), _F32), pltpu.VMEM((tm, d), _F32)],
        compiler_params=_params("arbitrary"),
    )(*args)


def _mix_kernel(*refs, n_pt, tps, db, has_v):
    it = iter(refs)
    h_ref = next(it)
    w_refs = [next(it) for _ in range(6)]
    aw_ref, aa_ref, ag_ref = next(it), next(it), next(it)
    av_ref = next(it) if has_v else None
    d2_ref, i2_ref, g2_ref = next(it), next(it), next(it)
    v2_ref = next(it) if has_v else None
    w0_ref, a0_ref = next(it), next(it)
    v0_ref = next(it) if has_v else None
    mu_ref, kkw_ref, kaw_ref, cw_ref, hl_ref, cs_ref = (next(it) for _ in range(6))
    vf_ref = next(it) if has_v else None
    r_o, lw_o, k_o, v_o, kk_o, kka_o, g_o, oc_o = (next(it) for _ in range(8))
    vf_o = None if has_v else next(it)
    cp_o, cso_o = next(it), next(it)
    wc_scr, pcarry, zcarry, pprev_scr, z1_scr, z2_scr = (next(it) for _ in range(6))

    i = pl.program_id(1)
    tm = h_ref.shape[0]
    tn = r_o.shape[1]

    @pl.when(i == 0)
    def _cache_weights():
        for g in range(6):
            wc_scr[g] = w_refs[g][0].astype(_BF16)

    h = h_ref[...]
    p = [jnp.dot(h, wc_scr[g], preferred_element_type=_F32) for g in range(3)]
    bg = jnp.dot(h, wc_scr[3], preferred_element_type=_F32)
    z = (jnp.dot(h, wc_scr[4], preferred_element_type=_F32)
         * jnp.dot(h, wc_scr[5], preferred_element_type=_F32))

    @pl.when(i < n_pt)
    def _prompt():
        first = (i % tps) == 0
        rows = lax.broadcasted_iota(jnp.int32, (tm, 1), 0)
        for g in range(3):
            prev_row = jnp.where(first, 0.0, pcarry[g, _SUBLANES - 1:_SUBLANES, :])
            pprev_scr[g] = jnp.where(rows == 0, prev_row, pltpu.roll(p[g], 1, 0))
            pcarry[g] = p[g][tm - _SUBLANES:, :]
        z1, z2 = _shift_rows_prompt(z, zcarry[...], first, rows)
        z1_scr[...] = z1
        z2_scr[...] = z2
        zcarry[...] = z[tm - _SUBLANES:, :]
        cp_o[0] = z[tm - _SUBLANES:, :]

    @pl.when(i >= n_pt)
    def _decode():
        hl = hl_ref[0].astype(_BF16)
        for g in range(3):
            hlp = jnp.dot(hl, wc_scr[g], preferred_element_type=_F32)
            pprev_scr[g] = jnp.concatenate([hlp, p[g][:tm - db, :]], axis=0)
        z1_scr[...] = jnp.concatenate([cs_ref[0, 1], z[:tm - db, :]], axis=0)
        z2_scr[...] = jnp.concatenate([cs_ref[0, 0], cs_ref[0, 1], z[:tm - 2 * db, :]], axis=0)
        cso_o[0] = z[tm - 2 * db:tm - db, :]
        cso_o[1] = z[tm - db:, :]

    mu = mu_ref[0]
    r = p[0] + (pprev_scr[0] - p[0]) * mu[0:1]
    k = p[1] + (pprev_scr[1] - p[1]) * mu[1:2]
    v = p[2] + (pprev_scr[2] - p[2]) * mu[2:3]

    zlog = w0_ref[0] + _dot(aw_ref[...], d2_ref[0])
    softplus = jnp.maximum(-zlog, 0.0) + jnp.log(1.0 + jnp.exp(-jnp.abs(zlog)))
    lw_o[...] = -jnp.exp(-softplus - 0.5)
    a = jax.nn.sigmoid(a0_ref[0] + _dot(aa_ref[...], i2_ref[0]))
    g_o[...] = _dot(ag_ref[...], g2_ref[0])
    if has_v:
        nu = jax.nn.sigmoid(v0_ref[0] + _dot(av_ref[...], v2_ref[0]))
        v = v + (vf_ref[...] - v) * nu
    else:
        vf_o[...] = v

    kk = k * kkw_ref[0]
    norm = jnp.sqrt(_head_sums(kk * kk, _segment_ones(tn)))
    kk = kk / jnp.maximum(norm, 1e-12)
    r_o[...] = r
    k_o[...] = k * (1.0 + (a - 1.0) * kaw_ref[0])
    v_o[...] = v
    kk_o[...] = kk
    kka_o[...] = kk * a

    cw = cw_ref[0]
    zc = z2_scr[...] * cw[0:1] + z1_scr[...] * cw[1:2] + z * cw[2:3]
    oc_o[...] = (bg * zc).astype(_BF16)


def _mix(l, h, acts, vf_in, w_in, lora2, vecs, mu_rkv, k_k3, k_a3, conv_w, state_shift, state_conv_t,
         *, tm, tn, n_pt, tps, db, bp):
    rows, d = h.shape
    has_v = l > 0
    g_dim = mu_rkv.shape[2]
    nj = g_dim // tn
    aw, aa, ag, av = acts
    d2, i2, g2, v2 = lora2
    w0, a0, v0 = vecs
    row_in = lambda arr: pl.BlockSpec((tm, arr.shape[1]), lambda j, i: (i, 0))
    col3 = lambda arr, ll: pl.BlockSpec((1, arr.shape[1], tn), lambda j, i: (ll, 0, j))
    in_specs = [row_in(h)] + [pl.BlockSpec((1, d, tn), lambda j, i, g=g: (l, 0, g * nj + j)) for g in range(6)]
    args = [h] + [w_in] * 6
    in_specs += [row_in(aw), row_in(aa), row_in(ag)]
    args += [aw, aa, ag]
    if has_v:
        in_specs.append(row_in(av))
        args.append(av)
    in_specs += [col3(d2, l), col3(i2, l), col3(g2, l)]
    args += [d2, i2, g2]
    if has_v:
        in_specs.append(col3(v2, l - 1))
        args.append(v2)
    in_specs += [col3(w0, l), col3(a0, l)]
    args += [w0, a0]
    if has_v:
        in_specs.append(col3(v0, l - 1))
        args.append(v0)
    in_specs += [col3(mu_rkv, l), col3(k_k3, l), col3(k_a3, l), col3(conv_w, l),
                 pl.BlockSpec((1, db, d), lambda j, i: (l, 0, 0)),
                 pl.BlockSpec((1, 2, db, tn), lambda j, i: (l, 0, 0, j))]
    args += [mu_rkv, k_k3, k_a3, conv_w, state_shift, state_conv_t]
    if has_v:
        in_specs.append(pl.BlockSpec((tm, tn), lambda j, i: (i, j)))
        args.append(vf_in)
    tile = pl.BlockSpec((tm, tn), lambda j, i: (i, j))
    outs = [(tile, jax.ShapeDtypeStruct((rows, g_dim), _F32)) for _ in range(7)]
    outs.append((tile, jax.ShapeDtypeStruct((rows, g_dim), _BF16)))
    if not has_v:
        outs.append((tile, jax.ShapeDtypeStruct((rows, g_dim), _F32)))
    outs.append((pl.BlockSpec((1, _SUBLANES, tn), lambda j, i: (jnp.minimum(i // tps, bp - 1), 0, j)),
                 jax.ShapeDtypeStruct((bp, _SUBLANES, g_dim), _F32)))
    outs.append((pl.BlockSpec((2, db, tn), lambda j, i: (0, 0, j)), jax.ShapeDtypeStruct((2, db, g_dim), _F32)))
    return pl.pallas_call(
        functools.partial(_mix_kernel, n_pt=n_pt, tps=tps, db=db, has_v=has_v),
        grid=(nj, rows // tm),
        in_specs=in_specs,
        out_specs=[o[0] for o in outs],
        out_shape=[o[1] for o in outs],
        scratch_shapes=[pltpu.VMEM((6, d, tn), _BF16), pltpu.VMEM((3, _SUBLANES, tn), _F32),
                        pltpu.VMEM((_SUBLANES, tn), _F32), pltpu.VMEM((3, tm, tn), _F32),
                        pltpu.VMEM((tm, tn), _F32), pltpu.VMEM((tm, tn), _F32)],
        compiler_params=_params("arbitrary", "arbitrary"),
    )(*args)


def _unit_lower_inverse_minus_identity(lmat, ri, ci, chunk):
    d8 = jnp.where((ri >> 3) == (ci >> 3), lmat, 0.0)
    d2 = _dot(d8, d8)
    x = d8 + d2 + _dot(d8, d2)
    d4 = _dot(d2, d2)
    x = x + d4 + _dot(x, d4)
    size = 16
    while size <= chunk:
        sh = size.bit_length() - 1
        e = jnp.where(((ri >> sh) == (ci >> sh)) & ((ri >> (sh - 1)) != (ci >> (sh - 1))), lmat, 0.0)
        y = e + _dot(x, e)
        x = x + y + _dot(y, x)
        size *= 2
    return x


def _wkv_kernel(*refs, chunk, nb, zero_init, n_chunks):
    (r_ref, lw_ref, k_ref, v_ref, kk_ref, kka_ref, g_ref, rk_ref, lnw_ref, lnb_ref) = refs[:10]
    if zero_init:
        o_ref, so_ref, s_scr = refs[10:]
        s0_ref = None
    else:
        s0_ref, o_ref, so_ref, s_scr = refs[10:]
    ci_grid = pl.program_id(1)
    g_dim = r_ref.shape[1]
    npair = g_dim // _PAIR
    c = chunk
    n2 = 2 * c

    lane = lax.broadcasted_iota(jnp.int32, (1, _PAIR), 1)
    m0 = (lane < _HEAD).astype(_F32)
    m1 = 1.0 - m0

    @pl.when(ci_grid == 0)
    def _init():
        if zero_init:
            s_scr[...] = jnp.zeros(s_scr.shape, _F32)
        else:
            zero = jnp.zeros((_HEAD, _HEAD), _F32)
            for u in range(nb):
                for p in range(npair):
                    top = jnp.concatenate([s0_ref[u, 2 * p], zero], axis=1)
                    bot = jnp.concatenate([zero, s0_ref[u, 2 * p + 1]], axis=1)
                    s_scr[u, p] = jnp.concatenate([top, bot], axis=0)

    ri = lax.broadcasted_iota(jnp.int32, (n2, n2), 0) & (c - 1)
    ci = lax.broadcasted_iota(jnp.int32, (n2, n2), 1) & (c - 1)
    strict = ri > ci
    incl = ri >= ci
    tri = (lax.broadcasted_iota(jnp.int32, (c, c), 0) >= lax.broadcasted_iota(jnp.int32, (c, c), 1)).astype(_BF16)
    seg_ones = _segment_ones(_SEG)
    rk = rk_ref[...]
    lnw = lnw_ref[...]
    lnb = lnb_ref[...]

    def stack(x):
        return jnp.concatenate([x * m0, x * m1], axis=0)

    def unit(u, carry):
        rows = slice(0, c) if nb == 1 else pl.ds(pl.multiple_of(u * c, c), c)
        lw = lw_ref[rows, :]
        l_hi, l_mid, l_lo = _split3(lw)
        cl = (jnp.dot(tri, l_hi, preferred_element_type=_F32)
              + jnp.dot(tri, l_mid, preferred_element_type=_F32)
              + jnp.dot(tri, l_lo, preferred_element_type=_F32))
        cl_end = cl[c - 1:c, :]
        r = r_ref[rows, :]
        k = k_ref[rows, :]
        v = v_ref[rows, :]
        kk = kk_ref[rows, :]
        kka = kka_ref[rows, :]
        e_neg = jnp.exp(-cl)
        e_tail = jnp.exp(cl_end - cl)
        p_end = jnp.exp(cl_end)
        rt = r * jnp.exp(cl)
        at = -(kk * jnp.exp(cl - lw))
        kt = k * e_neg
        bt = kka * e_neg
        kh = k * e_tail
        bh = kka * e_tail
        ys = []
        for p in range(npair):
            sl = slice(p * _PAIR, (p + 1) * _PAIR)
            a_p, r_p = at[:, sl], rt[:, sl]
            ar2 = jnp.concatenate([stack(a_p), stack(r_p)], axis=0)
            mb = _dot_nt(ar2, stack(bt[:, sl]))
            mk = _dot_nt(ar2, stack(kt[:, sl]))
            lab = jnp.where(strict, mb[:n2], 0.0)
            lak = jnp.where(strict, mk[:n2], 0.0)
            lrb = jnp.where(incl, mb[n2:], 0.0)
            lrk = jnp.where(incl, mk[n2:], 0.0)
            x_inv = _unit_lower_inverse_minus_identity(lab, ri, ci, c)
            s_old = s_scr[u, p]
            uy0 = _dot_nt(jnp.concatenate([a_p, r_p], axis=0), s_old)
            vs = stack(v[:, sl])
            ws = stack(uy0[:c]) + _dot(lak, vs)
            us = ws + _dot(x_inv, ws)
            ysd = _dot(lrb, us) + _dot(lrk, vs)
            ys.append(uy0[c:] + ysd[:c] + ysd[c:])
            uv = jnp.concatenate([us, vs], axis=0)
            bk = jnp.concatenate([stack(bh[:, sl]), stack(kh[:, sl])], axis=0)
            s_scr[u, p] = s_old * p_end[:, sl] + _dot_tn(uv, bk)
        y = jnp.concatenate(ys, axis=1)
        mean = _head_sums(y, seg_ones) * (1.0 / _HEAD)
        dy = y - mean
        var = _head_sums(dy * dy, seg_ones) * (1.0 / _HEAD)
        yn = dy * lax.rsqrt(var + _GN_EPS) * lnw + lnb
        bonus = _head_sums(r * k * rk, seg_ones) * v
        o_ref[rows, :] = ((yn + bonus) * g_ref[rows, :]).astype(o_ref.dtype)
        return carry

    if nb == 1:
        unit(0, 0)
    else:
        lax.fori_loop(0, nb, unit, 0)

    @pl.when(ci_grid == n_chunks - 1)
    def _final():
        for u in range(nb):
            for p in range(npair):
                s = s_scr[u, p]
                so_ref[u, 2 * p] = s[:_HEAD, :_HEAD]
                so_ref[u, 2 * p + 1] = s[_HEAD:, _HEAD:]


def _wkv(l, seqs, chunk, nb, n_seq, n_chunks, state0, r_k3, ln_w3, ln_b3, *, row_block_of):
    g_dim = seqs[0].shape[1]
    n_heads = g_dim // _HEAD
    rows = seqs[0].shape[0]
    zero_init = state0 is None
    blk = pl.BlockSpec((nb * chunk, g_dim), lambda bi, ci: (row_block_of(bi, ci), 0))
    vec = pl.BlockSpec((None, 1, g_dim), lambda bi, ci: (l, 0, 0))
    st = pl.BlockSpec((nb, n_heads, _HEAD, _HEAD), lambda bi, ci: (bi, 0, 0, 0))
    in_specs = [blk] * 7 + [vec] * 3
    args = list(seqs) + [r_k3, ln_w3, ln_b3]
    if not zero_init:
        in_specs.append(st)
        args.append(state0)
    return pl.pallas_call(
        functools.partial(_wkv_kernel, chunk=chunk, nb=nb, zero_init=zero_init, n_chunks=n_chunks),
        grid=(n_seq // nb, n_chunks),
        in_specs=in_specs,
        out_specs=[blk, st],
        out_shape=[jax.ShapeDtypeStruct((rows, g_dim), _BF16),
                   jax.ShapeDtypeStruct((n_seq, n_heads, _HEAD, _HEAD), _F32)],
        scratch_shapes=[pltpu.VMEM((nb, g_dim // _PAIR, _PAIR, _PAIR), _F32)],
        compiler_params=_params("arbitrary", "arbitrary"),
    )(*args)


def _cast_kernel(w_ref, o_ref):
    o_ref[...] = w_ref[...].astype(o_ref.dtype)


def _cast_bf16(w, rows_per_block):
    depth, k, n = w.shape
    spec = pl.BlockSpec((1, rows_per_block, n), lambda l, i: (l, i, 0))
    return pl.pallas_call(
        _cast_kernel, grid=(depth, k // rows_per_block), in_specs=[spec], out_specs=spec,
        out_shape=jax.ShapeDtypeStruct(w.shape, _BF16),
        compiler_params=_params("arbitrary", "arbitrary"),
    )(w)


def _outproj_kernel(op_ref, os_ref, oc_ref, x_ref, ga_ref, sh_ref, sc_ref, g_ref, w_ref,
                    x1_ref, h2_ref, *, n_pt, tps, db):
    i = pl.program_id(0)
    tm = x_ref.shape[0]
    ts = tm // db
    g_dim = op_ref.shape[1]

    o_rw = jnp.where(i < n_pt, op_ref[...], os_ref[...])
    acc = (jnp.dot(o_rw, w_ref[0, :g_dim, :], preferred_element_type=_F32)
           + jnp.dot(oc_ref[...], w_ref[0, g_dim:, :], preferred_element_type=_F32))

    def finish(ga, sc, sh):
        x1 = x_ref[...] + ga * acc
        x1_ref[...] = x1
        xn = x1 * lax.rsqrt(jnp.mean(x1 * x1, axis=-1, keepdims=True) + _RMS_EPS) * g_ref[0, 0]
        h2_ref[...] = (xn * (1.0 + sc) + sh).astype(_BF16)

    @pl.when(i < n_pt)
    def _prompt():
        finish(_mod_row(ga_ref, i, tps, db), _mod_row(sc_ref, i, tps, db), _mod_row(sh_ref, i, tps, db))

    @pl.when(i >= n_pt)
    def _decode():
        finish(_mod_tile(ga_ref, db, ts), _mod_tile(sc_ref, db, ts), _mod_tile(sh_ref, db, ts))


def _outproj(l, o_p, o_s, o_conv, x, mod, norm_g4, w_out, *, tm, n_pt, tps, db):
    rows, d = x.shape
    g_dim = o_p.shape[1]
    bc = mod.shape[2]
    modspec = lambda comp: pl.BlockSpec((1, 1, bc, d), lambda i: (l, comp, 0, 0))
    return pl.pallas_call(
        functools.partial(_outproj_kernel, n_pt=n_pt, tps=tps, db=db),
        grid=(rows // tm,),
        in_specs=[pl.BlockSpec((tm, g_dim), lambda i: (jnp.minimum(i, n_pt - 1), 0)),
                  pl.BlockSpec((tm, g_dim), lambda i: (0, 0)),
                  pl.BlockSpec((tm, o_conv.shape[1]), lambda i: (i, 0)),
                  pl.BlockSpec((tm, d), lambda i: (i, 0)),
                  modspec(2), modspec(3), modspec(4),
                  pl.BlockSpec((1, 1, 1, d), lambda i: (l, 1, 0, 0)),
                  pl.BlockSpec((1, d, d), lambda i: (l, 0, 0), pipeline_mode=pl.Buffered(1))],
        out_specs=[pl.BlockSpec((tm, d), lambda i: (i, 0)), pl.BlockSpec((tm, d), lambda i: (i, 0))],
        out_shape=[jax.ShapeDtypeStruct((rows, d), _F32), jax.ShapeDtypeStruct((rows, d), _BF16)],
        compiler_params=_params("arbitrary"),
    )(o_p, o_s, o_conv, x, mod, mod, mod, norm_g4, w_out)


def _ffn_kernel(*refs, n_pt, tps, db, nj, final_norm):
    it = iter(refs)
    (h2_ref, wa_ref, wb_ref, cwa_ref, cwb_ref, wd_ref, x1_ref, ga_ref, sfa_ref, sfb_ref) = (next(it) for _ in range(10))
    fg_ref = next(it) if final_norm else None
    x2_ref, fpa_o, fpb_o, fsa_o, fsb_o = (next(it) for _ in range(5))
    acc_scr, ca_scr, cb_scr, s1_scr, s2_scr = (next(it) for _ in range(5))
    i = pl.program_id(0)
    j = pl.program_id(1)
    tm = h2_ref.shape[0]
    ts = tm // db
    h2 = h2_ref[...]
    ua = jnp.dot(h2, wa_ref[0].astype(_BF16), preferred_element_type=_F32)
    ub = jnp.dot(h2, wb_ref[0].astype(_BF16), preferred_element_type=_F32)

    @pl.when(j == 0)
    def _zero():
        acc_scr[...] = jnp.zeros(acc_scr.shape, _F32)

    @pl.when(i < n_pt)
    def _prompt():
        first = (i % tps) == 0
        rows = lax.broadcasted_iota(jnp.int32, (tm, 1), 0)
        for idx, (u, c_scr, f_o) in enumerate(((ua, ca_scr, fpa_o), (ub, cb_scr, fpb_o))):
            u1, u2 = _shift_rows_prompt(u, c_scr[j], first, rows)
            s1_scr[idx] = u1
            s2_scr[idx] = u2
            c_scr[j] = u[tm - _SUBLANES:, :]
            f_o[0] = u[tm - _SUBLANES:, :]

    @pl.when(i >= n_pt)
    def _decode():
        for idx, (u, sf, f_o, fp_o) in enumerate(((ua, sfa_ref, fsa_o, fpa_o), (ub, sfb_ref, fsb_o, fpb_o))):
            s1_scr[idx] = jnp.concatenate([sf[0, 1], u[:tm - db, :]], axis=0)
            s2_scr[idx] = jnp.concatenate([sf[0, 0], sf[0, 1], u[:tm - 2 * db, :]], axis=0)
            f_o[0] = u[tm - 2 * db:tm - db, :]
            f_o[1] = u[tm - db:, :]
            fp_o[0] = jnp.zeros(fp_o.shape[1:], _F32)

    cwa = cwa_ref[0]
    cwb = cwb_ref[0]
    ca = s2_scr[0] * cwa[0:1] + s1_scr[0] * cwa[1:2] + ua * cwa[2:3]
    cb = s2_scr[1] * cwb[0:1] + s1_scr[1] * cwb[1:2] + ub * cwb[2:3]
    gact = (ca * jax.nn.sigmoid(ca)) * cb
    acc_scr[...] += jnp.dot(gact.astype(_BF16), wd_ref[0].astype(_BF16), preferred_element_type=_F32)

    def finish(ga):
        x2 = x1_ref[...] + ga * acc_scr[...]
        if final_norm:
            x2 = x2 * lax.rsqrt(jnp.mean(x2 * x2, axis=-1, keepdims=True) + _RMS_EPS) * fg_ref[...]
        x2_ref[...] = x2

    @pl.when((j == nj - 1) & (i < n_pt))
    def _finish_prompt():
        finish(_mod_row(ga_ref, i, tps, db))

    @pl.when((j == nj - 1) & (i >= n_pt))
    def _finish_decode():
        finish(_mod_tile(ga_ref, db, ts))


def _ffn(l, h2, x1, mod, ffn_up, ffn_conv, ffn_down, state_ffn_t, final_g, *, tm, tn, n_pt, tps, db, bp):
    rows, d = x1.shape
    f = ffn_down.shape[1]
    nj = f // tn
    bc = mod.shape[2]
    final_norm = final_g is not None
    in_specs = [pl.BlockSpec((tm, d), lambda i, j: (i, 0)),
                pl.BlockSpec((1, d, tn), lambda i, j: (l, 0, j)),
                pl.BlockSpec((1, d, tn), lambda i, j: (l, 0, nj + j)),
                pl.BlockSpec((1, 3, tn), lambda i, j: (l, 0, j)),
                pl.BlockSpec((1, 3, tn), lambda i, j: (l, 0, nj + j)),
                pl.BlockSpec((1, tn, d), lambda i, j: (l, j, 0)),
                pl.BlockSpec((tm, d), lambda i, j: (i, 0)),
                pl.BlockSpec((1, 1, bc, d), lambda i, j: (l, 5, 0, 0)),
                pl.BlockSpec((1, 2, db, tn), lambda i, j: (l, 0, 0, j)),
                pl.BlockSpec((1, 2, db, tn), lambda i, j: (l, 0, 0, nj + j))]
    args = [h2, ffn_up, ffn_up, ffn_conv, ffn_conv, ffn_down, x1, mod, state_ffn_t, state_ffn_t]
    if final_norm:
        in_specs.append(pl.BlockSpec((1, d), lambda i, j: (0, 0)))
        args.append(final_g)
    n_tiles = rows // tm
    pstate = pl.BlockSpec((1, _SUBLANES, tn), lambda i, j: (i, 0, j))
    sstate = pl.BlockSpec((2, db, tn), lambda i, j: (0, 0, jnp.where(i >= n_pt, j, 0)))
    return pl.pallas_call(
        functools.partial(_ffn_kernel, n_pt=n_pt, tps=tps, db=db, nj=nj, final_norm=final_norm),
        grid=(rows // tm, nj),
        in_specs=in_specs,
        out_specs=[pl.BlockSpec((tm, d), lambda i, j: (i, 0)), pstate, pstate, sstate, sstate],
        out_shape=[jax.ShapeDtypeStruct((rows, d), _F32),
                   jax.ShapeDtypeStruct((n_tiles, _SUBLANES, f), _F32),
                   jax.ShapeDtypeStruct((n_tiles, _SUBLANES, f), _F32),
                   jax.ShapeDtypeStruct((2, db, f), _F32), jax.ShapeDtypeStruct((2, db, f), _F32)],
        scratch_shapes=[pltpu.VMEM((tm, d), _F32), pltpu.VMEM((nj, _SUBLANES, tn), _F32),
                        pltpu.VMEM((nj, _SUBLANES, tn), _F32), pltpu.VMEM((2, tm, tn), _F32),
                        pltpu.VMEM((2, tm, tn), _F32)],
        compiler_params=_params("arbitrary", "arbitrary"),
    )(*args)


def _forward(x_prompt, x_sample, c_prompt, c_sample, state_wkv, state_shift, state_conv, state_ffn,
             ada_w, ada_b, norm_g, final_norm_g, w_in, mu_x, mu_rkv, decay_w0, decay_lora1, decay_lora2,
             iclr_a0, iclr_lora1, iclr_lora2, gate_lora1, gate_lora2, vres_v0, vres_lora1, vres_lora2,
             k_k, k_a, r_k, ln_x_w, ln_x_b, conv_w, w_out, ffn_up, ffn_conv, ffn_down,
             *, chunk=64, mix_tn=256, ffn_tn=256, wkv_nb=8):
    bp, t_len, d = x_prompt.shape
    db, ts, _ = x_sample.shape
    depth = ada_w.shape[0]
    g_dim = mu_rkv.shape[2]
    n_heads = g_dim // _HEAD
    f = ffn_down.shape[1]
    tm = ts * db
    assert t_len % tm == 0 and t_len % chunk == 0 and ts >= 2 and db % _SUBLANES == 0 and tm % chunk == 0
    tps = t_len // tm
    n_pt = bp * tps
    n_prompt_rows = bp * t_len
    dec_chunk = 16
    assert ts <= dec_chunk and db % wkv_nb == 0

    x = jnp.concatenate([x_prompt.reshape(n_prompt_rows, d),
                         x_sample.transpose(1, 0, 2).reshape(tm, d)], axis=0)
    pad = (-(db + bp)) % _SUBLANES
    c_all = jnp.concatenate([c_sample, c_prompt, jnp.zeros((pad, d), _F32)], axis=0)
    mod = _adaln_mod(c_all, ada_w, ada_b)

    norm_g4 = norm_g.reshape(depth, 2, 1, d)
    vec3 = lambda a: a.reshape(a.shape[0], 1, a.shape[1])
    w0_3, a0_3, v0_3, kk_3, ka_3 = vec3(decay_w0), vec3(iclr_a0), vec3(vres_v0), vec3(k_k), vec3(k_a)
    rk_3, lnw_3, lnb_3 = r_k.reshape(depth, 1, g_dim), vec3(ln_x_w), vec3(ln_x_b)
    state_conv_t = state_conv.transpose(0, 2, 1, 3)
    state_ffn_t = state_ffn.transpose(0, 2, 1, 3)
    tiles = dict(tm=tm, n_pt=n_pt, tps=tps, db=db)
    w_out = _cast_bf16(w_out, 512)

    new = {k: [] for k in ("wkv_p", "shift_p", "conv_p", "ffn_p", "wkv_s", "shift_s", "conv_s", "ffn_s")}
    v_first = None
    for l in range(depth):
        outs = _norm_lora(l, x, mod, norm_g4, mu_x, state_shift, decay_lora1, iclr_lora1, gate_lora1,
                          vres_lora1, bp=bp, **tiles)
        if l > 0:
            h, aw, aa, ag, av, hp_last, hs_last = outs
        else:
            h, aw, aa, ag, hp_last, hs_last = outs
            av = None
        mouts = _mix(l, h, (aw, aa, ag, av), v_first, w_in,
                     (decay_lora2, iclr_lora2, gate_lora2, vres_lora2), (w0_3, a0_3, v0_3),
                     mu_rkv, kk_3, ka_3, conv_w, state_shift, state_conv_t, tn=mix_tn, bp=bp, **tiles)
        if l > 0:
            r, lw, k, v, kk, kka, gate, o_conv, conv_p8, conv_s = mouts
        else:
            r, lw, k, v, kk, kka, gate, o_conv, v_first, conv_p8, conv_s = mouts
        seqs = (r, lw, k, v, kk, kka, gate)

        cpt = t_len // chunk
        o_p, wkv_p = _wkv(l, seqs, chunk, 1, bp, cpt, None, rk_3, lnw_3, lnb_3,
                          row_block_of=lambda bi, ci: bi * cpt + ci)

        def to_decode_seq(a, fill):
            a = a[n_prompt_rows:].reshape(ts, db, g_dim).transpose(1, 0, 2)
            padding = jnp.full((db, dec_chunk - ts, g_dim), fill, a.dtype)
            return jnp.concatenate([a, padding], axis=1).reshape(db * dec_chunk, g_dim)

        dseqs = tuple(to_decode_seq(a, 0.0) for a in seqs)
        o_s, wkv_s = _wkv(l, dseqs, dec_chunk, wkv_nb, db, 1, state_wkv[l], rk_3, lnw_3, lnb_3,
                          row_block_of=lambda bi, ci: bi)
        o_s = o_s.reshape(db, dec_chunk, g_dim)[:, :ts].transpose(1, 0, 2).reshape(tm, g_dim)

        x1, h2 = _outproj(l, o_p, o_s, o_conv, x, mod, norm_g4, w_out, **tiles)
        final_g = final_norm_g.reshape(1, d) if l == depth - 1 else None
        x, fpa, fpb, fsa, fsb = _ffn(l, h2, x1, mod, ffn_up, ffn_conv, ffn_down, state_ffn_t, final_g,
                                     tn=ffn_tn, bp=bp, **tiles)

        new["wkv_p"].append(wkv_p)
        new["wkv_s"].append(wkv_s)
        new["shift_p"].append(hp_last[:, _SUBLANES - 1])
        new["shift_s"].append(hs_last)
        new["conv_p"].append(conv_p8[:, _SUBLANES - 2:])
        new["conv_s"].append(conv_s.transpose(1, 0, 2))
        last_tiles = slice(tps - 1, n_pt, tps)
        new["ffn_p"].append(jnp.concatenate([fpa[last_tiles, _SUBLANES - 2:], fpb[last_tiles, _SUBLANES - 2:]],
                                            axis=-1))
        new["ffn_s"].append(jnp.concatenate([fsa, fsb], axis=-1).transpose(1, 0, 2))

    y_prompt = x[:n_prompt_rows].reshape(bp, t_len, d)
    y_sample = x[n_prompt_rows:].reshape(ts, db, d).transpose(1, 0, 2)
    st = {k: jnp.stack(vs) for k, vs in new.items()}
    return (y_prompt, y_sample, st["wkv_p"], st["shift_p"], st["conv_p"], st["ffn_p"],
            st["wkv_s"], st["shift_s"], st["conv_s"], st["ffn_s"])


def kernel(x_prompt, x_sample, c_prompt, c_sample, state_wkv, state_shift, state_conv, state_ffn, ada_w, ada_b, norm_g, final_norm_g, w_in, mu_x, mu_rkv, decay_w0, decay_lora1, decay_lora2, iclr_a0, iclr_lora1, iclr_lora2, gate_lora1, gate_lora2, vres_v0, vres_lora1, vres_lora2, k_k, k_a, r_k, ln_x_w, ln_x_b, conv_w, w_out, ffn_up, ffn_conv, ffn_down):
    return _forward(x_prompt, x_sample, c_prompt, c_sample, state_wkv, state_shift, state_conv, state_ffn,
                    ada_w, ada_b, norm_g, final_norm_g, w_in, mu_x, mu_rkv, decay_w0, decay_lora1, decay_lora2,
                    iclr_a0, iclr_lora1, iclr_lora2, gate_lora1, gate_lora2, vres_v0, vres_lora1, vres_lora2,
                    k_k, k_a, r_k, ln_x_w, ln_x_b, conv_w, w_out, ffn_up, ffn_conv, ffn_down)
```

```python
import functools

import jax
import jax.numpy as jnp
from jax import lax
from jax.experimental import pallas as pl
from jax.experimental.pallas import tpu as pltpu

_F32 = jnp.float32
_BF16 = jnp.bfloat16

_HEAD = 64
_PAIR = 2 * _HEAD
_SEG = 256
_RMS_EPS = 1e-6
_GN_EPS = 64e-5
_VMEM_LIMIT_BYTES = 56 * 1024 * 1024
_SUBLANES = 8


def _params(*sem):
    return pltpu.CompilerParams(dimension_semantics=sem, vmem_limit_bytes=_VMEM_LIMIT_BYTES)


def _dot(a, b):
    return jnp.dot(a.astype(_BF16), b.astype(_BF16), preferred_element_type=_F32)


def _dot_nt(a, b):
    return lax.dot_general(a.astype(_BF16), b.astype(_BF16), (((1,), (1,)), ((), ())),
                           preferred_element_type=_F32)


def _dot_tn(a, b):
    return lax.dot_general(a.astype(_BF16), b.astype(_BF16), (((0,), (0,)), ((), ())),
                           preferred_element_type=_F32)


def _split2(x):
    hi = x.astype(_BF16)
    lo = (x - hi.astype(_F32)).astype(_BF16)
    return hi, lo


def _split3(x):
    hi = x.astype(_BF16)
    r1 = x - hi.astype(_F32)
    mid = r1.astype(_BF16)
    lo = (r1 - mid.astype(_F32)).astype(_BF16)
    return hi, mid, lo


def _dot_exact_rhs(x, m_bf16):
    hi, lo = _split2(x)
    return (jnp.dot(hi, m_bf16, preferred_element_type=_F32)
            + jnp.dot(lo, m_bf16, preferred_element_type=_F32))


def _segment_ones(n):
    r = lax.broadcasted_iota(jnp.int32, (n, n), 0) // _HEAD
    c = lax.broadcasted_iota(jnp.int32, (n, n), 1) // _HEAD
    return (r == c).astype(_BF16)


def _head_sums(x, seg_ones):
    n = seg_ones.shape[0]
    parts = [_dot_exact_rhs(x[:, q * n:(q + 1) * n], seg_ones) for q in range(x.shape[1] // n)]
    return parts[0] if len(parts) == 1 else jnp.concatenate(parts, axis=1)


def _tile_rows(m, reps):
    return m if reps == 1 else jnp.concatenate([m] * reps, axis=0)


def _shift_rows_prompt(x, carry8, first, rows):
    l1 = jnp.where(first, 0.0, carry8[_SUBLANES - 1:_SUBLANES, :])
    l2 = jnp.where(first, 0.0, carry8[_SUBLANES - 2:_SUBLANES - 1, :])
    x1 = jnp.where(rows == 0, l1, pltpu.roll(x, 1, 0))
    x2 = jnp.where(rows == 0, l2, jnp.where(rows == 1, l1, pltpu.roll(x, 2, 0)))
    return x1, x2


def _mod_row(ref, i, tps, db):
    return ref[0, 0, pl.ds(db + i // tps, 1), :]


def _mod_tile(ref, db, ts):
    return _tile_rows(ref[0, 0, 0:db, :], ts)


def _mod_kernel(c_ref, w_ref, b_ref, o_ref):
    c = c_ref[...]
    s_hi, s_lo = _split2(c * jax.nn.sigmoid(c))
    w_hi, w_lo = _split2(w_ref[0])
    acc = (jnp.dot(s_hi, w_hi, preferred_element_type=_F32)
           + jnp.dot(s_hi, w_lo, preferred_element_type=_F32)
           + jnp.dot(s_lo, w_hi, preferred_element_type=_F32))
    o_ref[0, 0] = acc + b_ref[0]


def _adaln_mod(c_all, ada_w, ada_b):
    depth, d, six_d = ada_w.shape
    bc = c_all.shape[0]
    tn = 1024
    per = d // tn
    return pl.pallas_call(
        _mod_kernel,
        grid=(depth, six_d // tn),
        in_specs=[pl.BlockSpec((bc, d), lambda l, n: (0, 0)),
                  pl.BlockSpec((1, d, tn), lambda l, n: (l, 0, n)),
                  pl.BlockSpec((1, 1, tn), lambda l, n: (l, 0, n))],
        out_specs=pl.BlockSpec((1, 1, bc, tn), lambda l, n: (l, n // per, 0, n % per)),
        out_shape=jax.ShapeDtypeStruct((depth, 6, bc, d), _F32),
        compiler_params=_params("arbitrary", "arbitrary"),
    )(c_all, ada_w, ada_b.reshape(depth, 1, six_d))


def _norm_lora_kernel(*refs, n_pt, tps, db, has_v):
    if has_v:
        (x_ref, sh_ref, sc_ref, g_ref, mu_ref, hl_ref, d1_ref, i1_ref, g1_ref, v1_ref,
         h_ref, aw_ref, aa_ref, ag_ref, av_ref, hp_ref, hs_ref, carry_ref, hcur_ref, hprev_ref) = refs
    else:
        (x_ref, sh_ref, sc_ref, g_ref, mu_ref, hl_ref, d1_ref, i1_ref, g1_ref,
         h_ref, aw_ref, aa_ref, ag_ref, hp_ref, hs_ref, carry_ref, hcur_ref, hprev_ref) = refs
    i = pl.program_id(0)
    tm = x_ref.shape[0]
    ts = tm // db
    x = x_ref[...]
    xn = x * lax.rsqrt(jnp.mean(x * x, axis=-1, keepdims=True) + _RMS_EPS) * g_ref[0, 0]

    @pl.when(i < n_pt)
    def _prompt():
        h = xn * (1.0 + _mod_row(sc_ref, i, tps, db)) + _mod_row(sh_ref, i, tps, db)
        first = (i % tps) == 0
        rows = lax.broadcasted_iota(jnp.int32, (tm, 1), 0)
        prev_row = jnp.where(first, 0.0, carry_ref[_SUBLANES - 1:_SUBLANES, :])
        hprev_ref[...] = jnp.where(rows == 0, prev_row, pltpu.roll(h, 1, 0))
        hcur_ref[...] = h
        carry_ref[...] = h[tm - _SUBLANES:, :]
        hp_ref[0] = h[tm - _SUBLANES:, :]

    @pl.when(i >= n_pt)
    def _decode():
        h = xn * (1.0 + _mod_tile(sc_ref, db, ts)) + _mod_tile(sh_ref, db, ts)
        hprev_ref[...] = jnp.concatenate([hl_ref[0], h[:tm - db, :]], axis=0)
        hcur_ref[...] = h
        hs_ref[...] = h[tm - db:, :]

    h = hcur_ref[...]
    xx = hprev_ref[...] - h
    mu = mu_ref[0]
    h_ref[...] = h.astype(_BF16)
    aw_ref[...] = jnp.tanh(_dot(h + xx * mu[0:1], d1_ref[0]))
    aa_ref[...] = _dot(h + xx * mu[1:2], i1_ref[0])
    ag_ref[...] = jax.nn.sigmoid(_dot(h + xx * mu[2:3], g1_ref[0]))
    if has_v:
        av_ref[...] = _dot(h + xx * mu[3:4], v1_ref[0])


def _norm_lora(l, x, mod, norm_g4, mu_x, state_shift, decay_lora1, iclr_lora1, gate_lora1, vres_lora1,
               *, tm, n_pt, tps, db, bp):
    rows, d = x.shape
    has_v = l > 0
    bc = mod.shape[2]
    ld, la, lg = decay_lora1.shape[2], iclr_lora1.shape[2], gate_lora1.shape[2]
    full = lambda *shape: pl.BlockSpec(shape, lambda i: (l,) + (0,) * (len(shape) - 1))
    in_specs = [pl.BlockSpec((tm, d), lambda i: (i, 0)),
                pl.BlockSpec((1, 1, bc, d), lambda i: (l, 0, 0, 0)),
                pl.BlockSpec((1, 1, bc, d), lambda i: (l, 1, 0, 0)),
                pl.BlockSpec((1, 1, 1, d), lambda i: (l, 0, 0, 0)),
                full(1, 4, d), full(1, db, d), full(1, d, ld), full(1, d, la), full(1, d, lg)]
    args = [x, mod, mod, norm_g4, mu_x, state_shift, decay_lora1, iclr_lora1, gate_lora1]
    row_out = lambda n, dt: (pl.BlockSpec((tm, n), lambda i: (i, 0)), jax.ShapeDtypeStruct((rows, n), dt))
    outs = [row_out(d, _BF16), row_out(ld, _F32), row_out(la, _F32), row_out(lg, _F32)]
    if has_v:
        lv = vres_lora1.shape[2]
        in_specs.append(pl.BlockSpec((1, d, lv), lambda i: (l - 1, 0, 0)))
        args.append(vres_lora1)
        outs.append(row_out(lv, _F32))
    outs.append((pl.BlockSpec((1, _SUBLANES, d), lambda i: (jnp.minimum(i // tps, bp - 1), 0, 0)),
                 jax.ShapeDtypeStruct((bp, _SUBLANES, d), _F32)))
    outs.append((pl.BlockSpec((db, d), lambda i: (0, 0)), jax.ShapeDtypeStruct((db, d), _F32)))
    return pl.pallas_call(
        functools.partial(_norm_lora_kernel, n_pt=n_pt, tps=tps, db=db, has_v=has_v),
        grid=(rows // tm,),
        in_specs=in_specs,
        out_specs=[o[0] for o in outs],
        out_shape=[o[1] for o in outs],
        scratch_shapes=[pltpu.VMEM((_SUBLANES, d), _F32), pltpu.VMEM((tm, d), _F32), pltpu.VMEM((tm, d), _F32)],
        compiler_params=_params("arbitrary"),
    )(*args)


def _mix_kernel(*refs, n_pt, tps, db, has_v):
    it = iter(refs)
    h_ref = next(it)
    w_refs = [next(it) for _ in range(6)]
    aw_ref, aa_ref, ag_ref = next(it), next(it), next(it)
    av_ref = next(it) if has_v else None
    d2_ref, i2_ref, g2_ref = next(it), next(it), next(it)
    v2_ref = next(it) if has_v else None
    w0_ref, a0_ref = next(it), next(it)
    v0_ref = next(it) if has_v else None
    mu_ref, kkw_ref, kaw_ref, cw_ref, hl_ref, cs_ref = (next(it) for _ in range(6))
    vf_ref = next(it) if has_v else None
    r_o, lw_o, k_o, v_o, kk_o, kka_o, g_o, oc_o = (next(it) for _ in range(8))
    vf_o = None if has_v else next(it)
    cp_o, cso_o = next(it), next(it)
    wc_scr, pcarry, zcarry, pprev_scr, z1_scr, z2_scr = (next(it) for _ in range(6))

    i = pl.program_id(1)
    tm = h_ref.shape[0]
    tn = r_o.shape[1]

    @pl.when(i == 0)
    def _cache_weights():
        for g in range(6):
            wc_scr[g] = w_refs[g][0].astype(_BF16)

    h = h_ref[...]
    p = [jnp.dot(h, wc_scr[g], preferred_element_type=_F32) for g in range(3)]
    bg = jnp.dot(h, wc_scr[3], preferred_element_type=_F32)
    z = (jnp.dot(h, wc_scr[4], preferred_element_type=_F32)
         * jnp.dot(h, wc_scr[5], preferred_element_type=_F32))

    @pl.when(i < n_pt)
    def _prompt():
        first = (i % tps) == 0
        rows = lax.broadcasted_iota(jnp.int32, (tm, 1), 0)
        for g in range(3):
            prev_row = jnp.where(first, 0.0, pcarry[g, _SUBLANES - 1:_SUBLANES, :])
            pprev_scr[g] = jnp.where(rows == 0, prev_row, pltpu.roll(p[g], 1, 0))
            pcarry[g] = p[g][tm - _SUBLANES:, :]
        z1, z2 = _shift_rows_prompt(z, zcarry[...], first, rows)
        z1_scr[...] = z1
        z2_scr[...] = z2
        zcarry[...] = z[tm - _SUBLANES:, :]
        cp_o[0] = z[tm - _SUBLANES:, :]

    @pl.when(i >= n_pt)
    def _decode():
        hl = hl_ref[0].astype(_BF16)
        for g in range(3):
            hlp = jnp.dot(hl, wc_scr[g], preferred_element_type=_F32)
            pprev_scr[g] = jnp.concatenate([hlp, p[g][:tm - db, :]], axis=0)
        z1_scr[...] = jnp.concatenate([cs_ref[0, 1], z[:tm - db, :]], axis=0)
        z2_scr[...] = jnp.concatenate([cs_ref[0, 0], cs_ref[0, 1], z[:tm - 2 * db, :]], axis=0)
        cso_o[0] = z[tm - 2 * db:tm - db, :]
        cso_o[1] = z[tm - db:, :]

    mu = mu_ref[0]
    r = p[0] + (pprev_scr[0] - p[0]) * mu[0:1]
    k = p[1] + (pprev_scr[1] - p[1]) * mu[1:2]
    v = p[2] + (pprev_scr[2] - p[2]) * mu[2:3]

    zlog = w0_ref[0] + _dot(aw_ref[...], d2_ref[0])
    softplus = jnp.maximum(-zlog, 0.0) + jnp.log(1.0 + jnp.exp(-jnp.abs(zlog)))
    lw_o[...] = -jnp.exp(-softplus - 0.5)
    a = jax.nn.sigmoid(a0_ref[0] + _dot(aa_ref[...], i2_ref[0]))
    g_o[...] = _dot(ag_ref[...], g2_ref[0])
    if has_v:
        nu = jax.nn.sigmoid(v0_ref[0] + _dot(av_ref[...], v2_ref[0]))
        v = v + (vf_ref[...] - v) * nu
    else:
        vf_o[...] = v

    kk = k * kkw_ref[0]
    norm = jnp.sqrt(_head_sums(kk * kk, _segment_ones(tn)))
    kk = kk / jnp.maximum(norm, 1e-12)
    r_o[...] = r
    k_o[...] = k * (1.0 + (a - 1.0) * kaw_ref[0])
    v_o[...] = v
    kk_o[...] = kk
    kka_o[...] = kk * a

    cw = cw_ref[0]
    zc = z2_scr[...] * cw[0:1] + z1_scr[...] * cw[1:2] + z * cw[2:3]
    oc_o[...] = (bg * zc).astype(_BF16)


def _mix(l, h, acts, vf_in, w_in, lora2, vecs, mu_rkv, k_k3, k_a3, conv_w, state_shift, state_conv_t,
         *, tm, tn, n_pt, tps, db, bp):
    rows, d = h.shape
    has_v = l > 0
    g_dim = mu_rkv.shape[2]
    nj = g_dim // tn
    aw, aa, ag, av = acts
    d2, i2, g2, v2 = lora2
    w0, a0, v0 = vecs
    row_in = lambda arr: pl.BlockSpec((tm, arr.shape[1]), lambda j, i: (i, 0))
    col3 = lambda arr, ll: pl.BlockSpec((1, arr.shape[1], tn), lambda j, i: (ll, 0, j))
    in_specs = [row_in(h)] + [pl.BlockSpec((1, d, tn), lambda j, i, g=g: (l, 0, g * nj + j)) for g in range(6)]
    args = [h] + [w_in] * 6
    in_specs += [row_in(aw), row_in(aa), row_in(ag)]
    args += [aw, aa, ag]
    if has_v:
        in_specs.append(row_in(av))
        args.append(av)
    in_specs += [col3(d2, l), col3(i2, l), col3(g2, l)]
    args += [d2, i2, g2]
    if has_v:
        in_specs.append(col3(v2, l - 1))
        args.append(v2)
    in_specs += [col3(w0, l), col3(a0, l)]
    args += [w0, a0]
    if has_v:
        in_specs.append(col3(v0, l - 1))
        args.append(v0)
    in_specs += [col3(mu_rkv, l), col3(k_k3, l), col3(k_a3, l), col3(conv_w, l),
                 pl.BlockSpec((1, db, d), lambda j, i: (l, 0, 0)),
                 pl.BlockSpec((1, 2, db, tn), lambda j, i: (l, 0, 0, j))]
    args += [mu_rkv, k_k3, k_a3, conv_w, state_shift, state_conv_t]
    if has_v:
        in_specs.append(pl.BlockSpec((tm, tn), lambda j, i: (i, j)))
        args.append(vf_in)
    tile = pl.BlockSpec((tm, tn), lambda j, i: (i, j))
    outs = [(tile, jax.ShapeDtypeStruct((rows, g_dim), _F32)) for _ in range(7)]
    outs.append((tile, jax.ShapeDtypeStruct((rows, g_dim), _BF16)))
    if not has_v:
        outs.append((tile, jax.ShapeDtypeStruct((rows, g_dim), _F32)))
    outs.append((pl.BlockSpec((1, _SUBLANES, tn), lambda j, i: (jnp.minimum(i // tps, bp - 1), 0, j)),
                 jax.ShapeDtypeStruct((bp, _SUBLANES, g_dim), _F32)))
    outs.append((pl.BlockSpec((2, db, tn), lambda j, i: (0, 0, j)), jax.ShapeDtypeStruct((2, db, g_dim), _F32)))
    return pl.pallas_call(
        functools.partial(_mix_kernel, n_pt=n_pt, tps=tps, db=db, has_v=has_v),
        grid=(nj, rows // tm),
        in_specs=in_specs,
        out_specs=[o[0] for o in outs],
        out_shape=[o[1] for o in outs],
        scratch_shapes=[pltpu.VMEM((6, d, tn), _BF16), pltpu.VMEM((3, _SUBLANES, tn), _F32),
                        pltpu.VMEM((_SUBLANES, tn), _F32), pltpu.VMEM((3, tm, tn), _F32),
                        pltpu.VMEM((tm, tn), _F32), pltpu.VMEM((tm, tn), _F32)],
        compiler_params=_params("arbitrary", "arbitrary"),
    )(*args)


def _unit_lower_inverse_minus_identity(lmats, ri, ci, chunk):
    same8 = (ri >> 3) == (ci >> 3)
    d8 = [jnp.where(same8, m, 0.0) for m in lmats]
    d2 = [_dot(a, a) for a in d8]
    d3 = [_dot(a, b) for a, b in zip(d8, d2)]
    d4 = [_dot(b, b) for b in d2]
    x = [a + b + cc for a, b, cc in zip(d8, d2, d3)]
    xd4 = [_dot(a, b) for a, b in zip(x, d4)]
    x = [a + b + cc for a, b, cc in zip(x, d4, xd4)]
    size = 16
    while size <= chunk:
        sh = size.bit_length() - 1
        level = ((ri >> sh) == (ci >> sh)) & ((ri >> (sh - 1)) != (ci >> (sh - 1)))
        e = [jnp.where(level, m, 0.0) for m in lmats]
        y = [b + _dot(a, b) for a, b in zip(x, e)]
        x = [a + b + _dot(b, a) for a, b in zip(x, y)]
        size *= 2
    return x


def _wkv_kernel(*refs, chunk, nb, zero_init, n_chunks):
    (r_ref, lw_ref, k_ref, v_ref, kk_ref, kka_ref, g_ref, rk_ref, lnw_ref, lnb_ref) = refs[:10]
    if zero_init:
        o_ref, so_ref, s_scr = refs[10:]
        s0_ref = None
    else:
        s0_ref, o_ref, so_ref, s_scr = refs[10:]
    ci_grid = pl.program_id(1)
    g_dim = r_ref.shape[1]
    npair = g_dim // _PAIR
    c = chunk
    n2 = 2 * c

    lane = lax.broadcasted_iota(jnp.int32, (1, _PAIR), 1)
    m0 = (lane < _HEAD).astype(_F32)
    m1 = 1.0 - m0

    @pl.when(ci_grid == 0)
    def _init():
        if zero_init:
            s_scr[...] = jnp.zeros(s_scr.shape, _F32)
        else:
            zero = jnp.zeros((_HEAD, _HEAD), _F32)
            for u in range(nb):
                for p in range(npair):
                    top = jnp.concatenate([s0_ref[u, 2 * p], zero], axis=1)
                    bot = jnp.concatenate([zero, s0_ref[u, 2 * p + 1]], axis=1)
                    s_scr[u, p] = jnp.concatenate([top, bot], axis=0)

    ri = lax.broadcasted_iota(jnp.int32, (n2, n2), 0) & (c - 1)
    ci = lax.broadcasted_iota(jnp.int32, (n2, n2), 1) & (c - 1)
    strict = ri > ci
    incl = ri >= ci
    tri = (lax.broadcasted_iota(jnp.int32, (c, c), 0) >= lax.broadcasted_iota(jnp.int32, (c, c), 1)).astype(_BF16)
    seg_ones = _segment_ones(_SEG)
    rk = rk_ref[...]
    lnw = lnw_ref[...]
    lnb = lnb_ref[...]

    def stack(x):
        return jnp.concatenate([x * m0, x * m1], axis=0)

    def unit(u, carry):
        rows = slice(0, c) if nb == 1 else pl.ds(pl.multiple_of(u * c, c), c)
        lw = lw_ref[rows, :]
        l_hi, l_mid, l_lo = _split3(lw)
        cl = (jnp.dot(tri, l_hi, preferred_element_type=_F32)
              + jnp.dot(tri, l_mid, preferred_element_type=_F32)
              + jnp.dot(tri, l_lo, preferred_element_type=_F32))
        cl_end = cl[c - 1:c, :]
        r = r_ref[rows, :]
        k = k_ref[rows, :]
        v = v_ref[rows, :]
        kk = kk_ref[rows, :]
        kka = kka_ref[rows, :]
        e_neg = jnp.exp(-cl)
        e_tail = jnp.exp(cl_end - cl)
        p_end = jnp.exp(cl_end)
        rt = r * jnp.exp(cl)
        at = -(kk * jnp.exp(cl - lw))
        kt = k * e_neg
        bt = kka * e_neg
        kh = k * e_tail
        bh = kka * e_tail
        pairs = range(npair)
        sls = [slice(p * _PAIR, (p + 1) * _PAIR) for p in pairs]
        ar2 = [jnp.concatenate([stack(at[:, sl]), stack(rt[:, sl])], axis=0).astype(_BF16) for sl in sls]
        mb = [_dot_nt(a, stack(bt[:, sl])) for a, sl in zip(ar2, sls)]
        mk = [_dot_nt(a, stack(kt[:, sl])) for a, sl in zip(ar2, sls)]
        s_old = [s_scr[u, p] for p in pairs]
        uy0 = [_dot_nt(jnp.concatenate([at[:, sl], rt[:, sl]], axis=0), s)
               for sl, s in zip(sls, s_old)]
        x_inv = _unit_lower_inverse_minus_identity([jnp.where(strict, m[:n2], 0.0) for m in mb], ri, ci, c)
        vs = [stack(v[:, sl]).astype(_BF16) for sl in sls]
        ws = [stack(q[:c]) + _dot(jnp.where(strict, m[:n2], 0.0), vv) for q, m, vv in zip(uy0, mk, vs)]
        us = [w + _dot(xi, w) for xi, w in zip(x_inv, ws)]
        us = [q.astype(_BF16) for q in us]
        ysd = [_dot(jnp.where(incl, m1_[n2:], 0.0), uu) + _dot(jnp.where(incl, m2_[n2:], 0.0), vv)
               for m1_, m2_, uu, vv in zip(mb, mk, us, vs)]
        ys = [q[c:] + d[:c] + d[c:] for q, d in zip(uy0, ysd)]
        for p, sl in zip(pairs, sls):
            uv = jnp.concatenate([us[p], vs[p]], axis=0)
            bk = jnp.concatenate([stack(bh[:, sl]), stack(kh[:, sl])], axis=0)
            s_scr[u, p] = s_old[p] * p_end[:, sl] + _dot_tn(uv, bk)
        y = jnp.concatenate(ys, axis=1)
        mean = _head_sums(y, seg_ones) * (1.0 / _HEAD)
        dy = y - mean
        var = _head_sums(dy * dy, seg_ones) * (1.0 / _HEAD)
        yn = dy * lax.rsqrt(var + _GN_EPS) * lnw + lnb
        bonus = _head_sums(r * k * rk, seg_ones) * v
        o_ref[rows, :] = ((yn + bonus) * g_ref[rows, :]).astype(o_ref.dtype)
        return carry

    if nb == 1:
        unit(0, 0)
    else:
        lax.fori_loop(0, nb, unit, 0)

    @pl.when(ci_grid == n_chunks - 1)
    def _final():
        for u in range(nb):
            for p in range(npair):
                s = s_scr[u, p]
                so_ref[u, 2 * p] = s[:_HEAD, :_HEAD]
                so_ref[u, 2 * p + 1] = s[_HEAD:, _HEAD:]


def _wkv(l, seqs, chunk, nb, n_seq, n_chunks, state0, r_k3, ln_w3, ln_b3, *, row_block_of):
    g_dim = seqs[0].shape[1]
    n_heads = g_dim // _HEAD
    rows = seqs[0].shape[0]
    zero_init = state0 is None
    blk = pl.BlockSpec((nb * chunk, g_dim), lambda bi, ci: (row_block_of(bi, ci), 0))
    vec = pl.BlockSpec((None, 1, g_dim), lambda bi, ci: (l, 0, 0))
    st = pl.BlockSpec((nb, n_heads, _HEAD, _HEAD), lambda bi, ci: (bi, 0, 0, 0))
    in_specs = [blk] * 7 + [vec] * 3
    args = list(seqs) + [r_k3, ln_w3, ln_b3]
    if not zero_init:
        in_specs.append(st)
        args.append(state0)
    return pl.pallas_call(
        functools.partial(_wkv_kernel, chunk=chunk, nb=nb, zero_init=zero_init, n_chunks=n_chunks),
        grid=(n_seq // nb, n_chunks),
        in_specs=in_specs,
        out_specs=[blk, st],
        out_shape=[jax.ShapeDtypeStruct((rows, g_dim), _BF16),
                   jax.ShapeDtypeStruct((n_seq, n_heads, _HEAD, _HEAD), _F32)],
        scratch_shapes=[pltpu.VMEM((nb, g_dim // _PAIR, _PAIR, _PAIR), _F32)],
        compiler_params=_params("arbitrary", "arbitrary"),
    )(*args)


def _cast_kernel(w_ref, o_ref):
    o_ref[...] = w_ref[...].astype(o_ref.dtype)


def _cast_bf16(w, rows_per_block):
    depth, k, n = w.shape
    spec = pl.BlockSpec((1, rows_per_block, n), lambda l, i: (l, i, 0))
    return pl.pallas_call(
        _cast_kernel, grid=(depth, k // rows_per_block), in_specs=[spec], out_specs=spec,
        out_shape=jax.ShapeDtypeStruct(w.shape, _BF16),
        compiler_params=_params("arbitrary", "arbitrary"),
    )(w)


def _outproj_kernel(op_ref, os_ref, oc_ref, x_ref, ga_ref, sh_ref, sc_ref, g_ref, w_ref,
                    x1_ref, h2_ref, *, n_pt, tps, db):
    i = pl.program_id(0)
    tm = x_ref.shape[0]
    ts = tm // db
    g_dim = op_ref.shape[1]

    o_rw = jnp.where(i < n_pt, op_ref[...], os_ref[...])
    acc = (jnp.dot(o_rw, w_ref[0, :g_dim, :], preferred_element_type=_F32)
           + jnp.dot(oc_ref[...], w_ref[0, g_dim:, :], preferred_element_type=_F32))

    def finish(ga, sc, sh):
        x1 = x_ref[...] + ga * acc
        x1_ref[...] = x1
        xn = x1 * lax.rsqrt(jnp.mean(x1 * x1, axis=-1, keepdims=True) + _RMS_EPS) * g_ref[0, 0]
        h2_ref[...] = (xn * (1.0 + sc) + sh).astype(_BF16)

    @pl.when(i < n_pt)
    def _prompt():
        finish(_mod_row(ga_ref, i, tps, db), _mod_row(sc_ref, i, tps, db), _mod_row(sh_ref, i, tps, db))

    @pl.when(i >= n_pt)
    def _decode():
        finish(_mod_tile(ga_ref, db, ts), _mod_tile(sc_ref, db, ts), _mod_tile(sh_ref, db, ts))


def _outproj(l, o_p, o_s, o_conv, x, mod, norm_g4, w_out, *, tm, n_pt, tps, db):
    rows, d = x.shape
    g_dim = o_p.shape[1]
    bc = mod.shape[2]
    modspec = lambda comp: pl.BlockSpec((1, 1, bc, d), lambda i: (l, comp, 0, 0))
    return pl.pallas_call(
        functools.partial(_outproj_kernel, n_pt=n_pt, tps=tps, db=db),
        grid=(rows // tm,),
        in_specs=[pl.BlockSpec((tm, g_dim), lambda i: (jnp.minimum(i, n_pt - 1), 0)),
                  pl.BlockSpec((tm, g_dim), lambda i: (0, 0)),
                  pl.BlockSpec((tm, o_conv.shape[1]), lambda i: (i, 0)),
                  pl.BlockSpec((tm, d), lambda i: (i, 0)),
                  modspec(2), modspec(3), modspec(4),
                  pl.BlockSpec((1, 1, 1, d), lambda i: (l, 1, 0, 0)),
                  pl.BlockSpec((1, d, d), lambda i: (l, 0, 0), pipeline_mode=pl.Buffered(1))],
        out_specs=[pl.BlockSpec((tm, d), lambda i: (i, 0)), pl.BlockSpec((tm, d), lambda i: (i, 0))],
        out_shape=[jax.ShapeDtypeStruct((rows, d), _F32), jax.ShapeDtypeStruct((rows, d), _BF16)],
        compiler_params=_params("arbitrary"),
    )(o_p, o_s, o_conv, x, mod, mod, mod, norm_g4, w_out)


def _ffn_kernel(*refs, n_pt, tps, db, nj, final_norm):
    it = iter(refs)
    (h2_ref, wa_ref, wb_ref, cwa_ref, cwb_ref, wd_ref, x1_ref, ga_ref, sfa_ref, sfb_ref) = (next(it) for _ in range(10))
    fg_ref = next(it) if final_norm else None
    x2_ref, fpa_o, fpb_o, fsa_o, fsb_o = (next(it) for _ in range(5))
    acc_scr, ca_scr, cb_scr, s1_scr, s2_scr = (next(it) for _ in range(5))
    i = pl.program_id(0)
    j = pl.program_id(1)
    tm = h2_ref.shape[0]
    ts = tm // db
    h2 = h2_ref[...]
    ua = jnp.dot(h2, wa_ref[0].astype(_BF16), preferred_element_type=_F32)
    ub = jnp.dot(h2, wb_ref[0].astype(_BF16), preferred_element_type=_F32)

    @pl.when(j == 0)
    def _zero():
        acc_scr[...] = jnp.zeros(acc_scr.shape, _F32)

    @pl.when(i < n_pt)
    def _prompt():
        first = (i % tps) == 0
        rows = lax.broadcasted_iota(jnp.int32, (tm, 1), 0)
        for idx, (u, c_scr, f_o) in enumerate(((ua, ca_scr, fpa_o), (ub, cb_scr, fpb_o))):
            u1, u2 = _shift_rows_prompt(u, c_scr[j], first, rows)
            s1_scr[idx] = u1
            s2_scr[idx] = u2
            c_scr[j] = u[tm - _SUBLANES:, :]
            f_o[0] = u[tm - _SUBLANES:, :]

    @pl.when(i >= n_pt)
    def _decode():
        for idx, (u, sf, f_o, fp_o) in enumerate(((ua, sfa_ref, fsa_o, fpa_o), (ub, sfb_ref, fsb_o, fpb_o))):
            s1_scr[idx] = jnp.concatenate([sf[0, 1], u[:tm - db, :]], axis=0)
            s2_scr[idx] = jnp.concatenate([sf[0, 0], sf[0, 1], u[:tm - 2 * db, :]], axis=0)
            f_o[0] = u[tm - 2 * db:tm - db, :]
            f_o[1] = u[tm - db:, :]
            fp_o[0] = jnp.zeros(fp_o.shape[1:], _F32)

    cwa = cwa_ref[0]
    cwb = cwb_ref[0]
    ca = s2_scr[0] * cwa[0:1] + s1_scr[0] * cwa[1:2] + ua * cwa[2:3]
    cb = s2_scr[1] * cwb[0:1] + s1_scr[1] * cwb[1:2] + ub * cwb[2:3]
    gact = (ca * jax.nn.sigmoid(ca)) * cb
    acc_scr[...] += jnp.dot(gact.astype(_BF16), wd_ref[0].astype(_BF16), preferred_element_type=_F32)

    def finish(ga):
        x2 = x1_ref[...] + ga * acc_scr[...]
        if final_norm:
            x2 = x2 * lax.rsqrt(jnp.mean(x2 * x2, axis=-1, keepdims=True) + _RMS_EPS) * fg_ref[...]
        x2_ref[...] = x2

    @pl.when((j == nj - 1) & (i < n_pt))
    def _finish_prompt():
        finish(_mod_row(ga_ref, i, tps, db))

    @pl.when((j == nj - 1) & (i >= n_pt))
    def _finish_decode():
        finish(_mod_tile(ga_ref, db, ts))


def _ffn(l, h2, x1, mod, ffn_up, ffn_conv, ffn_down, state_ffn_t, final_g, *, tm, tn, n_pt, tps, db, bp):
    rows, d = x1.shape
    f = ffn_down.shape[1]
    nj = f // tn
    bc = mod.shape[2]
    final_norm = final_g is not None
    in_specs = [pl.BlockSpec((tm, d), lambda i, j: (i, 0)),
                pl.BlockSpec((1, d, tn), lambda i, j: (l, 0, j)),
                pl.BlockSpec((1, d, tn), lambda i, j: (l, 0, nj + j)),
                pl.BlockSpec((1, 3, tn), lambda i, j: (l, 0, j)),
                pl.BlockSpec((1, 3, tn), lambda i, j: (l, 0, nj + j)),
                pl.BlockSpec((1, tn, d), lambda i, j: (l, j, 0)),
                pl.BlockSpec((tm, d), lambda i, j: (i, 0)),
                pl.BlockSpec((1, 1, bc, d), lambda i, j: (l, 5, 0, 0)),
                pl.BlockSpec((1, 2, db, tn), lambda i, j: (l, 0, 0, j)),
                pl.BlockSpec((1, 2, db, tn), lambda i, j: (l, 0, 0, nj + j))]
    args = [h2, ffn_up, ffn_up, ffn_conv, ffn_conv, ffn_down, x1, mod, state_ffn_t, state_ffn_t]
    if final_norm:
        in_specs.append(pl.BlockSpec((1, d), lambda i, j: (0, 0)))
        args.append(final_g)
    n_tiles = rows // tm
    pstate = pl.BlockSpec((1, _SUBLANES, tn), lambda i, j: (i, 0, j))
    sstate = pl.BlockSpec((2, db, tn), lambda i, j: (0, 0, jnp.where(i >= n_pt, j, 0)))
    return pl.pallas_call(
        functools.partial(_ffn_kernel, n_pt=n_pt, tps=tps, db=db, nj=nj, final_norm=final_norm),
        grid=(rows // tm, nj),
        in_specs=in_specs,
        out_specs=[pl.BlockSpec((tm, d), lambda i, j: (i, 0)), pstate, pstate, sstate, sstate],
        out_shape=[jax.ShapeDtypeStruct((rows, d), _F32),
                   jax.ShapeDtypeStruct((n_tiles, _SUBLANES, f), _F32),
                   jax.ShapeDtypeStruct((n_tiles, _SUBLANES, f), _F32),
                   jax.ShapeDtypeStruct((2, db, f), _F32), jax.ShapeDtypeStruct((2, db, f), _F32)],
        scratch_shapes=[pltpu.VMEM((tm, d), _F32), pltpu.VMEM((nj, _SUBLANES, tn), _F32),
                        pltpu.VMEM((nj, _SUBLANES, tn), _F32), pltpu.VMEM((2, tm, tn), _F32),
                        pltpu.VMEM((2, tm, tn), _F32)],
        compiler_params=_params("arbitrary", "arbitrary"),
    )(*args)


def _forward(x_prompt, x_sample, c_prompt, c_sample, state_wkv, state_shift, state_conv, state_ffn,
             ada_w, ada_b, norm_g, final_norm_g, w_in, mu_x, mu_rkv, decay_w0, decay_lora1, decay_lora2,
             iclr_a0, iclr_lora1, iclr_lora2, gate_lora1, gate_lora2, vres_v0, vres_lora1, vres_lora2,
             k_k, k_a, r_k, ln_x_w, ln_x_b, conv_w, w_out, ffn_up, ffn_conv, ffn_down,
             *, chunk=64, mix_tn=256, ffn_tn=512, wkv_nb=8):
    bp, t_len, d = x_prompt.shape
    db, ts, _ = x_sample.shape
    depth = ada_w.shape[0]
    g_dim = mu_rkv.shape[2]
    n_heads = g_dim // _HEAD
    f = ffn_down.shape[1]
    tm = ts * db
    assert t_len % tm == 0 and t_len % chunk == 0 and ts >= 2 and db % _SUBLANES == 0 and tm % chunk == 0
    tps = t_len // tm
    n_pt = bp * tps
    n_prompt_rows = bp * t_len
    dec_chunk = 16
    assert ts <= dec_chunk and db % wkv_nb == 0

    x = jnp.concatenate([x_prompt.reshape(n_prompt_rows, d),
                         x_sample.transpose(1, 0, 2).reshape(tm, d)], axis=0)
    pad = (-(db + bp)) % _SUBLANES
    c_all = jnp.concatenate([c_sample, c_prompt, jnp.zeros((pad, d), _F32)], axis=0)
    mod = _adaln_mod(c_all, ada_w, ada_b)

    norm_g4 = norm_g.reshape(depth, 2, 1, d)
    vec3 = lambda a: a.reshape(a.shape[0], 1, a.shape[1])
    w0_3, a0_3, v0_3, kk_3, ka_3 = vec3(decay_w0), vec3(iclr_a0), vec3(vres_v0), vec3(k_k), vec3(k_a)
    rk_3, lnw_3, lnb_3 = r_k.reshape(depth, 1, g_dim), vec3(ln_x_w), vec3(ln_x_b)
    state_conv_t = state_conv.transpose(0, 2, 1, 3)
    state_ffn_t = state_ffn.transpose(0, 2, 1, 3)
    tiles = dict(tm=tm, n_pt=n_pt, tps=tps, db=db)
    w_out = _cast_bf16(w_out, 512)
    ffn_up = _cast_bf16(ffn_up, 128)
    ffn_down = _cast_bf16(ffn_down, 512)

    new = {k: [] for k in ("wkv_p", "shift_p", "conv_p", "ffn_p", "wkv_s", "shift_s", "conv_s", "ffn_s")}
    v_first = None
    for l in range(depth):
        outs = _norm_lora(l, x, mod, norm_g4, mu_x, state_shift, decay_lora1, iclr_lora1, gate_lora1,
                          vres_lora1, bp=bp, **tiles)
        if l > 0:
            h, aw, aa, ag, av, hp_last, hs_last = outs
        else:
            h, aw, aa, ag, hp_last, hs_last = outs
            av = None
        mouts = _mix(l, h, (aw, aa, ag, av), v_first, w_in,
                     (decay_lora2, iclr_lora2, gate_lora2, vres_lora2), (w0_3, a0_3, v0_3),
                     mu_rkv, kk_3, ka_3, conv_w, state_shift, state_conv_t, tn=mix_tn, bp=bp, **tiles)
        if l > 0:
            r, lw, k, v, kk, kka, gate, o_conv, conv_p8, conv_s = mouts
        else:
            r, lw, k, v, kk, kka, gate, o_conv, v_first, conv_p8, conv_s = mouts
        seqs = (r, lw, k, v, kk, kka, gate)

        cpt = t_len // chunk
        o_p, wkv_p = _wkv(l, seqs, chunk, 1, bp, cpt, None, rk_3, lnw_3, lnb_3,
                          row_block_of=lambda bi, ci: bi * cpt + ci)

        def to_decode_seq(a, fill):
            a = a[n_prompt_rows:].reshape(ts, db, g_dim).transpose(1, 0, 2)
            padding = jnp.full((db, dec_chunk - ts, g_dim), fill, a.dtype)
            return jnp.concatenate([a, padding], axis=1).reshape(db * dec_chunk, g_dim)

        dseqs = tuple(to_decode_seq(a, 0.0) for a in seqs)
        o_s, wkv_s = _wkv(l, dseqs, dec_chunk, wkv_nb, db, 1, state_wkv[l], rk_3, lnw_3, lnb_3,
                          row_block_of=lambda bi, ci: bi)
        o_s = o_s.reshape(db, dec_chunk, g_dim)[:, :ts].transpose(1, 0, 2).reshape(tm, g_dim)

        x1, h2 = _outproj(l, o_p, o_s, o_conv, x, mod, norm_g4, w_out, **tiles)
        final_g = final_norm_g.reshape(1, d) if l == depth - 1 else None
        x, fpa, fpb, fsa, fsb = _ffn(l, h2, x1, mod, ffn_up, ffn_conv, ffn_down, state_ffn_t, final_g,
                                     tn=ffn_tn, bp=bp, **tiles)

        new["wkv_p"].append(wkv_p)
        new["wkv_s"].append(wkv_s)
        new["shift_p"].append(hp_last[:, _SUBLANES - 1])
        new["shift_s"].append(hs_last)
        new["conv_p"].append(conv_p8[:, _SUBLANES - 2:])
        new["conv_s"].append(conv_s.transpose(1, 0, 2))
        last_tiles = slice(tps - 1, n_pt, tps)
        new["ffn_p"].append(jnp.concatenate([fpa[last_tiles, _SUBLANES - 2:], fpb[last_tiles, _SUBLANES - 2:]],
                                            axis=-1))
        new["ffn_s"].append(jnp.concatenate([fsa, fsb], axis=-1).transpose(1, 0, 2))

    y_prompt = x[:n_prompt_rows].reshape(bp, t_len, d)
    y_sample = x[n_prompt_rows:].reshape(ts, db, d).transpose(1, 0, 2)
    st = {k: jnp.stack(vs) for k, vs in new.items()}
    return (y_prompt, y_sample, st["wkv_p"], st["shift_p"], st["conv_p"], st["ffn_p"],
            st["wkv_s"], st["shift_s"], st["conv_s"], st["ffn_s"])


def kernel(x_prompt, x_sample, c_prompt, c_sample, state_wkv, state_shift, state_conv, state_ffn, ada_w, ada_b, norm_g, final_norm_g, w_in, mu_x, mu_rkv, decay_w0, decay_lora1, decay_lora2, iclr_a0, iclr_lora1, iclr_lora2, gate_lora1, gate_lora2, vres_v0, vres_lora1, vres_lora2, k_k, k_a, r_k, ln_x_w, ln_x_b, conv_w, w_out, ffn_up, ffn_conv, ffn_down):
    return _forward(x_prompt, x_sample, c_prompt, c_sample, state_wkv, state_shift, state_conv, state_ffn,
                    ada_w, ada_b, norm_g, final_norm_g, w_in, mu_x, mu_rkv, decay_w0, decay_lora1, decay_lora2,
                    iclr_a0, iclr_lora1, iclr_lora2, gate_lora1, gate_lora2, vres_v0, vres_lora1, vres_lora2,
                    k_k, k_a, r_k, ln_x_w, ln_x_b, conv_w, w_out, ffn_up, ffn_conv, ffn_down)
```

```python
import functools

import jax
import jax.numpy as jnp
from jax import lax
from jax.experimental import pallas as pl
from jax.experimental.pallas import tpu as pltpu

_F32 = jnp.float32
_BF16 = jnp.bfloat16

_HEAD = 64
_PAIR = 2 * _HEAD
_SEG = 256
_RMS_EPS = 1e-6
_GN_EPS = 64e-5
_VMEM_LIMIT_BYTES = 56 * 1024 * 1024
_SUBLANES = 8


def _params(*sem):
    return pltpu.CompilerParams(dimension_semantics=sem, vmem_limit_bytes=_VMEM_LIMIT_BYTES)


def _dot(a, b):
    return jnp.dot(a.astype(_BF16), b.astype(_BF16), preferred_element_type=_F32)


def _dot_nt(a, b):
    return lax.dot_general(a.astype(_BF16), b.astype(_BF16), (((1,), (1,)), ((), ())),
                           preferred_element_type=_F32)


def _dot_tn(a, b):
    return lax.dot_general(a.astype(_BF16), b.astype(_BF16), (((0,), (0,)), ((), ())),
                           preferred_element_type=_F32)


def _split2(x):
    hi = x.astype(_BF16)
    lo = (x - hi.astype(_F32)).astype(_BF16)
    return hi, lo


def _split3(x):
    hi = x.astype(_BF16)
    r1 = x - hi.astype(_F32)
    mid = r1.astype(_BF16)
    lo = (r1 - mid.astype(_F32)).astype(_BF16)
    return hi, mid, lo


def _dot_exact_rhs(x, m_bf16):
    hi, lo = _split2(x)
    return (jnp.dot(hi, m_bf16, preferred_element_type=_F32)
            + jnp.dot(lo, m_bf16, preferred_element_type=_F32))


def _segment_ones(n):
    r = lax.broadcasted_iota(jnp.int32, (n, n), 0) // _HEAD
    c = lax.broadcasted_iota(jnp.int32, (n, n), 1) // _HEAD
    return (r == c).astype(_BF16)


def _head_sums(x, seg_ones):
    n = seg_ones.shape[0]
    parts = [_dot_exact_rhs(x[:, q * n:(q + 1) * n], seg_ones) for q in range(x.shape[1] // n)]
    return parts[0] if len(parts) == 1 else jnp.concatenate(parts, axis=1)


def _tile_rows(m, reps):
    return m if reps == 1 else jnp.concatenate([m] * reps, axis=0)


def _shift_rows_prompt(x, carry8, first, two=True):
    r8 = lax.broadcasted_iota(jnp.int32, (_SUBLANES, 1), 0)
    l1 = jnp.where(first, 0.0, carry8[_SUBLANES - 1:_SUBLANES, :])
    x1 = pltpu.roll(x, 1, 0)
    x1 = jnp.concatenate([jnp.where(r8 == 0, l1, x1[:_SUBLANES, :]), x1[_SUBLANES:, :]], axis=0)
    if not two:
        return x1
    l2 = jnp.where(first, 0.0, carry8[_SUBLANES - 2:_SUBLANES - 1, :])
    x2 = pltpu.roll(x, 2, 0)
    top2 = jnp.where(r8 == 0, l2, jnp.where(r8 == 1, l1, x2[:_SUBLANES, :]))
    return x1, jnp.concatenate([top2, x2[_SUBLANES:, :]], axis=0)


def _mod_row(ref, i, tps, db):
    return ref[0, 0, pl.ds(db + i // tps, 1), :]


def _mod_tile(ref, db, ts):
    return _tile_rows(ref[0, 0, 0:db, :], ts)


def _mod_kernel(c_ref, w_ref, b_ref, o_ref):
    c = c_ref[...]
    s_hi, s_lo = _split2(c * jax.nn.sigmoid(c))
    w_hi, w_lo = _split2(w_ref[0])
    acc = (jnp.dot(s_hi, w_hi, preferred_element_type=_F32)
           + jnp.dot(s_hi, w_lo, preferred_element_type=_F32)
           + jnp.dot(s_lo, w_hi, preferred_element_type=_F32))
    o_ref[0, 0] = acc + b_ref[0]


def _adaln_mod(c_all, ada_w, ada_b):
    depth, d, six_d = ada_w.shape
    bc = c_all.shape[0]
    tn = 1024
    per = d // tn
    return pl.pallas_call(
        _mod_kernel,
        grid=(depth, six_d // tn),
        in_specs=[pl.BlockSpec((bc, d), lambda l, n: (0, 0)),
                  pl.BlockSpec((1, d, tn), lambda l, n: (l, 0, n)),
                  pl.BlockSpec((1, 1, tn), lambda l, n: (l, 0, n))],
        out_specs=pl.BlockSpec((1, 1, bc, tn), lambda l, n: (l, n // per, 0, n % per)),
        out_shape=jax.ShapeDtypeStruct((depth, 6, bc, d), _F32),
        compiler_params=_params("arbitrary", "arbitrary"),
    )(c_all, ada_w, ada_b.reshape(depth, 1, six_d))


def _norm_lora_kernel(*refs, n_pt, tps, db, has_v):
    if has_v:
        (x_ref, sh_ref, sc_ref, g_ref, mu_ref, hl_ref, d1_ref, i1_ref, g1_ref, v1_ref,
         h_ref, aw_ref, aa_ref, ag_ref, av_ref, hp_ref, hs_ref, carry_ref, hcur_ref, hprev_ref) = refs
    else:
        (x_ref, sh_ref, sc_ref, g_ref, mu_ref, hl_ref, d1_ref, i1_ref, g1_ref,
         h_ref, aw_ref, aa_ref, ag_ref, hp_ref, hs_ref, carry_ref, hcur_ref, hprev_ref) = refs
    i = pl.program_id(0)
    tm = x_ref.shape[0]
    ts = tm // db
    x = x_ref[...]
    xn = x * lax.rsqrt(jnp.mean(x * x, axis=-1, keepdims=True) + _RMS_EPS) * g_ref[0, 0]

    @pl.when(i < n_pt)
    def _prompt():
        h = xn * (1.0 + _mod_row(sc_ref, i, tps, db)) + _mod_row(sh_ref, i, tps, db)
        first = (i % tps) == 0
        hprev_ref[...] = _shift_rows_prompt(h, carry_ref[...], first, two=False)
        hcur_ref[...] = h
        carry_ref[...] = h[tm - _SUBLANES:, :]
        hp_ref[0] = h[tm - _SUBLANES:, :]

    @pl.when(i >= n_pt)
    def _decode():
        h = xn * (1.0 + _mod_tile(sc_ref, db, ts)) + _mod_tile(sh_ref, db, ts)
        hprev_ref[...] = jnp.concatenate([hl_ref[0], h[:tm - db, :]], axis=0)
        hcur_ref[...] = h
        hs_ref[...] = h[tm - db:, :]

    h = hcur_ref[...]
    xx = hprev_ref[...] - h
    mu = mu_ref[0]
    h_ref[...] = h.astype(_BF16)
    aw_ref[...] = jnp.tanh(_dot(h + xx * mu[0:1], d1_ref[0]))
    aa_ref[...] = _dot(h + xx * mu[1:2], i1_ref[0])
    ag_ref[...] = jax.nn.sigmoid(_dot(h + xx * mu[2:3], g1_ref[0]))
    if has_v:
        av_ref[...] = _dot(h + xx * mu[3:4], v1_ref[0])


def _norm_lora(l, x, mod, norm_g4, mu_x, state_shift, decay_lora1, iclr_lora1, gate_lora1, vres_lora1,
               *, tm, n_pt, tps, db, bp):
    rows, d = x.shape
    has_v = l > 0
    bc = mod.shape[2]
    ld, la, lg = decay_lora1.shape[2], iclr_lora1.shape[2], gate_lora1.shape[2]
    full = lambda *shape: pl.BlockSpec(shape, lambda i: (l,) + (0,) * (len(shape) - 1))
    in_specs = [pl.BlockSpec((tm, d), lambda i: (i, 0)),
                pl.BlockSpec((1, 1, bc, d), lambda i: (l, 0, 0, 0)),
                pl.BlockSpec((1, 1, bc, d), lambda i: (l, 1, 0, 0)),
                pl.BlockSpec((1, 1, 1, d), lambda i: (l, 0, 0, 0)),
                full(1, 4, d), full(1, db, d), full(1, d, ld), full(1, d, la), full(1, d, lg)]
    args = [x, mod, mod, norm_g4, mu_x, state_shift, decay_lora1, iclr_lora1, gate_lora1]
    row_out = lambda n, dt: (pl.BlockSpec((tm, n), lambda i: (i, 0)), jax.ShapeDtypeStruct((rows, n), dt))
    outs = [row_out(d, _BF16), row_out(ld, _F32), row_out(la, _F32), row_out(lg, _F32)]
    if has_v:
        lv = vres_lora1.shape[2]
        in_specs.append(pl.BlockSpec((1, d, lv), lambda i: (l - 1, 0, 0)))
        args.append(vres_lora1)
        outs.append(row_out(lv, _F32))
    outs.append((pl.BlockSpec((1, _SUBLANES, d), lambda i: (jnp.minimum(i // tps, bp - 1), 0, 0)),
                 jax.ShapeDtypeStruct((bp, _SUBLANES, d), _F32)))
    outs.append((pl.BlockSpec((db, d), lambda i: (0, 0)), jax.ShapeDtypeStruct((db, d), _F32)))
    return pl.pallas_call(
        functools.partial(_norm_lora_kernel, n_pt=n_pt, tps=tps, db=db, has_v=has_v),
        grid=(rows // tm,),
        in_specs=in_specs,
        out_specs=[o[0] for o in outs],
        out_shape=[o[1] for o in outs],
        scratch_shapes=[pltpu.VMEM((_SUBLANES, d), _F32), pltpu.VMEM((tm, d), _F32), pltpu.VMEM((tm, d), _F32)],
        compiler_params=_params("arbitrary"),
    )(*args)


def _mix_kernel(*refs, n_pt, tps, db, has_v):
    it = iter(refs)
    h_ref = next(it)
    w_refs = [next(it) for _ in range(6)]
    aw_ref, aa_ref, ag_ref = next(it), next(it), next(it)
    av_ref = next(it) if has_v else None
    d2_ref, i2_ref, g2_ref = next(it), next(it), next(it)
    v2_ref = next(it) if has_v else None
    w0_ref, a0_ref = next(it), next(it)
    v0_ref = next(it) if has_v else None
    mu_ref, kkw_ref, kaw_ref, cw_ref, hl_ref, cs_ref = (next(it) for _ in range(6))
    vf_ref = next(it) if has_v else None
    r_o, lw_o, k_o, v_o, kk_o, kka_o, g_o, oc_o = (next(it) for _ in range(8))
    vf_o = None if has_v else next(it)
    cp_o, cso_o = next(it), next(it)
    wc_scr, pcarry, zcarry, pprev_scr, z1_scr, z2_scr = (next(it) for _ in range(6))

    i = pl.program_id(1)
    tm = h_ref.shape[0]
    tn = r_o.shape[1]

    @pl.when(i == 0)
    def _cache_weights():
        for g in range(6):
            wc_scr[g] = w_refs[g][0].astype(_BF16)

    h = h_ref[...]
    p = [jnp.dot(h, wc_scr[g], preferred_element_type=_F32) for g in range(3)]
    bg = jnp.dot(h, wc_scr[3], preferred_element_type=_F32)
    z = (jnp.dot(h, wc_scr[4], preferred_element_type=_F32)
         * jnp.dot(h, wc_scr[5], preferred_element_type=_F32))

    @pl.when(i < n_pt)
    def _prompt():
        first = (i % tps) == 0
        for g in range(3):
            pprev_scr[g] = _shift_rows_prompt(p[g], pcarry[g], first, two=False)
            pcarry[g] = p[g][tm - _SUBLANES:, :]
        z1, z2 = _shift_rows_prompt(z, zcarry[...], first)
        z1_scr[...] = z1
        z2_scr[...] = z2
        zcarry[...] = z[tm - _SUBLANES:, :]
        cp_o[0] = z[tm - _SUBLANES:, :]

    @pl.when(i >= n_pt)
    def _decode():
        hl = hl_ref[0].astype(_BF16)
        for g in range(3):
            hlp = jnp.dot(hl, wc_scr[g], preferred_element_type=_F32)
            pprev_scr[g] = jnp.concatenate([hlp, p[g][:tm - db, :]], axis=0)
        z1_scr[...] = jnp.concatenate([cs_ref[0, 1], z[:tm - db, :]], axis=0)
        z2_scr[...] = jnp.concatenate([cs_ref[0, 0], cs_ref[0, 1], z[:tm - 2 * db, :]], axis=0)
        cso_o[0] = z[tm - 2 * db:tm - db, :]
        cso_o[1] = z[tm - db:, :]

    mu = mu_ref[0]
    r = p[0] + (pprev_scr[0] - p[0]) * mu[0:1]
    k = p[1] + (pprev_scr[1] - p[1]) * mu[1:2]
    v = p[2] + (pprev_scr[2] - p[2]) * mu[2:3]

    zlog = w0_ref[0] + _dot(aw_ref[...], d2_ref[0])
    softplus = jnp.maximum(-zlog, 0.0) + jnp.log(1.0 + jnp.exp(-jnp.abs(zlog)))
    lw_o[...] = -jnp.exp(-softplus - 0.5)
    a = jax.nn.sigmoid(a0_ref[0] + _dot(aa_ref[...], i2_ref[0]))
    g_o[...] = _dot(ag_ref[...], g2_ref[0])
    if has_v:
        nu = jax.nn.sigmoid(v0_ref[0] + _dot(av_ref[...], v2_ref[0]))
        v = v + (vf_ref[...] - v) * nu
    else:
        vf_o[...] = v

    kk = k * kkw_ref[0]
    norm = jnp.sqrt(_head_sums(kk * kk, _segment_ones(tn)))
    kk = kk / jnp.maximum(norm, 1e-12)
    r_o[...] = r
    k_o[...] = k * (1.0 + (a - 1.0) * kaw_ref[0])
    v_o[...] = v
    kk_o[...] = kk
    kka_o[...] = kk * a

    cw = cw_ref[0]
    zc = z2_scr[...] * cw[0:1] + z1_scr[...] * cw[1:2] + z * cw[2:3]
    oc_o[...] = (bg * zc).astype(_BF16)


def _mix(l, h, acts, vf_in, w_in, lora2, vecs, mu_rkv, k_k3, k_a3, conv_w, state_shift, state_conv_t,
         *, tm, tn, n_pt, tps, db, bp):
    rows, d = h.shape
    has_v = l > 0
    g_dim = mu_rkv.shape[2]
    nj = g_dim // tn
    aw, aa, ag, av = acts
    d2, i2, g2, v2 = lora2
    w0, a0, v0 = vecs
    row_in = lambda arr: pl.BlockSpec((tm, arr.shape[1]), lambda j, i: (i, 0))
    col3 = lambda arr, ll: pl.BlockSpec((1, arr.shape[1], tn), lambda j, i: (ll, 0, j))
    in_specs = [row_in(h)] + [pl.BlockSpec((1, d, tn), lambda j, i, g=g: (l, 0, g * nj + j)) for g in range(6)]
    args = [h] + [w_in] * 6
    in_specs += [row_in(aw), row_in(aa), row_in(ag)]
    args += [aw, aa, ag]
    if has_v:
        in_specs.append(row_in(av))
        args.append(av)
    in_specs += [col3(d2, l), col3(i2, l), col3(g2, l)]
    args += [d2, i2, g2]
    if has_v:
        in_specs.append(col3(v2, l - 1))
        args.append(v2)
    in_specs += [col3(w0, l), col3(a0, l)]
    args += [w0, a0]
    if has_v:
        in_specs.append(col3(v0, l - 1))
        args.append(v0)
    in_specs += [col3(mu_rkv, l), col3(k_k3, l), col3(k_a3, l), col3(conv_w, l),
                 pl.BlockSpec((1, db, d), lambda j, i: (l, 0, 0)),
                 pl.BlockSpec((1, 2, db, tn), lambda j, i: (l, 0, 0, j))]
    args += [mu_rkv, k_k3, k_a3, conv_w, state_shift, state_conv_t]
    if has_v:
        in_specs.append(pl.BlockSpec((tm, tn), lambda j, i: (i, j)))
        args.append(vf_in)
    tile = pl.BlockSpec((tm, tn), lambda j, i: (i, j))
    outs = [(tile, jax.ShapeDtypeStruct((rows, g_dim), _F32)) for _ in range(7)]
    outs.append((tile, jax.ShapeDtypeStruct((rows, g_dim), _BF16)))
    if not has_v:
        outs.append((tile, jax.ShapeDtypeStruct((rows, g_dim), _F32)))
    outs.append((pl.BlockSpec((1, _SUBLANES, tn), lambda j, i: (jnp.minimum(i // tps, bp - 1), 0, j)),
                 jax.ShapeDtypeStruct((bp, _SUBLANES, g_dim), _F32)))
    outs.append((pl.BlockSpec((2, db, tn), lambda j, i: (0, 0, j)), jax.ShapeDtypeStruct((2, db, g_dim), _F32)))
    return pl.pallas_call(
        functools.partial(_mix_kernel, n_pt=n_pt, tps=tps, db=db, has_v=has_v),
        grid=(nj, rows // tm),
        in_specs=in_specs,
        out_specs=[o[0] for o in outs],
        out_shape=[o[1] for o in outs],
        scratch_shapes=[pltpu.VMEM((6, d, tn), _BF16), pltpu.VMEM((3, _SUBLANES, tn), _F32),
                        pltpu.VMEM((_SUBLANES, tn), _F32), pltpu.VMEM((3, tm, tn), _F32),
                        pltpu.VMEM((tm, tn), _F32), pltpu.VMEM((tm, tn), _F32)],
        compiler_params=_params("arbitrary", "arbitrary"),
    )(*args)


def _unit_lower_inverse_minus_identity(lmats, ri, ci, chunk):
    same8 = (ri >> 3) == (ci >> 3)
    d8 = [jnp.where(same8, m, 0.0) for m in lmats]
    d2 = [_dot(a, a) for a in d8]
    d3 = [_dot(a, b) for a, b in zip(d8, d2)]
    d4 = [_dot(b, b) for b in d2]
    x = [a + b + cc for a, b, cc in zip(d8, d2, d3)]
    xd4 = [_dot(a, b) for a, b in zip(x, d4)]
    x = [a + b + cc for a, b, cc in zip(x, d4, xd4)]
    size = 16
    while size <= chunk:
        sh = size.bit_length() - 1
        level = ((ri >> sh) == (ci >> sh)) & ((ri >> (sh - 1)) != (ci >> (sh - 1)))
        e = [jnp.where(level, m, 0.0) for m in lmats]
        y = [b + _dot(a, b) for a, b in zip(x, e)]
        x = [a + b + _dot(b, a) for a, b in zip(x, y)]
        size *= 2
    return x


def _wkv_kernel(*refs, chunk, nb, zero_init, n_chunks):
    (r_ref, lw_ref, k_ref, v_ref, kk_ref, kka_ref, g_ref, rk_ref, lnw_ref, lnb_ref) = refs[:10]
    if zero_init:
        o_ref, so_ref, s_scr = refs[10:]
        s0_ref = None
    else:
        s0_ref, o_ref, so_ref, s_scr = refs[10:]
    ci_grid = pl.program_id(1)
    g_dim = r_ref.shape[1]
    npair = g_dim // _PAIR
    c = chunk
    n2 = 2 * c

    lane = lax.broadcasted_iota(jnp.int32, (1, _PAIR), 1)
    m0 = (lane < _HEAD).astype(_F32)
    m1 = 1.0 - m0

    @pl.when(ci_grid == 0)
    def _init():
        if zero_init:
            s_scr[...] = jnp.zeros(s_scr.shape, _F32)
        else:
            zero = jnp.zeros((_HEAD, _HEAD), _F32)
            for u in range(nb):
                for p in range(npair):
                    top = jnp.concatenate([s0_ref[u, 2 * p], zero], axis=1)
                    bot = jnp.concatenate([zero, s0_ref[u, 2 * p + 1]], axis=1)
                    s_scr[u, p] = jnp.concatenate([top, bot], axis=0)

    ri = lax.broadcasted_iota(jnp.int32, (n2, n2), 0) & (c - 1)
    ci = lax.broadcasted_iota(jnp.int32, (n2, n2), 1) & (c - 1)
    strict = ri > ci
    incl = ri >= ci
    tri = (lax.broadcasted_iota(jnp.int32, (c, c), 0) >= lax.broadcasted_iota(jnp.int32, (c, c), 1)).astype(_BF16)
    seg_ones = _segment_ones(_SEG)
    rk = rk_ref[...]
    lnw = lnw_ref[...]
    lnb = lnb_ref[...]

    def stack(x):
        return jnp.concatenate([x * m0, x * m1], axis=0)

    def unit(u, carry):
        rows = slice(0, c) if nb == 1 else pl.ds(pl.multiple_of(u * c, c), c)
        lw = lw_ref[rows, :]
        l_hi, l_mid, l_lo = _split3(lw)
        cl = (jnp.dot(tri, l_hi, preferred_element_type=_F32)
              + jnp.dot(tri, l_mid, preferred_element_type=_F32)
              + jnp.dot(tri, l_lo, preferred_element_type=_F32))
        cl_end = cl[c - 1:c, :]
        r = r_ref[rows, :]
        k = k_ref[rows, :]
        v = v_ref[rows, :]
        kk = kk_ref[rows, :]
        kka = kka_ref[rows, :]
        e_neg = jnp.exp(-cl)
        e_tail = jnp.exp(cl_end - cl)
        p_end = jnp.exp(cl_end)
        rt = r * jnp.exp(cl)
        at = -(kk * jnp.exp(cl - lw))
        kt = k * e_neg
        bt = kka * e_neg
        kh = k * e_tail
        bh = kka * e_tail
        pairs = range(npair)
        sls = [slice(p * _PAIR, (p + 1) * _PAIR) for p in pairs]
        ar2 = [jnp.concatenate([stack(at[:, sl]), stack(rt[:, sl])], axis=0).astype(_BF16) for sl in sls]
        mb = [_dot_nt(a, stack(bt[:, sl])) for a, sl in zip(ar2, sls)]
        mk = [_dot_nt(a, stack(kt[:, sl])) for a, sl in zip(ar2, sls)]
        s_old = [s_scr[u, p] for p in pairs]
        uy0 = [_dot_nt(jnp.concatenate([at[:, sl], rt[:, sl]], axis=0), s)
               for sl, s in zip(sls, s_old)]
        x_inv = _unit_lower_inverse_minus_identity([jnp.where(strict, m[:n2], 0.0) for m in mb], ri, ci, c)
        vs = [stack(v[:, sl]).astype(_BF16) for sl in sls]
        ws = [stack(q[:c]) + _dot(jnp.where(strict, m[:n2], 0.0), vv) for q, m, vv in zip(uy0, mk, vs)]
        us = [w + _dot(xi, w) for xi, w in zip(x_inv, ws)]
        us = [q.astype(_BF16) for q in us]
        ysd = [_dot(jnp.where(incl, m1_[n2:], 0.0), uu) + _dot(jnp.where(incl, m2_[n2:], 0.0), vv)
               for m1_, m2_, uu, vv in zip(mb, mk, us, vs)]
        ys = [q[c:] + d[:c] + d[c:] for q, d in zip(uy0, ysd)]
        for p, sl in zip(pairs, sls):
            uv = jnp.concatenate([us[p], vs[p]], axis=0)
            bk = jnp.concatenate([stack(bh[:, sl]), stack(kh[:, sl])], axis=0)
            s_scr[u, p] = s_old[p] * p_end[:, sl] + _dot_tn(uv, bk)
        y = jnp.concatenate(ys, axis=1)
        mean = _head_sums(y, seg_ones) * (1.0 / _HEAD)
        dy = y - mean
        var = _head_sums(dy * dy, seg_ones) * (1.0 / _HEAD)
        yn = dy * lax.rsqrt(var + _GN_EPS) * lnw + lnb
        bonus = _head_sums(r * k * rk, seg_ones) * v
        o_ref[rows, :] = ((yn + bonus) * g_ref[rows, :]).astype(o_ref.dtype)
        return carry

    if nb == 1:
        unit(0, 0)
    else:
        lax.fori_loop(0, nb, unit, 0)

    @pl.when(ci_grid == n_chunks - 1)
    def _final():
        for u in range(nb):
            for p in range(npair):
                s = s_scr[u, p]
                so_ref[u, 2 * p] = s[:_HEAD, :_HEAD]
                so_ref[u, 2 * p + 1] = s[_HEAD:, _HEAD:]


def _wkv(l, seqs, chunk, nb, n_seq, n_chunks, state0, r_k3, ln_w3, ln_b3, *, row_block_of):
    g_dim = seqs[0].shape[1]
    n_heads = g_dim // _HEAD
    rows = seqs[0].shape[0]
    zero_init = state0 is None
    blk = pl.BlockSpec((nb * chunk, g_dim), lambda bi, ci: (row_block_of(bi, ci), 0))
    vec = pl.BlockSpec((None, 1, g_dim), lambda bi, ci: (l, 0, 0))
    st = pl.BlockSpec((nb, n_heads, _HEAD, _HEAD), lambda bi, ci: (bi, 0, 0, 0))
    in_specs = [blk] * 7 + [vec] * 3
    args = list(seqs) + [r_k3, ln_w3, ln_b3]
    if not zero_init:
        in_specs.append(st)
        args.append(state0)
    return pl.pallas_call(
        functools.partial(_wkv_kernel, chunk=chunk, nb=nb, zero_init=zero_init, n_chunks=n_chunks),
        grid=(n_seq // nb, n_chunks),
        in_specs=in_specs,
        out_specs=[blk, st],
        out_shape=[jax.ShapeDtypeStruct((rows, g_dim), _BF16),
                   jax.ShapeDtypeStruct((n_seq, n_heads, _HEAD, _HEAD), _F32)],
        scratch_shapes=[pltpu.VMEM((nb, g_dim // _PAIR, _PAIR, _PAIR), _F32)],
        compiler_params=_params("arbitrary", "arbitrary"),
    )(*args)


def _wkv_decode_kernel(r_ref, lw_ref, k_ref, v_ref, kk_ref, kka_ref, g_ref, rk_ref, lnw_ref, lnb_ref, s0_ref,
                       o_ref, so_ref, *, ts, nb):
    g_dim = r_ref.shape[2]
    npair = g_dim // _PAIR
    n = ts * nb
    n2 = 2 * n
    nb_bits = nb.bit_length() - 1

    lane = lax.broadcasted_iota(jnp.int32, (1, _PAIR), 1)
    m0 = (lane < _HEAD).astype(_F32)
    m1 = 1.0 - m0

    def stack(x):
        return jnp.concatenate([x * m0, x * m1], axis=0)

    def rows_tb(ref):
        return jnp.concatenate([ref[t] for t in range(ts)], axis=0)

    lw_t = [lw_ref[t] for t in range(ts)]
    cl_t = [lw_t[0]]
    for t in range(1, ts):
        cl_t.append(cl_t[-1] + lw_t[t])
    cl = jnp.concatenate(cl_t, axis=0)
    lw = jnp.concatenate(lw_t, axis=0)
    cl_end_b = cl_t[-1]
    cl_end = _tile_rows(cl_end_b, ts)
    p_end = jnp.exp(cl_end_b)
    r, k, v, kk, kka = (rows_tb(ref) for ref in (r_ref, k_ref, v_ref, kk_ref, kka_ref))
    e_neg = jnp.exp(-cl)
    e_tail = jnp.exp(cl_end - cl)
    rt = r * jnp.exp(cl)
    at = -(kk * jnp.exp(cl - lw))
    kt = k * e_neg
    bt = kka * e_neg
    kh = k * e_tail
    bh = kka * e_tail

    ri = lax.broadcasted_iota(jnp.int32, (n2, n2), 0)
    ci = lax.broadcasted_iota(jnp.int32, (n2, n2), 1)
    same_seq = (ri & (nb - 1)) == (ci & (nb - 1))
    t_r = (ri & (n - 1)) >> nb_bits
    t_c = (ci & (n - 1)) >> nb_bits
    strict = same_seq & (t_r > t_c)
    incl = same_seq & (t_r >= t_c)
    seq_of_row2 = lax.broadcasted_iota(jnp.int32, (n2, 1), 0) & (nb - 1)
    seq_of_row4 = lax.broadcasted_iota(jnp.int32, (2 * n2, 1), 0) & (nb - 1)
    zero = jnp.zeros((_HEAD, _HEAD), _F32)

    pairs = range(npair)
    sls = [slice(p * _PAIR, (p + 1) * _PAIR) for p in pairs]
    ar2 = [jnp.concatenate([stack(at[:, sl]), stack(rt[:, sl])], axis=0).astype(_BF16) for sl in sls]
    mb = [_dot_nt(a, stack(bt[:, sl])) for a, sl in zip(ar2, sls)]
    mk = [_dot_nt(a, stack(kt[:, sl])) for a, sl in zip(ar2, sls)]
    lab = [jnp.where(strict, m[:n2], 0.0) for m in mb]
    d2 = [_dot(a, a) for a in lab]
    d3 = [_dot(a, b) for a, b in zip(lab, d2)]
    x_inv = [a + b + cc for a, b, cc in zip(lab, d2, d3)]

    def block_diag(b, p):
        top = jnp.concatenate([s0_ref[b, 2 * p], zero], axis=1)
        bot = jnp.concatenate([zero, s0_ref[b, 2 * p + 1]], axis=1)
        return jnp.concatenate([top, bot], axis=0)

    ys = []
    for p, sl in zip(pairs, sls):
        s_b = [block_diag(b, p) for b in range(nb)]
        ar = jnp.concatenate([at[:, sl], rt[:, sl]], axis=0)
        ar_cat = jnp.concatenate([jnp.where(seq_of_row2 == b, ar, 0.0).astype(_BF16) for b in range(nb)], axis=1)
        s_cat = jnp.concatenate([s.astype(_BF16) for s in s_b], axis=1)
        uy0 = _dot_nt(ar_cat, s_cat)
        vs = stack(v[:, sl]).astype(_BF16)
        ws = stack(uy0[:n]) + _dot(jnp.where(strict, mk[p][:n2], 0.0), vs)
        us = (ws + _dot(x_inv[p], ws)).astype(_BF16)
        ysd = _dot(jnp.where(incl, mb[p][n2:], 0.0), us) + _dot(jnp.where(incl, mk[p][n2:], 0.0), vs)
        ys.append(uy0[n:] + ysd[:n] + ysd[n:])
        uv = jnp.concatenate([us, vs], axis=0)
        uv_cat = jnp.concatenate([jnp.where(seq_of_row4 == b, uv, jnp.zeros_like(uv)) for b in range(nb)], axis=1)
        bk = jnp.concatenate([stack(bh[:, sl]), stack(kh[:, sl])], axis=0)
        upd = _dot_tn(uv_cat, bk)
        for b in range(nb):
            s_new = (s_b[b] * p_end[b:b + 1, sl] + upd[b * _PAIR:(b + 1) * _PAIR, :])
            so_ref[b, 2 * p] = s_new[:_HEAD, :_HEAD]
            so_ref[b, 2 * p + 1] = s_new[_HEAD:, _HEAD:]

    y = jnp.concatenate(ys, axis=1)
    seg_ones = _segment_ones(_SEG)
    mean = _head_sums(y, seg_ones) * (1.0 / _HEAD)
    dy = y - mean
    var = _head_sums(dy * dy, seg_ones) * (1.0 / _HEAD)
    yn = dy * lax.rsqrt(var + _GN_EPS) * lnw_ref[...] + lnb_ref[...]
    bonus = _head_sums(r * k * rk_ref[...], seg_ones) * v
    out = ((yn + bonus) * rows_tb(g_ref)).astype(o_ref.dtype)
    for t in range(ts):
        o_ref[t] = out[t * nb:(t + 1) * nb, :]


def _wkv_decode(l, seqs, state0, r_k3, ln_w3, ln_b3, *, ts, db, nb, first_block):
    g_dim = seqs[0].shape[2]
    n_heads = g_dim // _HEAD
    blk = pl.BlockSpec((ts, nb, g_dim), lambda bi: (first_block, bi, 0))
    vec = pl.BlockSpec((None, 1, g_dim), lambda bi: (l, 0, 0))
    st = pl.BlockSpec((nb, n_heads, _HEAD, _HEAD), lambda bi: (bi, 0, 0, 0))
    return pl.pallas_call(
        functools.partial(_wkv_decode_kernel, ts=ts, nb=nb),
        grid=(db // nb,),
        in_specs=[blk] * 7 + [vec] * 3 + [st],
        out_specs=[pl.BlockSpec((ts, nb, g_dim), lambda bi: (0, bi, 0)), st],
        out_shape=[jax.ShapeDtypeStruct((ts, db, g_dim), _BF16),
                   jax.ShapeDtypeStruct((db, n_heads, _HEAD, _HEAD), _F32)],
        compiler_params=_params("arbitrary"),
    )(*seqs, r_k3, ln_w3, ln_b3, state0)


def _cast_kernel(w_ref, o_ref):
    o_ref[...] = w_ref[...].astype(o_ref.dtype)


def _cast_bf16(w, rows_per_block):
    depth, k, n = w.shape
    spec = pl.BlockSpec((1, rows_per_block, n), lambda l, i: (l, i, 0))
    return pl.pallas_call(
        _cast_kernel, grid=(depth, k // rows_per_block), in_specs=[spec], out_specs=spec,
        out_shape=jax.ShapeDtypeStruct(w.shape, _BF16),
        compiler_params=_params("arbitrary", "arbitrary"),
    )(w)


def _outproj_kernel(op_ref, os_ref, oc_ref, x_ref, ga_ref, sh_ref, sc_ref, g_ref, w_ref,
                    x1_ref, h2_ref, *, n_pt, tps, db):
    i = pl.program_id(0)
    tm = x_ref.shape[0]
    ts = tm // db
    g_dim = op_ref.shape[1]

    o_rw = jnp.where(i < n_pt, op_ref[...], os_ref[...])
    acc = (jnp.dot(o_rw, w_ref[0, :g_dim, :], preferred_element_type=_F32)
           + jnp.dot(oc_ref[...], w_ref[0, g_dim:, :], preferred_element_type=_F32))

    def finish(ga, sc, sh):
        x1 = x_ref[...] + ga * acc
        x1_ref[...] = x1
        xn = x1 * lax.rsqrt(jnp.mean(x1 * x1, axis=-1, keepdims=True) + _RMS_EPS) * g_ref[0, 0]
        h2_ref[...] = (xn * (1.0 + sc) + sh).astype(_BF16)

    @pl.when(i < n_pt)
    def _prompt():
        finish(_mod_row(ga_ref, i, tps, db), _mod_row(sc_ref, i, tps, db), _mod_row(sh_ref, i, tps, db))

    @pl.when(i >= n_pt)
    def _decode():
        finish(_mod_tile(ga_ref, db, ts), _mod_tile(sc_ref, db, ts), _mod_tile(sh_ref, db, ts))


def _outproj(l, o_p, o_s, o_conv, x, mod, norm_g4, w_out, *, tm, n_pt, tps, db):
    rows, d = x.shape
    g_dim = o_p.shape[1]
    bc = mod.shape[2]
    modspec = lambda comp: pl.BlockSpec((1, 1, bc, d), lambda i: (l, comp, 0, 0))
    return pl.pallas_call(
        functools.partial(_outproj_kernel, n_pt=n_pt, tps=tps, db=db),
        grid=(rows // tm,),
        in_specs=[pl.BlockSpec((tm, g_dim), lambda i: (jnp.minimum(i, n_pt - 1), 0)),
                  pl.BlockSpec((tm, g_dim), lambda i: (0, 0)),
                  pl.BlockSpec((tm, o_conv.shape[1]), lambda i: (i, 0)),
                  pl.BlockSpec((tm, d), lambda i: (i, 0)),
                  modspec(2), modspec(3), modspec(4),
                  pl.BlockSpec((1, 1, 1, d), lambda i: (l, 1, 0, 0)),
                  pl.BlockSpec((1, d, d), lambda i: (l, 0, 0), pipeline_mode=pl.Buffered(1))],
        out_specs=[pl.BlockSpec((tm, d), lambda i: (i, 0)), pl.BlockSpec((tm, d), lambda i: (i, 0))],
        out_shape=[jax.ShapeDtypeStruct((rows, d), _F32), jax.ShapeDtypeStruct((rows, d), _BF16)],
        compiler_params=_params("arbitrary"),
    )(o_p, o_s, o_conv, x, mod, mod, mod, norm_g4, w_out)


def _ffn_kernel(*refs, n_pt, tps, db, nj, sub, final_norm):
    it = iter(refs)
    (h2_ref, wa_ref, wb_ref, cwa_ref, cwb_ref, wd_ref, x1_ref, ga_ref, sfa_ref, sfb_ref) = (next(it) for _ in range(10))
    fg_ref = next(it) if final_norm else None
    x2_ref, fpa_o, fpb_o, fsa_o, fsb_o = (next(it) for _ in range(5))
    acc_scr, ca_scr, cb_scr = (next(it) for _ in range(3))
    i = pl.program_id(0)
    j = pl.program_id(1)
    tm = h2_ref.shape[0]
    tn = wa_ref.shape[2]
    ts = tm // db

    @pl.when(j == 0)
    def _zero():
        acc_scr[...] = jnp.zeros(acc_scr.shape, _F32)

    def body(prompt):
        first = (i % tps) == 0
        h2 = h2_ref[...]
        contrib = None
        for s in range(tn // sub):
            cs = slice(s * sub, (s + 1) * sub)
            conv = []
            for w_ref, cw_ref, c_scr, sf, fp_o, fs_o in ((wa_ref, cwa_ref, ca_scr, sfa_ref, fpa_o, fsa_o),
                                                         (wb_ref, cwb_ref, cb_scr, sfb_ref, fpb_o, fsb_o)):
                u = jnp.dot(h2, w_ref[0, :, cs], preferred_element_type=_F32)
                if prompt:
                    u1, u2 = _shift_rows_prompt(u, c_scr[j, :, cs], first)
                    c_scr[j, :, cs] = u[tm - _SUBLANES:, :]
                    fp_o[0, :, cs] = u[tm - _SUBLANES:, :]
                else:
                    u1 = jnp.concatenate([sf[0, 1, :, cs], u[:tm - db, :]], axis=0)
                    u2 = jnp.concatenate([sf[0, 0, :, cs], sf[0, 1, :, cs], u[:tm - 2 * db, :]], axis=0)
                    fp_o[0, :, cs] = jnp.zeros((_SUBLANES, sub), _F32)
                    fs_o[0, :, cs] = u[tm - 2 * db:tm - db, :]
                    fs_o[1, :, cs] = u[tm - db:, :]
                cw = cw_ref[0, :, cs]
                conv.append(u2 * cw[0:1] + u1 * cw[1:2] + u * cw[2:3])
            gact = ((conv[0] * jax.nn.sigmoid(conv[0])) * conv[1]).astype(_BF16)
            part = jnp.dot(gact, wd_ref[0, cs, :], preferred_element_type=_F32)
            contrib = part if contrib is None else contrib + part
        acc_scr[...] += contrib

    @pl.when(i < n_pt)
    def _prompt():
        body(True)

    @pl.when(i >= n_pt)
    def _decode():
        body(False)

    def finish(ga):
        x2 = x1_ref[...] + ga * acc_scr[...]
        if final_norm:
            x2 = x2 * lax.rsqrt(jnp.mean(x2 * x2, axis=-1, keepdims=True) + _RMS_EPS) * fg_ref[...]
        x2_ref[...] = x2

    @pl.when((j == nj - 1) & (i < n_pt))
    def _finish_prompt():
        finish(_mod_row(ga_ref, i, tps, db))

    @pl.when((j == nj - 1) & (i >= n_pt))
    def _finish_decode():
        finish(_mod_tile(ga_ref, db, ts))


def _ffn(l, h2, x1, mod, ffn_up, ffn_conv, ffn_down, state_ffn_t, final_g, *, tm, tn, n_pt, tps, db, bp):
    rows, d = x1.shape
    f = ffn_down.shape[1]
    nj = f // tn
    bc = mod.shape[2]
    final_norm = final_g is not None
    in_specs = [pl.BlockSpec((tm, d), lambda i, j: (i, 0)),
                pl.BlockSpec((1, d, tn), lambda i, j: (l, 0, j)),
                pl.BlockSpec((1, d, tn), lambda i, j: (l, 0, nj + j)),
                pl.BlockSpec((1, 3, tn), lambda i, j: (l, 0, j)),
                pl.BlockSpec((1, 3, tn), lambda i, j: (l, 0, nj + j)),
                pl.BlockSpec((1, tn, d), lambda i, j: (l, j, 0)),
                pl.BlockSpec((tm, d), lambda i, j: (i, 0)),
                pl.BlockSpec((1, 1, bc, d), lambda i, j: (l, 5, 0, 0)),
                pl.BlockSpec((1, 2, db, tn), lambda i, j: (l, 0, 0, j)),
                pl.BlockSpec((1, 2, db, tn), lambda i, j: (l, 0, 0, nj + j))]
    args = [h2, ffn_up, ffn_up, ffn_conv, ffn_conv, ffn_down, x1, mod, state_ffn_t, state_ffn_t]
    if final_norm:
        in_specs.append(pl.BlockSpec((1, d), lambda i, j: (0, 0)))
        args.append(final_g)
    n_tiles = rows // tm
    pstate = pl.BlockSpec((1, _SUBLANES, tn), lambda i, j: (i, 0, j))
    sstate = pl.BlockSpec((2, db, tn), lambda i, j: (0, 0, jnp.where(i >= n_pt, j, 0)))
    return pl.pallas_call(
        functools.partial(_ffn_kernel, n_pt=n_pt, tps=tps, db=db, nj=nj, sub=min(tn, _SEG),
                          final_norm=final_norm),
        grid=(rows // tm, nj),
        in_specs=in_specs,
        out_specs=[pl.BlockSpec((tm, d), lambda i, j: (i, 0)), pstate, pstate, sstate, sstate],
        out_shape=[jax.ShapeDtypeStruct((rows, d), _F32),
                   jax.ShapeDtypeStruct((n_tiles, _SUBLANES, f), _F32),
                   jax.ShapeDtypeStruct((n_tiles, _SUBLANES, f), _F32),
                   jax.ShapeDtypeStruct((2, db, f), _F32), jax.ShapeDtypeStruct((2, db, f), _F32)],
        scratch_shapes=[pltpu.VMEM((tm, d), _F32), pltpu.VMEM((nj, _SUBLANES, tn), _F32),
                        pltpu.VMEM((nj, _SUBLANES, tn), _F32)],
        compiler_params=_params("arbitrary", "arbitrary"),
    )(*args)


def _forward(x_prompt, x_sample, c_prompt, c_sample, state_wkv, state_shift, state_conv, state_ffn,
             ada_w, ada_b, norm_g, final_norm_g, w_in, mu_x, mu_rkv, decay_w0, decay_lora1, decay_lora2,
             iclr_a0, iclr_lora1, iclr_lora2, gate_lora1, gate_lora2, vres_v0, vres_lora1, vres_lora2,
             k_k, k_a, r_k, ln_x_w, ln_x_b, conv_w, w_out, ffn_up, ffn_conv, ffn_down,
             *, chunk=64, mix_tn=256, ffn_tn=512, wkv_nb=16):
    bp, t_len, d = x_prompt.shape
    db, ts, _ = x_sample.shape
    depth = ada_w.shape[0]
    g_dim = mu_rkv.shape[2]
    n_heads = g_dim // _HEAD
    f = ffn_down.shape[1]
    tm = ts * db
    assert t_len % tm == 0 and t_len % chunk == 0 and ts >= 2 and db % _SUBLANES == 0 and tm % chunk == 0
    tps = t_len // tm
    n_pt = bp * tps
    n_prompt_rows = bp * t_len
    assert ts == 4 and db % wkv_nb == 0 and wkv_nb & (wkv_nb - 1) == 0

    x = jnp.concatenate([x_prompt.reshape(n_prompt_rows, d),
                         x_sample.transpose(1, 0, 2).reshape(tm, d)], axis=0)
    pad = (-(db + bp)) % _SUBLANES
    c_all = jnp.concatenate([c_sample, c_prompt, jnp.zeros((pad, d), _F32)], axis=0)
    mod = _adaln_mod(c_all, ada_w, ada_b)

    norm_g4 = norm_g.reshape(depth, 2, 1, d)
    vec3 = lambda a: a.reshape(a.shape[0], 1, a.shape[1])
    w0_3, a0_3, v0_3, kk_3, ka_3 = vec3(decay_w0), vec3(iclr_a0), vec3(vres_v0), vec3(k_k), vec3(k_a)
    rk_3, lnw_3, lnb_3 = r_k.reshape(depth, 1, g_dim), vec3(ln_x_w), vec3(ln_x_b)
    state_conv_t = state_conv.transpose(0, 2, 1, 3)
    state_ffn_t = state_ffn.transpose(0, 2, 1, 3)
    tiles = dict(tm=tm, n_pt=n_pt, tps=tps, db=db)
    w_out = _cast_bf16(w_out, 512)
    ffn_up = _cast_bf16(ffn_up, 128)
    ffn_down = _cast_bf16(ffn_down, 512)

    new = {k: [] for k in ("wkv_p", "shift_p", "conv_p", "ffn_p", "wkv_s", "shift_s", "conv_s", "ffn_s")}
    v_first = None
    for l in range(depth):
        outs = _norm_lora(l, x, mod, norm_g4, mu_x, state_shift, decay_lora1, iclr_lora1, gate_lora1,
                          vres_lora1, bp=bp, **tiles)
        if l > 0:
            h, aw, aa, ag, av, hp_last, hs_last = outs
        else:
            h, aw, aa, ag, hp_last, hs_last = outs
            av = None
        mouts = _mix(l, h, (aw, aa, ag, av), v_first, w_in,
                     (decay_lora2, iclr_lora2, gate_lora2, vres_lora2), (w0_3, a0_3, v0_3),
                     mu_rkv, kk_3, ka_3, conv_w, state_shift, state_conv_t, tn=mix_tn, bp=bp, **tiles)
        if l > 0:
            r, lw, k, v, kk, kka, gate, o_conv, conv_p8, conv_s = mouts
        else:
            r, lw, k, v, kk, kka, gate, o_conv, v_first, conv_p8, conv_s = mouts
        seqs = (r, lw, k, v, kk, kka, gate)

        cpt = t_len // chunk
        o_p, wkv_p = _wkv(l, seqs, chunk, 1, bp, cpt, None, rk_3, lnw_3, lnb_3,
                          row_block_of=lambda bi, ci: bi * cpt + ci)

        dseqs = tuple(a.reshape(a.shape[0] // db, db, g_dim) for a in seqs)
        o_s, wkv_s = _wkv_decode(l, dseqs, state_wkv[l], rk_3, lnw_3, lnb_3, ts=ts, db=db, nb=wkv_nb,
                                 first_block=n_prompt_rows // tm)
        o_s = o_s.reshape(tm, g_dim)

        x1, h2 = _outproj(l, o_p, o_s, o_conv, x, mod, norm_g4, w_out, **tiles)
        final_g = final_norm_g.reshape(1, d) if l == depth - 1 else None
        x, fpa, fpb, fsa, fsb = _ffn(l, h2, x1, mod, ffn_up, ffn_conv, ffn_down, state_ffn_t, final_g,
                                     tn=ffn_tn, bp=bp, **tiles)

        new["wkv_p"].append(wkv_p)
        new["wkv_s"].append(wkv_s)
        new["shift_p"].append(hp_last[:, _SUBLANES - 1])
        new["shift_s"].append(hs_last)
        new["conv_p"].append(conv_p8[:, _SUBLANES - 2:])
        new["conv_s"].append(conv_s.transpose(1, 0, 2))
        last_tiles = slice(tps - 1, n_pt, tps)
        new["ffn_p"].append(jnp.concatenate([fpa[last_tiles, _SUBLANES - 2:], fpb[last_tiles, _SUBLANES - 2:]],
                                            axis=-1))
        new["ffn_s"].append(jnp.concatenate([fsa, fsb], axis=-1).transpose(1, 0, 2))

    y_prompt = x[:n_prompt_rows].reshape(bp, t_len, d)
    y_sample = x[n_prompt_rows:].reshape(ts, db, d).transpose(1, 0, 2)
    st = {k: jnp.stack(vs) for k, vs in new.items()}
    return (y_prompt, y_sample, st["wkv_p"], st["shift_p"], st["conv_p"], st["ffn_p"],
            st["wkv_s"], st["shift_s"], st["conv_s"], st["ffn_s"])


def kernel(x_prompt, x_sample, c_prompt, c_sample, state_wkv, state_shift, state_conv, state_ffn, ada_w, ada_b, norm_g, final_norm_g, w_in, mu_x, mu_rkv, decay_w0, decay_lora1, decay_lora2, iclr_a0, iclr_lora1, iclr_lora2, gate_lora1, gate_lora2, vres_v0, vres_lora1, vres_lora2, k_k, k_a, r_k, ln_x_w, ln_x_b, conv_w, w_out, ffn_up, ffn_conv, ffn_down):
    return _forward(x_prompt, x_sample, c_prompt, c_sample, state_wkv, state_shift, state_conv, state_ffn,
                    ada_w, ada_b, norm_g, final_norm_g, w_in, mu_x, mu_rkv, decay_w0, decay_lora1, decay_lora2,
                    iclr_a0, iclr_lora1, iclr_lora2, gate_lora1, gate_lora2, vres_v0, vres_lora1, vres_lora2,
                    k_k, k_a, r_k, ln_x_w, ln_x_b, conv_w, w_out, ffn_up, ffn_conv, ffn_down)
```

```python
import functools
from typing import NamedTuple

import jax
import jax.numpy as jnp
from jax import lax
from jax.experimental import pallas as pl
from jax.experimental.pallas import tpu as pltpu

_F32 = jnp.float32
_BF16 = jnp.bfloat16

_HEAD = 64
_PAIR = 2 * _HEAD
_SEG = 256
_RMS_EPS = 1e-6
_GN_EPS = 64e-5
_VMEM_LIMIT_BYTES = 56 * 1024 * 1024
_SUBLANES = 8


def _params(*sem):
    return pltpu.CompilerParams(dimension_semantics=sem, vmem_limit_bytes=_VMEM_LIMIT_BYTES)


def _dot(a, b):
    return jnp.dot(a.astype(_BF16), b.astype(_BF16), preferred_element_type=_F32)


def _dot_nt(a, b):
    return lax.dot_general(a.astype(_BF16), b.astype(_BF16), (((1,), (1,)), ((), ())),
                           preferred_element_type=_F32)


def _dot_tn(a, b):
    return lax.dot_general(a.astype(_BF16), b.astype(_BF16), (((0,), (0,)), ((), ())),
                           preferred_element_type=_F32)


def _split2(x):
    hi = x.astype(_BF16)
    lo = (x - hi.astype(_F32)).astype(_BF16)
    return hi, lo


def _split3(x):
    hi = x.astype(_BF16)
    r1 = x - hi.astype(_F32)
    mid = r1.astype(_BF16)
    lo = (r1 - mid.astype(_F32)).astype(_BF16)
    return hi, mid, lo


def _dot_exact_rhs(x, m_bf16):
    hi, lo = _split2(x)
    return (jnp.dot(hi, m_bf16, preferred_element_type=_F32)
            + jnp.dot(lo, m_bf16, preferred_element_type=_F32))


def _segment_ones(n):
    r = lax.broadcasted_iota(jnp.int32, (n, n), 0) // _HEAD
    c = lax.broadcasted_iota(jnp.int32, (n, n), 1) // _HEAD
    return (r == c).astype(_BF16)


def _head_sums(x, seg_ones):
    n = seg_ones.shape[0]
    parts = [_dot_exact_rhs(x[:, q * n:(q + 1) * n], seg_ones) for q in range(x.shape[1] // n)]
    return parts[0] if len(parts) == 1 else jnp.concatenate(parts, axis=1)


def _tile_rows(m, reps):
    return m if reps == 1 else jnp.concatenate([m] * reps, axis=0)


def _shift_rows_prompt(x, carry8, first, two=True):
    r8 = lax.broadcasted_iota(jnp.int32, (_SUBLANES, 1), 0)
    l1 = jnp.where(first, 0.0, carry8[_SUBLANES - 1:_SUBLANES, :])
    x1 = pltpu.roll(x, 1, 0)
    x1 = jnp.concatenate([jnp.where(r8 == 0, l1, x1[:_SUBLANES, :]), x1[_SUBLANES:, :]], axis=0)
    if not two:
        return x1
    l2 = jnp.where(first, 0.0, carry8[_SUBLANES - 2:_SUBLANES - 1, :])
    x2 = pltpu.roll(x, 2, 0)
    top2 = jnp.where(r8 == 0, l2, jnp.where(r8 == 1, l1, x2[:_SUBLANES, :]))
    return x1, jnp.concatenate([top2, x2[_SUBLANES:, :]], axis=0)


class _Act(NamedTuple):
    prompt: jax.Array
    decode: jax.Array
    decode_block: int


def _act_specs(x, tm, n_pt):
    d = x.prompt.shape[1]
    return [pl.BlockSpec((tm, d), lambda i, *_: (jnp.minimum(i, n_pt - 1), 0)),
            pl.BlockSpec((tm, d), lambda i, *_: (x.decode_block, 0))]


def _mod_row(ref, i, tps, db):
    return ref[0, 0, pl.ds(db + i // tps, 1), :]


def _mod_tile(ref, db, ts):
    return _tile_rows(ref[0, 0, 0:db, :], ts)


def _mod_kernel(c_ref, w_ref, b_ref, o_ref):
    c = c_ref[...]
    s_hi, s_lo = _split2(c * jax.nn.sigmoid(c))
    w_hi, w_lo = _split2(w_ref[0])
    acc = (jnp.dot(s_hi, w_hi, preferred_element_type=_F32)
           + jnp.dot(s_hi, w_lo, preferred_element_type=_F32)
           + jnp.dot(s_lo, w_hi, preferred_element_type=_F32))
    o_ref[0, 0] = acc + b_ref[0]


def _adaln_mod(c_all, ada_w, ada_b):
    depth, d, six_d = ada_w.shape
    bc = c_all.shape[0]
    tn = 1024
    per = d // tn
    return pl.pallas_call(
        _mod_kernel,
        grid=(depth, six_d // tn),
        in_specs=[pl.BlockSpec((bc, d), lambda l, n: (0, 0)),
                  pl.BlockSpec((1, d, tn), lambda l, n: (l, 0, n)),
                  pl.BlockSpec((1, 1, tn), lambda l, n: (l, 0, n))],
        out_specs=pl.BlockSpec((1, 1, bc, tn), lambda l, n: (l, n // per, 0, n % per)),
        out_shape=jax.ShapeDtypeStruct((depth, 6, bc, d), _F32),
        compiler_params=_params("arbitrary", "arbitrary"),
    )(c_all, ada_w, ada_b.reshape(depth, 1, six_d))


def _norm_lora_kernel(*refs, n_pt, tps, db, has_v):
    if has_v:
        (xp_ref, xs_ref, sh_ref, sc_ref, g_ref, mu_ref, hl_ref, d1_ref, i1_ref, g1_ref, v1_ref,
         h_ref, aw_ref, aa_ref, ag_ref, av_ref, hp_ref, hs_ref, carry_ref, hcur_ref, hprev_ref) = refs
    else:
        (xp_ref, xs_ref, sh_ref, sc_ref, g_ref, mu_ref, hl_ref, d1_ref, i1_ref, g1_ref,
         h_ref, aw_ref, aa_ref, ag_ref, hp_ref, hs_ref, carry_ref, hcur_ref, hprev_ref) = refs
    i = pl.program_id(0)
    tm = xp_ref.shape[0]
    ts = tm // db
    x = jnp.where(i < n_pt, xp_ref[...], xs_ref[...])
    xn = x * lax.rsqrt(jnp.mean(x * x, axis=-1, keepdims=True) + _RMS_EPS) * g_ref[0, 0]

    @pl.when(i < n_pt)
    def _prompt():
        h = xn * (1.0 + _mod_row(sc_ref, i, tps, db)) + _mod_row(sh_ref, i, tps, db)
        first = (i % tps) == 0
        hprev_ref[...] = _shift_rows_prompt(h, carry_ref[...], first, two=False)
        hcur_ref[...] = h
        carry_ref[...] = h[tm - _SUBLANES:, :]
        hp_ref[0] = h[tm - _SUBLANES:, :]

    @pl.when(i >= n_pt)
    def _decode():
        h = xn * (1.0 + _mod_tile(sc_ref, db, ts)) + _mod_tile(sh_ref, db, ts)
        hprev_ref[...] = jnp.concatenate([hl_ref[0], h[:tm - db, :]], axis=0)
        hcur_ref[...] = h
        hs_ref[...] = h[tm - db:, :]

    h = hcur_ref[...]
    xx = hprev_ref[...] - h
    mu = mu_ref[0]
    h_ref[...] = h.astype(_BF16)
    aw_ref[...] = jnp.tanh(_dot(h + xx * mu[0:1], d1_ref[0]))
    aa_ref[...] = _dot(h + xx * mu[1:2], i1_ref[0])
    ag_ref[...] = jax.nn.sigmoid(_dot(h + xx * mu[2:3], g1_ref[0]))
    if has_v:
        av_ref[...] = _dot(h + xx * mu[3:4], v1_ref[0])


def _norm_lora(l, x, mod, norm_g4, mu_x, state_shift, decay_lora1, iclr_lora1, gate_lora1, vres_lora1,
               *, tm, n_pt, tps, db, bp):
    d = x.prompt.shape[1]
    rows = (n_pt + 1) * tm
    has_v = l > 0
    bc = mod.shape[2]
    ld, la, lg = decay_lora1.shape[2], iclr_lora1.shape[2], gate_lora1.shape[2]
    full = lambda *shape: pl.BlockSpec(shape, lambda i: (l,) + (0,) * (len(shape) - 1))
    in_specs = _act_specs(x, tm, n_pt) + [
                pl.BlockSpec((1, 1, bc, d), lambda i: (l, 0, 0, 0)),
                pl.BlockSpec((1, 1, bc, d), lambda i: (l, 1, 0, 0)),
                pl.BlockSpec((1, 1, 1, d), lambda i: (l, 0, 0, 0)),
                full(1, 4, d), full(1, db, d), full(1, d, ld), full(1, d, la), full(1, d, lg)]
    args = [x.prompt, x.decode, mod, mod, norm_g4, mu_x, state_shift, decay_lora1, iclr_lora1, gate_lora1]
    row_out = lambda n, dt: (pl.BlockSpec((tm, n), lambda i: (i, 0)), jax.ShapeDtypeStruct((rows, n), dt))
    outs = [row_out(d, _BF16), row_out(ld, _F32), row_out(la, _F32), row_out(lg, _F32)]
    if has_v:
        lv = vres_lora1.shape[2]
        in_specs.append(pl.BlockSpec((1, d, lv), lambda i: (l - 1, 0, 0)))
        args.append(vres_lora1)
        outs.append(row_out(lv, _F32))
    outs.append((pl.BlockSpec((1, _SUBLANES, d), lambda i: (jnp.minimum(i // tps, bp - 1), 0, 0)),
                 jax.ShapeDtypeStruct((bp, _SUBLANES, d), _F32)))
    outs.append((pl.BlockSpec((db, d), lambda i: (0, 0)), jax.ShapeDtypeStruct((db, d), _F32)))
    return pl.pallas_call(
        functools.partial(_norm_lora_kernel, n_pt=n_pt, tps=tps, db=db, has_v=has_v),
        grid=(rows // tm,),
        in_specs=in_specs,
        out_specs=[o[0] for o in outs],
        out_shape=[o[1] for o in outs],
        scratch_shapes=[pltpu.VMEM((_SUBLANES, d), _F32), pltpu.VMEM((tm, d), _F32), pltpu.VMEM((tm, d), _F32)],
        compiler_params=_params("arbitrary"),
    )(*args)


def _mix_kernel(*refs, n_pt, tps, db, has_v):
    it = iter(refs)
    h_ref = next(it)
    w_refs = [next(it) for _ in range(6)]
    aw_ref, aa_ref, ag_ref = next(it), next(it), next(it)
    av_ref = next(it) if has_v else None
    d2_ref, i2_ref, g2_ref = next(it), next(it), next(it)
    v2_ref = next(it) if has_v else None
    w0_ref, a0_ref = next(it), next(it)
    v0_ref = next(it) if has_v else None
    mu_ref, kkw_ref, kaw_ref, cw_ref, hl_ref, cs_ref = (next(it) for _ in range(6))
    vf_ref = next(it) if has_v else None
    r_o, lw_o, k_o, v_o, kk_o, kka_o, g_o, oc_o = (next(it) for _ in range(8))
    vf_o = None if has_v else next(it)
    cp_o, cso_o = next(it), next(it)
    wc_scr, pcarry, zcarry, pprev_scr, z1_scr, z2_scr = (next(it) for _ in range(6))

    i = pl.program_id(1)
    tm = h_ref.shape[0]
    tn = r_o.shape[1]

    @pl.when(i == 0)
    def _cache_weights():
        for g in range(6):
            wc_scr[g] = w_refs[g][0].astype(_BF16)

    h = h_ref[...]
    p = [jnp.dot(h, wc_scr[g], preferred_element_type=_F32) for g in range(3)]
    bg = jnp.dot(h, wc_scr[3], preferred_element_type=_F32)
    z = (jnp.dot(h, wc_scr[4], preferred_element_type=_F32)
         * jnp.dot(h, wc_scr[5], preferred_element_type=_F32))

    @pl.when(i < n_pt)
    def _prompt():
        first = (i % tps) == 0
        for g in range(3):
            pprev_scr[g] = _shift_rows_prompt(p[g], pcarry[g], first, two=False)
            pcarry[g] = p[g][tm - _SUBLANES:, :]
        z1, z2 = _shift_rows_prompt(z, zcarry[...], first)
        z1_scr[...] = z1
        z2_scr[...] = z2
        zcarry[...] = z[tm - _SUBLANES:, :]
        cp_o[0] = z[tm - _SUBLANES:, :]

    @pl.when(i >= n_pt)
    def _decode():
        hl = hl_ref[0].astype(_BF16)
        for g in range(3):
            hlp = jnp.dot(hl, wc_scr[g], preferred_element_type=_F32)
            pprev_scr[g] = jnp.concatenate([hlp, p[g][:tm - db, :]], axis=0)
        z1_scr[...] = jnp.concatenate([cs_ref[0, 1], z[:tm - db, :]], axis=0)
        z2_scr[...] = jnp.concatenate([cs_ref[0, 0], cs_ref[0, 1], z[:tm - 2 * db, :]], axis=0)
        cso_o[0] = z[tm - 2 * db:tm - db, :]
        cso_o[1] = z[tm - db:, :]

    mu = mu_ref[0]
    r = p[0] + (pprev_scr[0] - p[0]) * mu[0:1]
    k = p[1] + (pprev_scr[1] - p[1]) * mu[1:2]
    v = p[2] + (pprev_scr[2] - p[2]) * mu[2:3]

    zlog = w0_ref[0] + _dot(aw_ref[...], d2_ref[0])
    softplus = jnp.maximum(-zlog, 0.0) + jnp.log(1.0 + jnp.exp(-jnp.abs(zlog)))
    lw_o[...] = -jnp.exp(-softplus - 0.5)
    a = jax.nn.sigmoid(a0_ref[0] + _dot(aa_ref[...], i2_ref[0]))
    g_o[...] = _dot(ag_ref[...], g2_ref[0])
    if has_v:
        nu = jax.nn.sigmoid(v0_ref[0] + _dot(av_ref[...], v2_ref[0]))
        v = v + (vf_ref[...] - v) * nu
    else:
        vf_o[...] = v

    kk = k * kkw_ref[0]
    norm = jnp.sqrt(_head_sums(kk * kk, _segment_ones(tn)))
    kk = kk / jnp.maximum(norm, 1e-12)
    r_o[...] = r
    k_o[...] = k * (1.0 + (a - 1.0) * kaw_ref[0])
    v_o[...] = v
    kk_o[...] = kk
    kka_o[...] = kk * a

    cw = cw_ref[0]
    zc = z2_scr[...] * cw[0:1] + z1_scr[...] * cw[1:2] + z * cw[2:3]
    oc_o[...] = (bg * zc).astype(_BF16)


def _mix(l, h, acts, vf_in, w_in, lora2, vecs, mu_rkv, k_k3, k_a3, conv_w, state_shift, state_conv_t,
         *, tm, tn, n_pt, tps, db, bp):
    rows, d = h.shape
    has_v = l > 0
    g_dim = mu_rkv.shape[2]
    nj = g_dim // tn
    aw, aa, ag, av = acts
    d2, i2, g2, v2 = lora2
    w0, a0, v0 = vecs
    row_in = lambda arr: pl.BlockSpec((tm, arr.shape[1]), lambda j, i: (i, 0))
    col3 = lambda arr, ll: pl.BlockSpec((1, arr.shape[1], tn), lambda j, i: (ll, 0, j))
    in_specs = [row_in(h)] + [pl.BlockSpec((1, d, tn), lambda j, i, g=g: (l, 0, g * nj + j)) for g in range(6)]
    args = [h] + [w_in] * 6
    in_specs += [row_in(aw), row_in(aa), row_in(ag)]
    args += [aw, aa, ag]
    if has_v:
        in_specs.append(row_in(av))
        args.append(av)
    in_specs += [col3(d2, l), col3(i2, l), col3(g2, l)]
    args += [d2, i2, g2]
    if has_v:
        in_specs.append(col3(v2, l - 1))
        args.append(v2)
    in_specs += [col3(w0, l), col3(a0, l)]
    args += [w0, a0]
    if has_v:
        in_specs.append(col3(v0, l - 1))
        args.append(v0)
    in_specs += [col3(mu_rkv, l), col3(k_k3, l), col3(k_a3, l), col3(conv_w, l),
                 pl.BlockSpec((1, db, d), lambda j, i: (l, 0, 0)),
                 pl.BlockSpec((1, 2, db, tn), lambda j, i: (l, 0, 0, j))]
    args += [mu_rkv, k_k3, k_a3, conv_w, state_shift, state_conv_t]
    if has_v:
        in_specs.append(pl.BlockSpec((tm, tn), lambda j, i: (i, j)))
        args.append(vf_in)
    tile = pl.BlockSpec((tm, tn), lambda j, i: (i, j))
    outs = [(tile, jax.ShapeDtypeStruct((rows, g_dim), _F32)) for _ in range(7)]
    outs.append((tile, jax.ShapeDtypeStruct((rows, g_dim), _BF16)))
    if not has_v:
        outs.append((tile, jax.ShapeDtypeStruct((rows, g_dim), _F32)))
    outs.append((pl.BlockSpec((1, _SUBLANES, tn), lambda j, i: (jnp.minimum(i // tps, bp - 1), 0, j)),
                 jax.ShapeDtypeStruct((bp, _SUBLANES, g_dim), _F32)))
    outs.append((pl.BlockSpec((2, db, tn), lambda j, i: (0, 0, j)), jax.ShapeDtypeStruct((2, db, g_dim), _F32)))
    return pl.pallas_call(
        functools.partial(_mix_kernel, n_pt=n_pt, tps=tps, db=db, has_v=has_v),
        grid=(nj, rows // tm),
        in_specs=in_specs,
        out_specs=[o[0] for o in outs],
        out_shape=[o[1] for o in outs],
        scratch_shapes=[pltpu.VMEM((6, d, tn), _BF16), pltpu.VMEM((3, _SUBLANES, tn), _F32),
                        pltpu.VMEM((_SUBLANES, tn), _F32), pltpu.VMEM((3, tm, tn), _F32),
                        pltpu.VMEM((tm, tn), _F32), pltpu.VMEM((tm, tn), _F32)],
        compiler_params=_params("arbitrary", "arbitrary"),
    )(*args)


def _unit_lower_inverse_minus_identity(lmats, ri, ci, chunk):
    same8 = (ri >> 3) == (ci >> 3)
    d8 = [jnp.where(same8, m, 0.0) for m in lmats]
    d2 = [_dot(a, a) for a in d8]
    d3 = [_dot(a, b) for a, b in zip(d8, d2)]
    d4 = [_dot(b, b) for b in d2]
    x = [a + b + cc for a, b, cc in zip(d8, d2, d3)]
    xd4 = [_dot(a, b) for a, b in zip(x, d4)]
    x = [a + b + cc for a, b, cc in zip(x, d4, xd4)]
    size = 16
    while size <= chunk:
        sh = size.bit_length() - 1
        level = ((ri >> sh) == (ci >> sh)) & ((ri >> (sh - 1)) != (ci >> (sh - 1)))
        e = [jnp.where(level, m, 0.0) for m in lmats]
        y = [b + _dot(a, b) for a, b in zip(x, e)]
        x = [a + b + _dot(b, a) for a, b in zip(x, y)]
        size *= 2
    return x


def _wkv_kernel(*refs, chunk, nb, zero_init, n_chunks):
    (r_ref, lw_ref, k_ref, v_ref, kk_ref, kka_ref, g_ref, rk_ref, lnw_ref, lnb_ref) = refs[:10]
    if zero_init:
        o_ref, so_ref, s_scr = refs[10:]
        s0_ref = None
    else:
        s0_ref, o_ref, so_ref, s_scr = refs[10:]
    ci_grid = pl.program_id(1)
    g_dim = r_ref.shape[1]
    npair = g_dim // _PAIR
    c = chunk
    n2 = 2 * c

    lane = lax.broadcasted_iota(jnp.int32, (1, _PAIR), 1)
    m0 = (lane < _HEAD).astype(_F32)
    m1 = 1.0 - m0

    @pl.when(ci_grid == 0)
    def _init():
        if zero_init:
            s_scr[...] = jnp.zeros(s_scr.shape, _F32)
        else:
            zero = jnp.zeros((_HEAD, _HEAD), _F32)
            for u in range(nb):
                for p in range(npair):
                    top = jnp.concatenate([s0_ref[u, 2 * p], zero], axis=1)
                    bot = jnp.concatenate([zero, s0_ref[u, 2 * p + 1]], axis=1)
                    s_scr[u, p] = jnp.concatenate([top, bot], axis=0)

    ri = lax.broadcasted_iota(jnp.int32, (n2, n2), 0) & (c - 1)
    ci = lax.broadcasted_iota(jnp.int32, (n2, n2), 1) & (c - 1)
    strict = ri > ci
    incl = ri >= ci
    tri = (lax.broadcasted_iota(jnp.int32, (c, c), 0) >= lax.broadcasted_iota(jnp.int32, (c, c), 1)).astype(_BF16)
    seg_ones = _segment_ones(_SEG)
    rk = rk_ref[...]
    lnw = lnw_ref[...]
    lnb = lnb_ref[...]

    def stack(x):
        return jnp.concatenate([x * m0, x * m1], axis=0)

    def unit(u, carry):
        rows = slice(0, c) if nb == 1 else pl.ds(pl.multiple_of(u * c, c), c)
        lw = lw_ref[rows, :]
        l_hi, l_mid, l_lo = _split3(lw)
        cl = (jnp.dot(tri, l_hi, preferred_element_type=_F32)
              + jnp.dot(tri, l_mid, preferred_element_type=_F32)
              + jnp.dot(tri, l_lo, preferred_element_type=_F32))
        cl_end = cl[c - 1:c, :]
        r = r_ref[rows, :]
        k = k_ref[rows, :]
        v = v_ref[rows, :]
        kk = kk_ref[rows, :]
        kka = kka_ref[rows, :]
        e_neg = jnp.exp(-cl)
        e_tail = jnp.exp(cl_end - cl)
        p_end = jnp.exp(cl_end)
        rt = r * jnp.exp(cl)
        at = -(kk * jnp.exp(cl - lw))
        kt = k * e_neg
        bt = kka * e_neg
        kh = k * e_tail
        bh = kka * e_tail
        pairs = range(npair)
        sls = [slice(p * _PAIR, (p + 1) * _PAIR) for p in pairs]
        ar2 = [jnp.concatenate([stack(at[:, sl]), stack(rt[:, sl])], axis=0).astype(_BF16) for sl in sls]
        mb = [_dot_nt(a, stack(bt[:, sl])) for a, sl in zip(ar2, sls)]
        mk = [_dot_nt(a, stack(kt[:, sl])) for a, sl in zip(ar2, sls)]
        s_old = [s_scr[u, p] for p in pairs]
        uy0 = [_dot_nt(jnp.concatenate([at[:, sl], rt[:, sl]], axis=0), s)
               for sl, s in zip(sls, s_old)]
        x_inv = _unit_lower_inverse_minus_identity([jnp.where(strict, m[:n2], 0.0) for m in mb], ri, ci, c)
        vs = [stack(v[:, sl]).astype(_BF16) for sl in sls]
        ws = [stack(q[:c]) + _dot(jnp.where(strict, m[:n2], 0.0), vv) for q, m, vv in zip(uy0, mk, vs)]
        us = [w + _dot(xi, w) for xi, w in zip(x_inv, ws)]
        us = [q.astype(_BF16) for q in us]
        ysd = [_dot(jnp.where(incl, m1_[n2:], 0.0), uu) + _dot(jnp.where(incl, m2_[n2:], 0.0), vv)
               for m1_, m2_, uu, vv in zip(mb, mk, us, vs)]
        ys = [q[c:] + d[:c] + d[c:] for q, d in zip(uy0, ysd)]
        for p, sl in zip(pairs, sls):
            uv = jnp.concatenate([us[p], vs[p]], axis=0)
            bk = jnp.concatenate([stack(bh[:, sl]), stack(kh[:, sl])], axis=0)
            s_scr[u, p] = s_old[p] * p_end[:, sl] + _dot_tn(uv, bk)
        y = jnp.concatenate(ys, axis=1)
        mean = _head_sums(y, seg_ones) * (1.0 / _HEAD)
        dy = y - mean
        var = _head_sums(dy * dy, seg_ones) * (1.0 / _HEAD)
        yn = dy * lax.rsqrt(var + _GN_EPS) * lnw + lnb
        bonus = _head_sums(r * k * rk, seg_ones) * v
        o_ref[rows, :] = ((yn + bonus) * g_ref[rows, :]).astype(o_ref.dtype)
        return carry

    if nb == 1:
        unit(0, 0)
    else:
        lax.fori_loop(0, nb, unit, 0)

    @pl.when(ci_grid == n_chunks - 1)
    def _final():
        for u in range(nb):
            for p in range(npair):
                s = s_scr[u, p]
                so_ref[u, 2 * p] = s[:_HEAD, :_HEAD]
                so_ref[u, 2 * p + 1] = s[_HEAD:, _HEAD:]


def _wkv(l, seqs, chunk, nb, n_seq, n_chunks, state0, r_k3, ln_w3, ln_b3, *, row_block_of):
    g_dim = seqs[0].shape[1]
    n_heads = g_dim // _HEAD
    rows = seqs[0].shape[0]
    zero_init = state0 is None
    blk = pl.BlockSpec((nb * chunk, g_dim), lambda bi, ci: (row_block_of(bi, ci), 0))
    vec = pl.BlockSpec((None, 1, g_dim), lambda bi, ci: (l, 0, 0))
    st = pl.BlockSpec((nb, n_heads, _HEAD, _HEAD), lambda bi, ci: (bi, 0, 0, 0))
    in_specs = [blk] * 7 + [vec] * 3
    args = list(seqs) + [r_k3, ln_w3, ln_b3]
    if not zero_init:
        in_specs.append(st)
        args.append(state0)
    return pl.pallas_call(
        functools.partial(_wkv_kernel, chunk=chunk, nb=nb, zero_init=zero_init, n_chunks=n_chunks),
        grid=(n_seq // nb, n_chunks),
        in_specs=in_specs,
        out_specs=[blk, st],
        out_shape=[jax.ShapeDtypeStruct((n_seq * n_chunks * chunk, g_dim), _BF16),
                   jax.ShapeDtypeStruct((n_seq, n_heads, _HEAD, _HEAD), _F32)],
        scratch_shapes=[pltpu.VMEM((nb, g_dim // _PAIR, _PAIR, _PAIR), _F32)],
        compiler_params=_params("arbitrary", "arbitrary"),
    )(*args)


def _wkv_decode_kernel(r_ref, lw_ref, k_ref, v_ref, kk_ref, kka_ref, g_ref, rk_ref, lnw_ref, lnb_ref, s0_ref,
                       o_ref, so_ref, *, ts, nb):
    g_dim = r_ref.shape[2]
    npair = g_dim // _PAIR
    n = ts * nb
    n2 = 2 * n
    nb_bits = nb.bit_length() - 1

    lane = lax.broadcasted_iota(jnp.int32, (1, _PAIR), 1)
    m0 = (lane < _HEAD).astype(_F32)
    m1 = 1.0 - m0

    def stack(x):
        return jnp.concatenate([x * m0, x * m1], axis=0)

    def rows_tb(ref):
        return jnp.concatenate([ref[t] for t in range(ts)], axis=0)

    lw_t = [lw_ref[t] for t in range(ts)]
    cl_t = [lw_t[0]]
    for t in range(1, ts):
        cl_t.append(cl_t[-1] + lw_t[t])
    cl = jnp.concatenate(cl_t, axis=0)
    lw = jnp.concatenate(lw_t, axis=0)
    cl_end_b = cl_t[-1]
    cl_end = _tile_rows(cl_end_b, ts)
    p_end = jnp.exp(cl_end_b)
    r, k, v, kk, kka = (rows_tb(ref) for ref in (r_ref, k_ref, v_ref, kk_ref, kka_ref))
    e_neg = jnp.exp(-cl)
    e_tail = jnp.exp(cl_end - cl)
    rt = r * jnp.exp(cl)
    at = -(kk * jnp.exp(cl - lw))
    kt = k * e_neg
    bt = kka * e_neg
    kh = k * e_tail
    bh = kka * e_tail

    ri = lax.broadcasted_iota(jnp.int32, (n2, n2), 0)
    ci = lax.broadcasted_iota(jnp.int32, (n2, n2), 1)
    same_seq = (ri & (nb - 1)) == (ci & (nb - 1))
    t_r = (ri & (n - 1)) >> nb_bits
    t_c = (ci & (n - 1)) >> nb_bits
    strict = same_seq & (t_r > t_c)
    incl = same_seq & (t_r >= t_c)
    seq_of_row2 = lax.broadcasted_iota(jnp.int32, (n2, 1), 0) & (nb - 1)
    seq_of_row4 = lax.broadcasted_iota(jnp.int32, (2 * n2, 1), 0) & (nb - 1)
    zero = jnp.zeros((_HEAD, _HEAD), _F32)

    pairs = range(npair)
    sls = [slice(p * _PAIR, (p + 1) * _PAIR) for p in pairs]
    ar2 = [jnp.concatenate([stack(at[:, sl]), stack(rt[:, sl])], axis=0).astype(_BF16) for sl in sls]
    mb = [_dot_nt(a, stack(bt[:, sl])) for a, sl in zip(ar2, sls)]
    mk = [_dot_nt(a, stack(kt[:, sl])) for a, sl in zip(ar2, sls)]
    lab = [jnp.where(strict, m[:n2], 0.0) for m in mb]
    d2 = [_dot(a, a) for a in lab]
    d3 = [_dot(a, b) for a, b in zip(lab, d2)]
    x_inv = [a + b + cc for a, b, cc in zip(lab, d2, d3)]

    def block_diag(b, p):
        top = jnp.concatenate([s0_ref[b, 2 * p], zero], axis=1)
        bot = jnp.concatenate([zero, s0_ref[b, 2 * p + 1]], axis=1)
        return jnp.concatenate([top, bot], axis=0)

    ys = []
    for p, sl in zip(pairs, sls):
        s_b = [block_diag(b, p) for b in range(nb)]
        ar = jnp.concatenate([at[:, sl], rt[:, sl]], axis=0)
        ar_cat = jnp.concatenate([jnp.where(seq_of_row2 == b, ar, 0.0).astype(_BF16) for b in range(nb)], axis=1)
        s_cat = jnp.concatenate([s.astype(_BF16) for s in s_b], axis=1)
        uy0 = _dot_nt(ar_cat, s_cat)
        vs = stack(v[:, sl]).astype(_BF16)
        ws = stack(uy0[:n]) + _dot(jnp.where(strict, mk[p][:n2], 0.0), vs)
        us = (ws + _dot(x_inv[p], ws)).astype(_BF16)
        ysd = _dot(jnp.where(incl, mb[p][n2:], 0.0), us) + _dot(jnp.where(incl, mk[p][n2:], 0.0), vs)
        ys.append(uy0[n:] + ysd[:n] + ysd[n:])
        uv = jnp.concatenate([us, vs], axis=0)
        uv_cat = jnp.concatenate([jnp.where(seq_of_row4 == b, uv, jnp.zeros_like(uv)) for b in range(nb)], axis=1)
        bk = jnp.concatenate([stack(bh[:, sl]), stack(kh[:, sl])], axis=0)
        upd = _dot_tn(uv_cat, bk)
        for b in range(nb):
            s_new = (s_b[b] * p_end[b:b + 1, sl] + upd[b * _PAIR:(b + 1) * _PAIR, :])
            so_ref[b, 2 * p] = s_new[:_HEAD, :_HEAD]
            so_ref[b, 2 * p + 1] = s_new[_HEAD:, _HEAD:]

    y = jnp.concatenate(ys, axis=1)
    seg_ones = _segment_ones(_SEG)
    mean = _head_sums(y, seg_ones) * (1.0 / _HEAD)
    dy = y - mean
    var = _head_sums(dy * dy, seg_ones) * (1.0 / _HEAD)
    yn = dy * lax.rsqrt(var + _GN_EPS) * lnw_ref[...] + lnb_ref[...]
    bonus = _head_sums(r * k * rk_ref[...], seg_ones) * v
    out = ((yn + bonus) * rows_tb(g_ref)).astype(o_ref.dtype)
    for t in range(ts):
        o_ref[t] = out[t * nb:(t + 1) * nb, :]


def _wkv_decode(l, seqs, state0, r_k3, ln_w3, ln_b3, *, ts, db, nb, first_block):
    g_dim = seqs[0].shape[2]
    n_heads = g_dim // _HEAD
    blk = pl.BlockSpec((ts, nb, g_dim), lambda bi: (first_block, bi, 0))
    vec = pl.BlockSpec((None, 1, g_dim), lambda bi: (l, 0, 0))
    st = pl.BlockSpec((nb, n_heads, _HEAD, _HEAD), lambda bi: (bi, 0, 0, 0))
    return pl.pallas_call(
        functools.partial(_wkv_decode_kernel, ts=ts, nb=nb),
        grid=(db // nb,),
        in_specs=[blk] * 7 + [vec] * 3 + [st],
        out_specs=[pl.BlockSpec((ts, nb, g_dim), lambda bi: (0, bi, 0)), st],
        out_shape=[jax.ShapeDtypeStruct((ts, db, g_dim), _BF16),
                   jax.ShapeDtypeStruct((db, n_heads, _HEAD, _HEAD), _F32)],
        compiler_params=_params("arbitrary"),
    )(*seqs, r_k3, ln_w3, ln_b3, state0)


def _cast_kernel(w_ref, o_ref):
    o_ref[...] = w_ref[...].astype(o_ref.dtype)


def _cast_bf16(w, rows_per_block):
    depth, k, n = w.shape
    spec = pl.BlockSpec((1, rows_per_block, n), lambda l, i: (l, i, 0))
    return pl.pallas_call(
        _cast_kernel, grid=(depth, k // rows_per_block), in_specs=[spec], out_specs=spec,
        out_shape=jax.ShapeDtypeStruct(w.shape, _BF16),
        compiler_params=_params("arbitrary", "arbitrary"),
    )(w)


def _outproj_kernel(op_ref, os_ref, oc_ref, xp_ref, xs_ref, ga_ref, sh_ref, sc_ref, g_ref, w_ref,
                    x1_ref, h2_ref, *, n_pt, tps, db):
    i = pl.program_id(0)
    tm = xp_ref.shape[0]
    ts = tm // db
    g_dim = op_ref.shape[1]

    o_rw = jnp.where(i < n_pt, op_ref[...], os_ref[...])
    acc = (jnp.dot(o_rw, w_ref[0, :g_dim, :], preferred_element_type=_F32)
           + jnp.dot(oc_ref[...], w_ref[0, g_dim:, :], preferred_element_type=_F32))

    def finish(ga, sc, sh):
        x1 = jnp.where(i < n_pt, xp_ref[...], xs_ref[...]) + ga * acc
        x1_ref[...] = x1
        xn = x1 * lax.rsqrt(jnp.mean(x1 * x1, axis=-1, keepdims=True) + _RMS_EPS) * g_ref[0, 0]
        h2_ref[...] = (xn * (1.0 + sc) + sh).astype(_BF16)

    @pl.when(i < n_pt)
    def _prompt():
        finish(_mod_row(ga_ref, i, tps, db), _mod_row(sc_ref, i, tps, db), _mod_row(sh_ref, i, tps, db))

    @pl.when(i >= n_pt)
    def _decode():
        finish(_mod_tile(ga_ref, db, ts), _mod_tile(sc_ref, db, ts), _mod_tile(sh_ref, db, ts))


def _outproj(l, o_p, o_s, o_conv, x, mod, norm_g4, w_out, *, tm, n_pt, tps, db):
    d = x.prompt.shape[1]
    rows = (n_pt + 1) * tm
    g_dim = o_p.shape[1]
    bc = mod.shape[2]
    modspec = lambda comp: pl.BlockSpec((1, 1, bc, d), lambda i: (l, comp, 0, 0))
    return pl.pallas_call(
        functools.partial(_outproj_kernel, n_pt=n_pt, tps=tps, db=db),
        grid=(rows // tm,),
        in_specs=[pl.BlockSpec((tm, g_dim), lambda i: (jnp.minimum(i, n_pt - 1), 0)),
                  pl.BlockSpec((tm, g_dim), lambda i: (0, 0)),
                  pl.BlockSpec((tm, o_conv.shape[1]), lambda i: (i, 0))] + _act_specs(x, tm, n_pt) + [
                  modspec(2), modspec(3), modspec(4),
                  pl.BlockSpec((1, 1, 1, d), lambda i: (l, 1, 0, 0)),
                  pl.BlockSpec((1, d, d), lambda i: (l, 0, 0), pipeline_mode=pl.Buffered(1))],
        out_specs=[pl.BlockSpec((tm, d), lambda i: (i, 0)), pl.BlockSpec((tm, d), lambda i: (i, 0))],
        out_shape=[jax.ShapeDtypeStruct((rows, d), _F32), jax.ShapeDtypeStruct((rows, d), _BF16)],
        compiler_params=_params("arbitrary"),
    )(o_p, o_s, o_conv, x.prompt, x.decode, mod, mod, mod, norm_g4, w_out)


def _ffn_kernel(*refs, n_pt, tps, db, nj, final_norm):
    it = iter(refs)
    (h2_ref, wa_ref, wb_ref, cwa_ref, cwb_ref, wd_ref, x1_ref, ga_ref, sfa_ref, sfb_ref) = (next(it) for _ in range(10))
    fg_ref = next(it) if final_norm else None
    x2_ref = next(it)
    x2s_ref = next(it) if final_norm else x2_ref
    fpa_o, fpb_o, fsa_o, fsb_o = (next(it) for _ in range(4))
    acc_scr, ca_scr, cb_scr = (next(it) for _ in range(3))
    i = pl.program_id(0)
    j = pl.program_id(1)
    tm = h2_ref.shape[0]
    ts = tm // db

    @pl.when(j == 0)
    def _zero():
        acc_scr[...] = jnp.zeros(acc_scr.shape, _F32)

    def prompt_body():
        half = tm // 2
        halves = (slice(0, half), slice(half, tm))
        us = [[jnp.dot(h2_ref[rs, :], w_ref[0], preferred_element_type=_F32) for w_ref in (wa_ref, wb_ref)]
              for rs in halves]
        for r, rs in enumerate(halves):
            conv = []
            for idx, (cw_ref, c_scr, fp_o) in enumerate(((cwa_ref, ca_scr, fpa_o), (cwb_ref, cb_scr, fpb_o))):
                u = us[r][idx]
                if r == 0:
                    u1, u2 = _shift_rows_prompt(u, c_scr[j], (i % tps) == 0)
                else:
                    u1, u2 = _shift_rows_prompt(u, us[0][idx][half - _SUBLANES:, :], False)
                    c_scr[j] = u[half - _SUBLANES:, :]
                    fp_o[0] = u[half - _SUBLANES:, :]
                cw = cw_ref[0]
                conv.append(u2 * cw[0:1] + u1 * cw[1:2] + u * cw[2:3])
            gact = ((conv[0] * jax.nn.sigmoid(conv[0])) * conv[1]).astype(_BF16)
            acc_scr[rs, :] += jnp.dot(gact, wd_ref[0], preferred_element_type=_F32)

    def body(prompt):
        if prompt:
            return prompt_body()
        first = (i % tps) == 0
        h2 = h2_ref[...]
        ua = jnp.dot(h2, wa_ref[0], preferred_element_type=_F32)
        ub = jnp.dot(h2, wb_ref[0], preferred_element_type=_F32)
        conv = []
        for u, cw_ref, c_scr, sf, fp_o, fs_o in ((ua, cwa_ref, ca_scr, sfa_ref, fpa_o, fsa_o),
                                                 (ub, cwb_ref, cb_scr, sfb_ref, fpb_o, fsb_o)):
            if prompt:
                u1, u2 = _shift_rows_prompt(u, c_scr[j], first)
                c_scr[j] = u[tm - _SUBLANES:, :]
                fp_o[0] = u[tm - _SUBLANES:, :]
            else:
                u1 = jnp.concatenate([sf[0, 1], u[:tm - db, :]], axis=0)
                u2 = jnp.concatenate([sf[0, 0], sf[0, 1], u[:tm - 2 * db, :]], axis=0)
                fp_o[0] = jnp.zeros(fp_o.shape[1:], _F32)
                fs_o[0] = u[tm - 2 * db:tm - db, :]
                fs_o[1] = u[tm - db:, :]
            cw = cw_ref[0]
            conv.append(u2 * cw[0:1] + u1 * cw[1:2] + u * cw[2:3])
        gact = ((conv[0] * jax.nn.sigmoid(conv[0])) * conv[1]).astype(_BF16)
        acc_scr[...] += jnp.dot(gact, wd_ref[0], preferred_element_type=_F32)

    @pl.when(i < n_pt)
    def _prompt():
        body(True)

    @pl.when(i >= n_pt)
    def _decode():
        body(False)

    def finish(ga, out_ref):
        x2 = x1_ref[...] + ga * acc_scr[...]
        if final_norm:
            x2 = x2 * lax.rsqrt(jnp.mean(x2 * x2, axis=-1, keepdims=True) + _RMS_EPS) * fg_ref[...]
        out_ref[...] = x2

    @pl.when((j == nj - 1) & (i < n_pt))
    def _finish_prompt():
        finish(_mod_row(ga_ref, i, tps, db), x2_ref)

    @pl.when((j == nj - 1) & (i >= n_pt))
    def _finish_decode():
        finish(_mod_tile(ga_ref, db, ts), x2s_ref)


def _ffn(l, h2, x1, mod, ffn_up, ffn_conv, ffn_down, state_ffn_t, final_g, *, tm, tn, n_pt, tps, db, bp):
    rows, d = x1.shape
    f = ffn_down.shape[1]
    nj = f // tn
    bc = mod.shape[2]
    final_norm = final_g is not None
    in_specs = [pl.BlockSpec((tm, d), lambda i, j: (i, 0)),
                pl.BlockSpec((1, d, tn), lambda i, j: (l, 0, j)),
                pl.BlockSpec((1, d, tn), lambda i, j: (l, 0, nj + j)),
                pl.BlockSpec((1, 3, tn), lambda i, j: (l, 0, j)),
                pl.BlockSpec((1, 3, tn), lambda i, j: (l, 0, nj + j)),
                pl.BlockSpec((1, tn, d), lambda i, j: (l, j, 0)),
                pl.BlockSpec((tm, d), lambda i, j: (i, 0)),
                pl.BlockSpec((1, 1, bc, d), lambda i, j: (l, 5, 0, 0)),
                pl.BlockSpec((1, 2, db, tn), lambda i, j: (l, 0, 0, j)),
                pl.BlockSpec((1, 2, db, tn), lambda i, j: (l, 0, 0, nj + j))]
    args = [h2, ffn_up, ffn_up, ffn_conv, ffn_conv, ffn_down, x1, mod, state_ffn_t, state_ffn_t]
    if final_norm:
        in_specs.append(pl.BlockSpec((1, d), lambda i, j: (0, 0)))
        args.append(final_g)
    n_tiles = rows // tm
    pstate = pl.BlockSpec((1, _SUBLANES, tn), lambda i, j: (i, 0, j))
    sstate = pl.BlockSpec((2, db, tn), lambda i, j: (0, 0, jnp.where(i >= n_pt, j, 0)))
    if final_norm:
        x_specs = [pl.BlockSpec((tm, d), lambda i, j: (jnp.minimum(i, n_pt - 1), 0)),
                   pl.BlockSpec((tm, d), lambda i, j: (0, 0))]
        x_shapes = [jax.ShapeDtypeStruct((n_pt * tm, d), _F32), jax.ShapeDtypeStruct((tm, d), _F32)]
    else:
        x_specs = [pl.BlockSpec((tm, d), lambda i, j: (i, 0))]
        x_shapes = [jax.ShapeDtypeStruct((rows, d), _F32)]
    return pl.pallas_call(
        functools.partial(_ffn_kernel, n_pt=n_pt, tps=tps, db=db, nj=nj, final_norm=final_norm),
        grid=(rows // tm, nj),
        in_specs=in_specs,
        out_specs=x_specs + [pstate, pstate, sstate, sstate],
        out_shape=x_shapes + [
                   jax.ShapeDtypeStruct((n_tiles, _SUBLANES, f), _F32),
                   jax.ShapeDtypeStruct((n_tiles, _SUBLANES, f), _F32),
                   jax.ShapeDtypeStruct((2, db, f), _F32), jax.ShapeDtypeStruct((2, db, f), _F32)],
        scratch_shapes=[pltpu.VMEM((tm, d), _F32), pltpu.VMEM((nj, _SUBLANES, tn), _F32),
                        pltpu.VMEM((nj, _SUBLANES, tn), _F32)],
        compiler_params=_params("arbitrary", "arbitrary"),
    )(*args)


def _forward(x_prompt, x_sample, c_prompt, c_sample, state_wkv, state_shift, state_conv, state_ffn,
             ada_w, ada_b, norm_g, final_norm_g, w_in, mu_x, mu_rkv, decay_w0, decay_lora1, decay_lora2,
             iclr_a0, iclr_lora1, iclr_lora2, gate_lora1, gate_lora2, vres_v0, vres_lora1, vres_lora2,
             k_k, k_a, r_k, ln_x_w, ln_x_b, conv_w, w_out, ffn_up, ffn_conv, ffn_down,
             *, chunk=64, mix_tn=256, ffn_tn=512, wkv_nb=16):
    bp, t_len, d = x_prompt.shape
    db, ts, _ = x_sample.shape
    depth = ada_w.shape[0]
    g_dim = mu_rkv.shape[2]
    n_heads = g_dim // _HEAD
    f = ffn_down.shape[1]
    tm = ts * db
    assert t_len % tm == 0 and t_len % chunk == 0 and ts >= 2 and db % _SUBLANES == 0 and tm % chunk == 0
    tps = t_len // tm
    n_pt = bp * tps
    n_prompt_rows = bp * t_len
    assert ts == 4 and db % wkv_nb == 0 and wkv_nb & (wkv_nb - 1) == 0

    x = _Act(x_prompt.reshape(n_prompt_rows, d), x_sample.transpose(1, 0, 2).reshape(tm, d), 0)
    pad =(-(db + bp)) % _SUBLANES
    c_all = jnp.concatenate([c_sample, c_prompt, jnp.zeros((pad, d), _F32)], axis=0)
    mod = _adaln_mod(c_all, ada_w, ada_b)

    norm_g4 = norm_g.reshape(depth, 2, 1, d)
    vec3 = lambda a: a.reshape(a.shape[0], 1, a.shape[1])
    w0_3, a0_3, v0_3, kk_3, ka_3 = vec3(decay_w0), vec3(iclr_a0), vec3(vres_v0), vec3(k_k), vec3(k_a)
    rk_3, lnw_3, lnb_3 = r_k.reshape(depth, 1, g_dim), vec3(ln_x_w), vec3(ln_x_b)
    state_conv_t = state_conv.transpose(0, 2, 1, 3)
    state_ffn_t = state_ffn.transpose(0, 2, 1, 3)
    tiles = dict(tm=tm, n_pt=n_pt, tps=tps, db=db)
    w_out = _cast_bf16(w_out, 512)
    ffn_up = _cast_bf16(ffn_up, 128)
    ffn_down = _cast_bf16(ffn_down, 512)

    new = {k: [] for k in ("wkv_p", "shift_p", "conv_p", "ffn_p", "wkv_s", "shift_s", "conv_s", "ffn_s")}
    v_first = None
    for l in range(depth):
        outs = _norm_lora(l, x, mod, norm_g4, mu_x, state_shift, decay_lora1, iclr_lora1, gate_lora1,
                          vres_lora1, bp=bp, **tiles)
        if l > 0:
            h, aw, aa, ag, av, hp_last, hs_last = outs
        else:
            h, aw, aa, ag, hp_last, hs_last = outs
            av = None
        mouts = _mix(l, h, (aw, aa, ag, av), v_first, w_in,
                     (decay_lora2, iclr_lora2, gate_lora2, vres_lora2), (w0_3, a0_3, v0_3),
                     mu_rkv, kk_3, ka_3, conv_w, state_shift, state_conv_t, tn=mix_tn, bp=bp, **tiles)
        if l > 0:
            r, lw, k, v, kk, kka, gate, o_conv, conv_p8, conv_s = mouts
        else:
            r, lw, k, v, kk, kka, gate, o_conv, v_first, conv_p8, conv_s = mouts
        seqs = (r, lw, k, v, kk, kka, gate)

        cpt = t_len // chunk
        o_p, wkv_p = _wkv(l, seqs, chunk, 1, bp, cpt, None, rk_3, lnw_3, lnb_3,
                          row_block_of=lambda bi, ci: bi * cpt + ci)

        dseqs = tuple(a.reshape(a.shape[0] // db, db, g_dim) for a in seqs)
        o_s, wkv_s = _wkv_decode(l, dseqs, state_wkv[l], rk_3, lnw_3, lnb_3, ts=ts, db=db, nb=wkv_nb,
                                 first_block=n_prompt_rows // tm)
        o_s = o_s.reshape(tm, g_dim)

        x1, h2 = _outproj(l, o_p, o_s, o_conv, x, mod, norm_g4, w_out, **tiles)
        final_g = final_norm_g.reshape(1, d) if l == depth - 1 else None
        *x_out, fpa, fpb, fsa, fsb = _ffn(l, h2, x1, mod, ffn_up, ffn_conv, ffn_down, state_ffn_t, final_g,
                                          tn=ffn_tn, bp=bp, **tiles)
        x = _Act(x_out[0], x_out[0], n_pt) if len(x_out) == 1 else _Act(x_out[0], x_out[1], 0)

        new["wkv_p"].append(wkv_p)
        new["wkv_s"].append(wkv_s)
        new["shift_p"].append(hp_last[:, _SUBLANES - 1])
        new["shift_s"].append(hs_last)
        new["conv_p"].append(conv_p8[:, _SUBLANES - 2:])
        new["conv_s"].append(conv_s.transpose(1, 0, 2))
        last_tiles = slice(tps - 1, n_pt, tps)
        new["ffn_p"].append(jnp.concatenate([fpa[last_tiles, _SUBLANES - 2:], fpb[last_tiles, _SUBLANES - 2:]],
                                            axis=-1))
        new["ffn_s"].append(jnp.concatenate([fsa, fsb], axis=-1).transpose(1, 0, 2))

    y_prompt = x.prompt.reshape(bp, t_len, d)
    y_sample = x.decode.reshape(ts, db, d).transpose(1, 0, 2)
    st = {k: jnp.stack(vs) for k, vs in new.items()}
    return (y_prompt, y_sample, st["wkv_p"], st["shift_p"], st["conv_p"], st["ffn_p"],
            st["wkv_s"], st["shift_s"], st["conv_s"], st["ffn_s"])


def kernel(x_prompt, x_sample, c_prompt, c_sample, state_wkv, state_shift, state_conv, state_ffn, ada_w, ada_b, norm_g, final_norm_g, w_in, mu_x, mu_rkv, decay_w0, decay_lora1, decay_lora2, iclr_a0, iclr_lora1, iclr_lora2, gate_lora1, gate_lora2, vres_v0, vres_lora1, vres_lora2, k_k, k_a, r_k, ln_x_w, ln_x_b, conv_w, w_out, ffn_up, ffn_conv, ffn_down):
    return _forward(x_prompt, x_sample, c_prompt, c_sample, state_wkv, state_shift, state_conv, state_ffn,
                    ada_w, ada_b, norm_g, final_norm_g, w_in, mu_x, mu_rkv, decay_w0, decay_lora1, decay_lora2,
                    iclr_a0, iclr_lora1, iclr_lora2, gate_lora1, gate_lora2, vres_v0, vres_lora1, vres_lora2,
                    k_k, k_a, r_k, ln_x_w, ln_x_b, conv_w, w_out, ffn_up, ffn_conv, ffn_down)
```

```python
import functools
from typing import NamedTuple

import jax
import jax.numpy as jnp
from jax import lax
from jax.experimental import pallas as pl
from jax.experimental.pallas import tpu as pltpu

_F32 = jnp.float32
_BF16 = jnp.bfloat16

_HEAD = 64
_PAIR = 2 * _HEAD
_SEG = 256
_RMS_EPS = 1e-6
_GN_EPS = 64e-5
_VMEM_LIMIT_BYTES = 56 * 1024 * 1024
_SUBLANES = 8


def _params(*sem):
    return pltpu.CompilerParams(dimension_semantics=sem, vmem_limit_bytes=_VMEM_LIMIT_BYTES)


def _dot(a, b):
    return jnp.dot(a.astype(_BF16), b.astype(_BF16), preferred_element_type=_F32)


def _dot_nt(a, b):
    return lax.dot_general(a.astype(_BF16), b.astype(_BF16), (((1,), (1,)), ((), ())),
                           preferred_element_type=_F32)


def _dot_tn(a, b):
    return lax.dot_general(a.astype(_BF16), b.astype(_BF16), (((0,), (0,)), ((), ())),
                           preferred_element_type=_F32)


def _split2(x):
    hi = x.astype(_BF16)
    lo = (x - hi.astype(_F32)).astype(_BF16)
    return hi, lo


def _split3(x):
    hi = x.astype(_BF16)
    r1 = x - hi.astype(_F32)
    mid = r1.astype(_BF16)
    lo = (r1 - mid.astype(_F32)).astype(_BF16)
    return hi, mid, lo


def _dot_exact_rhs(x, m_bf16):
    hi, lo = _split2(x)
    return (jnp.dot(hi, m_bf16, preferred_element_type=_F32)
            + jnp.dot(lo, m_bf16, preferred_element_type=_F32))


def _segment_ones(n):
    r = lax.broadcasted_iota(jnp.int32, (n, n), 0) // _HEAD
    c = lax.broadcasted_iota(jnp.int32, (n, n), 1) // _HEAD
    return (r == c).astype(_BF16)


def _head_sums(x, seg_ones):
    n = seg_ones.shape[0]
    parts = [_dot_exact_rhs(x[:, q * n:(q + 1) * n], seg_ones) for q in range(x.shape[1] // n)]
    return parts[0] if len(parts) == 1 else jnp.concatenate(parts, axis=1)


def _tile_rows(m, reps):
    return m if reps == 1 else jnp.concatenate([m] * reps, axis=0)


def _shift_rows_prompt(x, carry8, first, two=True):
    r8 = lax.broadcasted_iota(jnp.int32, (_SUBLANES, 1), 0)
    l1 = jnp.where(first, 0.0, carry8[_SUBLANES - 1:_SUBLANES, :])
    x1 = pltpu.roll(x, 1, 0)
    x1 = jnp.concatenate([jnp.where(r8 == 0, l1, x1[:_SUBLANES, :]), x1[_SUBLANES:, :]], axis=0)
    if not two:
        return x1
    l2 = jnp.where(first, 0.0, carry8[_SUBLANES - 2:_SUBLANES - 1, :])
    x2 = pltpu.roll(x, 2, 0)
    top2 = jnp.where(r8 == 0, l2, jnp.where(r8 == 1, l1, x2[:_SUBLANES, :]))
    return x1, jnp.concatenate([top2, x2[_SUBLANES:, :]], axis=0)


class _Act(NamedTuple):
    prompt: jax.Array
    decode: jax.Array
    decode_block: int


def _act_specs(x, tm, n_pt):
    d = x.prompt.shape[1]
    return [pl.BlockSpec((tm, d), lambda i, *_: (jnp.minimum(i, n_pt - 1), 0)),
            pl.BlockSpec((tm, d), lambda i, *_: (x.decode_block, 0))]


def _mod_row(ref, i, tps, db):
    return ref[0, 0, pl.ds(db + i // tps, 1), :]


def _mod_tile(ref, db, ts):
    return _tile_rows(ref[0, 0, 0:db, :], ts)


def _mod_kernel(c_ref, w_ref, b_ref, o_ref):
    c = c_ref[...]
    s_hi, s_lo = _split2(c * jax.nn.sigmoid(c))
    w_hi, w_lo = _split2(w_ref[0])
    acc = (jnp.dot(s_hi, w_hi, preferred_element_type=_F32)
           + jnp.dot(s_hi, w_lo, preferred_element_type=_F32)
           + jnp.dot(s_lo, w_hi, preferred_element_type=_F32))
    o_ref[0, 0] = acc + b_ref[0]


def _adaln_mod(c_all, ada_w, ada_b):
    depth, d, six_d = ada_w.shape
    bc = c_all.shape[0]
    tn = 1024
    per = d // tn
    return pl.pallas_call(
        _mod_kernel,
        grid=(depth, six_d // tn),
        in_specs=[pl.BlockSpec((bc, d), lambda l, n: (0, 0)),
                  pl.BlockSpec((1, d, tn), lambda l, n: (l, 0, n)),
                  pl.BlockSpec((1, 1, tn), lambda l, n: (l, 0, n))],
        out_specs=pl.BlockSpec((1, 1, bc, tn), lambda l, n: (l, n // per, 0, n % per)),
        out_shape=jax.ShapeDtypeStruct((depth, 6, bc, d), _F32),
        compiler_params=_params("arbitrary", "arbitrary"),
    )(c_all, ada_w, ada_b.reshape(depth, 1, six_d))


def _norm_lora_kernel(*refs, n_pt, tps, db, has_v):
    if has_v:
        (xp_ref, xs_ref, sh_ref, sc_ref, g_ref, mu_ref, hl_ref, d1_ref, i1_ref, g1_ref, v1_ref,
         h_ref, aw_ref, aa_ref, ag_ref, av_ref, hp_ref, hs_ref, carry_ref, hcur_ref, hprev_ref) = refs
    else:
        (xp_ref, xs_ref, sh_ref, sc_ref, g_ref, mu_ref, hl_ref, d1_ref, i1_ref, g1_ref,
         h_ref, aw_ref, aa_ref, ag_ref, hp_ref, hs_ref, carry_ref, hcur_ref, hprev_ref) = refs
    i = pl.program_id(0)
    tm = xp_ref.shape[0]
    ts = tm // db
    x = jnp.where(i < n_pt, xp_ref[...], xs_ref[...])
    xn = x * lax.rsqrt(jnp.mean(x * x, axis=-1, keepdims=True) + _RMS_EPS) * g_ref[0, 0]

    @pl.when(i < n_pt)
    def _prompt():
        h = xn * (1.0 + _mod_row(sc_ref, i, tps, db)) + _mod_row(sh_ref, i, tps, db)
        first = (i % tps) == 0
        hprev_ref[...] = _shift_rows_prompt(h, carry_ref[...], first, two=False)
        hcur_ref[...] = h
        carry_ref[...] = h[tm - _SUBLANES:, :]
        hp_ref[0] = h[tm - _SUBLANES:, :]

    @pl.when(i >= n_pt)
    def _decode():
        h = xn * (1.0 + _mod_tile(sc_ref, db, ts)) + _mod_tile(sh_ref, db, ts)
        hprev_ref[...] = jnp.concatenate([hl_ref[0], h[:tm - db, :]], axis=0)
        hcur_ref[...] = h
        hs_ref[...] = h[tm - db:, :]

    h = hcur_ref[...]
    xx = hprev_ref[...] - h
    mu = mu_ref[0]
    h_ref[...] = h.astype(_BF16)
    aw_ref[...] = jnp.tanh(_dot(h + xx * mu[0:1], d1_ref[0]))
    aa_ref[...] = _dot(h + xx * mu[1:2], i1_ref[0])
    ag_ref[...] = jax.nn.sigmoid(_dot(h + xx * mu[2:3], g1_ref[0]))
    if has_v:
        av_ref[...] = _dot(h + xx * mu[3:4], v1_ref[0])


def _norm_lora(l, x, mod, norm_g4, mu_x, state_shift, decay_lora1, iclr_lora1, gate_lora1, vres_lora1,
               *, tm, n_pt, tps, db, bp):
    d = x.prompt.shape[1]
    rows = (n_pt + 1) * tm
    has_v = l > 0
    bc = mod.shape[2]
    ld, la, lg = decay_lora1.shape[2], iclr_lora1.shape[2], gate_lora1.shape[2]
    full = lambda *shape: pl.BlockSpec(shape, lambda i: (l,) + (0,) * (len(shape) - 1))
    in_specs = _act_specs(x, tm, n_pt) + [
                pl.BlockSpec((1, 1, bc, d), lambda i: (l, 0, 0, 0)),
                pl.BlockSpec((1, 1, bc, d), lambda i: (l, 1, 0, 0)),
                pl.BlockSpec((1, 1, 1, d), lambda i: (l, 0, 0, 0)),
                full(1, 4, d), full(1, db, d), full(1, d, ld), full(1, d, la), full(1, d, lg)]
    args = [x.prompt, x.decode, mod, mod, norm_g4, mu_x, state_shift, decay_lora1, iclr_lora1, gate_lora1]
    row_out = lambda n, dt: (pl.BlockSpec((tm, n), lambda i: (i, 0)), jax.ShapeDtypeStruct((rows, n), dt))
    outs = [row_out(d, _BF16), row_out(ld, _F32), row_out(la, _F32), row_out(lg, _F32)]
    if has_v:
        lv = vres_lora1.shape[2]
        in_specs.append(pl.BlockSpec((1, d, lv), lambda i: (l - 1, 0, 0)))
        args.append(vres_lora1)
        outs.append(row_out(lv, _F32))
    outs.append((pl.BlockSpec((1, _SUBLANES, d), lambda i: (jnp.minimum(i // tps, bp - 1), 0, 0)),
                 jax.ShapeDtypeStruct((bp, _SUBLANES, d), _F32)))
    outs.append((pl.BlockSpec((db, d), lambda i: (0, 0)), jax.ShapeDtypeStruct((db, d), _F32)))
    return pl.pallas_call(
        functools.partial(_norm_lora_kernel, n_pt=n_pt, tps=tps, db=db, has_v=has_v),
        grid=(rows // tm,),
        in_specs=in_specs,
        out_specs=[o[0] for o in outs],
        out_shape=[o[1] for o in outs],
        scratch_shapes=[pltpu.VMEM((_SUBLANES, d), _F32), pltpu.VMEM((tm, d), _F32), pltpu.VMEM((tm, d), _F32)],
        compiler_params=_params("arbitrary"),
    )(*args)


def _mix_kernel(*refs, n_pt, tps, db, has_v):
    it = iter(refs)
    h_ref = next(it)
    w_refs = [next(it) for _ in range(6)]
    aw_ref, aa_ref, ag_ref = next(it), next(it), next(it)
    av_ref = next(it) if has_v else None
    d2_ref, i2_ref, g2_ref = next(it), next(it), next(it)
    v2_ref = next(it) if has_v else None
    w0_ref, a0_ref = next(it), next(it)
    v0_ref = next(it) if has_v else None
    mu_ref, kkw_ref, kaw_ref, cw_ref, hl_ref, cs_ref = (next(it) for _ in range(6))
    vf_ref = next(it) if has_v else None
    r_o, lw_o, k_o, v_o, kk_o, kka_o, g_o, oc_o = (next(it) for _ in range(8))
    vf_o = None if has_v else next(it)
    cp_o, cso_o = next(it), next(it)
    wc_scr, pcarry, zcarry = (next(it) for _ in range(3))

    i = pl.program_id(1)
    tm = h_ref.shape[0]
    tn = r_o.shape[1]

    @pl.when(i == 0)
    def _cache_weights():
        for g in range(6):
            wc_scr[g] = w_refs[g][0].astype(_BF16)

    def project(rs):
        h = h_ref[rs, :]
        return [jnp.dot(h, wc_scr[g], preferred_element_type=_F32) for g in range(6)]

    def tail(rs, p, pprev, bg, z, z1, z2):
        mu = mu_ref[0]
        r = p[0] + (pprev[0] - p[0]) * mu[0:1]
        k = p[1] + (pprev[1] - p[1]) * mu[1:2]
        v = p[2] + (pprev[2] - p[2]) * mu[2:3]

        zlog = w0_ref[0] + _dot(aw_ref[rs, :], d2_ref[0])
        softplus = jnp.maximum(-zlog, 0.0) + jnp.log(1.0 + jnp.exp(-jnp.abs(zlog)))
        lw_o[rs, :] = -jnp.exp(-softplus - 0.5)
        a = jax.nn.sigmoid(a0_ref[0] + _dot(aa_ref[rs, :], i2_ref[0]))
        g_o[rs, :] = _dot(ag_ref[rs, :], g2_ref[0])
        if has_v:
            nu = jax.nn.sigmoid(v0_ref[0] + _dot(av_ref[rs, :], v2_ref[0]))
            v = v + (vf_ref[rs, :] - v) * nu
        else:
            vf_o[rs, :] = v

        kk = k * kkw_ref[0]
        norm = jnp.sqrt(_head_sums(kk * kk, _segment_ones(tn)))
        kk = kk / jnp.maximum(norm, 1e-12)
        r_o[rs, :] = r
        k_o[rs, :] = k * (1.0 + (a - 1.0) * kaw_ref[0])
        v_o[rs, :] = v
        kk_o[rs, :] = kk
        kka_o[rs, :] = kk * a

        cw = cw_ref[0]
        zc = z2 * cw[0:1] + z1 * cw[1:2] + z * cw[2:3]
        oc_o[rs, :] = (bg * zc).astype(_BF16)

    @pl.when(i < n_pt)
    def _prompt():
        half = tm // 2
        halves = (slice(0, half), slice(half, tm))
        proj = [project(rs) for rs in halves]
        zs = [q[4] * q[5] for q in proj]
        for idx, rs in enumerate(halves):
            p, z = proj[idx][:3], zs[idx]
            if idx == 0:
                first = (i % tps) == 0
                p_carry = [pcarry[g] for g in range(3)]
                z_carry = zcarry[...]
            else:
                first = False
                p_carry = [q[half - _SUBLANES:, :] for q in proj[0][:3]]
                z_carry = zs[0][half - _SUBLANES:, :]
            pprev = [_shift_rows_prompt(p[g], p_carry[g], first, two=False) for g in range(3)]
            z1, z2 = _shift_rows_prompt(z, z_carry, first)
            tail(rs, p, pprev, proj[idx][3], z, z1, z2)
        for g in range(3):
            pcarry[g] = proj[1][g][half - _SUBLANES:, :]
        zcarry[...] = zs[1][half - _SUBLANES:, :]
        cp_o[0] = zs[1][half - _SUBLANES:, :]

    @pl.when(i >= n_pt)
    def _decode():
        rs = slice(0, tm)
        proj = project(rs)
        p, z = proj[:3], proj[4] * proj[5]
        hl = hl_ref[0].astype(_BF16)
        pprev = [jnp.concatenate([jnp.dot(hl, wc_scr[g], preferred_element_type=_F32), p[g][:tm - db, :]], axis=0)
                 for g in range(3)]
        z1 = jnp.concatenate([cs_ref[0, 1], z[:tm - db, :]], axis=0)
        z2 = jnp.concatenate([cs_ref[0, 0], cs_ref[0, 1], z[:tm - 2 * db, :]], axis=0)
        cso_o[0] = z[tm - 2 * db:tm - db, :]
        cso_o[1] = z[tm - db:, :]
        tail(rs, p, pprev, proj[3], z, z1, z2)


def _mix(l, h, acts, vf_in, w_in, lora2, vecs, mu_rkv, k_k3, k_a3, conv_w, state_shift, state_conv_t,
         *, tm, tn, n_pt, tps, db, bp):
    rows, d = h.shape
    has_v = l > 0
    g_dim = mu_rkv.shape[2]
    nj = g_dim // tn
    aw, aa, ag, av = acts
    d2, i2, g2, v2 = lora2
    w0, a0, v0 = vecs
    row_in = lambda arr: pl.BlockSpec((tm, arr.shape[1]), lambda j, i: (i, 0))
    col3 = lambda arr, ll: pl.BlockSpec((1, arr.shape[1], tn), lambda j, i: (ll, 0, j))
    in_specs = [row_in(h)] + [pl.BlockSpec((1, d, tn), lambda j, i, g=g: (l, 0, g * nj + j)) for g in range(6)]
    args = [h] + [w_in] * 6
    in_specs += [row_in(aw), row_in(aa), row_in(ag)]
    args += [aw, aa, ag]
    if has_v:
        in_specs.append(row_in(av))
        args.append(av)
    in_specs += [col3(d2, l), col3(i2, l), col3(g2, l)]
    args += [d2, i2, g2]
    if has_v:
        in_specs.append(col3(v2, l - 1))
        args.append(v2)
    in_specs += [col3(w0, l), col3(a0, l)]
    args += [w0, a0]
    if has_v:
        in_specs.append(col3(v0, l - 1))
        args.append(v0)
    in_specs += [col3(mu_rkv, l), col3(k_k3, l), col3(k_a3, l), col3(conv_w, l),
                 pl.BlockSpec((1, db, d), lambda j, i: (l, 0, 0)),
                 pl.BlockSpec((1, 2, db, tn), lambda j, i: (l, 0, 0, j))]
    args += [mu_rkv, k_k3, k_a3, conv_w, state_shift, state_conv_t]
    if has_v:
        in_specs.append(pl.BlockSpec((tm, tn), lambda j, i: (i, j)))
        args.append(vf_in)
    tile = pl.BlockSpec((tm, tn), lambda j, i: (i, j))
    outs = [(tile, jax.ShapeDtypeStruct((rows, g_dim), _F32)) for _ in range(7)]
    outs.append((tile, jax.ShapeDtypeStruct((rows, g_dim), _BF16)))
    if not has_v:
        outs.append((tile, jax.ShapeDtypeStruct((rows, g_dim), _F32)))
    outs.append((pl.BlockSpec((1, _SUBLANES, tn), lambda j, i: (jnp.minimum(i // tps, bp - 1), 0, j)),
                 jax.ShapeDtypeStruct((bp, _SUBLANES, g_dim), _F32)))
    outs.append((pl.BlockSpec((2, db, tn), lambda j, i: (0, 0, j)), jax.ShapeDtypeStruct((2, db, g_dim), _F32)))
    return pl.pallas_call(
        functools.partial(_mix_kernel, n_pt=n_pt, tps=tps, db=db, has_v=has_v),
        grid=(nj, rows // tm),
        in_specs=in_specs,
        out_specs=[o[0] for o in outs],
        out_shape=[o[1] for o in outs],
        scratch_shapes=[pltpu.VMEM((6, d, tn), _BF16), pltpu.VMEM((3, _SUBLANES, tn), _F32),
                        pltpu.VMEM((_SUBLANES, tn), _F32)],
        compiler_params=_params("arbitrary", "arbitrary"),
    )(*args)


def _unit_lower_inverse_minus_identity(lmats, ri, ci, chunk):
    same8 = (ri >> 3) == (ci >> 3)
    d8 = [jnp.where(same8, m, 0.0) for m in lmats]
    d2 = [_dot(a, a) for a in d8]
    d3 = [_dot(a, b) for a, b in zip(d8, d2)]
    d4 = [_dot(b, b) for b in d2]
    x = [a + b + cc for a, b, cc in zip(d8, d2, d3)]
    xd4 = [_dot(a, b) for a, b in zip(x, d4)]
    x = [a + b + cc for a, b, cc in zip(x, d4, xd4)]
    size = 16
    while size <= chunk:
        sh = size.bit_length() - 1
        level = ((ri >> sh) == (ci >> sh)) & ((ri >> (sh - 1)) != (ci >> (sh - 1)))
        e = [jnp.where(level, m, 0.0) for m in lmats]
        y = [b + _dot(a, b) for a, b in zip(x, e)]
        x = [a + b + _dot(b, a) for a, b in zip(x, y)]
        size *= 2
    return x


def _wkv_kernel(*refs, chunk, nb, zero_init, n_chunks):
    (r_ref, lw_ref, k_ref, v_ref, kk_ref, kka_ref, g_ref, rk_ref, lnw_ref, lnb_ref) = refs[:10]
    if zero_init:
        o_ref, so_ref, s_scr = refs[10:]
        s0_ref = None
    else:
        s0_ref, o_ref, so_ref, s_scr = refs[10:]
    ci_grid = pl.program_id(1)
    g_dim = r_ref.shape[1]
    npair = g_dim // _PAIR
    c = chunk
    n2 = 2 * c

    lane = lax.broadcasted_iota(jnp.int32, (1, _PAIR), 1)
    m0 = (lane < _HEAD).astype(_F32)
    m1 = 1.0 - m0

    @pl.when(ci_grid == 0)
    def _init():
        if zero_init:
            s_scr[...] = jnp.zeros(s_scr.shape, _F32)
        else:
            zero = jnp.zeros((_HEAD, _HEAD), _F32)
            for u in range(nb):
                for p in range(npair):
                    top = jnp.concatenate([s0_ref[u, 2 * p], zero], axis=1)
                    bot = jnp.concatenate([zero, s0_ref[u, 2 * p + 1]], axis=1)
                    s_scr[u, p] = jnp.concatenate([top, bot], axis=0)

    ri = lax.broadcasted_iota(jnp.int32, (n2, n2), 0) & (c - 1)
    ci = lax.broadcasted_iota(jnp.int32, (n2, n2), 1) & (c - 1)
    strict = ri > ci
    incl = ri >= ci
    tri = (lax.broadcasted_iota(jnp.int32, (c, c), 0) >= lax.broadcasted_iota(jnp.int32, (c, c), 1)).astype(_BF16)
    seg_ones = _segment_ones(_SEG)
    rk = rk_ref[...]
    lnw = lnw_ref[...]
    lnb = lnb_ref[...]

    def stack(x):
        return jnp.concatenate([x * m0, x * m1], axis=0)

    def unit(u, carry):
        rows = slice(0, c) if nb == 1 else pl.ds(pl.multiple_of(u * c, c), c)
        lw = lw_ref[rows, :]
        l_hi, l_mid, l_lo = _split3(lw)
        cl = (jnp.dot(tri, l_hi, preferred_element_type=_F32)
              + jnp.dot(tri, l_mid, preferred_element_type=_F32)
              + jnp.dot(tri, l_lo, preferred_element_type=_F32))
        cl_end = cl[c - 1:c, :]
        r = r_ref[rows, :]
        k = k_ref[rows, :]
        v = v_ref[rows, :]
        kk = kk_ref[rows, :]
        kka = kka_ref[rows, :]
        e_neg = jnp.exp(-cl)
        e_tail = jnp.exp(cl_end - cl)
        p_end = jnp.exp(cl_end)
        rt = r * jnp.exp(cl)
        at = -(kk * jnp.exp(cl - lw))
        kt = k * e_neg
        bt = kka * e_neg
        kh = k * e_tail
        bh = kka * e_tail
        pairs = range(npair)
        sls = [slice(p * _PAIR, (p + 1) * _PAIR) for p in pairs]
        ar2 = [jnp.concatenate([stack(at[:, sl]), stack(rt[:, sl])], axis=0).astype(_BF16) for sl in sls]
        mb = [_dot_nt(a, stack(bt[:, sl])) for a, sl in zip(ar2, sls)]
        mk = [_dot_nt(a, stack(kt[:, sl])) for a, sl in zip(ar2, sls)]
        s_old = [s_scr[u, p] for p in pairs]
        uy0 = [_dot_nt(jnp.concatenate([at[:, sl], rt[:, sl]], axis=0), s)
               for sl, s in zip(sls, s_old)]
        x_inv = _unit_lower_inverse_minus_identity([jnp.where(strict, m[:n2], 0.0) for m in mb], ri, ci, c)
        vs = [stack(v[:, sl]).astype(_BF16) for sl in sls]
        ws = [stack(q[:c]) + _dot(jnp.where(strict, m[:n2], 0.0), vv) for q, m, vv in zip(uy0, mk, vs)]
        us = [w + _dot(xi, w) for xi, w in zip(x_inv, ws)]
        us = [q.astype(_BF16) for q in us]
        ysd = [_dot(jnp.where(incl, m1_[n2:], 0.0), uu) + _dot(jnp.where(incl, m2_[n2:], 0.0), vv)
               for m1_, m2_, uu, vv in zip(mb, mk, us, vs)]
        ys = [q[c:] + d[:c] + d[c:] for q, d in zip(uy0, ysd)]
        for p, sl in zip(pairs, sls):
            uv = jnp.concatenate([us[p], vs[p]], axis=0)
            bk = jnp.concatenate([stack(bh[:, sl]), stack(kh[:, sl])], axis=0)
            s_scr[u, p] = s_old[p] * p_end[:, sl] + _dot_tn(uv, bk)
        y = jnp.concatenate(ys, axis=1)
        mean = _head_sums(y, seg_ones) * (1.0 / _HEAD)
        dy = y - mean
        var = _head_sums(dy * dy, seg_ones) * (1.0 / _HEAD)
        yn = dy * lax.rsqrt(var + _GN_EPS) * lnw + lnb
        bonus = _head_sums(r * k * rk, seg_ones) * v
        o_ref[rows, :] = ((yn + bonus) * g_ref[rows, :]).astype(o_ref.dtype)
        return carry

    if nb == 1:
        unit(0, 0)
    else:
        lax.fori_loop(0, nb, unit, 0)

    @pl.when(ci_grid == n_chunks - 1)
    def _final():
        for u in range(nb):
            for p in range(npair):
                s = s_scr[u, p]
                so_ref[u, 2 * p] = s[:_HEAD, :_HEAD]
                so_ref[u, 2 * p + 1] = s[_HEAD:, _HEAD:]


def _wkv(l, seqs, chunk, nb, n_seq, n_chunks, state0, r_k3, ln_w3, ln_b3, *, row_block_of):
    g_dim = seqs[0].shape[1]
    n_heads = g_dim // _HEAD
    rows = seqs[0].shape[0]
    zero_init = state0 is None
    blk = pl.BlockSpec((nb * chunk, g_dim), lambda bi, ci: (row_block_of(bi, ci), 0))
    vec = pl.BlockSpec((None, 1, g_dim), lambda bi, ci: (l, 0, 0))
    st = pl.BlockSpec((nb, n_heads, _HEAD, _HEAD), lambda bi, ci: (bi, 0, 0, 0))
    in_specs = [blk] * 7 + [vec] * 3
    args = list(seqs) + [r_k3, ln_w3, ln_b3]
    if not zero_init:
        in_specs.append(st)
        args.append(state0)
    return pl.pallas_call(
        functools.partial(_wkv_kernel, chunk=chunk, nb=nb, zero_init=zero_init, n_chunks=n_chunks),
        grid=(n_seq // nb, n_chunks),
        in_specs=in_specs,
        out_specs=[blk, st],
        out_shape=[jax.ShapeDtypeStruct((n_seq * n_chunks * chunk, g_dim), _BF16),
                   jax.ShapeDtypeStruct((n_seq, n_heads, _HEAD, _HEAD), _F32)],
        scratch_shapes=[pltpu.VMEM((nb, g_dim // _PAIR, _PAIR, _PAIR), _F32)],
        compiler_params=_params("arbitrary", "arbitrary"),
    )(*args)


def _wkv_decode_kernel(r_ref, lw_ref, k_ref, v_ref, kk_ref, kka_ref, g_ref, rk_ref, lnw_ref, lnb_ref, s0_ref,
                       o_ref, so_ref, *, ts, nb):
    g_dim = r_ref.shape[2]
    npair = g_dim // _PAIR
    n = ts * nb
    n2 = 2 * n
    nb_bits = nb.bit_length() - 1

    lane = lax.broadcasted_iota(jnp.int32, (1, _PAIR), 1)
    m0 = (lane < _HEAD).astype(_F32)
    m1 = 1.0 - m0

    def stack(x):
        return jnp.concatenate([x * m0, x * m1], axis=0)

    def rows_tb(ref):
        return jnp.concatenate([ref[t] for t in range(ts)], axis=0)

    lw_t = [lw_ref[t] for t in range(ts)]
    cl_t = [lw_t[0]]
    for t in range(1, ts):
        cl_t.append(cl_t[-1] + lw_t[t])
    cl = jnp.concatenate(cl_t, axis=0)
    lw = jnp.concatenate(lw_t, axis=0)
    cl_end_b = cl_t[-1]
    cl_end = _tile_rows(cl_end_b, ts)
    p_end = jnp.exp(cl_end_b)
    r, k, v, kk, kka = (rows_tb(ref) for ref in (r_ref, k_ref, v_ref, kk_ref, kka_ref))
    e_neg = jnp.exp(-cl)
    e_tail = jnp.exp(cl_end - cl)
    rt = r * jnp.exp(cl)
    at = -(kk * jnp.exp(cl - lw))
    kt = k * e_neg
    bt = kka * e_neg
    kh = k * e_tail
    bh = kka * e_tail

    ri = lax.broadcasted_iota(jnp.int32, (n2, n2), 0)
    ci = lax.broadcasted_iota(jnp.int32, (n2, n2), 1)
    same_seq = (ri & (nb - 1)) == (ci & (nb - 1))
    t_r = (ri & (n - 1)) >> nb_bits
    t_c = (ci & (n - 1)) >> nb_bits
    strict = same_seq & (t_r > t_c)
    incl = same_seq & (t_r >= t_c)
    seq_of_row2 = lax.broadcasted_iota(jnp.int32, (n2, 1), 0) & (nb - 1)
    seq_of_row4 = lax.broadcasted_iota(jnp.int32, (2 * n2, 1), 0) & (nb - 1)
    zero = jnp.zeros((_HEAD, _HEAD), _F32)

    pairs = range(npair)
    sls = [slice(p * _PAIR, (p + 1) * _PAIR) for p in pairs]
    ar2 = [jnp.concatenate([stack(at[:, sl]), stack(rt[:, sl])], axis=0).astype(_BF16) for sl in sls]
    mb = [_dot_nt(a, stack(bt[:, sl])) for a, sl in zip(ar2, sls)]
    mk = [_dot_nt(a, stack(kt[:, sl])) for a, sl in zip(ar2, sls)]
    lab = [jnp.where(strict, m[:n2], 0.0) for m in mb]
    d2 = [_dot(a, a) for a in lab]
    d3 = [_dot(a, b) for a, b in zip(lab, d2)]
    x_inv = [a + b + cc for a, b, cc in zip(lab, d2, d3)]

    def block_diag(b, p):
        top = jnp.concatenate([s0_ref[b, 2 * p], zero], axis=1)
        bot = jnp.concatenate([zero, s0_ref[b, 2 * p + 1]], axis=1)
        return jnp.concatenate([top, bot], axis=0)

    ys = []
    for p, sl in zip(pairs, sls):
        s_b = [block_diag(b, p) for b in range(nb)]
        ar = jnp.concatenate([at[:, sl], rt[:, sl]], axis=0)
        ar_cat = jnp.concatenate([jnp.where(seq_of_row2 == b, ar, 0.0).astype(_BF16) for b in range(nb)], axis=1)
        s_cat = jnp.concatenate([s.astype(_BF16) for s in s_b], axis=1)
        uy0 = _dot_nt(ar_cat, s_cat)
        vs = stack(v[:, sl]).astype(_BF16)
        ws = stack(uy0[:n]) + _dot(jnp.where(strict, mk[p][:n2], 0.0), vs)
        us = (ws + _dot(x_inv[p], ws)).astype(_BF16)
        ysd = _dot(jnp.where(incl, mb[p][n2:], 0.0), us) + _dot(jnp.where(incl, mk[p][n2:], 0.0), vs)
        ys.append(uy0[n:] + ysd[:n] + ysd[n:])
        uv = jnp.concatenate([us, vs], axis=0)
        uv_cat = jnp.concatenate([jnp.where(seq_of_row4 == b, uv, jnp.zeros_like(uv)) for b in range(nb)], axis=1)
        bk = jnp.concatenate([stack(bh[:, sl]), stack(kh[:, sl])], axis=0)
        upd = _dot_tn(uv_cat, bk)
        for b in range(nb):
            s_new = (s_b[b] * p_end[b:b + 1, sl] + upd[b * _PAIR:(b + 1) * _PAIR, :])
            so_ref[b, 2 * p] = s_new[:_HEAD, :_HEAD]
            so_ref[b, 2 * p + 1] = s_new[_HEAD:, _HEAD:]

    y = jnp.concatenate(ys, axis=1)
    seg_ones = _segment_ones(_SEG)
    mean = _head_sums(y, seg_ones) * (1.0 / _HEAD)
    dy = y - mean
    var = _head_sums(dy * dy, seg_ones) * (1.0 / _HEAD)
    yn = dy * lax.rsqrt(var + _GN_EPS) * lnw_ref[...] + lnb_ref[...]
    bonus = _head_sums(r * k * rk_ref[...], seg_ones) * v
    out = ((yn + bonus) * rows_tb(g_ref)).astype(o_ref.dtype)
    for t in range(ts):
        o_ref[t] = out[t * nb:(t + 1) * nb, :]


def _wkv_decode(l, seqs, state0, r_k3, ln_w3, ln_b3, *, ts, db, nb, first_block):
    g_dim = seqs[0].shape[2]
    n_heads = g_dim // _HEAD
    blk = pl.BlockSpec((ts, nb, g_dim), lambda bi: (first_block, bi, 0))
    vec = pl.BlockSpec((None, 1, g_dim), lambda bi: (l, 0, 0))
    st = pl.BlockSpec((nb, n_heads, _HEAD, _HEAD), lambda bi: (bi, 0, 0, 0))
    return pl.pallas_call(
        functools.partial(_wkv_decode_kernel, ts=ts, nb=nb),
        grid=(db // nb,),
        in_specs=[blk] * 7 + [vec] * 3 + [st],
        out_specs=[pl.BlockSpec((ts, nb, g_dim), lambda bi: (0, bi, 0)), st],
        out_shape=[jax.ShapeDtypeStruct((ts, db, g_dim), _BF16),
                   jax.ShapeDtypeStruct((db, n_heads, _HEAD, _HEAD), _F32)],
        compiler_params=_params("arbitrary"),
    )(*seqs, r_k3, ln_w3, ln_b3, state0)


def _cast_kernel(w_ref, o_ref):
    o_ref[...] = w_ref[...].astype(o_ref.dtype)


def _cast_bf16(w, rows_per_block):
    depth, k, n = w.shape
    spec = pl.BlockSpec((1, rows_per_block, n), lambda l, i: (l, i, 0))
    return pl.pallas_call(
        _cast_kernel, grid=(depth, k // rows_per_block), in_specs=[spec], out_specs=spec,
        out_shape=jax.ShapeDtypeStruct(w.shape, _BF16),
        compiler_params=_params("arbitrary", "arbitrary"),
    )(w)


def _outproj_kernel(op_ref, os_ref, oc_ref, xp_ref, xs_ref, ga_ref, sh_ref, sc_ref, g_ref, w_ref,
                    x1_ref, h2_ref, *, n_pt, tps, db):
    i = pl.program_id(0)
    tm = xp_ref.shape[0]
    ts = tm // db
    g_dim = op_ref.shape[1]

    def rows_out(rs, o_ref, x_ref, ga, sc, sh):
        acc = (jnp.dot(o_ref[rs, :], w_ref[0, :g_dim, :], preferred_element_type=_F32)
               + jnp.dot(oc_ref[rs, :], w_ref[0, g_dim:, :], preferred_element_type=_F32))
        x1 = x_ref[rs, :] + ga * acc
        x1_ref[rs, :] = x1
        xn = x1 * lax.rsqrt(jnp.mean(x1 * x1, axis=-1, keepdims=True) + _RMS_EPS) * g_ref[0, 0]
        h2_ref[rs, :] = (xn * (1.0 + sc) + sh).astype(_BF16)

    half = tm // 2
    halves = (slice(0, half), slice(half, tm))

    @pl.when(i < n_pt)
    def _prompt():
        ga, sc, sh = (_mod_row(ref, i, tps, db) for ref in (ga_ref, sc_ref, sh_ref))
        for rs in halves:
            rows_out(rs, op_ref, xp_ref, ga, sc, sh)

    @pl.when(i >= n_pt)
    def _decode():
        ga, sc, sh = (_mod_tile(ref, db, ts) for ref in (ga_ref, sc_ref, sh_ref))
        for rs in halves:
            rows_out(rs, os_ref, xs_ref, ga[rs, :], sc[rs, :], sh[rs, :])


def _outproj(l, o_p, o_s, o_conv, x, mod, norm_g4, w_out, *, tm, n_pt, tps, db):
    d = x.prompt.shape[1]
    rows = (n_pt + 1) * tm
    g_dim = o_p.shape[1]
    bc = mod.shape[2]
    modspec = lambda comp: pl.BlockSpec((1, 1, bc, d), lambda i: (l, comp, 0, 0))
    return pl.pallas_call(
        functools.partial(_outproj_kernel, n_pt=n_pt, tps=tps, db=db),
        grid=(rows // tm,),
        in_specs=[pl.BlockSpec((tm, g_dim), lambda i: (jnp.minimum(i, n_pt - 1), 0)),
                  pl.BlockSpec((tm, g_dim), lambda i: (0, 0)),
                  pl.BlockSpec((tm, o_conv.shape[1]), lambda i: (i, 0))] + _act_specs(x, tm, n_pt) + [
                  modspec(2), modspec(3), modspec(4),
                  pl.BlockSpec((1, 1, 1, d), lambda i: (l, 1, 0, 0)),
                  pl.BlockSpec((1, d, d), lambda i: (l, 0, 0), pipeline_mode=pl.Buffered(1))],
        out_specs=[pl.BlockSpec((tm, d), lambda i: (i, 0)), pl.BlockSpec((tm, d), lambda i: (i, 0))],
        out_shape=[jax.ShapeDtypeStruct((rows, d), _F32), jax.ShapeDtypeStruct((rows, d), _BF16)],
        compiler_params=_params("arbitrary"),
    )(o_p, o_s, o_conv, x.prompt, x.decode, mod, mod, mod, norm_g4, w_out)


def _ffn_kernel(*refs, n_pt, tps, db, nj, final_norm):
    it = iter(refs)
    (h2_ref, wa_ref, wb_ref, cwa_ref, cwb_ref, wd_ref, x1_ref, ga_ref, sfa_ref, sfb_ref) = (next(it) for _ in range(10))
    fg_ref = next(it) if final_norm else None
    x2_ref = next(it)
    x2s_ref = next(it) if final_norm else x2_ref
    fpa_o, fpb_o, fsa_o, fsb_o = (next(it) for _ in range(4))
    acc_scr, ca_scr, cb_scr = (next(it) for _ in range(3))
    i = pl.program_id(0)
    j = pl.program_id(1)
    tm = h2_ref.shape[0]
    ts = tm // db

    @pl.when(j == 0)
    def _zero():
        acc_scr[...] = jnp.zeros(acc_scr.shape, _F32)

    def prompt_body():
        half = tm // 2
        halves = (slice(0, half), slice(half, tm))
        us = [[jnp.dot(h2_ref[rs, :], w_ref[0], preferred_element_type=_F32) for w_ref in (wa_ref, wb_ref)]
              for rs in halves]
        for r, rs in enumerate(halves):
            conv = []
            for idx, (cw_ref, c_scr, fp_o) in enumerate(((cwa_ref, ca_scr, fpa_o), (cwb_ref, cb_scr, fpb_o))):
                u = us[r][idx]
                if r == 0:
                    u1, u2 = _shift_rows_prompt(u, c_scr[j], (i % tps) == 0)
                else:
                    u1, u2 = _shift_rows_prompt(u, us[0][idx][half - _SUBLANES:, :], False)
                    c_scr[j] = u[half - _SUBLANES:, :]
                    fp_o[0] = u[half - _SUBLANES:, :]
                cw = cw_ref[0]
                conv.append(u2 * cw[0:1] + u1 * cw[1:2] + u * cw[2:3])
            gact = ((conv[0] * jax.nn.sigmoid(conv[0])) * conv[1]).astype(_BF16)
            acc_scr[rs, :] += jnp.dot(gact, wd_ref[0], preferred_element_type=_F32)

    def body(prompt):
        if prompt:
            return prompt_body()
        first = (i % tps) == 0
        h2 = h2_ref[...]
        ua = jnp.dot(h2, wa_ref[0], preferred_element_type=_F32)
        ub = jnp.dot(h2, wb_ref[0], preferred_element_type=_F32)
        conv = []
        for u, cw_ref, c_scr, sf, fp_o, fs_o in ((ua, cwa_ref, ca_scr, sfa_ref, fpa_o, fsa_o),
                                                 (ub, cwb_ref, cb_scr, sfb_ref, fpb_o, fsb_o)):
            if prompt:
                u1, u2 = _shift_rows_prompt(u, c_scr[j], first)
                c_scr[j] = u[tm - _SUBLANES:, :]
                fp_o[0] = u[tm - _SUBLANES:, :]
            else:
                u1 = jnp.concatenate([sf[0, 1], u[:tm - db, :]], axis=0)
                u2 = jnp.concatenate([sf[0, 0], sf[0, 1], u[:tm - 2 * db, :]], axis=0)
                fp_o[0] = jnp.zeros(fp_o.shape[1:], _F32)
                fs_o[0] = u[tm - 2 * db:tm - db, :]
                fs_o[1] = u[tm - db:, :]
            cw = cw_ref[0]
            conv.append(u2 * cw[0:1] + u1 * cw[1:2] + u * cw[2:3])
        gact = ((conv[0] * jax.nn.sigmoid(conv[0])) * conv[1]).astype(_BF16)
        acc_scr[...] += jnp.dot(gact, wd_ref[0], preferred_element_type=_F32)

    @pl.when(i < n_pt)
    def _prompt():
        body(True)

    @pl.when(i >= n_pt)
    def _decode():
        body(False)

    def finish(ga, out_ref):
        x2 = x1_ref[...] + ga * acc_scr[...]
        if final_norm:
            x2 = x2 * lax.rsqrt(jnp.mean(x2 * x2, axis=-1, keepdims=True) + _RMS_EPS) * fg_ref[...]
        out_ref[...] = x2

    @pl.when((j == nj - 1) & (i < n_pt))
    def _finish_prompt():
        finish(_mod_row(ga_ref, i, tps, db), x2_ref)

    @pl.when((j == nj - 1) & (i >= n_pt))
    def _finish_decode():
        finish(_mod_tile(ga_ref, db, ts), x2s_ref)


def _ffn(l, h2, x1, mod, ffn_up, ffn_conv, ffn_down, state_ffn_t, final_g, *, tm, tn, n_pt, tps, db, bp):
    rows, d = x1.shape
    f = ffn_down.shape[1]
    nj = f // tn
    bc = mod.shape[2]
    final_norm = final_g is not None
    in_specs = [pl.BlockSpec((tm, d), lambda i, j: (i, 0)),
                pl.BlockSpec((1, d, tn), lambda i, j: (l, 0, j)),
                pl.BlockSpec((1, d, tn), lambda i, j: (l, 0, nj + j)),
                pl.BlockSpec((1, 3, tn), lambda i, j: (l, 0, j)),
                pl.BlockSpec((1, 3, tn), lambda i, j: (l, 0, nj + j)),
                pl.BlockSpec((1, tn, d), lambda i, j: (l, j, 0)),
                pl.BlockSpec((tm, d), lambda i, j: (i, 0)),
                pl.BlockSpec((1, 1, bc, d), lambda i, j: (l, 5, 0, 0)),
                pl.BlockSpec((1, 2, db, tn), lambda i, j: (l, 0, 0, j)),
                pl.BlockSpec((1, 2, db, tn), lambda i, j: (l, 0, 0, nj + j))]
    args = [h2, ffn_up, ffn_up, ffn_conv, ffn_conv, ffn_down, x1, mod, state_ffn_t, state_ffn_t]
    if final_norm:
        in_specs.append(pl.BlockSpec((1, d), lambda i, j: (0, 0)))
        args.append(final_g)
    n_tiles = rows // tm
    pstate = pl.BlockSpec((1, _SUBLANES, tn), lambda i, j: (i, 0, j))
    sstate = pl.BlockSpec((2, db, tn), lambda i, j: (0, 0, jnp.where(i >= n_pt, j, 0)))
    if final_norm:
        x_specs = [pl.BlockSpec((tm, d), lambda i, j: (jnp.minimum(i, n_pt - 1), 0)),
                   pl.BlockSpec((tm, d), lambda i, j: (0, 0))]
        x_shapes = [jax.ShapeDtypeStruct((n_pt * tm, d), _F32), jax.ShapeDtypeStruct((tm, d), _F32)]
    else:
        x_specs = [pl.BlockSpec((tm, d), lambda i, j: (i, 0))]
        x_shapes = [jax.ShapeDtypeStruct((rows, d), _F32)]
    return pl.pallas_call(
        functools.partial(_ffn_kernel, n_pt=n_pt, tps=tps, db=db, nj=nj, final_norm=final_norm),
        grid=(rows // tm, nj),
        in_specs=in_specs,
        out_specs=x_specs + [pstate, pstate, sstate, sstate],
        out_shape=x_shapes + [
                   jax.ShapeDtypeStruct((n_tiles, _SUBLANES, f), _F32),
                   jax.ShapeDtypeStruct((n_tiles, _SUBLANES, f), _F32),
                   jax.ShapeDtypeStruct((2, db, f), _F32), jax.ShapeDtypeStruct((2, db, f), _F32)],
        scratch_shapes=[pltpu.VMEM((tm, d), _F32), pltpu.VMEM((nj, _SUBLANES, tn), _F32),
                        pltpu.VMEM((nj, _SUBLANES, tn), _F32)],
        compiler_params=_params("arbitrary", "arbitrary"),
    )(*args)


def _forward(x_prompt, x_sample, c_prompt, c_sample, state_wkv, state_shift, state_conv, state_ffn,
             ada_w, ada_b, norm_g, final_norm_g, w_in, mu_x, mu_rkv, decay_w0, decay_lora1, decay_lora2,
             iclr_a0, iclr_lora1, iclr_lora2, gate_lora1, gate_lora2, vres_v0, vres_lora1, vres_lora2,
             k_k, k_a, r_k, ln_x_w, ln_x_b, conv_w, w_out, ffn_up, ffn_conv, ffn_down,
             *, chunk=64, mix_tn=256, ffn_tn=512, wkv_nb=16):
    bp, t_len, d = x_prompt.shape
    db, ts, _ = x_sample.shape
    depth = ada_w.shape[0]
    g_dim = mu_rkv.shape[2]
    n_heads = g_dim // _HEAD
    f = ffn_down.shape[1]
    tm = ts * db
    assert t_len % tm == 0 and t_len % chunk == 0 and ts >= 2 and db % _SUBLANES == 0 and tm % chunk == 0
    tps = t_len // tm
    n_pt = bp * tps
    n_prompt_rows = bp * t_len
    assert ts == 4 and db % wkv_nb == 0 and wkv_nb & (wkv_nb - 1) == 0

    x = _Act(x_prompt.reshape(n_prompt_rows, d), x_sample.transpose(1, 0, 2).reshape(tm, d), 0)
    pad =(-(db + bp)) % _SUBLANES
    c_all = jnp.concatenate([c_sample, c_prompt, jnp.zeros((pad, d), _F32)], axis=0)
    mod = _adaln_mod(c_all, ada_w, ada_b)

    norm_g4 = norm_g.reshape(depth, 2, 1, d)
    vec3 = lambda a: a.reshape(a.shape[0], 1, a.shape[1])
    w0_3, a0_3, v0_3, kk_3, ka_3 = vec3(decay_w0), vec3(iclr_a0), vec3(vres_v0), vec3(k_k), vec3(k_a)
    rk_3, lnw_3, lnb_3 = r_k.reshape(depth, 1, g_dim), vec3(ln_x_w), vec3(ln_x_b)
    state_conv_t = state_conv.transpose(0, 2, 1, 3)
    state_ffn_t = state_ffn.transpose(0, 2, 1, 3)
    tiles = dict(tm=tm, n_pt=n_pt, tps=tps, db=db)
    w_out = _cast_bf16(w_out, 512)
    ffn_up = _cast_bf16(ffn_up, 128)
    ffn_down = _cast_bf16(ffn_down, 512)

    new = {k: [] for k in ("wkv_p", "shift_p", "conv_p", "ffn_p", "wkv_s", "shift_s", "conv_s", "ffn_s")}
    v_first = None
    for l in range(depth):
        outs = _norm_lora(l, x, mod, norm_g4, mu_x, state_shift, decay_lora1, iclr_lora1, gate_lora1,
                          vres_lora1, bp=bp, **tiles)
        if l > 0:
            h, aw, aa, ag, av, hp_last, hs_last = outs
        else:
            h, aw, aa, ag, hp_last, hs_last = outs
            av = None
        mouts = _mix(l, h, (aw, aa, ag, av), v_first, w_in,
                     (decay_lora2, iclr_lora2, gate_lora2, vres_lora2), (w0_3, a0_3, v0_3),
                     mu_rkv, kk_3, ka_3, conv_w, state_shift, state_conv_t, tn=mix_tn, bp=bp, **tiles)
        if l > 0:
            r, lw, k, v, kk, kka, gate, o_conv, conv_p8, conv_s = mouts
        else:
            r, lw, k, v, kk, kka, gate, o_conv, v_first, conv_p8, conv_s = mouts
        seqs = (r, lw, k, v, kk, kka, gate)

        cpt = t_len // chunk
        o_p, wkv_p = _wkv(l, seqs, chunk, 1, bp, cpt, None, rk_3, lnw_3, lnb_3,
                          row_block_of=lambda bi, ci: bi * cpt + ci)

        dseqs = tuple(a.reshape(a.shape[0] // db, db, g_dim) for a in seqs)
        o_s, wkv_s = _wkv_decode(l, dseqs, state_wkv[l], rk_3, lnw_3, lnb_3, ts=ts, db=db, nb=wkv_nb,
                                 first_block=n_prompt_rows // tm)
        o_s = o_s.reshape(tm, g_dim)

        x1, h2 = _outproj(l, o_p, o_s, o_conv, x, mod, norm_g4, w_out, **tiles)
        final_g = final_norm_g.reshape(1, d) if l == depth - 1 else None
        *x_out, fpa, fpb, fsa, fsb = _ffn(l, h2, x1, mod, ffn_up, ffn_conv, ffn_down, state_ffn_t, final_g,
                                          tn=ffn_tn, bp=bp, **tiles)
        x = _Act(x_out[0], x_out[0], n_pt) if len(x_out) == 1 else _Act(x_out[0], x_out[1], 0)

        new["wkv_p"].append(wkv_p)
        new["wkv_s"].append(wkv_s)
        new["shift_p"].append(hp_last[:, _SUBLANES - 1])
        new["shift_s"].append(hs_last)
        new["conv_p"].append(conv_p8[:, _SUBLANES - 2:])
        new["conv_s"].append(conv_s.transpose(1, 0, 2))
        last_tiles = slice(tps - 1, n_pt, tps)
        new["ffn_p"].append(jnp.concatenate([fpa[last_tiles, _SUBLANES - 2:], fpb[last_tiles, _SUBLANES - 2:]],
                                            axis=-1))
        new["ffn_s"].append(jnp.concatenate([fsa, fsb], axis=-1).transpose(1, 0, 2))

    y_prompt = x.prompt.reshape(bp, t_len, d)
    y_sample = x.decode.reshape(ts, db, d).transpose(1, 0, 2)
    st = {k: jnp.stack(vs) for k, vs in new.items()}
    return (y_prompt, y_sample, st["wkv_p"], st["shift_p"], st["conv_p"], st["ffn_p"],
            st["wkv_s"], st["shift_s"], st["conv_s"], st["ffn_s"])


def kernel(x_prompt, x_sample, c_prompt, c_sample, state_wkv, state_shift, state_conv, state_ffn, ada_w, ada_b, norm_g, final_norm_g, w_in, mu_x, mu_rkv, decay_w0, decay_lora1, decay_lora2, iclr_a0, iclr_lora1, iclr_lora2, gate_lora1, gate_lora2, vres_v0, vres_lora1, vres_lora2, k_k, k_a, r_k, ln_x_w, ln_x_b, conv_w, w_out, ffn_up, ffn_conv, ffn_down):
    return _forward(x_prompt, x_sample, c_prompt, c_sample, state_wkv, state_shift, state_conv, state_ffn,
                    ada_w, ada_b, norm_g, final_norm_g, w_in, mu_x, mu_rkv, decay_w0, decay_lora1, decay_lora2,
                    iclr_a0, iclr_lora1, iclr_lora2, gate_lora1, gate_lora2, vres_v0, vres_lora1, vres_lora2,
                    k_k, k_a, r_k, ln_x_w, ln_x_b, conv_w, w_out, ffn_up, ffn_conv, ffn_down)
```

```python
import functools
from typing import NamedTuple

import jax
import jax.numpy as jnp
from jax import lax
from jax.experimental import pallas as pl
from jax.experimental.pallas import tpu as pltpu

_F32 = jnp.float32
_BF16 = jnp.bfloat16

_HEAD = 64
_PAIR = 2 * _HEAD
_SEG = 256
_RMS_EPS = 1e-6
_GN_EPS = 64e-5
_VMEM_LIMIT_BYTES = 56 * 1024 * 1024
_SUBLANES = 8


def _params(*sem):
    return pltpu.CompilerParams(dimension_semantics=sem, vmem_limit_bytes=_VMEM_LIMIT_BYTES)


def _dot(a, b):
    return jnp.dot(a.astype(_BF16), b.astype(_BF16), preferred_element_type=_F32)


def _dot_nt(a, b):
    return lax.dot_general(a.astype(_BF16), b.astype(_BF16), (((1,), (1,)), ((), ())),
                           preferred_element_type=_F32)


def _dot_tn(a, b):
    return lax.dot_general(a.astype(_BF16), b.astype(_BF16), (((0,), (0,)), ((), ())),
                           preferred_element_type=_F32)


def _split2(x):
    hi = x.astype(_BF16)
    lo = (x - hi.astype(_F32)).astype(_BF16)
    return hi, lo


def _split3(x):
    hi = x.astype(_BF16)
    r1 = x - hi.astype(_F32)
    mid = r1.astype(_BF16)
    lo = (r1 - mid.astype(_F32)).astype(_BF16)
    return hi, mid, lo


def _dot_exact_rhs(x, m_bf16):
    hi, lo = _split2(x)
    return (jnp.dot(hi, m_bf16, preferred_element_type=_F32)
            + jnp.dot(lo, m_bf16, preferred_element_type=_F32))


def _segment_ones(n):
    r = lax.broadcasted_iota(jnp.int32, (n, n), 0) // _HEAD
    c = lax.broadcasted_iota(jnp.int32, (n, n), 1) // _HEAD
    return (r == c).astype(_BF16)


def _head_sums(x, seg_ones):
    n = seg_ones.shape[0]
    parts = [_dot_exact_rhs(x[:, q * n:(q + 1) * n], seg_ones) for q in range(x.shape[1] // n)]
    return parts[0] if len(parts) == 1 else jnp.concatenate(parts, axis=1)


def _tile_rows(m, reps):
    return m if reps == 1 else jnp.concatenate([m] * reps, axis=0)


def _shift_rows_prompt(x, carry8, first, two=True):
    r8 = lax.broadcasted_iota(jnp.int32, (_SUBLANES, 1), 0)
    l1 = jnp.where(first, 0.0, carry8[_SUBLANES - 1:_SUBLANES, :])
    x1 = pltpu.roll(x, 1, 0)
    x1 = jnp.concatenate([jnp.where(r8 == 0, l1, x1[:_SUBLANES, :]), x1[_SUBLANES:, :]], axis=0)
    if not two:
        return x1
    l2 = jnp.where(first, 0.0, carry8[_SUBLANES - 2:_SUBLANES - 1, :])
    x2 = pltpu.roll(x, 2, 0)
    top2 = jnp.where(r8 == 0, l2, jnp.where(r8 == 1, l1, x2[:_SUBLANES, :]))
    return x1, jnp.concatenate([top2, x2[_SUBLANES:, :]], axis=0)


class _Act(NamedTuple):
    prompt: jax.Array
    decode: jax.Array
    decode_block: int


def _act_specs(x, tm, n_pt):
    d = x.prompt.shape[1]
    return [pl.BlockSpec((tm, d), lambda i, *_: (jnp.minimum(i, n_pt - 1), 0)),
            pl.BlockSpec((tm, d), lambda i, *_: (x.decode_block, 0))]


def _mod_row(ref, i, tps, db):
    return ref[0, 0, pl.ds(db + i // tps, 1), :]


def _mod_tile(ref, db, ts):
    return _tile_rows(ref[0, 0, 0:db, :], ts)


def _mod_kernel(c_ref, w_ref, b_ref, o_ref):
    c = c_ref[...]
    s_hi, s_lo = _split2(c * jax.nn.sigmoid(c))
    w_hi, w_lo = _split2(w_ref[0])
    acc = (jnp.dot(s_hi, w_hi, preferred_element_type=_F32)
           + jnp.dot(s_hi, w_lo, preferred_element_type=_F32)
           + jnp.dot(s_lo, w_hi, preferred_element_type=_F32))
    o_ref[0, 0] = acc + b_ref[0]


def _adaln_mod(c_all, ada_w, ada_b):
    depth, d, six_d = ada_w.shape
    bc = c_all.shape[0]
    tn = 1024
    per = d // tn
    return pl.pallas_call(
        _mod_kernel,
        grid=(depth, six_d // tn),
        in_specs=[pl.BlockSpec((bc, d), lambda l, n: (0, 0)),
                  pl.BlockSpec((1, d, tn), lambda l, n: (l, 0, n)),
                  pl.BlockSpec((1, 1, tn), lambda l, n: (l, 0, n))],
        out_specs=pl.BlockSpec((1, 1, bc, tn), lambda l, n: (l, n // per, 0, n % per)),
        out_shape=jax.ShapeDtypeStruct((depth, 6, bc, d), _F32),
        compiler_params=_params("arbitrary", "arbitrary"),
    )(c_all, ada_w, ada_b.reshape(depth, 1, six_d))


def _norm_lora_kernel(*refs, n_pt, tps, db, has_v):
    if has_v:
        (xp_ref, xs_ref, sh_ref, sc_ref, g_ref, mu_ref, hl_ref, d1_ref, i1_ref, g1_ref, v1_ref,
         h_ref, aw_ref, aa_ref, ag_ref, av_ref, hp_ref, hs_ref, carry_ref, hcur_ref, hprev_ref) = refs
    else:
        (xp_ref, xs_ref, sh_ref, sc_ref, g_ref, mu_ref, hl_ref, d1_ref, i1_ref, g1_ref,
         h_ref, aw_ref, aa_ref, ag_ref, hp_ref, hs_ref, carry_ref, hcur_ref, hprev_ref) = refs
    i = pl.program_id(0)
    tm = xp_ref.shape[0]
    ts = tm // db
    x = jnp.where(i < n_pt, xp_ref[...], xs_ref[...])
    xn = x * lax.rsqrt(jnp.mean(x * x, axis=-1, keepdims=True) + _RMS_EPS) * g_ref[0, 0]

    @pl.when(i < n_pt)
    def _prompt():
        h = xn * (1.0 + _mod_row(sc_ref, i, tps, db)) + _mod_row(sh_ref, i, tps, db)
        first = (i % tps) == 0
        hprev_ref[...] = _shift_rows_prompt(h, carry_ref[...], first, two=False)
        hcur_ref[...] = h
        carry_ref[...] = h[tm - _SUBLANES:, :]
        hp_ref[0] = h[tm - _SUBLANES:, :]

    @pl.when(i >= n_pt)
    def _decode():
        h = xn * (1.0 + _mod_tile(sc_ref, db, ts)) + _mod_tile(sh_ref, db, ts)
        hprev_ref[...] = jnp.concatenate([hl_ref[0], h[:tm - db, :]], axis=0)
        hcur_ref[...] = h
        hs_ref[...] = h[tm - db:, :]

    h = hcur_ref[...]
    xx = hprev_ref[...] - h
    mu = mu_ref[0]
    h_ref[...] = h.astype(_BF16)
    aw_ref[...] = jnp.tanh(_dot(h + xx * mu[0:1], d1_ref[0]))
    aa_ref[...] = _dot(h + xx * mu[1:2], i1_ref[0])
    ag_ref[...] = jax.nn.sigmoid(_dot(h + xx * mu[2:3], g1_ref[0]))
    if has_v:
        av_ref[...] = _dot(h + xx * mu[3:4], v1_ref[0])


def _norm_lora(l, x, mod, norm_g4, mu_x, state_shift, decay_lora1, iclr_lora1, gate_lora1, vres_lora1,
               *, tm, n_pt, tps, db, bp):
    d = x.prompt.shape[1]
    rows = (n_pt + 1) * tm
    has_v = l > 0
    bc = mod.shape[2]
    ld, la, lg = decay_lora1.shape[2], iclr_lora1.shape[2], gate_lora1.shape[2]
    full = lambda *shape: pl.BlockSpec(shape, lambda i: (l,) + (0,) * (len(shape) - 1))
    in_specs = _act_specs(x, tm, n_pt) + [
                pl.BlockSpec((1, 1, bc, d), lambda i: (l, 0, 0, 0)),
                pl.BlockSpec((1, 1, bc, d), lambda i: (l, 1, 0, 0)),
                pl.BlockSpec((1, 1, 1, d), lambda i: (l, 0, 0, 0)),
                full(1, 4, d), full(1, db, d), full(1, d, ld), full(1, d, la), full(1, d, lg)]
    args = [x.prompt, x.decode, mod, mod, norm_g4, mu_x, state_shift, decay_lora1, iclr_lora1, gate_lora1]
    row_out = lambda n, dt: (pl.BlockSpec((tm, n), lambda i: (i, 0)), jax.ShapeDtypeStruct((rows, n), dt))
    outs = [row_out(d, _BF16), row_out(ld, _F32), row_out(la, _F32), row_out(lg, _F32)]
    if has_v:
        lv = vres_lora1.shape[2]
        in_specs.append(pl.BlockSpec((1, d, lv), lambda i: (l - 1, 0, 0)))
        args.append(vres_lora1)
        outs.append(row_out(lv, _F32))
    outs.append((pl.BlockSpec((1, _SUBLANES, d), lambda i: (jnp.minimum(i // tps, bp - 1), 0, 0)),
                 jax.ShapeDtypeStruct((bp, _SUBLANES, d), _F32)))
    outs.append((pl.BlockSpec((db, d), lambda i: (0, 0)), jax.ShapeDtypeStruct((db, d), _F32)))
    return pl.pallas_call(
        functools.partial(_norm_lora_kernel, n_pt=n_pt, tps=tps, db=db, has_v=has_v),
        grid=(rows // tm,),
        in_specs=in_specs,
        out_specs=[o[0] for o in outs],
        out_shape=[o[1] for o in outs],
        scratch_shapes=[pltpu.VMEM((_SUBLANES, d), _F32), pltpu.VMEM((tm, d), _F32), pltpu.VMEM((tm, d), _F32)],
        compiler_params=_params("arbitrary"),
    )(*args)


def _mix_kernel(*refs, n_pt, tps, db, has_v):
    it = iter(refs)
    h_ref = next(it)
    w_refs = [next(it) for _ in range(6)]
    aw_ref, aa_ref, ag_ref = next(it), next(it), next(it)
    av_ref = next(it) if has_v else None
    d2_ref, i2_ref, g2_ref = next(it), next(it), next(it)
    v2_ref = next(it) if has_v else None
    w0_ref, a0_ref = next(it), next(it)
    v0_ref = next(it) if has_v else None
    mu_ref, kkw_ref, kaw_ref, cw_ref, hl_ref, cs_ref = (next(it) for _ in range(6))
    vf_ref = next(it) if has_v else None
    seq_prompt = [next(it) for _ in range(7)]
    seq_decode = [next(it) for _ in range(7)]
    oc_o = next(it)
    vf_o = None if has_v else next(it)
    cp_o, cso_o = next(it), next(it)
    wc_scr, pcarry, zcarry = (next(it) for _ in range(3))

    i = pl.program_id(1)
    tm = h_ref.shape[0]
    tn = oc_o.shape[1]

    @pl.when(i == 0)
    def _cache_weights():
        for g in range(6):
            wc_scr[g] = w_refs[g][0].astype(_BF16)

    def project(rs):
        h = h_ref[rs, :]
        return [jnp.dot(h, wc_scr[g], preferred_element_type=_F32) for g in range(6)]

    def tail(rs, p, pprev, bg, z, z1, z2, seq_out):
        r_o, lw_o, k_o, v_o, kk_o, kka_o, g_o = seq_out
        mu = mu_ref[0]
        r = p[0] + (pprev[0] - p[0]) * mu[0:1]
        k = p[1] + (pprev[1] - p[1]) * mu[1:2]
        v = p[2] + (pprev[2] - p[2]) * mu[2:3]

        zlog = w0_ref[0] + _dot(aw_ref[rs, :], d2_ref[0])
        softplus = jnp.maximum(-zlog, 0.0) + jnp.log(1.0 + jnp.exp(-jnp.abs(zlog)))
        lw_o[rs, :] = -jnp.exp(-softplus - 0.5)
        a = jax.nn.sigmoid(a0_ref[0] + _dot(aa_ref[rs, :], i2_ref[0]))
        g_o[rs, :] = _dot(ag_ref[rs, :], g2_ref[0])
        if has_v:
            nu = jax.nn.sigmoid(v0_ref[0] + _dot(av_ref[rs, :], v2_ref[0]))
            v = v + (vf_ref[rs, :] - v) * nu
        else:
            vf_o[rs, :] = v

        kk = k * kkw_ref[0]
        norm = jnp.sqrt(_head_sums(kk * kk, _segment_ones(tn)))
        kk = kk / jnp.maximum(norm, 1e-12)
        r_o[rs, :] = r
        k_o[rs, :] = k * (1.0 + (a - 1.0) * kaw_ref[0])
        v_o[rs, :] = v
        kk_o[rs, :] = kk
        kka_o[rs, :] = kk * a

        cw = cw_ref[0]
        zc = z2 * cw[0:1] + z1 * cw[1:2] + z * cw[2:3]
        oc_o[rs, :] = (bg * zc).astype(_BF16)

    @pl.when(i < n_pt)
    def _prompt():
        half = tm // 2
        halves = (slice(0, half), slice(half, tm))
        proj = [project(rs) for rs in halves]
        zs = [q[4] * q[5] for q in proj]
        for idx, rs in enumerate(halves):
            p, z = proj[idx][:3], zs[idx]
            if idx == 0:
                first = (i % tps) == 0
                p_carry = [pcarry[g] for g in range(3)]
                z_carry = zcarry[...]
            else:
                first = False
                p_carry = [q[half - _SUBLANES:, :] for q in proj[0][:3]]
                z_carry = zs[0][half - _SUBLANES:, :]
            pprev = [_shift_rows_prompt(p[g], p_carry[g], first, two=False) for g in range(3)]
            z1, z2 = _shift_rows_prompt(z, z_carry, first)
            tail(rs, p, pprev, proj[idx][3], z, z1, z2, seq_prompt)
        for g in range(3):
            pcarry[g] = proj[1][g][half - _SUBLANES:, :]
        zcarry[...] = zs[1][half - _SUBLANES:, :]
        cp_o[0] = zs[1][half - _SUBLANES:, :]

    @pl.when(i >= n_pt)
    def _decode():
        rs = slice(0, tm)
        proj = project(rs)
        p, z = proj[:3], proj[4] * proj[5]
        hl = hl_ref[0].astype(_BF16)
        pprev = [jnp.concatenate([jnp.dot(hl, wc_scr[g], preferred_element_type=_F32), p[g][:tm - db, :]], axis=0)
                 for g in range(3)]
        z1 = jnp.concatenate([cs_ref[0, 1], z[:tm - db, :]], axis=0)
        z2 = jnp.concatenate([cs_ref[0, 0], cs_ref[0, 1], z[:tm - 2 * db, :]], axis=0)
        cso_o[0] = z[tm - 2 * db:tm - db, :]
        cso_o[1] = z[tm - db:, :]
        tail(rs, p, pprev, proj[3], z, z1, z2, seq_decode)


def _mix(l, h, acts, vf_in, w_in, lora2, vecs, mu_rkv, k_k3, k_a3, conv_w, state_shift, state_conv_t,
         *, tm, tn, n_pt, tps, db, bp):
    rows, d = h.shape
    has_v = l > 0
    g_dim = mu_rkv.shape[2]
    nj = g_dim // tn
    aw, aa, ag, av = acts
    d2, i2, g2, v2 = lora2
    w0, a0, v0 = vecs
    row_in = lambda arr: pl.BlockSpec((tm, arr.shape[1]), lambda j, i: (i, 0))
    col3 = lambda arr, ll: pl.BlockSpec((1, arr.shape[1], tn), lambda j, i: (ll, 0, j))
    in_specs = [row_in(h)] + [pl.BlockSpec((1, d, tn), lambda j, i, g=g: (l, 0, g * nj + j),
                                           pipeline_mode=pl.Buffered(1)) for g in range(6)]
    args = [h] + [w_in] * 6
    in_specs += [row_in(aw), row_in(aa), row_in(ag)]
    args += [aw, aa, ag]
    if has_v:
        in_specs.append(row_in(av))
        args.append(av)
    in_specs += [col3(d2, l), col3(i2, l), col3(g2, l)]
    args += [d2, i2, g2]
    if has_v:
        in_specs.append(col3(v2, l - 1))
        args.append(v2)
    in_specs += [col3(w0, l), col3(a0, l)]
    args += [w0, a0]
    if has_v:
        in_specs.append(col3(v0, l - 1))
        args.append(v0)
    in_specs += [col3(mu_rkv, l), col3(k_k3, l), col3(k_a3, l), col3(conv_w, l),
                 pl.BlockSpec((1, db, d), lambda j, i: (l, 0, 0)),
                 pl.BlockSpec((1, 2, db, tn), lambda j, i: (l, 0, 0, j))]
    args += [mu_rkv, k_k3, k_a3, conv_w, state_shift, state_conv_t]
    if has_v:
        in_specs.append(pl.BlockSpec((tm, tn), lambda j, i: (i, j)))
        args.append(vf_in)
    tile = pl.BlockSpec((tm, tn), lambda j, i: (i, j))
    prompt_tile = pl.BlockSpec((tm, tn), lambda j, i: (jnp.minimum(i, n_pt - 1), j))
    decode_tile = pl.BlockSpec((tm, tn), lambda j, i: (0, j))
    outs = [(prompt_tile, jax.ShapeDtypeStruct((n_pt * tm, g_dim), _F32)) for _ in range(7)]
    outs += [(decode_tile, jax.ShapeDtypeStruct((tm, g_dim), _F32)) for _ in range(7)]
    outs.append((tile, jax.ShapeDtypeStruct((rows, g_dim), _BF16)))
    if not has_v:
        outs.append((tile, jax.ShapeDtypeStruct((rows, g_dim), _F32)))
    outs.append((pl.BlockSpec((1, _SUBLANES, tn), lambda j, i: (jnp.minimum(i // tps, bp - 1), 0, j)),
                 jax.ShapeDtypeStruct((bp, _SUBLANES, g_dim), _F32)))
    outs.append((pl.BlockSpec((2, db, tn), lambda j, i: (0, 0, j)), jax.ShapeDtypeStruct((2, db, g_dim), _F32)))
    return pl.pallas_call(
        functools.partial(_mix_kernel, n_pt=n_pt, tps=tps, db=db, has_v=has_v),
        grid=(nj, rows // tm),
        in_specs=in_specs,
        out_specs=[o[0] for o in outs],
        out_shape=[o[1] for o in outs],
        scratch_shapes=[pltpu.VMEM((6, d, tn), _BF16), pltpu.VMEM((3, _SUBLANES, tn), _F32),
                        pltpu.VMEM((_SUBLANES, tn), _F32)],
        compiler_params=_params("arbitrary", "arbitrary"),
    )(*args)


def _unit_lower_inverse_minus_identity(lmats, ri, ci, chunk, mul):
    same8 = (ri >> 3) == (ci >> 3)
    d8 = [jnp.where(same8, m, 0.0) for m in lmats]
    d2 = [mul(a, a) for a in d8]
    d3 = [mul(a, b) for a, b in zip(d8, d2)]
    d4 = [mul(b, b) for b in d2]
    x = [a + b + cc for a, b, cc in zip(d8, d2, d3)]
    xd4 = [mul(a, b) for a, b in zip(x, d4)]
    x = [a + b + cc for a, b, cc in zip(x, d4, xd4)]
    size = 16
    while size <= chunk:
        sh = size.bit_length() - 1
        level = ((ri >> sh) == (ci >> sh)) & ((ri >> (sh - 1)) != (ci >> (sh - 1)))
        e = [jnp.where(level, m, 0.0) for m in lmats]
        y = [b + mul(a, b) for a, b in zip(x, e)]
        x = [a + b + mul(b, a) for a, b in zip(x, y)]
        size *= 2
    return x


def _wkv_kernel(r_ref, lw_ref, k_ref, v_ref, kk_ref, kka_ref, g_ref, rk_ref, lnw_ref, lnb_ref,
                o_ref, so_ref, s_scr, *, n_chunks):
    ci_grid = pl.program_id(1)
    nb, c, g_dim = r_ref.shape
    npair = g_dim // _PAIR
    n2 = 2 * c

    lane = lax.broadcasted_iota(jnp.int32, (1, _PAIR), 1)
    m0 = (lane < _HEAD).astype(_F32)
    m1 = 1.0 - m0

    @pl.when(ci_grid == 0)
    def _init():
        s_scr[...] = jnp.zeros(s_scr.shape, _F32)

    ri = lax.broadcasted_iota(jnp.int32, (n2, n2), 0) & (c - 1)
    ci = lax.broadcasted_iota(jnp.int32, (n2, n2), 1) & (c - 1)
    strict = ri > ci
    incl = ri >= ci
    tri = (lax.broadcasted_iota(jnp.int32, (c, c), 0) >= lax.broadcasted_iota(jnp.int32, (c, c), 1)).astype(_BF16)
    seg_ones = _segment_ones(_SEG)

    def stack(x):
        return jnp.concatenate([x * m0, x * m1], axis=0)

    units = [(u, p) for u in range(nb) for p in range(npair)]
    sl = lambda p: slice(p * _PAIR, (p + 1) * _PAIR)

    seq = []
    for u in range(nb):
        lw = lw_ref[u]
        l_hi, l_mid, l_lo = _split3(lw)
        cl = (jnp.dot(tri, l_hi, preferred_element_type=_F32)
              + jnp.dot(tri, l_mid, preferred_element_type=_F32)
              + jnp.dot(tri, l_lo, preferred_element_type=_F32))
        cl_end = cl[c - 1:c, :]
        r, k, v, kk, kka = r_ref[u], k_ref[u], v_ref[u], kk_ref[u], kka_ref[u]
        e_neg = jnp.exp(-cl)
        e_tail = jnp.exp(cl_end - cl)
        seq.append(dict(r=r, k=k, v=v, p_end=jnp.exp(cl_end), rt=r * jnp.exp(cl), at=-(kk * jnp.exp(cl - lw)),
                        kt=k * e_neg, bt=kka * e_neg, kh=k * e_tail, bh=kka * e_tail))

    ar2 = [jnp.concatenate([stack(seq[u]["at"][:, sl(p)]), stack(seq[u]["rt"][:, sl(p)])], axis=0).astype(_BF16)
           for u, p in units]
    mb = [_dot_nt(a, stack(seq[u]["bt"][:, sl(p)])) for a, (u, p) in zip(ar2, units)]
    mk = [_dot_nt(a, stack(seq[u]["kt"][:, sl(p)])) for a, (u, p) in zip(ar2, units)]
    s_old = [s_scr[u, p] for u, p in units]
    uy0 = [_dot_nt(jnp.concatenate([seq[u]["at"][:, sl(p)], seq[u]["rt"][:, sl(p)]], axis=0), s)
           for (u, p), s in zip(units, s_old)]
    x_inv = _unit_lower_inverse_minus_identity([jnp.where(strict, m[:n2], 0.0) for m in mb], ri, ci, c, _dot)
    vs = [stack(seq[u]["v"][:, sl(p)]).astype(_BF16) for u, p in units]
    ws = [stack(q[:c]) + _dot(jnp.where(strict, m[:n2], 0.0), vv) for q, m, vv in zip(uy0, mk, vs)]
    us = [(w + _dot(xi, w)).astype(_BF16) for xi, w in zip(x_inv, ws)]
    ysd = [_dot(jnp.where(incl, m1_[n2:], 0.0), uu) + _dot(jnp.where(incl, m2_[n2:], 0.0), vv)
           for m1_, m2_, uu, vv in zip(mb, mk, us, vs)]
    ys = [q[c:] + d[:c] + d[c:] for q, d in zip(uy0, ysd)]
    for idx, (u, p) in enumerate(units):
        uv = jnp.concatenate([us[idx], vs[idx]], axis=0)
        bk = jnp.concatenate([stack(seq[u]["bh"][:, sl(p)]), stack(seq[u]["kh"][:, sl(p)])], axis=0)
        s_scr[u, p] = s_old[idx] * seq[u]["p_end"][:, sl(p)] + _dot_tn(uv, bk)
    for u in range(nb):
        y = jnp.concatenate(ys[u * npair:(u + 1) * npair], axis=1)
        mean = _head_sums(y, seg_ones) * (1.0 / _HEAD)
        dy = y - mean
        var = _head_sums(dy * dy, seg_ones) * (1.0 / _HEAD)
        yn = dy * lax.rsqrt(var + _GN_EPS) * lnw_ref[...] + lnb_ref[...]
        bonus = _head_sums(seq[u]["r"] * seq[u]["k"] * rk_ref[...], seg_ones) * seq[u]["v"]
        o_ref[u] = ((yn + bonus) * g_ref[u]).astype(o_ref.dtype)

    @pl.when(ci_grid == n_chunks - 1)
    def _final():
        for u, p in units:
            s = s_scr[u, p]
            so_ref[u, 2 * p] = s[:_HEAD, :_HEAD]
            so_ref[u, 2 * p + 1] = s[_HEAD:, _HEAD:]


def _wkv(l, seqs, chunk, nb, r_k3, ln_w3, ln_b3):
    n_seq, t_len, g_dim = seqs[0].shape
    n_heads = g_dim // _HEAD
    n_chunks = t_len // chunk
    blk = pl.BlockSpec((nb, chunk, g_dim), lambda bi, ci: (bi, ci, 0))
    vec = pl.BlockSpec((None, 1, g_dim), lambda bi, ci: (l, 0, 0))
    return pl.pallas_call(
        functools.partial(_wkv_kernel, n_chunks=n_chunks),
        grid=(n_seq // nb, n_chunks),
        in_specs=[blk] * 7 + [vec] * 3,
        out_specs=[blk, pl.BlockSpec((nb, n_heads, _HEAD, _HEAD), lambda bi, ci: (bi, 0, 0, 0))],
        out_shape=[jax.ShapeDtypeStruct((n_seq, t_len, g_dim), _BF16),
                   jax.ShapeDtypeStruct((n_seq, n_heads, _HEAD, _HEAD), _F32)],
        scratch_shapes=[pltpu.VMEM((nb, g_dim // _PAIR, _PAIR, _PAIR), _F32)],
        compiler_params=_params("arbitrary", "arbitrary"),
    )(*seqs, r_k3, ln_w3, ln_b3)


def _wkv_decode_kernel(r_ref, lw_ref, k_ref, v_ref, kk_ref, kka_ref, g_ref, rk_ref, lnw_ref, lnb_ref, s0_ref,
                       o_ref, so_ref, *, ts, nb):
    g_dim = r_ref.shape[2]
    npair = g_dim // _PAIR
    n = ts * nb
    n2 = 2 * n
    nb_bits = nb.bit_length() - 1

    lane = lax.broadcasted_iota(jnp.int32, (1, _PAIR), 1)
    m0 = (lane < _HEAD).astype(_F32)
    m1 = 1.0 - m0

    def stack(x):
        return jnp.concatenate([x * m0, x * m1], axis=0)

    def rows_tb(ref):
        return jnp.concatenate([ref[t] for t in range(ts)], axis=0)

    lw_t = [lw_ref[t] for t in range(ts)]
    cl_t = [lw_t[0]]
    for t in range(1, ts):
        cl_t.append(cl_t[-1] + lw_t[t])
    cl = jnp.concatenate(cl_t, axis=0)
    lw = jnp.concatenate(lw_t, axis=0)
    cl_end_b = cl_t[-1]
    cl_end = _tile_rows(cl_end_b, ts)
    p_end = jnp.exp(cl_end_b)
    r, k, v, kk, kka = (rows_tb(ref) for ref in (r_ref, k_ref, v_ref, kk_ref, kka_ref))
    e_neg = jnp.exp(-cl)
    e_tail = jnp.exp(cl_end - cl)
    rt = r * jnp.exp(cl)
    at = -(kk * jnp.exp(cl - lw))
    kt = k * e_neg
    bt = kka * e_neg
    kh = k * e_tail
    bh = kka * e_tail

    ri = lax.broadcasted_iota(jnp.int32, (n2, n2), 0)
    ci = lax.broadcasted_iota(jnp.int32, (n2, n2), 1)
    same_seq = (ri & (nb - 1)) == (ci & (nb - 1))
    t_r = (ri & (n - 1)) >> nb_bits
    t_c = (ci & (n - 1)) >> nb_bits
    strict = same_seq & (t_r > t_c)
    incl = same_seq & (t_r >= t_c)
    seq_of_row2 = lax.broadcasted_iota(jnp.int32, (n2, 1), 0) & (nb - 1)
    seq_of_row4 = lax.broadcasted_iota(jnp.int32, (2 * n2, 1), 0) & (nb - 1)
    zero = jnp.zeros((_HEAD, _HEAD), _F32)

    pairs = range(npair)
    sls = [slice(p * _PAIR, (p + 1) * _PAIR) for p in pairs]
    ar2 = [jnp.concatenate([stack(at[:, sl]), stack(rt[:, sl])], axis=0).astype(_BF16) for sl in sls]
    mb = [_dot_nt(a, stack(bt[:, sl])) for a, sl in zip(ar2, sls)]
    mk = [_dot_nt(a, stack(kt[:, sl])) for a, sl in zip(ar2, sls)]
    lab = [jnp.where(strict, m[:n2], 0.0) for m in mb]
    d2 = [_dot(a, a) for a in lab]
    d3 = [_dot(a, b) for a, b in zip(lab, d2)]
    x_inv = [a + b + cc for a, b, cc in zip(lab, d2, d3)]

    def block_diag(b, p):
        top = jnp.concatenate([s0_ref[b, 2 * p], zero], axis=1)
        bot = jnp.concatenate([zero, s0_ref[b, 2 * p + 1]], axis=1)
        return jnp.concatenate([top, bot], axis=0)

    ys = []
    for p, sl in zip(pairs, sls):
        s_b = [block_diag(b, p) for b in range(nb)]
        ar = jnp.concatenate([at[:, sl], rt[:, sl]], axis=0)
        ar_cat = jnp.concatenate([jnp.where(seq_of_row2 == b, ar, 0.0).astype(_BF16) for b in range(nb)], axis=1)
        s_cat = jnp.concatenate([s.astype(_BF16) for s in s_b], axis=1)
        uy0 = _dot_nt(ar_cat, s_cat)
        vs = stack(v[:, sl]).astype(_BF16)
        ws = stack(uy0[:n]) + _dot(jnp.where(strict, mk[p][:n2], 0.0), vs)
        us = (ws + _dot(x_inv[p], ws)).astype(_BF16)
        ysd = _dot(jnp.where(incl, mb[p][n2:], 0.0), us) + _dot(jnp.where(incl, mk[p][n2:], 0.0), vs)
        ys.append(uy0[n:] + ysd[:n] + ysd[n:])
        uv = jnp.concatenate([us, vs], axis=0)
        uv_cat = jnp.concatenate([jnp.where(seq_of_row4 == b, uv, jnp.zeros_like(uv)) for b in range(nb)], axis=1)
        bk = jnp.concatenate([stack(bh[:, sl]), stack(kh[:, sl])], axis=0)
        upd = _dot_tn(uv_cat, bk)
        for b in range(nb):
            s_new = (s_b[b] * p_end[b:b + 1, sl] + upd[b * _PAIR:(b + 1) * _PAIR, :])
            so_ref[b, 2 * p] = s_new[:_HEAD, :_HEAD]
            so_ref[b, 2 * p + 1] = s_new[_HEAD:, _HEAD:]

    y = jnp.concatenate(ys, axis=1)
    seg_ones = _segment_ones(_SEG)
    mean = _head_sums(y, seg_ones) * (1.0 / _HEAD)
    dy = y - mean
    var = _head_sums(dy * dy, seg_ones) * (1.0 / _HEAD)
    yn = dy * lax.rsqrt(var + _GN_EPS) * lnw_ref[...] + lnb_ref[...]
    bonus = _head_sums(r * k * rk_ref[...], seg_ones) * v
    out = ((yn + bonus) * rows_tb(g_ref)).astype(o_ref.dtype)
    for t in range(ts):
        o_ref[t] = out[t * nb:(t + 1) * nb, :]


def _wkv_decode(l, seqs, state0, r_k3, ln_w3, ln_b3, *, ts, db, nb, first_block):
    g_dim = seqs[0].shape[2]
    n_heads = g_dim // _HEAD
    blk = pl.BlockSpec((ts, nb, g_dim), lambda bi: (first_block, bi, 0))
    vec = pl.BlockSpec((None, 1, g_dim), lambda bi: (l, 0, 0))
    st = pl.BlockSpec((nb, n_heads, _HEAD, _HEAD), lambda bi: (bi, 0, 0, 0))
    return pl.pallas_call(
        functools.partial(_wkv_decode_kernel, ts=ts, nb=nb),
        grid=(db // nb,),
        in_specs=[blk] * 7 + [vec] * 3 + [st],
        out_specs=[pl.BlockSpec((ts, nb, g_dim), lambda bi: (0, bi, 0)), st],
        out_shape=[jax.ShapeDtypeStruct((ts, db, g_dim), _BF16),
                   jax.ShapeDtypeStruct((db, n_heads, _HEAD, _HEAD), _F32)],
        compiler_params=_params("arbitrary"),
    )(*seqs, r_k3, ln_w3, ln_b3, state0)


def _cast_kernel(w_ref, o_ref):
    o_ref[...] = w_ref[...].astype(o_ref.dtype)


def _cast_bf16(w, rows_per_block):
    depth, k, n = w.shape
    spec = pl.BlockSpec((1, rows_per_block, n), lambda l, i: (l, i, 0))
    return pl.pallas_call(
        _cast_kernel, grid=(depth, k // rows_per_block), in_specs=[spec], out_specs=spec,
        out_shape=jax.ShapeDtypeStruct(w.shape, _BF16),
        compiler_params=_params("arbitrary", "arbitrary"),
    )(w)


def _outproj_kernel(op_ref, os_ref, oc_ref, xp_ref, xs_ref, ga_ref, sh_ref, sc_ref, g_ref, w_ref,
                    x1_ref, h2_ref, *, n_pt, tps, db):
    i = pl.program_id(0)
    tm = xp_ref.shape[0]
    ts = tm // db
    g_dim = op_ref.shape[1]

    def rows_out(rs, o_ref, x_ref, ga, sc, sh):
        acc = (jnp.dot(o_ref[rs, :], w_ref[0, :g_dim, :], preferred_element_type=_F32)
               + jnp.dot(oc_ref[rs, :], w_ref[0, g_dim:, :], preferred_element_type=_F32))
        x1 = x_ref[rs, :] + ga * acc
        x1_ref[rs, :] = x1
        xn = x1 * lax.rsqrt(jnp.mean(x1 * x1, axis=-1, keepdims=True) + _RMS_EPS) * g_ref[0, 0]
        h2_ref[rs, :] = (xn * (1.0 + sc) + sh).astype(_BF16)

    half = tm // 2
    halves = (slice(0, half), slice(half, tm))

    @pl.when(i < n_pt)
    def _prompt():
        ga, sc, sh = (_mod_row(ref, i, tps, db) for ref in (ga_ref, sc_ref, sh_ref))
        for rs in halves:
            rows_out(rs, op_ref, xp_ref, ga, sc, sh)

    @pl.when(i >= n_pt)
    def _decode():
        ga, sc, sh = (_mod_tile(ref, db, ts) for ref in (ga_ref, sc_ref, sh_ref))
        for rs in halves:
            rows_out(rs, os_ref, xs_ref, ga[rs, :], sc[rs, :], sh[rs, :])


def _outproj(l, o_p, o_s, o_conv, x, mod, norm_g4, w_out, *, tm, n_pt, tps, db):
    d = x.prompt.shape[1]
    rows = (n_pt + 1) * tm
    g_dim = o_p.shape[1]
    bc = mod.shape[2]
    modspec = lambda comp: pl.BlockSpec((1, 1, bc, d), lambda i: (l, comp, 0, 0))
    return pl.pallas_call(
        functools.partial(_outproj_kernel, n_pt=n_pt, tps=tps, db=db),
        grid=(rows // tm,),
        in_specs=[pl.BlockSpec((tm, g_dim), lambda i: (jnp.minimum(i, n_pt - 1), 0)),
                  pl.BlockSpec((tm, g_dim), lambda i: (0, 0)),
                  pl.BlockSpec((tm, o_conv.shape[1]), lambda i: (i, 0))] + _act_specs(x, tm, n_pt) + [
                  modspec(2), modspec(3), modspec(4),
                  pl.BlockSpec((1, 1, 1, d), lambda i: (l, 1, 0, 0)),
                  pl.BlockSpec((1, d, d), lambda i: (l, 0, 0), pipeline_mode=pl.Buffered(1))],
        out_specs=[pl.BlockSpec((tm, d), lambda i: (i, 0)), pl.BlockSpec((tm, d), lambda i: (i, 0))],
        out_shape=[jax.ShapeDtypeStruct((rows, d), _F32), jax.ShapeDtypeStruct((rows, d), _BF16)],
        compiler_params=_params("arbitrary"),
    )(o_p, o_s, o_conv, x.prompt, x.decode, mod, mod, mod, norm_g4, w_out)


def _ffn_kernel(*refs, n_pt, tps, db, nj, final_norm):
    it = iter(refs)
    (h2_ref, wa_ref, wb_ref, cwa_ref, cwb_ref, wd_ref, x1_ref, ga_ref, sfa_ref, sfb_ref) = (next(it) for _ in range(10))
    fg_ref = next(it) if final_norm else None
    x2_ref = next(it)
    x2s_ref = next(it) if final_norm else x2_ref
    fpa_o, fpb_o, fsa_o, fsb_o = (next(it) for _ in range(4))
    acc_scr, ca_scr, cb_scr = (next(it) for _ in range(3))
    i = pl.program_id(0)
    j = pl.program_id(1)
    tm = h2_ref.shape[0]
    ts = tm // db

    @pl.when(j == 0)
    def _zero():
        acc_scr[...] = jnp.zeros(acc_scr.shape, _F32)

    def prompt_body():
        half = tm // 2
        halves = (slice(0, half), slice(half, tm))
        us = [[jnp.dot(h2_ref[rs, :], w_ref[0], preferred_element_type=_F32) for w_ref in (wa_ref, wb_ref)]
              for rs in halves]
        for r, rs in enumerate(halves):
            conv = []
            for idx, (cw_ref, c_scr, fp_o) in enumerate(((cwa_ref, ca_scr, fpa_o), (cwb_ref, cb_scr, fpb_o))):
                u = us[r][idx]
                if r == 0:
                    u1, u2 = _shift_rows_prompt(u, c_scr[j], (i % tps) == 0)
                else:
                    u1, u2 = _shift_rows_prompt(u, us[0][idx][half - _SUBLANES:, :], False)
                    c_scr[j] = u[half - _SUBLANES:, :]
                    fp_o[0] = u[half - _SUBLANES:, :]
                cw = cw_ref[0]
                conv.append(u2 * cw[0:1] + u1 * cw[1:2] + u * cw[2:3])
            gact = ((conv[0] * jax.nn.sigmoid(conv[0])) * conv[1]).astype(_BF16)
            acc_scr[rs, :] += jnp.dot(gact, wd_ref[0], preferred_element_type=_F32)

    def body(prompt):
        if prompt:
            return prompt_body()
        first = (i % tps) == 0
        h2 = h2_ref[...]
        ua = jnp.dot(h2, wa_ref[0], preferred_element_type=_F32)
        ub = jnp.dot(h2, wb_ref[0], preferred_element_type=_F32)
        conv = []
        for u, cw_ref, c_scr, sf, fp_o, fs_o in ((ua, cwa_ref, ca_scr, sfa_ref, fpa_o, fsa_o),
                                                 (ub, cwb_ref, cb_scr, sfb_ref, fpb_o, fsb_o)):
            if prompt:
                u1, u2 = _shift_rows_prompt(u, c_scr[j], first)
                c_scr[j] = u[tm - _SUBLANES:, :]
                fp_o[0] = u[tm - _SUBLANES:, :]
            else:
                u1 = jnp.concatenate([sf[0, 1], u[:tm - db, :]], axis=0)
                u2 = jnp.concatenate([sf[0, 0], sf[0, 1], u[:tm - 2 * db, :]], axis=0)
                fp_o[0] = jnp.zeros(fp_o.shape[1:], _F32)
                fs_o[0] = u[tm - 2 * db:tm - db, :]
                fs_o[1] = u[tm - db:, :]
            cw = cw_ref[0]
            conv.append(u2 * cw[0:1] + u1 * cw[1:2] + u * cw[2:3])
        gact = ((conv[0] * jax.nn.sigmoid(conv[0])) * conv[1]).astype(_BF16)
        acc_scr[...] += jnp.dot(gact, wd_ref[0], preferred_element_type=_F32)

    @pl.when(i < n_pt)
    def _prompt():
        body(True)

    @pl.when(i >= n_pt)
    def _decode():
        body(False)

    def finish(ga, out_ref):
        x2 = x1_ref[...] + ga * acc_scr[...]
        if final_norm:
            x2 = x2 * lax.rsqrt(jnp.mean(x2 * x2, axis=-1, keepdims=True) + _RMS_EPS) * fg_ref[...]
        out_ref[...] = x2

    @pl.when((j == nj - 1) & (i < n_pt))
    def _finish_prompt():
        finish(_mod_row(ga_ref, i, tps, db), x2_ref)

    @pl.when((j == nj - 1) & (i >= n_pt))
    def _finish_decode():
        finish(_mod_tile(ga_ref, db, ts), x2s_ref)


def _ffn(l, h2, x1, mod, ffn_up, ffn_conv, ffn_down, state_ffn_t, final_g, *, tm, tn, n_pt, tps, db, bp):
    rows, d = x1.shape
    f = ffn_down.shape[1]
    nj = f // tn
    bc = mod.shape[2]
    final_norm = final_g is not None
    in_specs = [pl.BlockSpec((tm, d), lambda i, j: (i, 0)),
                pl.BlockSpec((1, d, tn), lambda i, j: (l, 0, j)),
                pl.BlockSpec((1, d, tn), lambda i, j: (l, 0, nj + j)),
                pl.BlockSpec((1, 3, tn), lambda i, j: (l, 0, j)),
                pl.BlockSpec((1, 3, tn), lambda i, j: (l, 0, nj + j)),
                pl.BlockSpec((1, tn, d), lambda i, j: (l, j, 0)),
                pl.BlockSpec((tm, d), lambda i, j: (i, 0)),
                pl.BlockSpec((1, 1, bc, d), lambda i, j: (l, 5, 0, 0)),
                pl.BlockSpec((1, 2, db, tn), lambda i, j: (l, 0, 0, j)),
                pl.BlockSpec((1, 2, db, tn), lambda i, j: (l, 0, 0, nj + j))]
    args = [h2, ffn_up, ffn_up, ffn_conv, ffn_conv, ffn_down, x1, mod, state_ffn_t, state_ffn_t]
    if final_norm:
        in_specs.append(pl.BlockSpec((1, d), lambda i, j: (0, 0)))
        args.append(final_g)
    n_tiles = rows // tm
    pstate = pl.BlockSpec((1, _SUBLANES, tn), lambda i, j: (i, 0, j))
    sstate = pl.BlockSpec((2, db, tn), lambda i, j: (0, 0, jnp.where(i >= n_pt, j, 0)))
    if final_norm:
        x_specs = [pl.BlockSpec((tm, d), lambda i, j: (jnp.minimum(i, n_pt - 1), 0)),
                   pl.BlockSpec((tm, d), lambda i, j: (0, 0))]
        x_shapes = [jax.ShapeDtypeStruct((n_pt * tm, d), _F32), jax.ShapeDtypeStruct((tm, d), _F32)]
    else:
        x_specs = [pl.BlockSpec((tm, d), lambda i, j: (i, 0))]
        x_shapes = [jax.ShapeDtypeStruct((rows, d), _F32)]
    return pl.pallas_call(
        functools.partial(_ffn_kernel, n_pt=n_pt, tps=tps, db=db, nj=nj, final_norm=final_norm),
        grid=(rows // tm, nj),
        in_specs=in_specs,
        out_specs=x_specs + [pstate, pstate, sstate, sstate],
        out_shape=x_shapes + [
                   jax.ShapeDtypeStruct((n_tiles, _SUBLANES, f), _F32),
                   jax.ShapeDtypeStruct((n_tiles, _SUBLANES, f), _F32),
                   jax.ShapeDtypeStruct((2, db, f), _F32), jax.ShapeDtypeStruct((2, db, f), _F32)],
        scratch_shapes=[pltpu.VMEM((tm, d), _F32), pltpu.VMEM((nj, _SUBLANES, tn), _F32),
                        pltpu.VMEM((nj, _SUBLANES, tn), _F32)],
        compiler_params=_params("arbitrary", "arbitrary"),
    )(*args)


def _forward(x_prompt, x_sample, c_prompt, c_sample, state_wkv, state_shift, state_conv, state_ffn,
             ada_w, ada_b, norm_g, final_norm_g, w_in, mu_x, mu_rkv, decay_w0, decay_lora1, decay_lora2,
             iclr_a0, iclr_lora1, iclr_lora2, gate_lora1, gate_lora2, vres_v0, vres_lora1, vres_lora2,
             k_k, k_a, r_k, ln_x_w, ln_x_b, conv_w, w_out, ffn_up, ffn_conv, ffn_down,
             *, chunk=64, mix_tn=256, ffn_tn=512, wkv_nb=16, wkv_prompt_nb=2):
    bp, t_len, d = x_prompt.shape
    db, ts, _ = x_sample.shape
    depth = ada_w.shape[0]
    g_dim = mu_rkv.shape[2]
    n_heads = g_dim // _HEAD
    f = ffn_down.shape[1]
    tm = ts * db
    assert t_len % tm == 0 and t_len % chunk == 0 and ts >= 2 and db % _SUBLANES == 0 and tm % chunk == 0
    tps = t_len // tm
    n_pt = bp * tps
    n_prompt_rows = bp * t_len
    assert ts == 4 and db % wkv_nb == 0 and wkv_nb & (wkv_nb - 1) == 0

    x = _Act(x_prompt.reshape(n_prompt_rows, d), x_sample.transpose(1, 0, 2).reshape(tm, d), 0)
    pad =(-(db + bp)) % _SUBLANES
    c_all = jnp.concatenate([c_sample, c_prompt, jnp.zeros((pad, d), _F32)], axis=0)
    mod = _adaln_mod(c_all, ada_w, ada_b)

    norm_g4 = norm_g.reshape(depth, 2, 1, d)
    vec3 = lambda a: a.reshape(a.shape[0], 1, a.shape[1])
    w0_3, a0_3, v0_3, kk_3, ka_3 = vec3(decay_w0), vec3(iclr_a0), vec3(vres_v0), vec3(k_k), vec3(k_a)
    rk_3, lnw_3, lnb_3 = r_k.reshape(depth, 1, g_dim), vec3(ln_x_w), vec3(ln_x_b)
    state_conv_t = state_conv.transpose(0, 2, 1, 3)
    state_ffn_t = state_ffn.transpose(0, 2, 1, 3)
    tiles = dict(tm=tm, n_pt=n_pt, tps=tps, db=db)
    w_out = _cast_bf16(w_out, 512)
    ffn_up = _cast_bf16(ffn_up, 128)
    ffn_down = _cast_bf16(ffn_down, 512)

    new = {k: [] for k in ("wkv_p", "shift_p", "conv_p", "ffn_p", "wkv_s", "shift_s", "conv_s", "ffn_s")}
    v_first = None
    for l in range(depth):
        outs = _norm_lora(l, x, mod, norm_g4, mu_x, state_shift, decay_lora1, iclr_lora1, gate_lora1,
                          vres_lora1, bp=bp, **tiles)
        if l > 0:
            h, aw, aa, ag, av, hp_last, hs_last = outs
        else:
            h, aw, aa, ag, hp_last, hs_last = outs
            av = None
        mouts = _mix(l, h, (aw, aa, ag, av), v_first, w_in,
                     (decay_lora2, iclr_lora2, gate_lora2, vres_lora2), (w0_3, a0_3, v0_3),
                     mu_rkv, kk_3, ka_3, conv_w, state_shift, state_conv_t, tn=mix_tn, bp=bp, **tiles)
        seqs_p, seqs_s, rest = mouts[:7], mouts[7:14], mouts[14:]
        if l > 0:
            o_conv, conv_p8, conv_s = rest
        else:
            o_conv, v_first, conv_p8, conv_s = rest

        o_p, wkv_p = _wkv(l, tuple(a.reshape(bp, t_len, g_dim) for a in seqs_p), chunk, wkv_prompt_nb,
                          rk_3, lnw_3, lnb_3)
        o_p = o_p.reshape(n_prompt_rows, g_dim)
        o_s, wkv_s = _wkv_decode(l, tuple(a.reshape(ts, db, g_dim) for a in seqs_s), state_wkv[l],
                                 rk_3, lnw_3, lnb_3, ts=ts, db=db, nb=wkv_nb, first_block=0)
        o_s = o_s.reshape(tm, g_dim)

        x1, h2 = _outproj(l, o_p, o_s, o_conv, x, mod, norm_g4, w_out, **tiles)
        final_g = final_norm_g.reshape(1, d) if l == depth - 1 else None
        *x_out, fpa, fpb, fsa, fsb = _ffn(l, h2, x1, mod, ffn_up, ffn_conv, ffn_down, state_ffn_t, final_g,
                                          tn=ffn_tn, bp=bp, **tiles)
        x = _Act(x_out[0], x_out[0], n_pt) if len(x_out) == 1 else _Act(x_out[0], x_out[1], 0)

        new["wkv_p"].append(wkv_p)
        new["wkv_s"].append(wkv_s)
        new["shift_p"].append(hp_last[:, _SUBLANES - 1])
        new["shift_s"].append(hs_last)
        new["conv_p"].append(conv_p8[:, _SUBLANES - 2:])
        new["conv_s"].append(conv_s.transpose(1, 0, 2))
        last_tiles = slice(tps - 1, n_pt, tps)
        new["ffn_p"].append(jnp.concatenate([fpa[last_tiles, _SUBLANES - 2:], fpb[last_tiles, _SUBLANES - 2:]],
                                            axis=-1))
        new["ffn_s"].append(jnp.concatenate([fsa, fsb], axis=-1).transpose(1, 0, 2))

    y_prompt = x.prompt.reshape(bp, t_len, d)
    y_sample = x.decode.reshape(ts, db, d).transpose(1, 0, 2)
    st = {k: jnp.stack(vs) for k, vs in new.items()}
    return (y_prompt, y_sample, st["wkv_p"], st["shift_p"], st["conv_p"], st["ffn_p"],
            st["wkv_s"], st["shift_s"], st["conv_s"], st["ffn_s"])


def kernel(x_prompt, x_sample, c_prompt, c_sample, state_wkv, state_shift, state_conv, state_ffn, ada_w, ada_b, norm_g, final_norm_g, w_in, mu_x, mu_rkv, decay_w0, decay_lora1, decay_lora2, iclr_a0, iclr_lora1, iclr_lora2, gate_lora1, gate_lora2, vres_v0, vres_lora1, vres_lora2, k_k, k_a, r_k, ln_x_w, ln_x_b, conv_w, w_out, ffn_up, ffn_conv, ffn_down):
    return _forward(x_prompt, x_sample, c_prompt, c_sample, state_wkv, state_shift, state_conv, state_ffn,
                    ada_w, ada_b, norm_g, final_norm_g, w_in, mu_x, mu_rkv, decay_w0, decay_lora1, decay_lora2,
                    iclr_a0, iclr_lora1, iclr_lora2, gate_lora1, gate_lora2, vres_v0, vres_lora1, vres_lora2,
                    k_k, k_a, r_k, ln_x_w, ln_x_b, conv_w, w_out, ffn_up, ffn_conv, ffn_down)
```

```python
import functools
from typing import NamedTuple

import jax
import jax.numpy as jnp
from jax import lax
from jax.experimental import pallas as pl
from jax.experimental.pallas import tpu as pltpu

_F32 = jnp.float32
_BF16 = jnp.bfloat16

_HEAD = 64
_PAIR = 2 * _HEAD
_SEG = 256
_RMS_EPS = 1e-6
_GN_EPS = 64e-5
_VMEM_LIMIT_BYTES = 56 * 1024 * 1024
_SUBLANES = 8


def _params(*sem):
    return pltpu.CompilerParams(dimension_semantics=sem, vmem_limit_bytes=_VMEM_LIMIT_BYTES)


def _dot(a, b):
    return jnp.dot(a.astype(_BF16), b.astype(_BF16), preferred_element_type=_F32)


def _dot_nt(a, b):
    return lax.dot_general(a.astype(_BF16), b.astype(_BF16), (((1,), (1,)), ((), ())),
                           preferred_element_type=_F32)


def _dot_tn(a, b):
    return lax.dot_general(a.astype(_BF16), b.astype(_BF16), (((0,), (0,)), ((), ())),
                           preferred_element_type=_F32)


def _split2(x):
    hi = x.astype(_BF16)
    lo = (x - hi.astype(_F32)).astype(_BF16)
    return hi, lo


def _split3(x):
    hi = x.astype(_BF16)
    r1 = x - hi.astype(_F32)
    mid = r1.astype(_BF16)
    lo = (r1 - mid.astype(_F32)).astype(_BF16)
    return hi, mid, lo


def _dot_exact_rhs(x, m_bf16):
    hi, lo = _split2(x)
    return (jnp.dot(hi, m_bf16, preferred_element_type=_F32)
            + jnp.dot(lo, m_bf16, preferred_element_type=_F32))


def _segment_ones(n):
    r = lax.broadcasted_iota(jnp.int32, (n, n), 0) // _HEAD
    c = lax.broadcasted_iota(jnp.int32, (n, n), 1) // _HEAD
    return (r == c).astype(_BF16)


def _head_sums(x, seg_ones):
    n = seg_ones.shape[0]
    parts = [_dot_exact_rhs(x[:, q * n:(q + 1) * n], seg_ones) for q in range(x.shape[1] // n)]
    return parts[0] if len(parts) == 1 else jnp.concatenate(parts, axis=1)


def _tile_rows(m, reps):
    return m if reps == 1 else jnp.concatenate([m] * reps, axis=0)


def _shift_rows_prompt(x, carry8, first, two=True):
    r8 = lax.broadcasted_iota(jnp.int32, (_SUBLANES, 1), 0)
    l1 = jnp.where(first, 0.0, carry8[_SUBLANES - 1:_SUBLANES, :])
    x1 = pltpu.roll(x, 1, 0)
    x1 = jnp.concatenate([jnp.where(r8 == 0, l1, x1[:_SUBLANES, :]), x1[_SUBLANES:, :]], axis=0)
    if not two:
        return x1
    l2 = jnp.where(first, 0.0, carry8[_SUBLANES - 2:_SUBLANES - 1, :])
    x2 = pltpu.roll(x, 2, 0)
    top2 = jnp.where(r8 == 0, l2, jnp.where(r8 == 1, l1, x2[:_SUBLANES, :]))
    return x1, jnp.concatenate([top2, x2[_SUBLANES:, :]], axis=0)


class _Act(NamedTuple):
    prompt: jax.Array
    decode: jax.Array
    decode_block: int


def _act_specs(x, tm, n_pt):
    d = x.prompt.shape[1]
    return [pl.BlockSpec((tm, d), lambda i, *_: (jnp.minimum(i, n_pt - 1), 0)),
            pl.BlockSpec((tm, d), lambda i, *_: (x.decode_block, 0))]


def _mod_row(ref, i, tps, db):
    return ref[0, 0, pl.ds(db + i // tps, 1), :]


def _mod_tile(ref, db, ts):
    return _tile_rows(ref[0, 0, 0:db, :], ts)


def _mod_kernel(c_ref, w_ref, b_ref, o_ref):
    c = c_ref[...]
    s_hi, s_lo = _split2(c * jax.nn.sigmoid(c))
    w = w_ref[0].astype(_BF16)
    acc = jnp.dot(s_hi, w, preferred_element_type=_F32) + jnp.dot(s_lo, w, preferred_element_type=_F32)
    o_ref[0, 0] = acc + b_ref[0]


def _adaln_mod(c_all, ada_w, ada_b):
    depth, d, six_d = ada_w.shape
    bc = c_all.shape[0]
    tn = 1024
    per = d // tn
    return pl.pallas_call(
        _mod_kernel,
        grid=(depth, six_d // tn),
        in_specs=[pl.BlockSpec((bc, d), lambda l, n: (0, 0)),
                  pl.BlockSpec((1, d, tn), lambda l, n: (l, 0, n)),
                  pl.BlockSpec((1, 1, tn), lambda l, n: (l, 0, n))],
        out_specs=pl.BlockSpec((1, 1, bc, tn), lambda l, n: (l, n // per, 0, n % per)),
        out_shape=jax.ShapeDtypeStruct((depth, 6, bc, d), _F32),
        compiler_params=_params("arbitrary", "arbitrary"),
    )(c_all, ada_w, ada_b.reshape(depth, 1, six_d))


def _norm_lora_kernel(*refs, n_pt, tps, db, has_v):
    n_lora = 4 if has_v else 3
    it = iter(refs)
    xp_ref, xs_ref, sh_ref, sc_ref, g_ref, mu_ref, hl_ref = (next(it) for _ in range(7))
    w_refs = [next(it) for _ in range(n_lora)]
    h_ref = next(it)
    act_refs = [next(it) for _ in range(n_lora)]
    hp_ref, hs_ref, carry_ref, hcur_ref, hprev_ref = (next(it) for _ in range(5))
    wh_scr = [next(it) for _ in range(n_lora)]
    wx_scr = [next(it) for _ in range(n_lora)]
    i = pl.program_id(0)

    @pl.when(i == 0)
    def _cache_weights():
        for g in range(n_lora):
            w = w_refs[g][0]
            wh_scr[g][...] = w.astype(_BF16)
            wx_scr[g][...] = (w * mu_ref[0][:, g:g + 1]).astype(_BF16)

    tm = xp_ref.shape[0]
    ts = tm // db
    x = jnp.where(i < n_pt, xp_ref[...], xs_ref[...])
    xn = x * lax.rsqrt(jnp.mean(x * x, axis=-1, keepdims=True) + _RMS_EPS) * g_ref[0, 0]

    @pl.when(i < n_pt)
    def _prompt():
        h = xn * (1.0 + _mod_row(sc_ref, i, tps, db)) + _mod_row(sh_ref, i, tps, db)
        first = (i % tps) == 0
        hprev_ref[...] = _shift_rows_prompt(h, carry_ref[...], first, two=False)
        hcur_ref[...] = h
        carry_ref[...] = h[tm - _SUBLANES:, :]
        hp_ref[0] = h[tm - _SUBLANES:, :]

    @pl.when(i >= n_pt)
    def _decode():
        h = xn * (1.0 + _mod_tile(sc_ref, db, ts)) + _mod_tile(sh_ref, db, ts)
        hprev_ref[...] = jnp.concatenate([hl_ref[0], h[:tm - db, :]], axis=0)
        hcur_ref[...] = h
        hs_ref[...] = h[tm - db:, :]

    h = hcur_ref[...]
    hb = h.astype(_BF16)
    xb = (hprev_ref[...] - h).astype(_BF16)
    h_ref[...] = hb
    pre = [jnp.dot(hb, wh_scr[g][...], preferred_element_type=_F32)
           + jnp.dot(xb, wx_scr[g][...], preferred_element_type=_F32) for g in range(n_lora)]
    act_refs[0][...] = jnp.tanh(pre[0])
    act_refs[1][...] = pre[1]
    act_refs[2][...] = jax.nn.sigmoid(pre[2])
    if has_v:
        act_refs[3][...] = pre[3]


def _norm_lora(l, x, mod, norm_g4, mu_x_t, state_shift, decay_lora1, iclr_lora1, gate_lora1, vres_lora1,
               *, tm, n_pt, tps, db, bp):
    d = x.prompt.shape[1]
    rows = (n_pt + 1) * tm
    has_v = l > 0
    bc = mod.shape[2]
    ld, la, lg = decay_lora1.shape[2], iclr_lora1.shape[2], gate_lora1.shape[2]
    full = lambda *shape: pl.BlockSpec(shape, lambda i: (l,) + (0,) * (len(shape) - 1))
    in_specs = _act_specs(x, tm, n_pt) + [
                pl.BlockSpec((1, 1, bc, d), lambda i: (l, 0, 0, 0)),
                pl.BlockSpec((1, 1, bc, d), lambda i: (l, 1, 0, 0)),
                pl.BlockSpec((1, 1, 1, d), lambda i: (l, 0, 0, 0)),
                full(1, d, 4), full(1, db, d), full(1, d, ld), full(1, d, la), full(1, d, lg)]
    args = [x.prompt, x.decode, mod, mod, norm_g4, mu_x_t, state_shift, decay_lora1, iclr_lora1, gate_lora1]
    widths = [ld, la, lg] + ([vres_lora1.shape[2]] if has_v else [])
    row_out = lambda n, dt: (pl.BlockSpec((tm, n), lambda i: (i, 0)), jax.ShapeDtypeStruct((rows, n), dt))
    outs = [row_out(d, _BF16), row_out(ld, _F32), row_out(la, _F32), row_out(lg, _F32)]
    if has_v:
        lv = vres_lora1.shape[2]
        in_specs.append(pl.BlockSpec((1, d, lv), lambda i: (l - 1, 0, 0)))
        args.append(vres_lora1)
        outs.append(row_out(lv, _F32))
    outs.append((pl.BlockSpec((1, _SUBLANES, d), lambda i: (jnp.minimum(i // tps, bp - 1), 0, 0)),
                 jax.ShapeDtypeStruct((bp, _SUBLANES, d), _F32)))
    outs.append((pl.BlockSpec((db, d), lambda i: (0, 0)), jax.ShapeDtypeStruct((db, d), _F32)))
    return pl.pallas_call(
        functools.partial(_norm_lora_kernel, n_pt=n_pt, tps=tps, db=db, has_v=has_v),
        grid=(rows // tm,),
        in_specs=in_specs,
        out_specs=[o[0] for o in outs],
        out_shape=[o[1] for o in outs],
        scratch_shapes=([pltpu.VMEM((_SUBLANES, d), _F32), pltpu.VMEM((tm, d), _F32), pltpu.VMEM((tm, d), _F32)]
                        + [pltpu.VMEM((d, n), _BF16) for n in widths] * 2),
        compiler_params=_params("arbitrary"),
    )(*args)


def _mix_kernel(*refs, n_pt, tps, db, has_v):
    it = iter(refs)
    h_ref = next(it)
    w_refs = [next(it) for _ in range(6)]
    aw_ref, aa_ref, ag_ref = next(it), next(it), next(it)
    av_ref = next(it) if has_v else None
    d2_ref, i2_ref, g2_ref = next(it), next(it), next(it)
    v2_ref = next(it) if has_v else None
    w0_ref, a0_ref = next(it), next(it)
    v0_ref = next(it) if has_v else None
    mu_ref, kkw_ref, kaw_ref, cw_ref, hl_ref, cs_ref = (next(it) for _ in range(6))
    vf_ref = next(it) if has_v else None
    seq_prompt = [next(it) for _ in range(7)]
    seq_decode = [next(it) for _ in range(7)]
    oc_o = next(it)
    vf_o = None if has_v else next(it)
    cp_o, cso_o = next(it), next(it)
    wc_scr, pcarry, zcarry = (next(it) for _ in range(3))

    i = pl.program_id(1)
    tm = h_ref.shape[0]
    tn = oc_o.shape[1]

    @pl.when(i == 0)
    def _cache_weights():
        for g in range(6):
            wc_scr[g] = w_refs[g][0].astype(_BF16)

    def project(rs):
        h = h_ref[rs, :]
        return [jnp.dot(h, wc_scr[g], preferred_element_type=_F32) for g in range(6)]

    def tail(rs, p, pprev, bg, z, z1, z2, seq_out):
        r_o, lw_o, k_o, v_o, kk_o, kka_o, g_o = seq_out
        mu = mu_ref[0]
        r = p[0] + (pprev[0] - p[0]) * mu[0:1]
        k = p[1] + (pprev[1] - p[1]) * mu[1:2]
        v = p[2] + (pprev[2] - p[2]) * mu[2:3]

        zlog = w0_ref[0] + _dot(aw_ref[rs, :], d2_ref[0])
        softplus = jnp.maximum(-zlog, 0.0) + jnp.log(1.0 + jnp.exp(-jnp.abs(zlog)))
        lw_o[rs, :] = -jnp.exp(-softplus - 0.5)
        a = jax.nn.sigmoid(a0_ref[0] + _dot(aa_ref[rs, :], i2_ref[0]))
        g_o[rs, :] = _dot(ag_ref[rs, :], g2_ref[0])
        if has_v:
            nu = jax.nn.sigmoid(v0_ref[0] + _dot(av_ref[rs, :], v2_ref[0]))
            v = v + (vf_ref[rs, :] - v) * nu
        else:
            vf_o[rs, :] = v

        kk = k * kkw_ref[0]
        norm = jnp.sqrt(_head_sums(kk * kk, _segment_ones(tn)))
        kk = kk / jnp.maximum(norm, 1e-12)
        r_o[rs, :] = r
        k_o[rs, :] = k * (1.0 + (a - 1.0) * kaw_ref[0])
        v_o[rs, :] = v
        kk_o[rs, :] = kk
        kka_o[rs, :] = kk * a

        cw = cw_ref[0]
        zc = z2 * cw[0:1] + z1 * cw[1:2] + z * cw[2:3]
        oc_o[rs, :] = (bg * zc).astype(_BF16)

    @pl.when(i < n_pt)
    def _prompt():
        half = tm // 2
        halves = (slice(0, half), slice(half, tm))
        proj = [project(rs) for rs in halves]
        zs = [q[4] * q[5] for q in proj]
        for idx, rs in enumerate(halves):
            p, z = proj[idx][:3], zs[idx]
            if idx == 0:
                first = (i % tps) == 0
                p_carry = [pcarry[g] for g in range(3)]
                z_carry = zcarry[...]
            else:
                first = False
                p_carry = [q[half - _SUBLANES:, :] for q in proj[0][:3]]
                z_carry = zs[0][half - _SUBLANES:, :]
            pprev = [_shift_rows_prompt(p[g], p_carry[g], first, two=False) for g in range(3)]
            z1, z2 = _shift_rows_prompt(z, z_carry, first)
            tail(rs, p, pprev, proj[idx][3], z, z1, z2, seq_prompt)
        for g in range(3):
            pcarry[g] = proj[1][g][half - _SUBLANES:, :]
        zcarry[...] = zs[1][half - _SUBLANES:, :]
        cp_o[0] = zs[1][half - _SUBLANES:, :]

    @pl.when(i >= n_pt)
    def _decode():
        rs = slice(0, tm)
        proj = project(rs)
        p, z = proj[:3], proj[4] * proj[5]
        hl = hl_ref[0].astype(_BF16)
        pprev = [jnp.concatenate([jnp.dot(hl, wc_scr[g], preferred_element_type=_F32), p[g][:tm - db, :]], axis=0)
                 for g in range(3)]
        z1 = jnp.concatenate([cs_ref[0, 1], z[:tm - db, :]], axis=0)
        z2 = jnp.concatenate([cs_ref[0, 0], cs_ref[0, 1], z[:tm - 2 * db, :]], axis=0)
        cso_o[0] = z[tm - 2 * db:tm - db, :]
        cso_o[1] = z[tm - db:, :]
        tail(rs, p, pprev, proj[3], z, z1, z2, seq_decode)


def _mix(l, h, acts, vf_in, w_in, lora2, vecs, mu_rkv, k_k3, k_a3, conv_w, state_shift, state_conv_t,
         *, tm, tn, n_pt, tps, db, bp):
    rows, d = h.shape
    has_v = l > 0
    g_dim = mu_rkv.shape[2]
    nj = g_dim // tn
    aw, aa, ag, av = acts
    d2, i2, g2, v2 = lora2
    w0, a0, v0 = vecs
    row_in = lambda arr: pl.BlockSpec((tm, arr.shape[1]), lambda j, i: (i, 0))
    col3 = lambda arr, ll: pl.BlockSpec((1, arr.shape[1], tn), lambda j, i: (ll, 0, j))
    in_specs = [row_in(h)] + [pl.BlockSpec((1, d, tn), lambda j, i, g=g: (l, 0, g * nj + j),
                                           pipeline_mode=pl.Buffered(1)) for g in range(6)]
    args = [h] + [w_in] * 6
    in_specs += [row_in(aw), row_in(aa), row_in(ag)]
    args += [aw, aa, ag]
    if has_v:
        in_specs.append(row_in(av))
        args.append(av)
    in_specs += [col3(d2, l), col3(i2, l), col3(g2, l)]
    args += [d2, i2, g2]
    if has_v:
        in_specs.append(col3(v2, l - 1))
        args.append(v2)
    in_specs += [col3(w0, l), col3(a0, l)]
    args += [w0, a0]
    if has_v:
        in_specs.append(col3(v0, l - 1))
        args.append(v0)
    in_specs += [col3(mu_rkv, l), col3(k_k3, l), col3(k_a3, l), col3(conv_w, l),
                 pl.BlockSpec((1, db, d), lambda j, i: (l, 0, 0)),
                 pl.BlockSpec((1, 2, db, tn), lambda j, i: (l, 0, 0, j))]
    args += [mu_rkv, k_k3, k_a3, conv_w, state_shift, state_conv_t]
    if has_v:
        in_specs.append(pl.BlockSpec((tm, tn), lambda j, i: (i, j)))
        args.append(vf_in)
    tile = pl.BlockSpec((tm, tn), lambda j, i: (i, j))
    prompt_tile = pl.BlockSpec((tm, tn), lambda j, i: (jnp.minimum(i, n_pt - 1), j))
    decode_tile = pl.BlockSpec((tm, tn), lambda j, i: (0, j))
    outs = [(prompt_tile, jax.ShapeDtypeStruct((n_pt * tm, g_dim), _F32)) for _ in range(7)]
    outs += [(decode_tile, jax.ShapeDtypeStruct((tm, g_dim), _F32)) for _ in range(7)]
    outs.append((tile, jax.ShapeDtypeStruct((rows, g_dim), _BF16)))
    if not has_v:
        outs.append((tile, jax.ShapeDtypeStruct((rows, g_dim), _F32)))
    outs.append((pl.BlockSpec((1, _SUBLANES, tn), lambda j, i: (jnp.minimum(i // tps, bp - 1), 0, j)),
                 jax.ShapeDtypeStruct((bp, _SUBLANES, g_dim), _F32)))
    outs.append((pl.BlockSpec((2, db, tn), lambda j, i: (0, 0, j)), jax.ShapeDtypeStruct((2, db, g_dim), _F32)))
    return pl.pallas_call(
        functools.partial(_mix_kernel, n_pt=n_pt, tps=tps, db=db, has_v=has_v),
        grid=(nj, rows // tm),
        in_specs=in_specs,
        out_specs=[o[0] for o in outs],
        out_shape=[o[1] for o in outs],
        scratch_shapes=[pltpu.VMEM((6, d, tn), _BF16), pltpu.VMEM((3, _SUBLANES, tn), _F32),
                        pltpu.VMEM((_SUBLANES, tn), _F32)],
        compiler_params=_params("arbitrary", "arbitrary"),
    )(*args)


def _unit_lower_inverse_minus_identity(lmats, ri, ci, chunk, mul):
    same8 = (ri >> 3) == (ci >> 3)
    d8 = [jnp.where(same8, m, 0.0) for m in lmats]
    d2 = [mul(a, a) for a in d8]
    d3 = [mul(a, b) for a, b in zip(d8, d2)]
    d4 = [mul(b, b) for b in d2]
    x = [a + b + cc for a, b, cc in zip(d8, d2, d3)]
    xd4 = [mul(a, b) for a, b in zip(x, d4)]
    x = [a + b + cc for a, b, cc in zip(x, d4, xd4)]
    size = 16
    while size <= chunk:
        sh = size.bit_length() - 1
        level = ((ri >> sh) == (ci >> sh)) & ((ri >> (sh - 1)) != (ci >> (sh - 1)))
        e = [jnp.where(level, m, 0.0) for m in lmats]
        y = [b + mul(a, b) for a, b in zip(x, e)]
        x = [a + b + mul(b, a) for a, b in zip(x, y)]
        size *= 2
    return x


def _wkv_kernel(r_ref, lw_ref, k_ref, v_ref, kk_ref, kka_ref, g_ref, rk_ref, lnw_ref, lnb_ref,
                o_ref, so_ref, s_scr, *, n_chunks):
    ci_grid = pl.program_id(1)
    nb, c, g_dim = r_ref.shape
    npair = g_dim // _PAIR
    n2 = 2 * c

    lane = lax.broadcasted_iota(jnp.int32, (1, _PAIR), 1)
    m0 = (lane < _HEAD).astype(_F32)
    m1 = 1.0 - m0

    @pl.when(ci_grid == 0)
    def _init():
        s_scr[...] = jnp.zeros(s_scr.shape, _F32)

    ri = lax.broadcasted_iota(jnp.int32, (n2, n2), 0) & (c - 1)
    ci = lax.broadcasted_iota(jnp.int32, (n2, n2), 1) & (c - 1)
    strict = ri > ci
    incl = ri >= ci
    tri = (lax.broadcasted_iota(jnp.int32, (c, c), 0) >= lax.broadcasted_iota(jnp.int32, (c, c), 1)).astype(_BF16)
    seg_ones = _segment_ones(_SEG)

    def stack(x):
        return jnp.concatenate([x * m0, x * m1], axis=0)

    units = [(u, p) for u in range(nb) for p in range(npair)]
    sl = lambda p: slice(p * _PAIR, (p + 1) * _PAIR)

    seq = []
    for u in range(nb):
        lw = lw_ref[u]
        l_hi, l_mid, l_lo = _split3(lw)
        cl = (jnp.dot(tri, l_hi, preferred_element_type=_F32)
              + jnp.dot(tri, l_mid, preferred_element_type=_F32)
              + jnp.dot(tri, l_lo, preferred_element_type=_F32))
        cl_end = cl[c - 1:c, :]
        r, k, v, kk, kka = r_ref[u], k_ref[u], v_ref[u], kk_ref[u], kka_ref[u]
        e_neg = jnp.exp(-cl)
        e_tail = jnp.exp(cl_end - cl)
        seq.append(dict(r=r, k=k, v=v, p_end=jnp.exp(cl_end), rt=r * jnp.exp(cl), at=-(kk * jnp.exp(cl - lw)),
                        kt=k * e_neg, bt=kka * e_neg, kh=k * e_tail, bh=kka * e_tail))

    ar2 = [jnp.concatenate([stack(seq[u]["at"][:, sl(p)]), stack(seq[u]["rt"][:, sl(p)])], axis=0).astype(_BF16)
           for u, p in units]
    mb = [_dot_nt(a, stack(seq[u]["bt"][:, sl(p)])) for a, (u, p) in zip(ar2, units)]
    mk = [_dot_nt(a, stack(seq[u]["kt"][:, sl(p)])) for a, (u, p) in zip(ar2, units)]
    s_old = [s_scr[u, p] for u, p in units]
    uy0 = [_dot_nt(jnp.concatenate([seq[u]["at"][:, sl(p)], seq[u]["rt"][:, sl(p)]], axis=0), s)
           for (u, p), s in zip(units, s_old)]
    x_inv = _unit_lower_inverse_minus_identity([jnp.where(strict, m[:n2], 0.0) for m in mb], ri, ci, c, _dot)
    vs = [stack(seq[u]["v"][:, sl(p)]).astype(_BF16) for u, p in units]
    ws = [stack(q[:c]) + _dot(jnp.where(strict, m[:n2], 0.0), vv) for q, m, vv in zip(uy0, mk, vs)]
    us = [(w + _dot(xi, w)).astype(_BF16) for xi, w in zip(x_inv, ws)]
    ysd = [_dot(jnp.where(incl, m1_[n2:], 0.0), uu) + _dot(jnp.where(incl, m2_[n2:], 0.0), vv)
           for m1_, m2_, uu, vv in zip(mb, mk, us, vs)]
    ys = [q[c:] + d[:c] + d[c:] for q, d in zip(uy0, ysd)]
    for idx, (u, p) in enumerate(units):
        uv = jnp.concatenate([us[idx], vs[idx]], axis=0)
        bk = jnp.concatenate([stack(seq[u]["bh"][:, sl(p)]), stack(seq[u]["kh"][:, sl(p)])], axis=0)
        s_scr[u, p] = s_old[idx] * seq[u]["p_end"][:, sl(p)] + _dot_tn(uv, bk)
    for u in range(nb):
        y = jnp.concatenate(ys[u * npair:(u + 1) * npair], axis=1)
        mean = _head_sums(y, seg_ones) * (1.0 / _HEAD)
        dy = y - mean
        var = _head_sums(dy * dy, seg_ones) * (1.0 / _HEAD)
        yn = dy * lax.rsqrt(var + _GN_EPS) * lnw_ref[...] + lnb_ref[...]
        bonus = _head_sums(seq[u]["r"] * seq[u]["k"] * rk_ref[...], seg_ones) * seq[u]["v"]
        o_ref[u] = ((yn + bonus) * g_ref[u]).astype(o_ref.dtype)

    @pl.when(ci_grid == n_chunks - 1)
    def _final():
        for u, p in units:
            s = s_scr[u, p]
            so_ref[u, 2 * p] = s[:_HEAD, :_HEAD]
            so_ref[u, 2 * p + 1] = s[_HEAD:, _HEAD:]


def _wkv(l, seqs, chunk, nb, r_k3, ln_w3, ln_b3):
    n_seq, t_len, g_dim = seqs[0].shape
    n_heads = g_dim // _HEAD
    n_chunks = t_len // chunk
    blk = pl.BlockSpec((nb, chunk, g_dim), lambda bi, ci: (bi, ci, 0))
    vec = pl.BlockSpec((None, 1, g_dim), lambda bi, ci: (l, 0, 0))
    return pl.pallas_call(
        functools.partial(_wkv_kernel, n_chunks=n_chunks),
        grid=(n_seq // nb, n_chunks),
        in_specs=[blk] * 7 + [vec] * 3,
        out_specs=[blk, pl.BlockSpec((nb, n_heads, _HEAD, _HEAD), lambda bi, ci: (bi, 0, 0, 0))],
        out_shape=[jax.ShapeDtypeStruct((n_seq, t_len, g_dim), _BF16),
                   jax.ShapeDtypeStruct((n_seq, n_heads, _HEAD, _HEAD), _F32)],
        scratch_shapes=[pltpu.VMEM((nb, g_dim // _PAIR, _PAIR, _PAIR), _F32)],
        compiler_params=_params("arbitrary", "arbitrary"),
    )(*seqs, r_k3, ln_w3, ln_b3)


def _wkv_decode_kernel(r_ref, lw_ref, k_ref, v_ref, kk_ref, kka_ref, g_ref, rk_ref, lnw_ref, lnb_ref, s0_ref,
                       o_ref, so_ref, *, ts, nb):
    g_dim = r_ref.shape[2]
    npair = g_dim // _PAIR
    n = ts * nb
    n2 = 2 * n
    nb_bits = nb.bit_length() - 1

    lane = lax.broadcasted_iota(jnp.int32, (1, _PAIR), 1)
    m0 = (lane < _HEAD).astype(_F32)
    m1 = 1.0 - m0

    def stack(x):
        return jnp.concatenate([x * m0, x * m1], axis=0)

    def rows_tb(ref):
        return jnp.concatenate([ref[t] for t in range(ts)], axis=0)

    lw_t = [lw_ref[t] for t in range(ts)]
    cl_t = [lw_t[0]]
    for t in range(1, ts):
        cl_t.append(cl_t[-1] + lw_t[t])
    cl = jnp.concatenate(cl_t, axis=0)
    lw = jnp.concatenate(lw_t, axis=0)
    cl_end_b = cl_t[-1]
    cl_end = _tile_rows(cl_end_b, ts)
    p_end = jnp.exp(cl_end_b)
    r, k, v, kk, kka = (rows_tb(ref) for ref in (r_ref, k_ref, v_ref, kk_ref, kka_ref))
    e_neg = jnp.exp(-cl)
    e_tail = jnp.exp(cl_end - cl)
    rt = r * jnp.exp(cl)
    at = -(kk * jnp.exp(cl - lw))
    kt = k * e_neg
    bt = kka * e_neg
    kh = k * e_tail
    bh = kka * e_tail

    ri = lax.broadcasted_iota(jnp.int32, (n2, n2), 0)
    ci = lax.broadcasted_iota(jnp.int32, (n2, n2), 1)
    same_seq = (ri & (nb - 1)) == (ci & (nb - 1))
    t_r = (ri & (n - 1)) >> nb_bits
    t_c = (ci & (n - 1)) >> nb_bits
    strict = same_seq & (t_r > t_c)
    incl = same_seq & (t_r >= t_c)
    seq_of_row2 = lax.broadcasted_iota(jnp.int32, (n2, 1), 0) & (nb - 1)
    seq_of_row4 = lax.broadcasted_iota(jnp.int32, (2 * n2, 1), 0) & (nb - 1)
    zero = jnp.zeros((_HEAD, _HEAD), _F32)

    pairs = range(npair)
    sls = [slice(p * _PAIR, (p + 1) * _PAIR) for p in pairs]
    ar2 = [jnp.concatenate([stack(at[:, sl]), stack(rt[:, sl])], axis=0).astype(_BF16) for sl in sls]
    mb = [_dot_nt(a, stack(bt[:, sl])) for a, sl in zip(ar2, sls)]
    mk = [_dot_nt(a, stack(kt[:, sl])) for a, sl in zip(ar2, sls)]
    lab = [jnp.where(strict, m[:n2], 0.0) for m in mb]
    d2 = [_dot(a, a) for a in lab]
    d3 = [_dot(a, b) for a, b in zip(lab, d2)]
    x_inv = [a + b + cc for a, b, cc in zip(lab, d2, d3)]

    def block_diag(b, p):
        top = jnp.concatenate([s0_ref[b, 2 * p], zero], axis=1)
        bot = jnp.concatenate([zero, s0_ref[b, 2 * p + 1]], axis=1)
        return jnp.concatenate([top, bot], axis=0)

    ys = []
    for p, sl in zip(pairs, sls):
        s_b = [block_diag(b, p) for b in range(nb)]
        ar = jnp.concatenate([at[:, sl], rt[:, sl]], axis=0)
        ar_cat = jnp.concatenate([jnp.where(seq_of_row2 == b, ar, 0.0).astype(_BF16) for b in range(nb)], axis=1)
        s_cat = jnp.concatenate([s.astype(_BF16) for s in s_b], axis=1)
        uy0 = _dot_nt(ar_cat, s_cat)
        vs = stack(v[:, sl]).astype(_BF16)
        ws = stack(uy0[:n]) + _dot(jnp.where(strict, mk[p][:n2], 0.0), vs)
        us = (ws + _dot(x_inv[p], ws)).astype(_BF16)
        ysd = _dot(jnp.where(incl, mb[p][n2:], 0.0), us) + _dot(jnp.where(incl, mk[p][n2:], 0.0), vs)
        ys.append(uy0[n:] + ysd[:n] + ysd[n:])
        uv = jnp.concatenate([us, vs], axis=0)
        uv_cat = jnp.concatenate([jnp.where(seq_of_row4 == b, uv, jnp.zeros_like(uv)) for b in range(nb)], axis=1)
        bk = jnp.concatenate([stack(bh[:, sl]), stack(kh[:, sl])], axis=0)
        upd = _dot_tn(uv_cat, bk)
        for b in range(nb):
            s_new = (s_b[b] * p_end[b:b + 1, sl] + upd[b * _PAIR:(b + 1) * _PAIR, :])
            so_ref[b, 2 * p] = s_new[:_HEAD, :_HEAD]
            so_ref[b, 2 * p + 1] = s_new[_HEAD:, _HEAD:]

    y = jnp.concatenate(ys, axis=1)
    seg_ones = _segment_ones(_SEG)
    mean = _head_sums(y, seg_ones) * (1.0 / _HEAD)
    dy = y - mean
    var = _head_sums(dy * dy, seg_ones) * (1.0 / _HEAD)
    yn = dy * lax.rsqrt(var + _GN_EPS) * lnw_ref[...] + lnb_ref[...]
    bonus = _head_sums(r * k * rk_ref[...], seg_ones) * v
    out = ((yn + bonus) * rows_tb(g_ref)).astype(o_ref.dtype)
    for t in range(ts):
        o_ref[t] = out[t * nb:(t + 1) * nb, :]


def _wkv_decode(l, seqs, state0, r_k3, ln_w3, ln_b3, *, ts, db, nb, first_block):
    g_dim = seqs[0].shape[2]
    n_heads = g_dim // _HEAD
    blk = pl.BlockSpec((ts, nb, g_dim), lambda bi: (first_block, bi, 0))
    vec = pl.BlockSpec((None, 1, g_dim), lambda bi: (l, 0, 0))
    st = pl.BlockSpec((nb, n_heads, _HEAD, _HEAD), lambda bi: (bi, 0, 0, 0))
    return pl.pallas_call(
        functools.partial(_wkv_decode_kernel, ts=ts, nb=nb),
        grid=(db // nb,),
        in_specs=[blk] * 7 + [vec] * 3 + [st],
        out_specs=[pl.BlockSpec((ts, nb, g_dim), lambda bi: (0, bi, 0)), st],
        out_shape=[jax.ShapeDtypeStruct((ts, db, g_dim), _BF16),
                   jax.ShapeDtypeStruct((db, n_heads, _HEAD, _HEAD), _F32)],
        compiler_params=_params("arbitrary"),
    )(*seqs, r_k3, ln_w3, ln_b3, state0)


def _cast_kernel(w_ref, o_ref):
    o_ref[...] = w_ref[...].astype(o_ref.dtype)


def _cast_bf16(w, rows_per_block):
    depth, k, n = w.shape
    spec = pl.BlockSpec((1, rows_per_block, n), lambda l, i: (l, i, 0))
    return pl.pallas_call(
        _cast_kernel, grid=(depth, k // rows_per_block), in_specs=[spec], out_specs=spec,
        out_shape=jax.ShapeDtypeStruct(w.shape, _BF16),
        compiler_params=_params("arbitrary", "arbitrary"),
    )(w)


def _outproj_kernel(op_ref, os_ref, oc_ref, xp_ref, xs_ref, ga_ref, sh_ref, sc_ref, g_ref, w_ref,
                    x1_ref, h2_ref, *, n_pt, tps, db):
    i = pl.program_id(0)
    tm = xp_ref.shape[0]
    ts = tm // db
    g_dim = op_ref.shape[1]

    def rows_out(rs, o_ref, x_ref, ga, sc, sh):
        acc = (jnp.dot(o_ref[rs, :], w_ref[0, :g_dim, :], preferred_element_type=_F32)
               + jnp.dot(oc_ref[rs, :], w_ref[0, g_dim:, :], preferred_element_type=_F32))
        x1 = x_ref[rs, :] + ga * acc
        x1_ref[rs, :] = x1
        xn = x1 * lax.rsqrt(jnp.mean(x1 * x1, axis=-1, keepdims=True) + _RMS_EPS) * g_ref[0, 0]
        h2_ref[rs, :] = (xn * (1.0 + sc) + sh).astype(_BF16)

    half = tm // 2
    halves = (slice(0, half), slice(half, tm))

    @pl.when(i < n_pt)
    def _prompt():
        ga, sc, sh = (_mod_row(ref, i, tps, db) for ref in (ga_ref, sc_ref, sh_ref))
        for rs in halves:
            rows_out(rs, op_ref, xp_ref, ga, sc, sh)

    @pl.when(i >= n_pt)
    def _decode():
        ga, sc, sh = (_mod_tile(ref, db, ts) for ref in (ga_ref, sc_ref, sh_ref))
        for rs in halves:
            rows_out(rs, os_ref, xs_ref, ga[rs, :], sc[rs, :], sh[rs, :])


def _outproj(l, o_p, o_s, o_conv, x, mod, norm_g4, w_out, *, tm, n_pt, tps, db):
    d = x.prompt.shape[1]
    rows = (n_pt + 1) * tm
    g_dim = o_p.shape[1]
    bc = mod.shape[2]
    modspec = lambda comp: pl.BlockSpec((1, 1, bc, d), lambda i: (l, comp, 0, 0))
    return pl.pallas_call(
        functools.partial(_outproj_kernel, n_pt=n_pt, tps=tps, db=db),
        grid=(rows // tm,),
        in_specs=[pl.BlockSpec((tm, g_dim), lambda i: (jnp.minimum(i, n_pt - 1), 0)),
                  pl.BlockSpec((tm, g_dim), lambda i: (0, 0)),
                  pl.BlockSpec((tm, o_conv.shape[1]), lambda i: (i, 0))] + _act_specs(x, tm, n_pt) + [
                  modspec(2), modspec(3), modspec(4),
                  pl.BlockSpec((1, 1, 1, d), lambda i: (l, 1, 0, 0)),
                  pl.BlockSpec((1, d, d), lambda i: (l, 0, 0), pipeline_mode=pl.Buffered(1))],
        out_specs=[pl.BlockSpec((tm, d), lambda i: (i, 0)), pl.BlockSpec((tm, d), lambda i: (i, 0))],
        out_shape=[jax.ShapeDtypeStruct((rows, d), _F32), jax.ShapeDtypeStruct((rows, d), _BF16)],
        compiler_params=_params("arbitrary"),
    )(o_p, o_s, o_conv, x.prompt, x.decode, mod, mod, mod, norm_g4, w_out)


def _ffn_kernel(*refs, n_pt, tps, db, nj, final_norm):
    it = iter(refs)
    (h2_ref, wa_ref, wb_ref, cwa_ref, cwb_ref, wd_ref, x1_ref, ga_ref, sfa_ref, sfb_ref) = (next(it) for _ in range(10))
    fg_ref = next(it) if final_norm else None
    x2_ref = next(it)
    x2s_ref = next(it) if final_norm else x2_ref
    fpa_o, fpb_o, fsa_o, fsb_o = (next(it) for _ in range(4))
    acc_scr, ca_scr, cb_scr = (next(it) for _ in range(3))
    i = pl.program_id(0)
    j = pl.program_id(1)
    tm = h2_ref.shape[0]
    ts = tm // db

    @pl.when(j == 0)
    def _zero():
        acc_scr[...] = jnp.zeros(acc_scr.shape, _F32)

    def prompt_body():
        half = tm // 2
        halves = (slice(0, half), slice(half, tm))
        us = [[jnp.dot(h2_ref[rs, :], w_ref[0], preferred_element_type=_F32) for w_ref in (wa_ref, wb_ref)]
              for rs in halves]
        for r, rs in enumerate(halves):
            conv = []
            for idx, (cw_ref, c_scr, fp_o) in enumerate(((cwa_ref, ca_scr, fpa_o), (cwb_ref, cb_scr, fpb_o))):
                u = us[r][idx]
                if r == 0:
                    u1, u2 = _shift_rows_prompt(u, c_scr[j], (i % tps) == 0)
                else:
                    u1, u2 = _shift_rows_prompt(u, us[0][idx][half - _SUBLANES:, :], False)
                    c_scr[j] = u[half - _SUBLANES:, :]
                    fp_o[0] = u[half - _SUBLANES:, :]
                cw = cw_ref[0]
                conv.append(u2 * cw[0:1] + u1 * cw[1:2] + u * cw[2:3])
            gact = ((conv[0] * jax.nn.sigmoid(conv[0])) * conv[1]).astype(_BF16)
            acc_scr[rs, :] += jnp.dot(gact, wd_ref[0], preferred_element_type=_F32)

    def body(prompt):
        if prompt:
            return prompt_body()
        first = (i % tps) == 0
        h2 = h2_ref[...]
        ua = jnp.dot(h2, wa_ref[0], preferred_element_type=_F32)
        ub = jnp.dot(h2, wb_ref[0], preferred_element_type=_F32)
        conv = []
        for u, cw_ref, c_scr, sf, fp_o, fs_o in ((ua, cwa_ref, ca_scr, sfa_ref, fpa_o, fsa_o),
                                                 (ub, cwb_ref, cb_scr, sfb_ref, fpb_o, fsb_o)):
            if prompt:
                u1, u2 = _shift_rows_prompt(u, c_scr[j], first)
                c_scr[j] = u[tm - _SUBLANES:, :]
                fp_o[0] = u[tm - _SUBLANES:, :]
            else:
                u1 = jnp.concatenate([sf[0, 1], u[:tm - db, :]], axis=0)
                u2 = jnp.concatenate([sf[0, 0], sf[0, 1], u[:tm - 2 * db, :]], axis=0)
                fp_o[0] = jnp.zeros(fp_o.shape[1:], _F32)
                fs_o[0] = u[tm - 2 * db:tm - db, :]
                fs_o[1] = u[tm - db:, :]
            cw = cw_ref[0]
            conv.append(u2 * cw[0:1] + u1 * cw[1:2] + u * cw[2:3])
        gact = ((conv[0] * jax.nn.sigmoid(conv[0])) * conv[1]).astype(_BF16)
        acc_scr[...] += jnp.dot(gact, wd_ref[0], preferred_element_type=_F32)

    @pl.when(i < n_pt)
    def _prompt():
        body(True)

    @pl.when(i >= n_pt)
    def _decode():
        body(False)

    def finish(ga, out_ref):
        x2 = x1_ref[...] + ga * acc_scr[...]
        if final_norm:
            x2 = x2 * lax.rsqrt(jnp.mean(x2 * x2, axis=-1, keepdims=True) + _RMS_EPS) * fg_ref[...]
        out_ref[...] = x2

    @pl.when((j == nj - 1) & (i < n_pt))
    def _finish_prompt():
        finish(_mod_row(ga_ref, i, tps, db), x2_ref)

    @pl.when((j == nj - 1) & (i >= n_pt))
    def _finish_decode():
        finish(_mod_tile(ga_ref, db, ts), x2s_ref)


def _ffn(l, h2, x1, mod, ffn_up, ffn_conv, ffn_down, state_ffn_t, final_g, *, tm, tn, n_pt, tps, db, bp):
    rows, d = x1.shape
    f = ffn_down.shape[1]
    nj = f // tn
    bc = mod.shape[2]
    final_norm = final_g is not None
    in_specs = [pl.BlockSpec((tm, d), lambda i, j: (i, 0)),
                pl.BlockSpec((1, d, tn), lambda i, j: (l, 0, j)),
                pl.BlockSpec((1, d, tn), lambda i, j: (l, 0, nj + j)),
                pl.BlockSpec((1, 3, tn), lambda i, j: (l, 0, j)),
                pl.BlockSpec((1, 3, tn), lambda i, j: (l, 0, nj + j)),
                pl.BlockSpec((1, tn, d), lambda i, j: (l, j, 0)),
                pl.BlockSpec((tm, d), lambda i, j: (i, 0)),
                pl.BlockSpec((1, 1, bc, d), lambda i, j: (l, 5, 0, 0)),
                pl.BlockSpec((1, 2, db, tn), lambda i, j: (l, 0, 0, jnp.where(i >= n_pt, j, 0))),
                pl.BlockSpec((1, 2, db, tn), lambda i, j: (l, 0, 0, nj + jnp.where(i >= n_pt, j, 0)))]
    args = [h2, ffn_up, ffn_up, ffn_conv, ffn_conv, ffn_down, x1, mod, state_ffn_t, state_ffn_t]
    if final_norm:
        in_specs.append(pl.BlockSpec((1, d), lambda i, j: (0, 0)))
        args.append(final_g)
    n_tiles = rows // tm
    pstate = pl.BlockSpec((1, _SUBLANES, tn), lambda i, j: (i, 0, j))
    sstate = pl.BlockSpec((2, db, tn), lambda i, j: (0, 0, jnp.where(i >= n_pt, j, 0)))
    if final_norm:
        x_specs = [pl.BlockSpec((tm, d), lambda i, j: (jnp.minimum(i, n_pt - 1), 0)),
                   pl.BlockSpec((tm, d), lambda i, j: (0, 0))]
        x_shapes = [jax.ShapeDtypeStruct((n_pt * tm, d), _F32), jax.ShapeDtypeStruct((tm, d), _F32)]
    else:
        x_specs = [pl.BlockSpec((tm, d), lambda i, j: (i, 0))]
        x_shapes = [jax.ShapeDtypeStruct((rows, d), _F32)]
    return pl.pallas_call(
        functools.partial(_ffn_kernel, n_pt=n_pt, tps=tps, db=db, nj=nj, final_norm=final_norm),
        grid=(rows // tm, nj),
        in_specs=in_specs,
        out_specs=x_specs + [pstate, pstate, sstate, sstate],
        out_shape=x_shapes + [
                   jax.ShapeDtypeStruct((n_tiles, _SUBLANES, f), _F32),
                   jax.ShapeDtypeStruct((n_tiles, _SUBLANES, f), _F32),
                   jax.ShapeDtypeStruct((2, db, f), _F32), jax.ShapeDtypeStruct((2, db, f), _F32)],
        scratch_shapes=[pltpu.VMEM((tm, d), _F32), pltpu.VMEM((nj, _SUBLANES, tn), _F32),
                        pltpu.VMEM((nj, _SUBLANES, tn), _F32)],
        compiler_params=_params("arbitrary", "arbitrary"),
    )(*args)


def _forward(x_prompt, x_sample, c_prompt, c_sample, state_wkv, state_shift, state_conv, state_ffn,
             ada_w, ada_b, norm_g, final_norm_g, w_in, mu_x, mu_rkv, decay_w0, decay_lora1, decay_lora2,
             iclr_a0, iclr_lora1, iclr_lora2, gate_lora1, gate_lora2, vres_v0, vres_lora1, vres_lora2,
             k_k, k_a, r_k, ln_x_w, ln_x_b, conv_w, w_out, ffn_up, ffn_conv, ffn_down,
             *, chunk=64, mix_tn=256, ffn_tn=512, wkv_nb=16, wkv_prompt_nb=2):
    bp, t_len, d = x_prompt.shape
    db, ts, _ = x_sample.shape
    depth = ada_w.shape[0]
    g_dim = mu_rkv.shape[2]
    n_heads = g_dim // _HEAD
    f = ffn_down.shape[1]
    tm = ts * db
    assert t_len % tm == 0 and t_len % chunk == 0 and ts >= 2 and db % _SUBLANES == 0 and tm % chunk == 0
    tps = t_len // tm
    n_pt = bp * tps
    n_prompt_rows = bp * t_len
    assert ts == 4 and db % wkv_nb == 0 and wkv_nb & (wkv_nb - 1) == 0

    x = _Act(x_prompt.reshape(n_prompt_rows, d), x_sample.transpose(1, 0, 2).reshape(tm, d), 0)
    pad =(-(db + bp)) % _SUBLANES
    c_all = jnp.concatenate([c_sample, c_prompt, jnp.zeros((pad, d), _F32)], axis=0)
    mod = _adaln_mod(c_all, ada_w, ada_b)

    norm_g4 = norm_g.reshape(depth, 2, 1, d)
    vec3 = lambda a: a.reshape(a.shape[0], 1, a.shape[1])
    w0_3, a0_3, v0_3, kk_3, ka_3 = vec3(decay_w0), vec3(iclr_a0), vec3(vres_v0), vec3(k_k), vec3(k_a)
    rk_3, lnw_3, lnb_3 = r_k.reshape(depth, 1, g_dim), vec3(ln_x_w), vec3(ln_x_b)
    state_conv_t = state_conv.transpose(0, 2, 1, 3)
    state_ffn_t = state_ffn.transpose(0, 2, 1, 3)
    tiles = dict(tm=tm, n_pt=n_pt, tps=tps, db=db)
    w_out = _cast_bf16(w_out, 512)
    ffn_up = _cast_bf16(ffn_up, 128)
    ffn_down = _cast_bf16(ffn_down, 512)

    new = {k: [] for k in ("wkv_p", "shift_p", "conv_p", "ffn_p", "wkv_s", "shift_s", "conv_s", "ffn_s")}
    v_first = None
    for l in range(depth):
        outs = _norm_lora(l, x, mod, norm_g4, mu_x.transpose(0, 2, 1), state_shift, decay_lora1, iclr_lora1, gate_lora1,
                          vres_lora1, bp=bp, **tiles)
        if l > 0:
            h, aw, aa, ag, av, hp_last, hs_last = outs
        else:
            h, aw, aa, ag, hp_last, hs_last = outs
            av = None
        mouts = _mix(l, h, (aw, aa, ag, av), v_first, w_in,
                     (decay_lora2, iclr_lora2, gate_lora2, vres_lora2), (w0_3, a0_3, v0_3),
                     mu_rkv, kk_3, ka_3, conv_w, state_shift, state_conv_t, tn=mix_tn, bp=bp, **tiles)
        seqs_p, seqs_s, rest = mouts[:7], mouts[7:14], mouts[14:]
        if l > 0:
            o_conv, conv_p8, conv_s = rest
        else:
            o_conv, v_first, conv_p8, conv_s = rest

        o_p, wkv_p = _wkv(l, tuple(a.reshape(bp, t_len, g_dim) for a in seqs_p), chunk, wkv_prompt_nb,
                          rk_3, lnw_3, lnb_3)
        o_p = o_p.reshape(n_prompt_rows, g_dim)
        o_s, wkv_s = _wkv_decode(l, tuple(a.reshape(ts, db, g_dim) for a in seqs_s), state_wkv[l],
                                 rk_3, lnw_3, lnb_3, ts=ts, db=db, nb=wkv_nb, first_block=0)
        o_s = o_s.reshape(tm, g_dim)

        x1, h2 = _outproj(l, o_p, o_s, o_conv, x, mod, norm_g4, w_out, **tiles)
        final_g = final_norm_g.reshape(1, d) if l == depth - 1 else None
        *x_out, fpa, fpb, fsa, fsb = _ffn(l, h2, x1, mod, ffn_up, ffn_conv, ffn_down, state_ffn_t, final_g,
                                          tn=ffn_tn, bp=bp, **tiles)
        x = _Act(x_out[0], x_out[0], n_pt) if len(x_out) == 1 else _Act(x_out[0], x_out[1], 0)

        new["wkv_p"].append(wkv_p)
        new["wkv_s"].append(wkv_s)
        new["shift_p"].append(hp_last[:, _SUBLANES - 1])
        new["shift_s"].append(hs_last)
        new["conv_p"].append(conv_p8[:, _SUBLANES - 2:])
        new["conv_s"].append(conv_s.transpose(1, 0, 2))
        last_tiles = slice(tps - 1, n_pt, tps)
        new["ffn_p"].append(jnp.concatenate([fpa[last_tiles, _SUBLANES - 2:], fpb[last_tiles, _SUBLANES - 2:]],
                                            axis=-1))
        new["ffn_s"].append(jnp.concatenate([fsa, fsb], axis=-1).transpose(1, 0, 2))

    y_prompt = x.prompt.reshape(bp, t_len, d)
    y_sample = x.decode.reshape(ts, db, d).transpose(1, 0, 2)
    st = {k: jnp.stack(vs) for k, vs in new.items()}
    return (y_prompt, y_sample, st["wkv_p"], st["shift_p"], st["conv_p"], st["ffn_p"],
            st["wkv_s"], st["shift_s"], st["conv_s"], st["ffn_s"])


def kernel(x_prompt, x_sample, c_prompt, c_sample, state_wkv, state_shift, state_conv, state_ffn, ada_w, ada_b, norm_g, final_norm_g, w_in, mu_x, mu_rkv, decay_w0, decay_lora1, decay_lora2, iclr_a0, iclr_lora1, iclr_lora2, gate_lora1, gate_lora2, vres_v0, vres_lora1, vres_lora2, k_k, k_a, r_k, ln_x_w, ln_x_b, conv_w, w_out, ffn_up, ffn_conv, ffn_down):
    return _forward(x_prompt, x_sample, c_prompt, c_sample, state_wkv, state_shift, state_conv, state_ffn,
                    ada_w, ada_b, norm_g, final_norm_g, w_in, mu_x, mu_rkv, decay_w0, decay_lora1, decay_lora2,
                    iclr_a0, iclr_lora1, iclr_lora2, gate_lora1, gate_lora2, vres_v0, vres_lora1, vres_lora2,
                    k_k, k_a, r_k, ln_x_w, ln_x_b, conv_w, w_out, ffn_up, ffn_conv, ffn_down)
```

```python
import functools
from typing import NamedTuple

import jax
import jax.numpy as jnp
from jax import lax
from jax.experimental import pallas as pl
from jax.experimental.pallas import tpu as pltpu

_F32 = jnp.float32
_BF16 = jnp.bfloat16

_HEAD = 64
_PAIR = 2 * _HEAD
_SEG = 256
_RMS_EPS = 1e-6
_GN_EPS = 64e-5
_VMEM_LIMIT_BYTES = 56 * 1024 * 1024
_SUBLANES = 8


def _params(*sem):
    return pltpu.CompilerParams(dimension_semantics=sem, vmem_limit_bytes=_VMEM_LIMIT_BYTES)


def _dot(a, b):
    return jnp.dot(a.astype(_BF16), b.astype(_BF16), preferred_element_type=_F32)


def _dot_nt(a, b):
    return lax.dot_general(a.astype(_BF16), b.astype(_BF16), (((1,), (1,)), ((), ())),
                           preferred_element_type=_F32)


def _dot_tn(a, b):
    return lax.dot_general(a.astype(_BF16), b.astype(_BF16), (((0,), (0,)), ((), ())),
                           preferred_element_type=_F32)


def _split2(x):
    hi = x.astype(_BF16)
    lo = (x - hi.astype(_F32)).astype(_BF16)
    return hi, lo


def _split3(x):
    hi = x.astype(_BF16)
    r1 = x - hi.astype(_F32)
    mid = r1.astype(_BF16)
    lo = (r1 - mid.astype(_F32)).astype(_BF16)
    return hi, mid, lo


def _dot_exact_rhs(x, m_bf16):
    hi, lo = _split2(x)
    return (jnp.dot(hi, m_bf16, preferred_element_type=_F32)
            + jnp.dot(lo, m_bf16, preferred_element_type=_F32))


def _segment_ones(n):
    r = lax.broadcasted_iota(jnp.int32, (n, n), 0) // _HEAD
    c = lax.broadcasted_iota(jnp.int32, (n, n), 1) // _HEAD
    return (r == c).astype(_BF16)


def _head_sums(x, seg_ones):
    n = seg_ones.shape[0]
    parts = [_dot_exact_rhs(x[:, q * n:(q + 1) * n], seg_ones) for q in range(x.shape[1] // n)]
    return parts[0] if len(parts) == 1 else jnp.concatenate(parts, axis=1)


def _tile_rows(m, reps):
    return m if reps == 1 else jnp.concatenate([m] * reps, axis=0)


def _shift_rows_prompt(x, carry8, first, two=True):
    r8 = lax.broadcasted_iota(jnp.int32, (_SUBLANES, 1), 0)
    l1 = jnp.where(first, 0.0, carry8[_SUBLANES - 1:_SUBLANES, :])
    x1 = pltpu.roll(x, 1, 0)
    x1 = jnp.concatenate([jnp.where(r8 == 0, l1, x1[:_SUBLANES, :]), x1[_SUBLANES:, :]], axis=0)
    if not two:
        return x1
    l2 = jnp.where(first, 0.0, carry8[_SUBLANES - 2:_SUBLANES - 1, :])
    x2 = pltpu.roll(x, 2, 0)
    top2 = jnp.where(r8 == 0, l2, jnp.where(r8 == 1, l1, x2[:_SUBLANES, :]))
    return x1, jnp.concatenate([top2, x2[_SUBLANES:, :]], axis=0)


class _Act(NamedTuple):
    prompt: jax.Array
    decode: jax.Array
    decode_block: int


def _act_specs(x, tm, n_pt):
    d = x.prompt.shape[1]
    return [pl.BlockSpec((tm, d), lambda i, *_: (jnp.minimum(i, n_pt - 1), 0)),
            pl.BlockSpec((tm, d), lambda i, *_: (x.decode_block, 0))]


def _mod_row(ref, i, tps, db):
    return ref[0, 0, pl.ds(db + i // tps, 1), :]


def _mod_tile(ref, db, ts):
    return _tile_rows(ref[0, 0, 0:db, :], ts)


def _mod_kernel(c_ref, w_ref, b_ref, o_ref):
    c = c_ref[...]
    s_hi, s_lo = _split2(c * jax.nn.sigmoid(c))
    w = w_ref[0].astype(_BF16)
    acc = jnp.dot(s_hi, w, preferred_element_type=_F32) + jnp.dot(s_lo, w, preferred_element_type=_F32)
    o_ref[0, 0] = acc + b_ref[0]


def _adaln_mod(c_all, ada_w, ada_b):
    depth, d, six_d = ada_w.shape
    bc = c_all.shape[0]
    tn = 1024
    per = d // tn
    return pl.pallas_call(
        _mod_kernel,
        grid=(depth, six_d // tn),
        in_specs=[pl.BlockSpec((bc, d), lambda l, n: (0, 0)),
                  pl.BlockSpec((1, d, tn), lambda l, n: (l, 0, n)),
                  pl.BlockSpec((1, 1, tn), lambda l, n: (l, 0, n))],
        out_specs=pl.BlockSpec((1, 1, bc, tn), lambda l, n: (l, n // per, 0, n % per)),
        out_shape=jax.ShapeDtypeStruct((depth, 6, bc, d), _F32),
        compiler_params=_params("arbitrary", "arbitrary"),
    )(c_all, ada_w, ada_b.reshape(depth, 1, six_d))


def _norm_lora_kernel(*refs, n_pt, tps, db, has_v):
    n_lora = 4 if has_v else 3
    it = iter(refs)
    xp_ref, xs_ref, sh_ref, sc_ref, g_ref, mu_ref, hl_ref = (next(it) for _ in range(7))
    w_refs = [next(it) for _ in range(n_lora)]
    h_ref = next(it)
    act_refs = [next(it) for _ in range(n_lora)]
    hp_ref, hs_ref, carry_ref, hcur_ref, hprev_ref = (next(it) for _ in range(5))
    i = pl.program_id(0)

    tm = xp_ref.shape[0]
    ts = tm // db
    x = jnp.where(i < n_pt, xp_ref[...], xs_ref[...])
    xn = x * lax.rsqrt(jnp.mean(x * x, axis=-1, keepdims=True) + _RMS_EPS) * g_ref[0, 0]

    @pl.when(i < n_pt)
    def _prompt():
        h = xn * (1.0 + _mod_row(sc_ref, i, tps, db)) + _mod_row(sh_ref, i, tps, db)
        first = (i % tps) == 0
        hprev_ref[...] = _shift_rows_prompt(h, carry_ref[...], first, two=False)
        hcur_ref[...] = h
        carry_ref[...] = h[tm - _SUBLANES:, :]
        hp_ref[0] = h[tm - _SUBLANES:, :]

    @pl.when(i >= n_pt)
    def _decode():
        h = xn * (1.0 + _mod_tile(sc_ref, db, ts)) + _mod_tile(sh_ref, db, ts)
        hprev_ref[...] = jnp.concatenate([hl_ref[0], h[:tm - db, :]], axis=0)
        hcur_ref[...] = h
        hs_ref[...] = h[tm - db:, :]

    h = hcur_ref[...]
    xx = hprev_ref[...] - h
    mu = mu_ref[0]
    h_ref[...] = h.astype(_BF16)
    pre = [_dot(h + xx * mu[g:g + 1], w_refs[g][0]) for g in range(n_lora)]
    act_refs[0][...] = jnp.tanh(pre[0])
    act_refs[1][...] = pre[1]
    act_refs[2][...] = jax.nn.sigmoid(pre[2])
    if has_v:
        act_refs[3][...] = pre[3]


def _norm_lora(l, x, mod, norm_g4, mu_x, state_shift, decay_lora1, iclr_lora1, gate_lora1, vres_lora1,
               *, tm, n_pt, tps, db, bp):
    d = x.prompt.shape[1]
    rows = (n_pt + 1) * tm
    has_v = l > 0
    bc = mod.shape[2]
    ld, la, lg = decay_lora1.shape[2], iclr_lora1.shape[2], gate_lora1.shape[2]
    full = lambda *shape: pl.BlockSpec(shape, lambda i: (l,) + (0,) * (len(shape) - 1))
    in_specs = _act_specs(x, tm, n_pt) + [
                pl.BlockSpec((1, 1, bc, d), lambda i: (l, 0, 0, 0)),
                pl.BlockSpec((1, 1, bc, d), lambda i: (l, 1, 0, 0)),
                pl.BlockSpec((1, 1, 1, d), lambda i: (l, 0, 0, 0)),
                full(1, 4, d), full(1, db, d), full(1, d, ld), full(1, d, la), full(1, d, lg)]
    args = [x.prompt, x.decode, mod, mod, norm_g4, mu_x, state_shift, decay_lora1, iclr_lora1, gate_lora1]
    row_out = lambda n, dt: (pl.BlockSpec((tm, n), lambda i: (i, 0)), jax.ShapeDtypeStruct((rows, n), dt))
    outs = [row_out(d, _BF16), row_out(ld, _F32), row_out(la, _F32), row_out(lg, _F32)]
    if has_v:
        lv = vres_lora1.shape[2]
        in_specs.append(pl.BlockSpec((1, d, lv), lambda i: (l - 1, 0, 0)))
        args.append(vres_lora1)
        outs.append(row_out(lv, _F32))
    outs.append((pl.BlockSpec((1, _SUBLANES, d), lambda i: (jnp.minimum(i // tps, bp - 1), 0, 0)),
                 jax.ShapeDtypeStruct((bp, _SUBLANES, d), _F32)))
    outs.append((pl.BlockSpec((db, d), lambda i: (0, 0)), jax.ShapeDtypeStruct((db, d), _F32)))
    return pl.pallas_call(
        functools.partial(_norm_lora_kernel, n_pt=n_pt, tps=tps, db=db, has_v=has_v),
        grid=(rows // tm,),
        in_specs=in_specs,
        out_specs=[o[0] for o in outs],
        out_shape=[o[1] for o in outs],
        scratch_shapes=[pltpu.VMEM((_SUBLANES, d), _F32), pltpu.VMEM((tm, d), _F32), pltpu.VMEM((tm, d), _F32)],
        compiler_params=_params("arbitrary"),
    )(*args)


def _mix_kernel(*refs, n_pt, tps, db, has_v):
    it = iter(refs)
    h_ref = next(it)
    w_refs = [next(it) for _ in range(6)]
    aw_ref, aa_ref, ag_ref = next(it), next(it), next(it)
    av_ref = next(it) if has_v else None
    d2_ref, i2_ref, g2_ref = next(it), next(it), next(it)
    v2_ref = next(it) if has_v else None
    w0_ref, a0_ref = next(it), next(it)
    v0_ref = next(it) if has_v else None
    mu_ref, kkw_ref, kaw_ref, cw_ref, hl_ref, cs_ref = (next(it) for _ in range(6))
    vf_ref = next(it) if has_v else None
    seq_prompt = [next(it) for _ in range(7)]
    seq_decode = [next(it) for _ in range(7)]
    oc_o = next(it)
    vf_o = None if has_v else next(it)
    cp_o, cso_o = next(it), next(it)
    wc_scr, pcarry, zcarry = (next(it) for _ in range(3))

    i = pl.program_id(1)
    tm = h_ref.shape[0]
    tn = oc_o.shape[1]

    @pl.when(i == 0)
    def _cache_weights():
        for g in range(6):
            wc_scr[g] = w_refs[g][0].astype(_BF16)

    def project(rs):
        h = h_ref[rs, :]
        return [jnp.dot(h, wc_scr[g], preferred_element_type=_F32) for g in range(6)]

    def tail(rs, p, pprev, bg, z, z1, z2, seq_out):
        r_o, lw_o, k_o, v_o, kk_o, kka_o, g_o = seq_out
        mu = mu_ref[0]
        r = p[0] + (pprev[0] - p[0]) * mu[0:1]
        k = p[1] + (pprev[1] - p[1]) * mu[1:2]
        v = p[2] + (pprev[2] - p[2]) * mu[2:3]

        zlog = w0_ref[0] + _dot(aw_ref[rs, :], d2_ref[0])
        softplus = jnp.maximum(-zlog, 0.0) + jnp.log(1.0 + jnp.exp(-jnp.abs(zlog)))
        lw_o[rs, :] = -jnp.exp(-softplus - 0.5)
        a = jax.nn.sigmoid(a0_ref[0] + _dot(aa_ref[rs, :], i2_ref[0]))
        g_o[rs, :] = _dot(ag_ref[rs, :], g2_ref[0])
        if has_v:
            nu = jax.nn.sigmoid(v0_ref[0] + _dot(av_ref[rs, :], v2_ref[0]))
            v = v + (vf_ref[rs, :] - v) * nu
        else:
            vf_o[rs, :] = v

        kk = k * kkw_ref[0]
        norm = jnp.sqrt(_head_sums(kk * kk, _segment_ones(tn)))
        kk = kk / jnp.maximum(norm, 1e-12)
        r_o[rs, :] = r
        k_o[rs, :] = k * (1.0 + (a - 1.0) * kaw_ref[0])
        v_o[rs, :] = v
        kk_o[rs, :] = kk
        kka_o[rs, :] = kk * a

        cw = cw_ref[0]
        zc = z2 * cw[0:1] + z1 * cw[1:2] + z * cw[2:3]
        oc_o[rs, :] = (bg * zc).astype(_BF16)

    @pl.when(i < n_pt)
    def _prompt():
        half = tm // 2
        halves = (slice(0, half), slice(half, tm))
        proj = [project(rs) for rs in halves]
        zs = [q[4] * q[5] for q in proj]
        for idx, rs in enumerate(halves):
            p, z = proj[idx][:3], zs[idx]
            if idx == 0:
                first = (i % tps) == 0
                p_carry = [pcarry[g] for g in range(3)]
                z_carry = zcarry[...]
            else:
                first = False
                p_carry = [q[half - _SUBLANES:, :] for q in proj[0][:3]]
                z_carry = zs[0][half - _SUBLANES:, :]
            pprev = [_shift_rows_prompt(p[g], p_carry[g], first, two=False) for g in range(3)]
            z1, z2 = _shift_rows_prompt(z, z_carry, first)
            tail(rs, p, pprev, proj[idx][3], z, z1, z2, seq_prompt)
        for g in range(3):
            pcarry[g] = proj[1][g][half - _SUBLANES:, :]
        zcarry[...] = zs[1][half - _SUBLANES:, :]
        cp_o[0] = zs[1][half - _SUBLANES:, :]

    @pl.when(i >= n_pt)
    def _decode():
        rs = slice(0, tm)
        proj = project(rs)
        p, z = proj[:3], proj[4] * proj[5]
        hl = hl_ref[0].astype(_BF16)
        pprev = [jnp.concatenate([jnp.dot(hl, wc_scr[g], preferred_element_type=_F32), p[g][:tm - db, :]], axis=0)
                 for g in range(3)]
        z1 = jnp.concatenate([cs_ref[0, 1], z[:tm - db, :]], axis=0)
        z2 = jnp.concatenate([cs_ref[0, 0], cs_ref[0, 1], z[:tm - 2 * db, :]], axis=0)
        cso_o[0] = z[tm - 2 * db:tm - db, :]
        cso_o[1] = z[tm - db:, :]
        tail(rs, p, pprev, proj[3], z, z1, z2, seq_decode)


def _mix(l, h, acts, vf_in, w_in, lora2, vecs, mu_rkv, k_k3, k_a3, conv_w, state_shift, state_conv_t,
         *, tm, tn, n_pt, tps, db, bp):
    rows, d = h.shape
    has_v = l > 0
    g_dim = mu_rkv.shape[2]
    nj = g_dim // tn
    aw, aa, ag, av = acts
    d2, i2, g2, v2 = lora2
    w0, a0, v0 = vecs
    row_in = lambda arr: pl.BlockSpec((tm, arr.shape[1]), lambda j, i: (i, 0))
    col3 = lambda arr, ll: pl.BlockSpec((1, arr.shape[1], tn), lambda j, i: (ll, 0, j))
    in_specs = [row_in(h)] + [pl.BlockSpec((1, d, tn), lambda j, i, g=g: (l, 0, g * nj + j),
                                           pipeline_mode=pl.Buffered(1)) for g in range(6)]
    args = [h] + [w_in] * 6
    in_specs += [row_in(aw), row_in(aa), row_in(ag)]
    args += [aw, aa, ag]
    if has_v:
        in_specs.append(row_in(av))
        args.append(av)
    in_specs += [col3(d2, l), col3(i2, l), col3(g2, l)]
    args += [d2, i2, g2]
    if has_v:
        in_specs.append(col3(v2, l - 1))
        args.append(v2)
    in_specs += [col3(w0, l), col3(a0, l)]
    args += [w0, a0]
    if has_v:
        in_specs.append(col3(v0, l - 1))
        args.append(v0)
    in_specs += [col3(mu_rkv, l), col3(k_k3, l), col3(k_a3, l), col3(conv_w, l),
                 pl.BlockSpec((1, db, d), lambda j, i: (l, 0, 0)),
                 pl.BlockSpec((1, 2, db, tn), lambda j, i: (l, 0, 0, j))]
    args += [mu_rkv, k_k3, k_a3, conv_w, state_shift, state_conv_t]
    if has_v:
        in_specs.append(pl.BlockSpec((tm, tn), lambda j, i: (i, j)))
        args.append(vf_in)
    tile = pl.BlockSpec((tm, tn), lambda j, i: (i, j))
    prompt_tile = pl.BlockSpec((tm, tn), lambda j, i: (jnp.minimum(i, n_pt - 1), j))
    decode_tile = pl.BlockSpec((tm, tn), lambda j, i: (0, j))
    outs = [(prompt_tile, jax.ShapeDtypeStruct((n_pt * tm, g_dim), _F32)) for _ in range(7)]
    outs += [(decode_tile, jax.ShapeDtypeStruct((tm, g_dim), _F32)) for _ in range(7)]
    outs.append((tile, jax.ShapeDtypeStruct((rows, g_dim), _BF16)))
    if not has_v:
        outs.append((tile, jax.ShapeDtypeStruct((rows, g_dim), _F32)))
    outs.append((pl.BlockSpec((1, _SUBLANES, tn), lambda j, i: (jnp.minimum(i // tps, bp - 1), 0, j)),
                 jax.ShapeDtypeStruct((bp, _SUBLANES, g_dim), _F32)))
    outs.append((pl.BlockSpec((2, db, tn), lambda j, i: (0, 0, j)), jax.ShapeDtypeStruct((2, db, g_dim), _F32)))
    return pl.pallas_call(
        functools.partial(_mix_kernel, n_pt=n_pt, tps=tps, db=db, has_v=has_v),
        grid=(nj, rows // tm),
        in_specs=in_specs,
        out_specs=[o[0] for o in outs],
        out_shape=[o[1] for o in outs],
        scratch_shapes=[pltpu.VMEM((6, d, tn), _BF16), pltpu.VMEM((3, _SUBLANES, tn), _F32),
                        pltpu.VMEM((_SUBLANES, tn), _F32)],
        compiler_params=_params("arbitrary", "arbitrary"),
    )(*args)


def _unit_lower_inverse_minus_identity(lmats, ri, ci, chunk, mul):
    same8 = (ri >> 3) == (ci >> 3)
    d8 = [jnp.where(same8, m, 0.0) for m in lmats]
    d2 = [mul(a, a) for a in d8]
    d3 = [mul(a, b) for a, b in zip(d8, d2)]
    d4 = [mul(b, b) for b in d2]
    x = [a + b + cc for a, b, cc in zip(d8, d2, d3)]
    xd4 = [mul(a, b) for a, b in zip(x, d4)]
    x = [a + b + cc for a, b, cc in zip(x, d4, xd4)]
    size = 16
    while size <= chunk:
        sh = size.bit_length() - 1
        level = ((ri >> sh) == (ci >> sh)) & ((ri >> (sh - 1)) != (ci >> (sh - 1)))
        e = [jnp.where(level, m, 0.0) for m in lmats]
        y = [b + mul(a, b) for a, b in zip(x, e)]
        x = [a + b + mul(b, a) for a, b in zip(x, y)]
        size *= 2
    return x


def _wkv_kernel(r_ref, lw_ref, k_ref, v_ref, kk_ref, kka_ref, g_ref, rk_ref, lnw_ref, lnb_ref,
                o_ref, so_ref, s_scr, *, n_chunks):
    ci_grid = pl.program_id(1)
    nb, c, g_dim = r_ref.shape
    npair = g_dim // _PAIR
    n2 = 2 * c

    lane = lax.broadcasted_iota(jnp.int32, (1, _PAIR), 1)
    m0 = (lane < _HEAD).astype(_F32)
    m1 = 1.0 - m0

    @pl.when(ci_grid == 0)
    def _init():
        s_scr[...] = jnp.zeros(s_scr.shape, _F32)

    ri = lax.broadcasted_iota(jnp.int32, (n2, n2), 0) & (c - 1)
    ci = lax.broadcasted_iota(jnp.int32, (n2, n2), 1) & (c - 1)
    strict = ri > ci
    incl = ri >= ci
    tri = (lax.broadcasted_iota(jnp.int32, (c, c), 0) >= lax.broadcasted_iota(jnp.int32, (c, c), 1)).astype(_BF16)
    seg_ones = _segment_ones(_SEG)

    def stack(x):
        return jnp.concatenate([x * m0, x * m1], axis=0)

    units = [(u, p) for u in range(nb) for p in range(npair)]
    sl = lambda p: slice(p * _PAIR, (p + 1) * _PAIR)

    seq = []
    for u in range(nb):
        lw = lw_ref[u]
        l_hi, l_mid, l_lo = _split3(lw)
        cl = (jnp.dot(tri, l_hi, preferred_element_type=_F32)
              + jnp.dot(tri, l_mid, preferred_element_type=_F32)
              + jnp.dot(tri, l_lo, preferred_element_type=_F32))
        cl_end = cl[c - 1:c, :]
        r, k, v, kk, kka = r_ref[u], k_ref[u], v_ref[u], kk_ref[u], kka_ref[u]
        e_neg = jnp.exp(-cl)
        e_tail = jnp.exp(cl_end - cl)
        seq.append(dict(r=r, k=k, v=v, p_end=jnp.exp(cl_end), rt=r * jnp.exp(cl), at=-(kk * jnp.exp(cl - lw)),
                        kt=k * e_neg, bt=kka * e_neg, kh=k * e_tail, bh=kka * e_tail))

    ar2 = [jnp.concatenate([stack(seq[u]["at"][:, sl(p)]), stack(seq[u]["rt"][:, sl(p)])], axis=0).astype(_BF16)
           for u, p in units]
    mb = [_dot_nt(a, stack(seq[u]["bt"][:, sl(p)])) for a, (u, p) in zip(ar2, units)]
    mk = [_dot_nt(a, stack(seq[u]["kt"][:, sl(p)])) for a, (u, p) in zip(ar2, units)]
    s_old = [s_scr[u, p] for u, p in units]
    uy0 = [_dot_nt(jnp.concatenate([seq[u]["at"][:, sl(p)], seq[u]["rt"][:, sl(p)]], axis=0), s)
           for (u, p), s in zip(units, s_old)]
    x_inv = _unit_lower_inverse_minus_identity([jnp.where(strict, m[:n2], 0.0) for m in mb], ri, ci, c, _dot)
    vs = [stack(seq[u]["v"][:, sl(p)]).astype(_BF16) for u, p in units]
    ws = [stack(q[:c]) + _dot(jnp.where(strict, m[:n2], 0.0), vv) for q, m, vv in zip(uy0, mk, vs)]
    us = [(w + _dot(xi, w)).astype(_BF16) for xi, w in zip(x_inv, ws)]
    ysd = [_dot(jnp.where(incl, m1_[n2:], 0.0), uu) + _dot(jnp.where(incl, m2_[n2:], 0.0), vv)
           for m1_, m2_, uu, vv in zip(mb, mk, us, vs)]
    ys = [q[c:] + d[:c] + d[c:] for q, d in zip(uy0, ysd)]
    for idx, (u, p) in enumerate(units):
        uv = jnp.concatenate([us[idx], vs[idx]], axis=0)
        bk = jnp.concatenate([stack(seq[u]["bh"][:, sl(p)]), stack(seq[u]["kh"][:, sl(p)])], axis=0)
        s_scr[u, p] = s_old[idx] * seq[u]["p_end"][:, sl(p)] + _dot_tn(uv, bk)
    for u in range(nb):
        y = jnp.concatenate(ys[u * npair:(u + 1) * npair], axis=1)
        mean = _head_sums(y, seg_ones) * (1.0 / _HEAD)
        dy = y - mean
        var = _head_sums(dy * dy, seg_ones) * (1.0 / _HEAD)
        yn = dy * lax.rsqrt(var + _GN_EPS) * lnw_ref[...] + lnb_ref[...]
        bonus = _head_sums(seq[u]["r"] * seq[u]["k"] * rk_ref[...], seg_ones) * seq[u]["v"]
        o_ref[u] = ((yn + bonus) * g_ref[u]).astype(o_ref.dtype)

    @pl.when(ci_grid == n_chunks - 1)
    def _final():
        for u, p in units:
            s = s_scr[u, p]
            so_ref[u, 2 * p] = s[:_HEAD, :_HEAD]
            so_ref[u, 2 * p + 1] = s[_HEAD:, _HEAD:]


def _wkv(l, seqs, chunk, nb, r_k3, ln_w3, ln_b3):
    n_seq, t_len, g_dim = seqs[0].shape
    n_heads = g_dim // _HEAD
    n_chunks = t_len // chunk
    blk = pl.BlockSpec((nb, chunk, g_dim), lambda bi, ci: (bi, ci, 0))
    vec = pl.BlockSpec((None, 1, g_dim), lambda bi, ci: (l, 0, 0))
    return pl.pallas_call(
        functools.partial(_wkv_kernel, n_chunks=n_chunks),
        grid=(n_seq // nb, n_chunks),
        in_specs=[blk] * 7 + [vec] * 3,
        out_specs=[blk, pl.BlockSpec((nb, n_heads, _HEAD, _HEAD), lambda bi, ci: (bi, 0, 0, 0))],
        out_shape=[jax.ShapeDtypeStruct((n_seq, t_len, g_dim), _BF16),
                   jax.ShapeDtypeStruct((n_seq, n_heads, _HEAD, _HEAD), _F32)],
        scratch_shapes=[pltpu.VMEM((nb, g_dim // _PAIR, _PAIR, _PAIR), _F32)],
        compiler_params=_params("arbitrary", "arbitrary"),
    )(*seqs, r_k3, ln_w3, ln_b3)


def _wkv_decode_kernel(r_ref, lw_ref, k_ref, v_ref, kk_ref, kka_ref, g_ref, rk_ref, lnw_ref, lnb_ref, s0_ref,
                       o_ref, so_ref, *, ts, nb):
    g_dim = r_ref.shape[2]
    npair = g_dim // _PAIR
    n = ts * nb
    n2 = 2 * n
    nb_bits = nb.bit_length() - 1

    lane = lax.broadcasted_iota(jnp.int32, (1, _PAIR), 1)
    m0 = (lane < _HEAD).astype(_F32)
    m1 = 1.0 - m0

    def stack(x):
        return jnp.concatenate([x * m0, x * m1], axis=0)

    def rows_tb(ref):
        return jnp.concatenate([ref[t] for t in range(ts)], axis=0)

    lw_t = [lw_ref[t] for t in range(ts)]
    cl_t = [lw_t[0]]
    for t in range(1, ts):
        cl_t.append(cl_t[-1] + lw_t[t])
    cl = jnp.concatenate(cl_t, axis=0)
    lw = jnp.concatenate(lw_t, axis=0)
    cl_end_b = cl_t[-1]
    cl_end = _tile_rows(cl_end_b, ts)
    p_end = jnp.exp(cl_end_b)
    r, k, v, kk, kka = (rows_tb(ref) for ref in (r_ref, k_ref, v_ref, kk_ref, kka_ref))
    e_neg = jnp.exp(-cl)
    e_tail = jnp.exp(cl_end - cl)
    rt = r * jnp.exp(cl)
    at = -(kk * jnp.exp(cl - lw))
    kt = k * e_neg
    bt = kka * e_neg
    kh = k * e_tail
    bh = kka * e_tail

    ri = lax.broadcasted_iota(jnp.int32, (n2, n2), 0)
    ci = lax.broadcasted_iota(jnp.int32, (n2, n2), 1)
    same_seq = (ri & (nb - 1)) == (ci & (nb - 1))
    t_r = (ri & (n - 1)) >> nb_bits
    t_c = (ci & (n - 1)) >> nb_bits
    strict = same_seq & (t_r > t_c)
    incl = same_seq & (t_r >= t_c)
    seq_of_row2 = lax.broadcasted_iota(jnp.int32, (n2, 1), 0) & (nb - 1)
    seq_of_row4 = lax.broadcasted_iota(jnp.int32, (2 * n2, 1), 0) & (nb - 1)
    zero = jnp.zeros((_HEAD, _HEAD), _F32)

    pairs = range(npair)
    sls = [slice(p * _PAIR, (p + 1) * _PAIR) for p in pairs]
    ar2 = [jnp.concatenate([stack(at[:, sl]), stack(rt[:, sl])], axis=0).astype(_BF16) for sl in sls]
    mb = [_dot_nt(a, stack(bt[:, sl])) for a, sl in zip(ar2, sls)]
    mk = [_dot_nt(a, stack(kt[:, sl])) for a, sl in zip(ar2, sls)]
    lab = [jnp.where(strict, m[:n2], 0.0) for m in mb]
    d2 = [_dot(a, a) for a in lab]
    d3 = [_dot(a, b) for a, b in zip(lab, d2)]
    x_inv = [a + b + cc for a, b, cc in zip(lab, d2, d3)]

    def block_diag(b, p):
        top = jnp.concatenate([s0_ref[b, 2 * p], zero], axis=1)
        bot = jnp.concatenate([zero, s0_ref[b, 2 * p + 1]], axis=1)
        return jnp.concatenate([top, bot], axis=0)

    ys = []
    for p, sl in zip(pairs, sls):
        s_b = [block_diag(b, p) for b in range(nb)]
        ar = jnp.concatenate([at[:, sl], rt[:, sl]], axis=0)
        ar_cat = jnp.concatenate([jnp.where(seq_of_row2 == b, ar, 0.0).astype(_BF16) for b in range(nb)], axis=1)
        s_cat = jnp.concatenate([s.astype(_BF16) for s in s_b], axis=1)
        uy0 = _dot_nt(ar_cat, s_cat)
        vs = stack(v[:, sl]).astype(_BF16)
        ws = stack(uy0[:n]) + _dot(jnp.where(strict, mk[p][:n2], 0.0), vs)
        us = (ws + _dot(x_inv[p], ws)).astype(_BF16)
        ysd = _dot(jnp.where(incl, mb[p][n2:], 0.0), us) + _dot(jnp.where(incl, mk[p][n2:], 0.0), vs)
        ys.append(uy0[n:] + ysd[:n] + ysd[n:])
        uv = jnp.concatenate([us, vs], axis=0)
        uv_cat = jnp.concatenate([jnp.where(seq_of_row4 == b, uv, jnp.zeros_like(uv)) for b in range(nb)], axis=1)
        bk = jnp.concatenate([stack(bh[:, sl]), stack(kh[:, sl])], axis=0)
        upd = _dot_tn(uv_cat, bk)
        for b in range(nb):
            s_new = (s_b[b] * p_end[b:b + 1, sl] + upd[b * _PAIR:(b + 1) * _PAIR, :])
            so_ref[b, 2 * p] = s_new[:_HEAD, :_HEAD]
            so_ref[b, 2 * p + 1] = s_new[_HEAD:, _HEAD:]

    y = jnp.concatenate(ys, axis=1)
    seg_ones = _segment_ones(_SEG)
    mean = _head_sums(y, seg_ones) * (1.0 / _HEAD)
    dy = y - mean
    var = _head_sums(dy * dy, seg_ones) * (1.0 / _HEAD)
    yn = dy * lax.rsqrt(var + _GN_EPS) * lnw_ref[...] + lnb_ref[...]
    bonus = _head_sums(r * k * rk_ref[...], seg_ones) * v
    out = ((yn + bonus) * rows_tb(g_ref)).astype(o_ref.dtype)
    for t in range(ts):
        o_ref[t] = out[t * nb:(t + 1) * nb, :]


def _wkv_decode(l, seqs, state0, r_k3, ln_w3, ln_b3, *, ts, db, nb, first_block):
    g_dim = seqs[0].shape[2]
    n_heads = g_dim // _HEAD
    blk = pl.BlockSpec((ts, nb, g_dim), lambda bi: (first_block, bi, 0))
    vec = pl.BlockSpec((None, 1, g_dim), lambda bi: (l, 0, 0))
    st = pl.BlockSpec((nb, n_heads, _HEAD, _HEAD), lambda bi: (bi, 0, 0, 0))
    return pl.pallas_call(
        functools.partial(_wkv_decode_kernel, ts=ts, nb=nb),
        grid=(db // nb,),
        in_specs=[blk] * 7 + [vec] * 3 + [st],
        out_specs=[pl.BlockSpec((ts, nb, g_dim), lambda bi: (0, bi, 0)), st],
        out_shape=[jax.ShapeDtypeStruct((ts, db, g_dim), _BF16),
                   jax.ShapeDtypeStruct((db, n_heads, _HEAD, _HEAD), _F32)],
        compiler_params=_params("arbitrary"),
    )(*seqs, r_k3, ln_w3, ln_b3, state0)


def _cast_kernel(w_ref, o_ref):
    o_ref[...] = w_ref[...].astype(o_ref.dtype)


def _cast_bf16(w, rows_per_block):
    depth, k, n = w.shape
    spec = pl.BlockSpec((1, rows_per_block, n), lambda l, i: (l, i, 0))
    return pl.pallas_call(
        _cast_kernel, grid=(depth, k // rows_per_block), in_specs=[spec], out_specs=spec,
        out_shape=jax.ShapeDtypeStruct(w.shape, _BF16),
        compiler_params=_params("arbitrary", "arbitrary"),
    )(w)


def _outproj_kernel(op_ref, os_ref, oc_ref, xp_ref, xs_ref, ga_ref, sh_ref, sc_ref, g_ref, w_ref,
                    x1_ref, h2_ref, *, n_pt, tps, db):
    i = pl.program_id(0)
    tm = xp_ref.shape[0]
    ts = tm // db
    g_dim = op_ref.shape[1]

    def rows_out(rs, o_ref, x_ref, ga, sc, sh):
        acc = (jnp.dot(o_ref[rs, :], w_ref[0, :g_dim, :], preferred_element_type=_F32)
               + jnp.dot(oc_ref[rs, :], w_ref[0, g_dim:, :], preferred_element_type=_F32))
        x1 = x_ref[rs, :] + ga * acc
        x1_ref[rs, :] = x1
        xn = x1 * lax.rsqrt(jnp.mean(x1 * x1, axis=-1, keepdims=True) + _RMS_EPS) * g_ref[0, 0]
        h2_ref[rs, :] = (xn * (1.0 + sc) + sh).astype(_BF16)

    half = tm // 2
    halves = (slice(0, half), slice(half, tm))

    @pl.when(i < n_pt)
    def _prompt():
        ga, sc, sh = (_mod_row(ref, i, tps, db) for ref in (ga_ref, sc_ref, sh_ref))
        for rs in halves:
            rows_out(rs, op_ref, xp_ref, ga, sc, sh)

    @pl.when(i >= n_pt)
    def _decode():
        ga, sc, sh = (_mod_tile(ref, db, ts) for ref in (ga_ref, sc_ref, sh_ref))
        for rs in halves:
            rows_out(rs, os_ref, xs_ref, ga[rs, :], sc[rs, :], sh[rs, :])


def _outproj(l, o_p, o_s, o_conv, x, mod, norm_g4, w_out, *, tm, n_pt, tps, db):
    d = x.prompt.shape[1]
    rows = (n_pt + 1) * tm
    g_dim = o_p.shape[1]
    bc = mod.shape[2]
    modspec = lambda comp: pl.BlockSpec((1, 1, bc, d), lambda i: (l, comp, 0, 0))
    return pl.pallas_call(
        functools.partial(_outproj_kernel, n_pt=n_pt, tps=tps, db=db),
        grid=(rows // tm,),
        in_specs=[pl.BlockSpec((tm, g_dim), lambda i: (jnp.minimum(i, n_pt - 1), 0)),
                  pl.BlockSpec((tm, g_dim), lambda i: (0, 0)),
                  pl.BlockSpec((tm, o_conv.shape[1]), lambda i: (i, 0))] + _act_specs(x, tm, n_pt) + [
                  modspec(2), modspec(3), modspec(4),
                  pl.BlockSpec((1, 1, 1, d), lambda i: (l, 1, 0, 0)),
                  pl.BlockSpec((1, d, d), lambda i: (l, 0, 0), pipeline_mode=pl.Buffered(1))],
        out_specs=[pl.BlockSpec((tm, d), lambda i: (i, 0)), pl.BlockSpec((tm, d), lambda i: (i, 0))],
        out_shape=[jax.ShapeDtypeStruct((rows, d), _F32), jax.ShapeDtypeStruct((rows, d), _BF16)],
        compiler_params=_params("arbitrary"),
    )(o_p, o_s, o_conv, x.prompt, x.decode, mod, mod, mod, norm_g4, w_out)


def _ffn_kernel(*refs, n_pt, tps, db, nj, final_norm):
    it = iter(refs)
    (h2_ref, wa_ref, wb_ref, cwa_ref, cwb_ref, wd_ref, x1_ref, ga_ref, sfa_ref, sfb_ref) = (next(it) for _ in range(10))
    fg_ref = next(it) if final_norm else None
    x2_ref = next(it)
    x2s_ref = next(it) if final_norm else x2_ref
    fpa_o, fpb_o, fsa_o, fsb_o = (next(it) for _ in range(4))
    acc_scr, ca_scr, cb_scr = (next(it) for _ in range(3))
    i = pl.program_id(0)
    j = pl.program_id(1)
    tm = h2_ref.shape[0]
    ts = tm // db

    @pl.when(j == 0)
    def _zero():
        acc_scr[...] = jnp.zeros(acc_scr.shape, _F32)

    def prompt_body():
        half = tm // 2
        halves = (slice(0, half), slice(half, tm))
        us = [[jnp.dot(h2_ref[rs, :], w_ref[0], preferred_element_type=_F32) for w_ref in (wa_ref, wb_ref)]
              for rs in halves]
        wd = wd_ref[0].astype(_BF16)
        for r, rs in enumerate(halves):
            conv = []
            for idx, (cw_ref, c_scr, fp_o) in enumerate(((cwa_ref, ca_scr, fpa_o), (cwb_ref, cb_scr, fpb_o))):
                u = us[r][idx]
                if r == 0:
                    u1, u2 = _shift_rows_prompt(u, c_scr[j], (i % tps) == 0)
                else:
                    u1, u2 = _shift_rows_prompt(u, us[0][idx][half - _SUBLANES:, :], False)
                    c_scr[j] = u[half - _SUBLANES:, :]
                    fp_o[0] = u[half - _SUBLANES:, :]
                cw = cw_ref[0]
                conv.append(u2 * cw[0:1] + u1 * cw[1:2] + u * cw[2:3])
            gact = ((conv[0] * jax.nn.sigmoid(conv[0])) * conv[1]).astype(_BF16)
            acc_scr[rs, :] += jnp.dot(gact, wd, preferred_element_type=_F32)

    def decode_body():
        h2 = h2_ref[...]
        conv = []
        for w_ref, cw_ref, sf, fp_o, fs_o in ((wa_ref, cwa_ref, sfa_ref, fpa_o, fsa_o),
                                              (wb_ref, cwb_ref, sfb_ref, fpb_o, fsb_o)):
            u = jnp.dot(h2, w_ref[0], preferred_element_type=_F32)
            u1 = jnp.concatenate([sf[0, 1], u[:tm - db, :]], axis=0)
            u2 = jnp.concatenate([sf[0, 0], sf[0, 1], u[:tm - 2 * db, :]], axis=0)
            fp_o[0] = jnp.zeros(fp_o.shape[1:], _F32)
            fs_o[0] = u[tm - 2 * db:tm - db, :]
            fs_o[1] = u[tm - db:, :]
            cw = cw_ref[0]
            conv.append(u2 * cw[0:1] + u1 * cw[1:2] + u * cw[2:3])
        gact = ((conv[0] * jax.nn.sigmoid(conv[0])) * conv[1]).astype(_BF16)
        acc_scr[...] += jnp.dot(gact, wd_ref[0].astype(_BF16), preferred_element_type=_F32)

    pl.when(i < n_pt)(prompt_body)
    pl.when(i >= n_pt)(decode_body)

    def finish(ga, out_ref):
        x2 = x1_ref[...] + ga * acc_scr[...]
        if final_norm:
            x2 = x2 * lax.rsqrt(jnp.mean(x2 * x2, axis=-1, keepdims=True) + _RMS_EPS) * fg_ref[...]
        out_ref[...] = x2

    @pl.when((j == nj - 1) & (i < n_pt))
    def _finish_prompt():
        finish(_mod_row(ga_ref, i, tps, db), x2_ref)

    @pl.when((j == nj - 1) & (i >= n_pt))
    def _finish_decode():
        finish(_mod_tile(ga_ref, db, ts), x2s_ref)


def _ffn(l, h2, x1, mod, ffn_up, ffn_conv, ffn_down, state_ffn_t, final_g, *, tm, tn, n_pt, tps, db, bp):
    rows, d = x1.shape
    f = ffn_down.shape[1]
    nj = f // tn
    bc = mod.shape[2]
    final_norm = final_g is not None
    in_specs = [pl.BlockSpec((tm, d), lambda i, j: (i, 0)),
                pl.BlockSpec((1, d, tn), lambda i, j: (l, 0, j)),
                pl.BlockSpec((1, d, tn), lambda i, j: (l, 0, nj + j)),
                pl.BlockSpec((1, 3, tn), lambda i, j: (l, 0, j)),
                pl.BlockSpec((1, 3, tn), lambda i, j: (l, 0, nj + j)),
                pl.BlockSpec((1, tn, d), lambda i, j: (l, j, 0)),
                pl.BlockSpec((tm, d), lambda i, j: (i, 0)),
                pl.BlockSpec((1, 1, bc, d), lambda i, j: (l, 5, 0, 0)),
                pl.BlockSpec((1, 2, db, tn), lambda i, j: (l, 0, 0, jnp.where(i >= n_pt, j, 0))),
                pl.BlockSpec((1, 2, db, tn), lambda i, j: (l, 0, 0, nj + jnp.where(i >= n_pt, j, 0)))]
    args = [h2, ffn_up, ffn_up, ffn_conv, ffn_conv, ffn_down, x1, mod, state_ffn_t, state_ffn_t]
    if final_norm:
        in_specs.append(pl.BlockSpec((1, d), lambda i, j: (0, 0)))
        args.append(final_g)
    n_tiles = rows // tm
    pstate = pl.BlockSpec((1, _SUBLANES, tn), lambda i, j: (i, 0, j))
    sstate = pl.BlockSpec((2, db, tn), lambda i, j: (0, 0, jnp.where(i >= n_pt, j, 0)))
    if final_norm:
        x_specs = [pl.BlockSpec((tm, d), lambda i, j: (jnp.minimum(i, n_pt - 1), 0)),
                   pl.BlockSpec((tm, d), lambda i, j: (0, 0))]
        x_shapes = [jax.ShapeDtypeStruct((n_pt * tm, d), _F32), jax.ShapeDtypeStruct((tm, d), _F32)]
    else:
        x_specs = [pl.BlockSpec((tm, d), lambda i, j: (i, 0))]
        x_shapes = [jax.ShapeDtypeStruct((rows, d), _F32)]
    return pl.pallas_call(
        functools.partial(_ffn_kernel, n_pt=n_pt, tps=tps, db=db, nj=nj, final_norm=final_norm),
        grid=(rows // tm, nj),
        in_specs=in_specs,
        out_specs=x_specs + [pstate, pstate, sstate, sstate],
        out_shape=x_shapes + [
                   jax.ShapeDtypeStruct((n_tiles, _SUBLANES, f), _F32),
                   jax.ShapeDtypeStruct((n_tiles, _SUBLANES, f), _F32),
                   jax.ShapeDtypeStruct((2, db, f), _F32), jax.ShapeDtypeStruct((2, db, f), _F32)],
        scratch_shapes=[pltpu.VMEM((tm, d), _F32), pltpu.VMEM((nj, _SUBLANES, tn), _F32),
                        pltpu.VMEM((nj, _SUBLANES, tn), _F32)],
        compiler_params=_params("arbitrary", "arbitrary"),
    )(*args)


def _forward(x_prompt, x_sample, c_prompt, c_sample, state_wkv, state_shift, state_conv, state_ffn,
             ada_w, ada_b, norm_g, final_norm_g, w_in, mu_x, mu_rkv, decay_w0, decay_lora1, decay_lora2,
             iclr_a0, iclr_lora1, iclr_lora2, gate_lora1, gate_lora2, vres_v0, vres_lora1, vres_lora2,
             k_k, k_a, r_k, ln_x_w, ln_x_b, conv_w, w_out, ffn_up, ffn_conv, ffn_down,
             *, chunk=64, mix_tn=256, ffn_tn=512, wkv_nb=16, wkv_prompt_nb=2):
    bp, t_len, d = x_prompt.shape
    db, ts, _ = x_sample.shape
    depth = ada_w.shape[0]
    g_dim = mu_rkv.shape[2]
    n_heads = g_dim // _HEAD
    f = ffn_down.shape[1]
    tm = ts * db
    assert t_len % tm == 0 and t_len % chunk == 0 and ts >= 2 and db % _SUBLANES == 0 and tm % chunk == 0
    tps = t_len // tm
    n_pt = bp * tps
    n_prompt_rows = bp * t_len
    assert ts == 4 and db % wkv_nb == 0 and wkv_nb & (wkv_nb - 1) == 0

    x = _Act(x_prompt.reshape(n_prompt_rows, d), x_sample.transpose(1, 0, 2).reshape(tm, d), 0)
    pad =(-(db + bp)) % _SUBLANES
    c_all = jnp.concatenate([c_sample, c_prompt, jnp.zeros((pad, d), _F32)], axis=0)
    mod = _adaln_mod(c_all, ada_w, ada_b)

    norm_g4 = norm_g.reshape(depth, 2, 1, d)
    vec3 = lambda a: a.reshape(a.shape[0], 1, a.shape[1])
    w0_3, a0_3, v0_3, kk_3, ka_3 = vec3(decay_w0), vec3(iclr_a0), vec3(vres_v0), vec3(k_k), vec3(k_a)
    rk_3, lnw_3, lnb_3 = r_k.reshape(depth, 1, g_dim), vec3(ln_x_w), vec3(ln_x_b)
    state_conv_t = state_conv.transpose(0, 2, 1, 3)
    state_ffn_t = state_ffn.transpose(0, 2, 1, 3)
    tiles = dict(tm=tm, n_pt=n_pt, tps=tps, db=db)
    w_out = _cast_bf16(w_out, 512)
    ffn_up = _cast_bf16(ffn_up, 128)

    new = {k: [] for k in ("wkv_p", "shift_p", "conv_p", "ffn_p", "wkv_s", "shift_s", "conv_s", "ffn_s")}
    v_first = None
    for l in range(depth):
        outs = _norm_lora(l, x, mod, norm_g4, mu_x, state_shift, decay_lora1, iclr_lora1, gate_lora1,
                          vres_lora1, bp=bp, **tiles)
        if l > 0:
            h, aw, aa, ag, av, hp_last, hs_last = outs
        else:
            h, aw, aa, ag, hp_last, hs_last = outs
            av = None
        mouts = _mix(l, h, (aw, aa, ag, av), v_first, w_in,
                     (decay_lora2, iclr_lora2, gate_lora2, vres_lora2), (w0_3, a0_3, v0_3),
                     mu_rkv, kk_3, ka_3, conv_w, state_shift, state_conv_t, tn=mix_tn, bp=bp, **tiles)
        seqs_p, seqs_s, rest = mouts[:7], mouts[7:14], mouts[14:]
        if l > 0:
            o_conv, conv_p8, conv_s = rest
        else:
            o_conv, v_first, conv_p8, conv_s = rest

        o_p, wkv_p = _wkv(l, tuple(a.reshape(bp, t_len, g_dim) for a in seqs_p), chunk, wkv_prompt_nb,
                          rk_3, lnw_3, lnb_3)
        o_p = o_p.reshape(n_prompt_rows, g_dim)
        o_s, wkv_s = _wkv_decode(l, tuple(a.reshape(ts, db, g_dim) for a in seqs_s), state_wkv[l],
                                 rk_3, lnw_3, lnb_3, ts=ts, db=db, nb=wkv_nb, first_block=0)
        o_s = o_s.reshape(tm, g_dim)

        x1, h2 = _outproj(l, o_p, o_s, o_conv, x, mod, norm_g4, w_out, **tiles)
        final_g = final_norm_g.reshape(1, d) if l == depth - 1 else None
        *x_out, fpa, fpb, fsa, fsb = _ffn(l, h2, x1, mod, ffn_up, ffn_conv, ffn_down, state_ffn_t, final_g,
                                          tn=ffn_tn, bp=bp, **tiles)
        x = _Act(x_out[0], x_out[0], n_pt) if len(x_out) == 1 else _Act(x_out[0], x_out[1], 0)

        new["wkv_p"].append(wkv_p)
        new["wkv_s"].append(wkv_s)
        new["shift_p"].append(hp_last[:, _SUBLANES - 1])
        new["shift_s"].append(hs_last)
        new["conv_p"].append(conv_p8[:, _SUBLANES - 2:])
        new["conv_s"].append(conv_s.transpose(1, 0, 2))
        last_tiles = slice(tps - 1, n_pt, tps)
        new["ffn_p"].append(jnp.concatenate([fpa[last_tiles, _SUBLANES - 2:], fpb[last_tiles, _SUBLANES - 2:]],
                                            axis=-1))
        new["ffn_s"].append(jnp.concatenate([fsa, fsb], axis=-1).transpose(1, 0, 2))

    y_prompt = x.prompt.reshape(bp, t_len, d)
    y_sample = x.decode.reshape(ts, db, d).transpose(1, 0, 2)
    st = {k: jnp.stack(vs) for k, vs in new.items()}
    return (y_prompt, y_sample, st["wkv_p"], st["shift_p"], st["conv_p"], st["ffn_p"],
            st["wkv_s"], st["shift_s"], st["conv_s"], st["ffn_s"])


def kernel(x_prompt, x_sample, c_prompt, c_sample, state_wkv, state_shift, state_conv, state_ffn, ada_w, ada_b, norm_g, final_norm_g, w_in, mu_x, mu_rkv, decay_w0, decay_lora1, decay_lora2, iclr_a0, iclr_lora1, iclr_lora2, gate_lora1, gate_lora2, vres_v0, vres_lora1, vres_lora2, k_k, k_a, r_k, ln_x_w, ln_x_b, conv_w, w_out, ffn_up, ffn_conv, ffn_down):
    return _forward(x_prompt, x_sample, c_prompt, c_sample, state_wkv, state_shift, state_conv, state_ffn,
                    ada_w, ada_b, norm_g, final_norm_g, w_in, mu_x, mu_rkv, decay_w0, decay_lora1, decay_lora2,
                    iclr_a0, iclr_lora1, iclr_lora2, gate_lora1, gate_lora2, vres_v0, vres_lora1, vres_lora2,
                    k_k, k_a, r_k, ln_x_w, ln_x_b, conv_w, w_out, ffn_up, ffn_conv, ffn_down)
```

```python
import functools
from typing import NamedTuple

import jax
import jax.numpy as jnp
from jax import lax
from jax.experimental import pallas as pl
from jax.experimental.pallas import tpu as pltpu

_F32 = jnp.float32
_BF16 = jnp.bfloat16

_HEAD = 64
_PAIR = 2 * _HEAD
_SEG = 256
_RMS_EPS = 1e-6
_GN_EPS = 64e-5
_VMEM_LIMIT_BYTES = 56 * 1024 * 1024
_SUBLANES = 8

_WKV_CHUNK = 64
_WKV_PROMPT_SEQS = 2
_WKV_DECODE_SEQS = 16
_MIX_COLS = 256
_FFN_COLS = 512


def _params(*sem):
    return pltpu.CompilerParams(dimension_semantics=sem, vmem_limit_bytes=_VMEM_LIMIT_BYTES)


def _dot(a, b):
    return jnp.dot(a.astype(_BF16), b.astype(_BF16), preferred_element_type=_F32)


def _dot_nt(a, b):
    return lax.dot_general(a.astype(_BF16), b.astype(_BF16), (((1,), (1,)), ((), ())),
                           preferred_element_type=_F32)


def _dot_tn(a, b):
    return lax.dot_general(a.astype(_BF16), b.astype(_BF16), (((0,), (0,)), ((), ())),
                           preferred_element_type=_F32)


def _split2(x):
    hi = x.astype(_BF16)
    lo = (x - hi.astype(_F32)).astype(_BF16)
    return hi, lo


def _split3(x):
    hi = x.astype(_BF16)
    r1 = x - hi.astype(_F32)
    mid = r1.astype(_BF16)
    lo = (r1 - mid.astype(_F32)).astype(_BF16)
    return hi, mid, lo


def _dot_exact_rhs(x, m_bf16):
    hi, lo = _split2(x)
    return (jnp.dot(hi, m_bf16, preferred_element_type=_F32)
            + jnp.dot(lo, m_bf16, preferred_element_type=_F32))


def _segment_ones(n):
    r = lax.broadcasted_iota(jnp.int32, (n, n), 0) // _HEAD
    c = lax.broadcasted_iota(jnp.int32, (n, n), 1) // _HEAD
    return (r == c).astype(_BF16)


def _head_sums(x, seg_ones):
    n = seg_ones.shape[0]
    parts = [_dot_exact_rhs(x[:, q * n:(q + 1) * n], seg_ones) for q in range(x.shape[1] // n)]
    return parts[0] if len(parts) == 1 else jnp.concatenate(parts, axis=1)


def _tile_rows(m, reps):
    return m if reps == 1 else jnp.concatenate([m] * reps, axis=0)


def _shift_rows_prompt(x, carry8, first, two=True):
    r8 = lax.broadcasted_iota(jnp.int32, (_SUBLANES, 1), 0)
    l1 = jnp.where(first, 0.0, carry8[_SUBLANES - 1:_SUBLANES, :])
    x1 = pltpu.roll(x, 1, 0)
    x1 = jnp.concatenate([jnp.where(r8 == 0, l1, x1[:_SUBLANES, :]), x1[_SUBLANES:, :]], axis=0)
    if not two:
        return x1
    l2 = jnp.where(first, 0.0, carry8[_SUBLANES - 2:_SUBLANES - 1, :])
    x2 = pltpu.roll(x, 2, 0)
    top2 = jnp.where(r8 == 0, l2, jnp.where(r8 == 1, l1, x2[:_SUBLANES, :]))
    return x1, jnp.concatenate([top2, x2[_SUBLANES:, :]], axis=0)


class _Act(NamedTuple):
    prompt: jax.Array
    decode: jax.Array
    decode_block: int


def _act_specs(x, tm, n_pt):
    d = x.prompt.shape[1]
    return [pl.BlockSpec((tm, d), lambda i, *_: (jnp.minimum(i, n_pt - 1), 0)),
            pl.BlockSpec((tm, d), lambda i, *_: (x.decode_block, 0))]


def _mod_row(ref, i, tps, db):
    return ref[0, 0, pl.ds(db + i // tps, 1), :]


def _mod_tile(ref, db, ts):
    return _tile_rows(ref[0, 0, 0:db, :], ts)


def _mod_kernel(c_ref, w_ref, b_ref, o_ref):
    c = c_ref[...]
    s_hi, s_lo = _split2(c * jax.nn.sigmoid(c))
    w = w_ref[0].astype(_BF16)
    acc = jnp.dot(s_hi, w, preferred_element_type=_F32) + jnp.dot(s_lo, w, preferred_element_type=_F32)
    o_ref[0, 0] = acc + b_ref[0]


def _adaln_mod(c_all, ada_w, ada_b):
    depth, d, six_d = ada_w.shape
    bc = c_all.shape[0]
    tn = 1024
    per = d // tn
    return pl.pallas_call(
        _mod_kernel,
        grid=(depth, six_d // tn),
        in_specs=[pl.BlockSpec((bc, d), lambda l, n: (0, 0)),
                  pl.BlockSpec((1, d, tn), lambda l, n: (l, 0, n)),
                  pl.BlockSpec((1, 1, tn), lambda l, n: (l, 0, n))],
        out_specs=pl.BlockSpec((1, 1, bc, tn), lambda l, n: (l, n // per, 0, n % per)),
        out_shape=jax.ShapeDtypeStruct((depth, 6, bc, d), _F32),
        compiler_params=_params("arbitrary", "arbitrary"),
    )(c_all, ada_w, ada_b.reshape(depth, 1, six_d))


def _norm_lora_kernel(*refs, n_pt, tps, db, has_v):
    n_lora = 4 if has_v else 3
    it = iter(refs)
    xp_ref, xs_ref, sh_ref, sc_ref, g_ref, mu_ref, hl_ref = (next(it) for _ in range(7))
    w_refs = [next(it) for _ in range(n_lora)]
    h_ref = next(it)
    act_refs = [next(it) for _ in range(n_lora)]
    hp_ref, hs_ref, carry_ref, hcur_ref, hprev_ref = (next(it) for _ in range(5))
    i = pl.program_id(0)

    tm = xp_ref.shape[0]
    ts = tm // db
    x = jnp.where(i < n_pt, xp_ref[...], xs_ref[...])
    xn = x * lax.rsqrt(jnp.mean(x * x, axis=-1, keepdims=True) + _RMS_EPS) * g_ref[0, 0]

    @pl.when(i < n_pt)
    def _prompt():
        h = xn * (1.0 + _mod_row(sc_ref, i, tps, db)) + _mod_row(sh_ref, i, tps, db)
        first = (i % tps) == 0
        hprev_ref[...] = _shift_rows_prompt(h, carry_ref[...], first, two=False)
        hcur_ref[...] = h
        carry_ref[...] = h[tm - _SUBLANES:, :]
        hp_ref[0] = h[tm - _SUBLANES:, :]

    @pl.when(i >= n_pt)
    def _decode():
        h = xn * (1.0 + _mod_tile(sc_ref, db, ts)) + _mod_tile(sh_ref, db, ts)
        hprev_ref[...] = jnp.concatenate([hl_ref[0], h[:tm - db, :]], axis=0)
        hcur_ref[...] = h
        hs_ref[...] = h[tm - db:, :]

    h = hcur_ref[...]
    xx = hprev_ref[...] - h
    mu = mu_ref[0]
    h_ref[...] = h.astype(_BF16)
    pre = [_dot(h + xx * mu[g:g + 1], w_refs[g][0]) for g in range(n_lora)]
    act_refs[0][...] = jnp.tanh(pre[0])
    act_refs[1][...] = pre[1]
    act_refs[2][...] = jax.nn.sigmoid(pre[2])
    if has_v:
        act_refs[3][...] = pre[3]


def _norm_lora(l, x, mod, norm_g4, mu_x, state_shift, decay_lora1, iclr_lora1, gate_lora1, vres_lora1,
               *, tm, n_pt, tps, db, bp):
    d = x.prompt.shape[1]
    rows = (n_pt + 1) * tm
    has_v = l > 0
    bc = mod.shape[2]
    ld, la, lg = decay_lora1.shape[2], iclr_lora1.shape[2], gate_lora1.shape[2]
    full = lambda *shape: pl.BlockSpec(shape, lambda i: (l,) + (0,) * (len(shape) - 1))
    in_specs = _act_specs(x, tm, n_pt) + [
                pl.BlockSpec((1, 1, bc, d), lambda i: (l, 0, 0, 0)),
                pl.BlockSpec((1, 1, bc, d), lambda i: (l, 1, 0, 0)),
                pl.BlockSpec((1, 1, 1, d), lambda i: (l, 0, 0, 0)),
                full(1, 4, d), full(1, db, d), full(1, d, ld), full(1, d, la), full(1, d, lg)]
    args = [x.prompt, x.decode, mod, mod, norm_g4, mu_x, state_shift, decay_lora1, iclr_lora1, gate_lora1]
    row_out = lambda n, dt: (pl.BlockSpec((tm, n), lambda i: (i, 0)), jax.ShapeDtypeStruct((rows, n), dt))
    outs = [row_out(d, _BF16), row_out(ld, _F32), row_out(la, _F32), row_out(lg, _F32)]
    if has_v:
        lv = vres_lora1.shape[2]
        in_specs.append(pl.BlockSpec((1, d, lv), lambda i: (l - 1, 0, 0)))
        args.append(vres_lora1)
        outs.append(row_out(lv, _F32))
    outs.append((pl.BlockSpec((1, _SUBLANES, d), lambda i: (jnp.minimum(i // tps, bp - 1), 0, 0)),
                 jax.ShapeDtypeStruct((bp, _SUBLANES, d), _F32)))
    outs.append((pl.BlockSpec((db, d), lambda i: (0, 0)), jax.ShapeDtypeStruct((db, d), _F32)))
    return pl.pallas_call(
        functools.partial(_norm_lora_kernel, n_pt=n_pt, tps=tps, db=db, has_v=has_v),
        grid=(rows // tm,),
        in_specs=in_specs,
        out_specs=[o[0] for o in outs],
        out_shape=[o[1] for o in outs],
        scratch_shapes=[pltpu.VMEM((_SUBLANES, d), _F32), pltpu.VMEM((tm, d), _F32), pltpu.VMEM((tm, d), _F32)],
        compiler_params=_params("arbitrary"),
    )(*args)


def _mix_kernel(*refs, n_pt, tps, db, has_v):
    it = iter(refs)
    h_ref = next(it)
    w_refs = [next(it) for _ in range(6)]
    aw_ref, aa_ref, ag_ref = next(it), next(it), next(it)
    av_ref = next(it) if has_v else None
    d2_ref, i2_ref, g2_ref = next(it), next(it), next(it)
    v2_ref = next(it) if has_v else None
    w0_ref, a0_ref = next(it), next(it)
    v0_ref = next(it) if has_v else None
    mu_ref, kkw_ref, kaw_ref, cw_ref, hl_ref, cs_ref = (next(it) for _ in range(6))
    vf_ref = next(it) if has_v else None
    seq_prompt = [next(it) for _ in range(7)]
    seq_decode = [next(it) for _ in range(7)]
    oc_o = next(it)
    vf_o = None if has_v else next(it)
    cp_o, cso_o = next(it), next(it)
    wc_scr, pcarry, zcarry = (next(it) for _ in range(3))

    i = pl.program_id(1)
    tm = h_ref.shape[0]
    tn = oc_o.shape[1]

    @pl.when(i == 0)
    def _cache_weights():
        for g in range(6):
            wc_scr[g] = w_refs[g][0].astype(_BF16)

    def project(rs):
        h = h_ref[rs, :]
        return [jnp.dot(h, wc_scr[g], preferred_element_type=_F32) for g in range(6)]

    def tail(rs, p, pprev, bg, z, z1, z2, seq_out):
        r_o, lw_o, k_o, v_o, kk_o, kka_o, g_o = seq_out
        mu = mu_ref[0]
        r = p[0] + (pprev[0] - p[0]) * mu[0:1]
        k = p[1] + (pprev[1] - p[1]) * mu[1:2]
        v = p[2] + (pprev[2] - p[2]) * mu[2:3]

        zlog = w0_ref[0] + _dot(aw_ref[rs, :], d2_ref[0])
        softplus = jnp.maximum(-zlog, 0.0) + jnp.log(1.0 + jnp.exp(-jnp.abs(zlog)))
        lw_o[rs, :] = -jnp.exp(-softplus - 0.5)
        a = jax.nn.sigmoid(a0_ref[0] + _dot(aa_ref[rs, :], i2_ref[0]))
        g_o[rs, :] = _dot(ag_ref[rs, :], g2_ref[0])
        if has_v:
            nu = jax.nn.sigmoid(v0_ref[0] + _dot(av_ref[rs, :], v2_ref[0]))
            v = v + (vf_ref[rs, :] - v) * nu
        else:
            vf_o[rs, :] = v

        kk = k * kkw_ref[0]
        norm = jnp.sqrt(_head_sums(kk * kk, _segment_ones(tn)))
        kk = kk / jnp.maximum(norm, 1e-12)
        r_o[rs, :] = r
        k_o[rs, :] = k * (1.0 + (a - 1.0) * kaw_ref[0])
        v_o[rs, :] = v
        kk_o[rs, :] = kk
        kka_o[rs, :] = kk * a

        cw = cw_ref[0]
        zc = z2 * cw[0:1] + z1 * cw[1:2] + z * cw[2:3]
        oc_o[rs, :] = (bg * zc).astype(_BF16)

    @pl.when(i < n_pt)
    def _prompt():
        half = tm // 2
        halves = (slice(0, half), slice(half, tm))
        proj = [project(rs) for rs in halves]
        zs = [q[4] * q[5] for q in proj]
        for idx, rs in enumerate(halves):
            p, z = proj[idx][:3], zs[idx]
            if idx == 0:
                first = (i % tps) == 0
                p_carry = [pcarry[g] for g in range(3)]
                z_carry = zcarry[...]
            else:
                first = False
                p_carry = [q[half - _SUBLANES:, :] for q in proj[0][:3]]
                z_carry = zs[0][half - _SUBLANES:, :]
            pprev = [_shift_rows_prompt(p[g], p_carry[g], first, two=False) for g in range(3)]
            z1, z2 = _shift_rows_prompt(z, z_carry, first)
            tail(rs, p, pprev, proj[idx][3], z, z1, z2, seq_prompt)
        for g in range(3):
            pcarry[g] = proj[1][g][half - _SUBLANES:, :]
        zcarry[...] = zs[1][half - _SUBLANES:, :]
        cp_o[0] = zs[1][half - _SUBLANES:, :]

    @pl.when(i >= n_pt)
    def _decode():
        rs = slice(0, tm)
        proj = project(rs)
        p, z = proj[:3], proj[4] * proj[5]
        hl = hl_ref[0].astype(_BF16)
        pprev = [jnp.concatenate([jnp.dot(hl, wc_scr[g], preferred_element_type=_F32), p[g][:tm - db, :]], axis=0)
                 for g in range(3)]
        z1 = jnp.concatenate([cs_ref[0, 1], z[:tm - db, :]], axis=0)
        z2 = jnp.concatenate([cs_ref[0, 0], cs_ref[0, 1], z[:tm - 2 * db, :]], axis=0)
        cso_o[0] = z[tm - 2 * db:tm - db, :]
        cso_o[1] = z[tm - db:, :]
        tail(rs, p, pprev, proj[3], z, z1, z2, seq_decode)


def _mix(l, h, acts, vf_in, w_in, lora2, vecs, mu_rkv, k_k3, k_a3, conv_w, state_shift, state_conv_t,
         *, tm, tn, n_pt, tps, db, bp):
    rows, d = h.shape
    has_v = l > 0
    g_dim = mu_rkv.shape[2]
    nj = g_dim // tn
    aw, aa, ag, av = acts
    d2, i2, g2, v2 = lora2
    w0, a0, v0 = vecs
    row_in = lambda arr: pl.BlockSpec((tm, arr.shape[1]), lambda j, i: (i, 0))
    col3 = lambda arr, ll: pl.BlockSpec((1, arr.shape[1], tn), lambda j, i: (ll, 0, j))
    in_specs = [row_in(h)] + [pl.BlockSpec((1, d, tn), lambda j, i, g=g: (l, 0, g * nj + j)) for g in range(6)]
    args = [h] + [w_in] * 6
    in_specs += [row_in(aw), row_in(aa), row_in(ag)]
    args += [aw, aa, ag]
    if has_v:
        in_specs.append(row_in(av))
        args.append(av)
    in_specs += [col3(d2, l), col3(i2, l), col3(g2, l)]
    args += [d2, i2, g2]
    if has_v:
        in_specs.append(col3(v2, l - 1))
        args.append(v2)
    in_specs += [col3(w0, l), col3(a0, l)]
    args += [w0, a0]
    if has_v:
        in_specs.append(col3(v0, l - 1))
        args.append(v0)
    in_specs += [col3(mu_rkv, l), col3(k_k3, l), col3(k_a3, l), col3(conv_w, l),
                 pl.BlockSpec((1, db, d), lambda j, i: (l, 0, 0)),
                 pl.BlockSpec((1, 2, db, tn), lambda j, i: (l, 0, 0, j))]
    args += [mu_rkv, k_k3, k_a3, conv_w, state_shift, state_conv_t]
    if has_v:
        in_specs.append(pl.BlockSpec((tm, tn), lambda j, i: (i, j)))
        args.append(vf_in)
    tile = pl.BlockSpec((tm, tn), lambda j, i: (i, j))
    prompt_tile = pl.BlockSpec((tm, tn), lambda j, i: (jnp.minimum(i, n_pt - 1), j))
    decode_tile = pl.BlockSpec((tm, tn), lambda j, i: (0, j), pipeline_mode=pl.Buffered(1))
    outs = [(prompt_tile, jax.ShapeDtypeStruct((n_pt * tm, g_dim), _F32)) for _ in range(7)]
    outs += [(decode_tile, jax.ShapeDtypeStruct((tm, g_dim), _F32)) for _ in range(7)]
    outs.append((tile, jax.ShapeDtypeStruct((rows, g_dim), _BF16)))
    if not has_v:
        outs.append((tile, jax.ShapeDtypeStruct((rows, g_dim), _F32)))
    outs.append((pl.BlockSpec((1, _SUBLANES, tn), lambda j, i: (jnp.minimum(i // tps, bp - 1), 0, j)),
                 jax.ShapeDtypeStruct((bp, _SUBLANES, g_dim), _F32)))
    outs.append((pl.BlockSpec((2, db, tn), lambda j, i: (0, 0, j)), jax.ShapeDtypeStruct((2, db, g_dim), _F32)))
    return pl.pallas_call(
        functools.partial(_mix_kernel, n_pt=n_pt, tps=tps, db=db, has_v=has_v),
        grid=(nj, rows // tm),
        in_specs=in_specs,
        out_specs=[o[0] for o in outs],
        out_shape=[o[1] for o in outs],
        scratch_shapes=[pltpu.VMEM((6, d, tn), _BF16), pltpu.VMEM((3, _SUBLANES, tn), _F32),
                        pltpu.VMEM((_SUBLANES, tn), _F32)],
        compiler_params=_params("arbitrary", "arbitrary"),
    )(*args)


def _unit_lower_inverse_minus_identity(lmats, ri, ci, chunk, mul):
    same8 = (ri >> 3) == (ci >> 3)
    d8 = [jnp.where(same8, m, 0.0) for m in lmats]
    d2 = [mul(a, a) for a in d8]
    d3 = [mul(a, b) for a, b in zip(d8, d2)]
    d4 = [mul(b, b) for b in d2]
    x = [a + b + cc for a, b, cc in zip(d8, d2, d3)]
    xd4 = [mul(a, b) for a, b in zip(x, d4)]
    x = [a + b + cc for a, b, cc in zip(x, d4, xd4)]
    size = 16
    while size <= chunk:
        sh = size.bit_length() - 1
        level = ((ri >> sh) == (ci >> sh)) & ((ri >> (sh - 1)) != (ci >> (sh - 1)))
        e = [jnp.where(level, m, 0.0) for m in lmats]
        y = [b + mul(a, b) for a, b in zip(x, e)]
        x = [a + b + mul(b, a) for a, b in zip(x, y)]
        size *= 2
    return x


def _wkv_kernel(r_ref, lw_ref, k_ref, v_ref, kk_ref, kka_ref, g_ref, rk_ref, lnw_ref, lnb_ref,
                o_ref, so_ref, s_scr, *, n_chunks):
    ci_grid = pl.program_id(1)
    nb, c, g_dim = r_ref.shape
    npair = g_dim // _PAIR
    n2 = 2 * c

    lane = lax.broadcasted_iota(jnp.int32, (1, _PAIR), 1)
    m0 = (lane < _HEAD).astype(_F32)
    m1 = 1.0 - m0

    @pl.when(ci_grid == 0)
    def _init():
        s_scr[...] = jnp.zeros(s_scr.shape, _F32)

    ri = lax.broadcasted_iota(jnp.int32, (n2, n2), 0) & (c - 1)
    ci = lax.broadcasted_iota(jnp.int32, (n2, n2), 1) & (c - 1)
    strict = ri > ci
    incl = ri >= ci
    tri = (lax.broadcasted_iota(jnp.int32, (c, c), 0) >= lax.broadcasted_iota(jnp.int32, (c, c), 1)).astype(_BF16)
    seg_ones = _segment_ones(_SEG)

    def stack(x):
        return jnp.concatenate([x * m0, x * m1], axis=0)

    units = [(u, p) for u in range(nb) for p in range(npair)]
    sl = lambda p: slice(p * _PAIR, (p + 1) * _PAIR)

    seq = []
    for u in range(nb):
        lw = lw_ref[u]
        l_hi, l_mid, l_lo = _split3(lw)
        cl = (jnp.dot(tri, l_hi, preferred_element_type=_F32)
              + jnp.dot(tri, l_mid, preferred_element_type=_F32)
              + jnp.dot(tri, l_lo, preferred_element_type=_F32))
        cl_end = cl[c - 1:c, :]
        r, k, v, kk, kka = r_ref[u], k_ref[u], v_ref[u], kk_ref[u], kka_ref[u]
        e_neg = jnp.exp(-cl)
        e_tail = jnp.exp(cl_end - cl)
        seq.append(dict(r=r, k=k, v=v, p_end=jnp.exp(cl_end), rt=r * jnp.exp(cl), at=-(kk * jnp.exp(cl - lw)),
                        kt=k * e_neg, bt=kka * e_neg, kh=k * e_tail, bh=kka * e_tail))

    ar2 = [jnp.concatenate([stack(seq[u]["at"][:, sl(p)]), stack(seq[u]["rt"][:, sl(p)])], axis=0).astype(_BF16)
           for u, p in units]
    mb = [_dot_nt(a, stack(seq[u]["bt"][:, sl(p)])) for a, (u, p) in zip(ar2, units)]
    mk = [_dot_nt(a, stack(seq[u]["kt"][:, sl(p)])) for a, (u, p) in zip(ar2, units)]
    s_old = [s_scr[u, p] for u, p in units]
    uy0 = [_dot_nt(jnp.concatenate([seq[u]["at"][:, sl(p)], seq[u]["rt"][:, sl(p)]], axis=0), s)
           for (u, p), s in zip(units, s_old)]
    x_inv = _unit_lower_inverse_minus_identity([jnp.where(strict, m[:n2], 0.0) for m in mb], ri, ci, c, _dot)
    vs = [stack(seq[u]["v"][:, sl(p)]).astype(_BF16) for u, p in units]
    ws = [stack(q[:c]) + _dot(jnp.where(strict, m[:n2], 0.0), vv) for q, m, vv in zip(uy0, mk, vs)]
    us = [(w + _dot(xi, w)).astype(_BF16) for xi, w in zip(x_inv, ws)]
    ysd = [_dot(jnp.where(incl, m1_[n2:], 0.0), uu) + _dot(jnp.where(incl, m2_[n2:], 0.0), vv)
           for m1_, m2_, uu, vv in zip(mb, mk, us, vs)]
    ys = [q[c:] + d[:c] + d[c:] for q, d in zip(uy0, ysd)]
    for idx, (u, p) in enumerate(units):
        uv = jnp.concatenate([us[idx], vs[idx]], axis=0)
        bk = jnp.concatenate([stack(seq[u]["bh"][:, sl(p)]), stack(seq[u]["kh"][:, sl(p)])], axis=0)
        s_scr[u, p] = s_old[idx] * seq[u]["p_end"][:, sl(p)] + _dot_tn(uv, bk)
    for u in range(nb):
        y = jnp.concatenate(ys[u * npair:(u + 1) * npair], axis=1)
        mean = _head_sums(y, seg_ones) * (1.0 / _HEAD)
        dy = y - mean
        var = _head_sums(dy * dy, seg_ones) * (1.0 / _HEAD)
        yn = dy * lax.rsqrt(var + _GN_EPS) * lnw_ref[...] + lnb_ref[...]
        bonus = _head_sums(seq[u]["r"] * seq[u]["k"] * rk_ref[...], seg_ones) * seq[u]["v"]
        o_ref[u] = ((yn + bonus) * g_ref[u]).astype(o_ref.dtype)

    @pl.when(ci_grid == n_chunks - 1)
    def _final():
        for u, p in units:
            s = s_scr[u, p]
            so_ref[u, 2 * p] = s[:_HEAD, :_HEAD]
            so_ref[u, 2 * p + 1] = s[_HEAD:, _HEAD:]


def _wkv(l, seqs, chunk, nb, r_k3, ln_w3, ln_b3):
    n_seq, t_len, g_dim = seqs[0].shape
    n_heads = g_dim // _HEAD
    n_chunks = t_len // chunk
    blk = pl.BlockSpec((nb, chunk, g_dim), lambda bi, ci: (bi, ci, 0))
    vec = pl.BlockSpec((None, 1, g_dim), lambda bi, ci: (l, 0, 0))
    return pl.pallas_call(
        functools.partial(_wkv_kernel, n_chunks=n_chunks),
        grid=(n_seq // nb, n_chunks),
        in_specs=[blk] * 7 + [vec] * 3,
        out_specs=[blk, pl.BlockSpec((nb, n_heads, _HEAD, _HEAD), lambda bi, ci: (bi, 0, 0, 0))],
        out_shape=[jax.ShapeDtypeStruct((n_seq, t_len, g_dim), _BF16),
                   jax.ShapeDtypeStruct((n_seq, n_heads, _HEAD, _HEAD), _F32)],
        scratch_shapes=[pltpu.VMEM((nb, g_dim // _PAIR, _PAIR, _PAIR), _F32)],
        compiler_params=_params("arbitrary", "arbitrary"),
    )(*seqs, r_k3, ln_w3, ln_b3)


def _wkv_decode_kernel(r_ref, lw_ref, k_ref, v_ref, kk_ref, kka_ref, g_ref, rk_ref, lnw_ref, lnb_ref, s0_ref,
                       o_ref, so_ref, *, ts, nb):
    g_dim = r_ref.shape[2]
    npair = g_dim // _PAIR
    n = ts * nb
    n2 = 2 * n
    nb_bits = nb.bit_length() - 1

    lane = lax.broadcasted_iota(jnp.int32, (1, _PAIR), 1)
    m0 = (lane < _HEAD).astype(_F32)
    m1 = 1.0 - m0

    def stack(x):
        return jnp.concatenate([x * m0, x * m1], axis=0)

    def rows_tb(ref):
        return jnp.concatenate([ref[t] for t in range(ts)], axis=0)

    lw_t = [lw_ref[t] for t in range(ts)]
    cl_t = [lw_t[0]]
    for t in range(1, ts):
        cl_t.append(cl_t[-1] + lw_t[t])
    cl = jnp.concatenate(cl_t, axis=0)
    lw = jnp.concatenate(lw_t, axis=0)
    cl_end_b = cl_t[-1]
    cl_end = _tile_rows(cl_end_b, ts)
    p_end = jnp.exp(cl_end_b)
    r, k, v, kk, kka = (rows_tb(ref) for ref in (r_ref, k_ref, v_ref, kk_ref, kka_ref))
    e_neg = jnp.exp(-cl)
    e_tail = jnp.exp(cl_end - cl)
    rt = r * jnp.exp(cl)
    at = -(kk * jnp.exp(cl - lw))
    kt = k * e_neg
    bt = kka * e_neg
    kh = k * e_tail
    bh = kka * e_tail

    ri = lax.broadcasted_iota(jnp.int32, (n2, n2), 0)
    ci = lax.broadcasted_iota(jnp.int32, (n2, n2), 1)
    same_seq = (ri & (nb - 1)) == (ci & (nb - 1))
    t_r = (ri & (n - 1)) >> nb_bits
    t_c = (ci & (n - 1)) >> nb_bits
    strict = same_seq & (t_r > t_c)
    incl = same_seq & (t_r >= t_c)
    seq_of_row2 = lax.broadcasted_iota(jnp.int32, (n2, 1), 0) & (nb - 1)
    seq_of_row4 = lax.broadcasted_iota(jnp.int32, (2 * n2, 1), 0) & (nb - 1)
    zero = jnp.zeros((_HEAD, _HEAD), _F32)

    pairs = range(npair)
    sls = [slice(p * _PAIR, (p + 1) * _PAIR) for p in pairs]
    ar2 = [jnp.concatenate([stack(at[:, sl]), stack(rt[:, sl])], axis=0).astype(_BF16) for sl in sls]
    mb = [_dot_nt(a, stack(bt[:, sl])) for a, sl in zip(ar2, sls)]
    mk = [_dot_nt(a, stack(kt[:, sl])) for a, sl in zip(ar2, sls)]
    lab = [jnp.where(strict, m[:n2], 0.0) for m in mb]
    d2 = [_dot(a, a) for a in lab]
    d3 = [_dot(a, b) for a, b in zip(lab, d2)]
    x_inv = [a + b + cc for a, b, cc in zip(lab, d2, d3)]

    def block_diag(b, p):
        top = jnp.concatenate([s0_ref[b, 2 * p], zero], axis=1)
        bot = jnp.concatenate([zero, s0_ref[b, 2 * p + 1]], axis=1)
        return jnp.concatenate([top, bot], axis=0)

    ys = []
    for p, sl in zip(pairs, sls):
        s_b = [block_diag(b, p) for b in range(nb)]
        ar = jnp.concatenate([at[:, sl], rt[:, sl]], axis=0)
        ar_cat = jnp.concatenate([jnp.where(seq_of_row2 == b, ar, 0.0).astype(_BF16) for b in range(nb)], axis=1)
        s_cat = jnp.concatenate([s.astype(_BF16) for s in s_b], axis=1)
        uy0 = _dot_nt(ar_cat, s_cat)
        vs = stack(v[:, sl]).astype(_BF16)
        ws = stack(uy0[:n]) + _dot(jnp.where(strict, mk[p][:n2], 0.0), vs)
        us = (ws + _dot(x_inv[p], ws)).astype(_BF16)
        ysd = _dot(jnp.where(incl, mb[p][n2:], 0.0), us) + _dot(jnp.where(incl, mk[p][n2:], 0.0), vs)
        ys.append(uy0[n:] + ysd[:n] + ysd[n:])
        uv = jnp.concatenate([us, vs], axis=0)
        uv_cat = jnp.concatenate([jnp.where(seq_of_row4 == b, uv, jnp.zeros_like(uv)) for b in range(nb)], axis=1)
        bk = jnp.concatenate([stack(bh[:, sl]), stack(kh[:, sl])], axis=0)
        upd = _dot_tn(uv_cat, bk)
        for b in range(nb):
            s_new = (s_b[b] * p_end[b:b + 1, sl] + upd[b * _PAIR:(b + 1) * _PAIR, :])
            so_ref[b, 2 * p] = s_new[:_HEAD, :_HEAD]
            so_ref[b, 2 * p + 1] = s_new[_HEAD:, _HEAD:]

    y = jnp.concatenate(ys, axis=1)
    seg_ones = _segment_ones(_SEG)
    mean = _head_sums(y, seg_ones) * (1.0 / _HEAD)
    dy = y - mean
    var = _head_sums(dy * dy, seg_ones) * (1.0 / _HEAD)
    yn = dy * lax.rsqrt(var + _GN_EPS) * lnw_ref[...] + lnb_ref[...]
    bonus = _head_sums(r * k * rk_ref[...], seg_ones) * v
    out = ((yn + bonus) * rows_tb(g_ref)).astype(o_ref.dtype)
    for t in range(ts):
        o_ref[t] = out[t * nb:(t + 1) * nb, :]


def _wkv_decode(l, seqs, state0, r_k3, ln_w3, ln_b3, *, ts, db, nb):
    g_dim = seqs[0].shape[2]
    n_heads = g_dim // _HEAD
    blk = pl.BlockSpec((ts, nb, g_dim), lambda bi: (0, bi, 0))
    vec = pl.BlockSpec((None, 1, g_dim), lambda bi: (l, 0, 0))
    st = pl.BlockSpec((nb, n_heads, _HEAD, _HEAD), lambda bi: (bi, 0, 0, 0))
    return pl.pallas_call(
        functools.partial(_wkv_decode_kernel, ts=ts, nb=nb),
        grid=(db // nb,),
        in_specs=[blk] * 7 + [vec] * 3 + [st],
        out_specs=[pl.BlockSpec((ts, nb, g_dim), lambda bi: (0, bi, 0)), st],
        out_shape=[jax.ShapeDtypeStruct((ts, db, g_dim), _BF16),
                   jax.ShapeDtypeStruct((db, n_heads, _HEAD, _HEAD), _F32)],
        compiler_params=_params("arbitrary"),
    )(*seqs, r_k3, ln_w3, ln_b3, state0)


def _cast_kernel(w_ref, o_ref):
    o_ref[...] = w_ref[...].astype(o_ref.dtype)


def _cast_bf16(w, rows_per_block):
    depth, k, n = w.shape
    spec = pl.BlockSpec((1, rows_per_block, n), lambda l, i: (l, i, 0))
    return pl.pallas_call(
        _cast_kernel, grid=(depth, k // rows_per_block), in_specs=[spec], out_specs=spec,
        out_shape=jax.ShapeDtypeStruct(w.shape, _BF16),
        compiler_params=_params("arbitrary", "arbitrary"),
    )(w)


def _outproj_kernel(op_ref, os_ref, oc_ref, xp_ref, xs_ref, ga_ref, sh_ref, sc_ref, g_ref, w_ref,
                    x1_ref, h2_ref, *, n_pt, tps, db):
    i = pl.program_id(0)
    tm = xp_ref.shape[0]
    ts = tm // db
    g_dim = op_ref.shape[1]

    def rows_out(rs, o_ref, x_ref, ga, sc, sh):
        acc = (jnp.dot(o_ref[rs, :], w_ref[0, :g_dim, :], preferred_element_type=_F32)
               + jnp.dot(oc_ref[rs, :], w_ref[0, g_dim:, :], preferred_element_type=_F32))
        x1 = x_ref[rs, :] + ga * acc
        x1_ref[rs, :] = x1
        xn = x1 * lax.rsqrt(jnp.mean(x1 * x1, axis=-1, keepdims=True) + _RMS_EPS) * g_ref[0, 0]
        h2_ref[rs, :] = (xn * (1.0 + sc) + sh).astype(_BF16)

    half = tm // 2
    halves = (slice(0, half), slice(half, tm))

    @pl.when(i < n_pt)
    def _prompt():
        ga, sc, sh = (_mod_row(ref, i, tps, db) for ref in (ga_ref, sc_ref, sh_ref))
        for rs in halves:
            rows_out(rs, op_ref, xp_ref, ga, sc, sh)

    @pl.when(i >= n_pt)
    def _decode():
        ga, sc, sh = (_mod_tile(ref, db, ts) for ref in (ga_ref, sc_ref, sh_ref))
        for rs in halves:
            rows_out(rs, os_ref, xs_ref, ga[rs, :], sc[rs, :], sh[rs, :])


def _outproj(l, o_p, o_s, o_conv, x, mod, norm_g4, w_out, *, tm, n_pt, tps, db):
    d = x.prompt.shape[1]
    rows = (n_pt + 1) * tm
    g_dim = o_p.shape[1]
    bc = mod.shape[2]
    modspec = lambda comp: pl.BlockSpec((1, 1, bc, d), lambda i: (l, comp, 0, 0))
    return pl.pallas_call(
        functools.partial(_outproj_kernel, n_pt=n_pt, tps=tps, db=db),
        grid=(rows // tm,),
        in_specs=[pl.BlockSpec((tm, g_dim), lambda i: (jnp.minimum(i, n_pt - 1), 0)),
                  pl.BlockSpec((tm, g_dim), lambda i: (0, 0)),
                  pl.BlockSpec((tm, o_conv.shape[1]), lambda i: (i, 0))] + _act_specs(x, tm, n_pt) + [
                  modspec(2), modspec(3), modspec(4),
                  pl.BlockSpec((1, 1, 1, d), lambda i: (l, 1, 0, 0)),
                  pl.BlockSpec((1, d, d), lambda i: (l, 0, 0), pipeline_mode=pl.Buffered(1))],
        out_specs=[pl.BlockSpec((tm, d), lambda i: (i, 0)), pl.BlockSpec((tm, d), lambda i: (i, 0))],
        out_shape=[jax.ShapeDtypeStruct((rows, d), _F32), jax.ShapeDtypeStruct((rows, d), _BF16)],
        compiler_params=_params("arbitrary"),
    )(o_p, o_s, o_conv, x.prompt, x.decode, mod, mod, mod, norm_g4, w_out)


def _ffn_kernel(*refs, n_pt, tps, db, nj, final_norm):
    it = iter(refs)
    (h2_ref, wa_ref, wb_ref, cwa_ref, cwb_ref, wd_ref, x1_ref, ga_ref, sfa_ref, sfb_ref) = (next(it) for _ in range(10))
    fg_ref = next(it) if final_norm else None
    x2_ref = next(it)
    x2s_ref = next(it) if final_norm else x2_ref
    fpa_o, fpb_o, fsa_o, fsb_o = (next(it) for _ in range(4))
    acc_scr, ca_scr, cb_scr = (next(it) for _ in range(3))
    i = pl.program_id(0)
    j = pl.program_id(1)
    tm = h2_ref.shape[0]
    ts = tm // db

    @pl.when(j == 0)
    def _zero():
        acc_scr[...] = jnp.zeros(acc_scr.shape, _F32)

    def prompt_body():
        half = tm // 2
        halves = (slice(0, half), slice(half, tm))
        us = [[jnp.dot(h2_ref[rs, :], w_ref[0], preferred_element_type=_F32) for w_ref in (wa_ref, wb_ref)]
              for rs in halves]
        wd = wd_ref[0].astype(_BF16)
        for r, rs in enumerate(halves):
            conv = []
            for idx, (cw_ref, c_scr, fp_o) in enumerate(((cwa_ref, ca_scr, fpa_o), (cwb_ref, cb_scr, fpb_o))):
                u = us[r][idx]
                if r == 0:
                    u1, u2 = _shift_rows_prompt(u, c_scr[j], (i % tps) == 0)
                else:
                    u1, u2 = _shift_rows_prompt(u, us[0][idx][half - _SUBLANES:, :], False)
                    c_scr[j] = u[half - _SUBLANES:, :]
                    fp_o[0] = u[half - _SUBLANES:, :]
                cw = cw_ref[0]
                conv.append(u2 * cw[0:1] + u1 * cw[1:2] + u * cw[2:3])
            gact = ((conv[0] * jax.nn.sigmoid(conv[0])) * conv[1]).astype(_BF16)
            acc_scr[rs, :] += jnp.dot(gact, wd, preferred_element_type=_F32)

    def decode_body():
        h2 = h2_ref[...]
        conv = []
        for w_ref, cw_ref, sf, fp_o, fs_o in ((wa_ref, cwa_ref, sfa_ref, fpa_o, fsa_o),
                                              (wb_ref, cwb_ref, sfb_ref, fpb_o, fsb_o)):
            u = jnp.dot(h2, w_ref[0], preferred_element_type=_F32)
            u1 = jnp.concatenate([sf[0, 1], u[:tm - db, :]], axis=0)
            u2 = jnp.concatenate([sf[0, 0], sf[0, 1], u[:tm - 2 * db, :]], axis=0)
            fp_o[0] = jnp.zeros(fp_o.shape[1:], _F32)
            fs_o[0] = u[tm - 2 * db:tm - db, :]
            fs_o[1] = u[tm - db:, :]
            cw = cw_ref[0]
            conv.append(u2 * cw[0:1] + u1 * cw[1:2] + u * cw[2:3])
        gact = ((conv[0] * jax.nn.sigmoid(conv[0])) * conv[1]).astype(_BF16)
        acc_scr[...] += jnp.dot(gact, wd_ref[0].astype(_BF16), preferred_element_type=_F32)

    pl.when(i < n_pt)(prompt_body)
    pl.when(i >= n_pt)(decode_body)

    def finish(ga, out_ref):
        x2 = x1_ref[...] + ga * acc_scr[...]
        if final_norm:
            x2 = x2 * lax.rsqrt(jnp.mean(x2 * x2, axis=-1, keepdims=True) + _RMS_EPS) * fg_ref[...]
        out_ref[...] = x2

    @pl.when((j == nj - 1) & (i < n_pt))
    def _finish_prompt():
        finish(_mod_row(ga_ref, i, tps, db), x2_ref)

    @pl.when((j == nj - 1) & (i >= n_pt))
    def _finish_decode():
        finish(_mod_tile(ga_ref, db, ts), x2s_ref)


def _ffn(l, h2, x1, mod, ffn_up, ffn_conv, ffn_down, state_ffn_t, final_g, *, tm, tn, n_pt, tps, db, bp):
    rows, d = x1.shape
    f = ffn_down.shape[1]
    nj = f // tn
    bc = mod.shape[2]
    final_norm = final_g is not None
    in_specs = [pl.BlockSpec((tm, d), lambda i, j: (i, 0)),
                pl.BlockSpec((1, d, tn), lambda i, j: (l, 0, j)),
                pl.BlockSpec((1, d, tn), lambda i, j: (l, 0, nj + j)),
                pl.BlockSpec((1, 3, tn), lambda i, j: (l, 0, j)),
                pl.BlockSpec((1, 3, tn), lambda i, j: (l, 0, nj + j)),
                pl.BlockSpec((1, tn, d), lambda i, j: (l, j, 0)),
                pl.BlockSpec((tm, d), lambda i, j: (i, 0)),
                pl.BlockSpec((1, 1, bc, d), lambda i, j: (l, 5, 0, 0)),
                pl.BlockSpec((1, 2, db, tn), lambda i, j: (l, 0, 0, jnp.where(i >= n_pt, j, 0))),
                pl.BlockSpec((1, 2, db, tn), lambda i, j: (l, 0, 0, nj + jnp.where(i >= n_pt, j, 0)))]
    args = [h2, ffn_up, ffn_up, ffn_conv, ffn_conv, ffn_down, x1, mod, state_ffn_t, state_ffn_t]
    if final_norm:
        in_specs.append(pl.BlockSpec((1, d), lambda i, j: (0, 0)))
        args.append(final_g)
    n_tiles = rows // tm
    pstate = pl.BlockSpec((1, _SUBLANES, tn), lambda i, j: (i, 0, j))
    sstate = pl.BlockSpec((2, db, tn), lambda i, j: (0, 0, jnp.where(i >= n_pt, j, 0)))
    if final_norm:
        x_specs = [pl.BlockSpec((tm, d), lambda i, j: (jnp.minimum(i, n_pt - 1), 0)),
                   pl.BlockSpec((tm, d), lambda i, j: (0, 0))]
        x_shapes = [jax.ShapeDtypeStruct((n_pt * tm, d), _F32), jax.ShapeDtypeStruct((tm, d), _F32)]
    else:
        x_specs = [pl.BlockSpec((tm, d), lambda i, j: (i, 0))]
        x_shapes = [jax.ShapeDtypeStruct((rows, d), _F32)]
    return pl.pallas_call(
        functools.partial(_ffn_kernel, n_pt=n_pt, tps=tps, db=db, nj=nj, final_norm=final_norm),
        grid=(rows // tm, nj),
        in_specs=in_specs,
        out_specs=x_specs + [pstate, pstate, sstate, sstate],
        out_shape=x_shapes + [
                   jax.ShapeDtypeStruct((n_tiles, _SUBLANES, f), _F32),
                   jax.ShapeDtypeStruct((n_tiles, _SUBLANES, f), _F32),
                   jax.ShapeDtypeStruct((2, db, f), _F32), jax.ShapeDtypeStruct((2, db, f), _F32)],
        scratch_shapes=[pltpu.VMEM((tm, d), _F32), pltpu.VMEM((nj, _SUBLANES, tn), _F32),
                        pltpu.VMEM((nj, _SUBLANES, tn), _F32)],
        compiler_params=_params("arbitrary", "arbitrary"),
    )(*args)


def _forward(x_prompt, x_sample, c_prompt, c_sample, state_wkv, state_shift, state_conv, state_ffn,
             ada_w, ada_b, norm_g, final_norm_g, w_in, mu_x, mu_rkv, decay_w0, decay_lora1, decay_lora2,
             iclr_a0, iclr_lora1, iclr_lora2, gate_lora1, gate_lora2, vres_v0, vres_lora1, vres_lora2,
             k_k, k_a, r_k, ln_x_w, ln_x_b, conv_w, w_out, ffn_up, ffn_conv, ffn_down,
             *, chunk=_WKV_CHUNK, mix_tn=_MIX_COLS, ffn_tn=_FFN_COLS, wkv_decode_nb=_WKV_DECODE_SEQS,
             wkv_prompt_nb=_WKV_PROMPT_SEQS):
    bp, t_len, d = x_prompt.shape
    db, ts, _ = x_sample.shape
    depth = ada_w.shape[0]
    g_dim = mu_rkv.shape[2]
    tm = ts * db
    assert t_len % tm == 0 and t_len % chunk == 0 and db % _SUBLANES == 0 and bp % wkv_prompt_nb == 0
    assert 2 <= ts <= 4 and ts & (ts - 1) == 0 and db % wkv_decode_nb == 0 and wkv_decode_nb & (wkv_decode_nb - 1) == 0
    tps = t_len // tm
    n_pt = bp * tps
    n_prompt_rows = bp * t_len

    x = _Act(x_prompt.reshape(n_prompt_rows, d), x_sample.transpose(1, 0, 2).reshape(tm, d), 0)
    pad = (-(db + bp)) % _SUBLANES
    c_all = jnp.concatenate([c_sample, c_prompt, jnp.zeros((pad, d), _F32)], axis=0)
    mod = _adaln_mod(c_all, ada_w, ada_b)

    norm_g4 = norm_g.reshape(depth, 2, 1, d)
    vec3 = lambda a: a.reshape(a.shape[0], 1, a.shape[1])
    w0_3, a0_3, v0_3, kk_3, ka_3 = vec3(decay_w0), vec3(iclr_a0), vec3(vres_v0), vec3(k_k), vec3(k_a)
    rk_3, lnw_3, lnb_3 = r_k.reshape(depth, 1, g_dim), vec3(ln_x_w), vec3(ln_x_b)
    state_conv_t = state_conv.transpose(0, 2, 1, 3)
    state_ffn_t = state_ffn.transpose(0, 2, 1, 3)
    tiles = dict(tm=tm, n_pt=n_pt, tps=tps, db=db)
    w_out = _cast_bf16(w_out, 512)
    ffn_up = _cast_bf16(ffn_up, 128)

    new = {k: [] for k in ("wkv_p", "shift_p", "conv_p", "ffn_p", "wkv_s", "shift_s", "conv_s", "ffn_s")}
    v_first = None
    for l in range(depth):
        outs = _norm_lora(l, x, mod, norm_g4, mu_x, state_shift, decay_lora1, iclr_lora1, gate_lora1,
                          vres_lora1, bp=bp, **tiles)
        if l > 0:
            h, aw, aa, ag, av, hp_last, hs_last = outs
        else:
            h, aw, aa, ag, hp_last, hs_last = outs
            av = None
        mouts = _mix(l, h, (aw, aa, ag, av), v_first, w_in,
                     (decay_lora2, iclr_lora2, gate_lora2, vres_lora2), (w0_3, a0_3, v0_3),
                     mu_rkv, kk_3, ka_3, conv_w, state_shift, state_conv_t, tn=mix_tn, bp=bp, **tiles)
        seqs_p, seqs_s, rest = mouts[:7], mouts[7:14], mouts[14:]
        if l > 0:
            o_conv, conv_p8, conv_s = rest
        else:
            o_conv, v_first, conv_p8, conv_s = rest

        o_p, wkv_p = _wkv(l, tuple(a.reshape(bp, t_len, g_dim) for a in seqs_p), chunk, wkv_prompt_nb,
                          rk_3, lnw_3, lnb_3)
        o_p = o_p.reshape(n_prompt_rows, g_dim)
        o_s, wkv_s = _wkv_decode(l, tuple(a.reshape(ts, db, g_dim) for a in seqs_s), state_wkv[l],
                                 rk_3, lnw_3, lnb_3, ts=ts, db=db, nb=wkv_decode_nb)
        o_s = o_s.reshape(tm, g_dim)

        x1, h2 = _outproj(l, o_p, o_s, o_conv, x, mod, norm_g4, w_out, **tiles)
        final_g = final_norm_g.reshape(1, d) if l == depth - 1 else None
        *x_out, fpa, fpb, fsa, fsb = _ffn(l, h2, x1, mod, ffn_up, ffn_conv, ffn_down, state_ffn_t, final_g,
                                          tn=ffn_tn, bp=bp, **tiles)
        x = _Act(x_out[0], x_out[0], n_pt) if len(x_out) == 1 else _Act(x_out[0], x_out[1], 0)

        new["wkv_p"].append(wkv_p)
        new["wkv_s"].append(wkv_s)
        new["shift_p"].append(hp_last[:, _SUBLANES - 1])
        new["shift_s"].append(hs_last)
        new["conv_p"].append(conv_p8[:, _SUBLANES - 2:])
        new["conv_s"].append(conv_s.transpose(1, 0, 2))
        last_tiles = slice(tps - 1, n_pt, tps)
        new["ffn_p"].append(jnp.concatenate([fpa[last_tiles, _SUBLANES - 2:], fpb[last_tiles, _SUBLANES - 2:]],
                                            axis=-1))
        new["ffn_s"].append(jnp.concatenate([fsa, fsb], axis=-1).transpose(1, 0, 2))

    y_prompt = x.prompt.reshape(bp, t_len, d)
    y_sample = x.decode.reshape(ts, db, d).transpose(1, 0, 2)
    st = {k: jnp.stack(vs) for k, vs in new.items()}
    return (y_prompt, y_sample, st["wkv_p"], st["shift_p"], st["conv_p"], st["ffn_p"],
            st["wkv_s"], st["shift_s"], st["conv_s"], st["ffn_s"])


def kernel(x_prompt, x_sample, c_prompt, c_sample, state_wkv, state_shift, state_conv, state_ffn, ada_w, ada_b, norm_g, final_norm_g, w_in, mu_x, mu_rkv, decay_w0, decay_lora1, decay_lora2, iclr_a0, iclr_lora1, iclr_lora2, gate_lora1, gate_lora2, vres_v0, vres_lora1, vres_lora2, k_k, k_a, r_k, ln_x_w, ln_x_b, conv_w, w_out, ffn_up, ffn_conv, ffn_down):
    return _forward(x_prompt, x_sample, c_prompt, c_sample, state_wkv, state_shift, state_conv, state_ffn,
                    ada_w, ada_b, norm_g, final_norm_g, w_in, mu_x, mu_rkv, decay_w0, decay_lora1, decay_lora2,
                    iclr_a0, iclr_lora1, iclr_lora2, gate_lora1, gate_lora2, vres_v0, vres_lora1, vres_lora2,
                    k_k, k_a, r_k, ln_x_w, ln_x_b, conv_w, w_out, ffn_up, ffn_conv, ffn_down)
```

```python
import functools
from typing import NamedTuple

import jax
import jax.numpy as jnp
from jax import lax
from jax.experimental import pallas as pl
from jax.experimental.pallas import tpu as pltpu

_F32 = jnp.float32
_BF16 = jnp.bfloat16

_HEAD = 64
_PAIR = 2 * _HEAD
_SEG = 256
_RMS_EPS = 1e-6
_GN_EPS = 64e-5
_VMEM_LIMIT_BYTES = 56 * 1024 * 1024
_SUBLANES = 8

_WKV_CHUNK = 64
_WKV_PROMPT_SEQS = 2
_WKV_DECODE_SEQS = 16
_MIX_COLS = 256
_FFN_COLS = 512


def _params(*sem):
    return pltpu.CompilerParams(dimension_semantics=sem, vmem_limit_bytes=_VMEM_LIMIT_BYTES)


def _dot(a, b):
    return jnp.dot(a.astype(_BF16), b.astype(_BF16), preferred_element_type=_F32)


def _dot_nt(a, b):
    return lax.dot_general(a.astype(_BF16), b.astype(_BF16), (((1,), (1,)), ((), ())),
                           preferred_element_type=_F32)


def _dot_tn(a, b):
    return lax.dot_general(a.astype(_BF16), b.astype(_BF16), (((0,), (0,)), ((), ())),
                           preferred_element_type=_F32)


def _split2(x):
    hi = x.astype(_BF16)
    lo = (x - hi.astype(_F32)).astype(_BF16)
    return hi, lo


def _split3(x):
    hi = x.astype(_BF16)
    r1 = x - hi.astype(_F32)
    mid = r1.astype(_BF16)
    lo = (r1 - mid.astype(_F32)).astype(_BF16)
    return hi, mid, lo


def _dot_exact_rhs(x, m_bf16):
    hi, lo = _split2(x)
    return (jnp.dot(hi, m_bf16, preferred_element_type=_F32)
            + jnp.dot(lo, m_bf16, preferred_element_type=_F32))


def _segment_ones(n):
    r = lax.broadcasted_iota(jnp.int32, (n, n), 0) // _HEAD
    c = lax.broadcasted_iota(jnp.int32, (n, n), 1) // _HEAD
    return (r == c).astype(_BF16)


def _head_sums(x, seg_ones):
    n = seg_ones.shape[0]
    parts = [_dot_exact_rhs(x[:, q * n:(q + 1) * n], seg_ones) for q in range(x.shape[1] // n)]
    return parts[0] if len(parts) == 1 else jnp.concatenate(parts, axis=1)


def _tile_rows(m, reps):
    return m if reps == 1 else jnp.concatenate([m] * reps, axis=0)


def _shift_rows_prompt(x, carry8, first, two=True):
    r8 = lax.broadcasted_iota(jnp.int32, (_SUBLANES, 1), 0)
    l1 = jnp.where(first, 0.0, carry8[_SUBLANES - 1:_SUBLANES, :])
    x1 = pltpu.roll(x, 1, 0)
    x1 = jnp.concatenate([jnp.where(r8 == 0, l1, x1[:_SUBLANES, :]), x1[_SUBLANES:, :]], axis=0)
    if not two:
        return x1
    l2 = jnp.where(first, 0.0, carry8[_SUBLANES - 2:_SUBLANES - 1, :])
    x2 = pltpu.roll(x, 2, 0)
    top2 = jnp.where(r8 == 0, l2, jnp.where(r8 == 1, l1, x2[:_SUBLANES, :]))
    return x1, jnp.concatenate([top2, x2[_SUBLANES:, :]], axis=0)


class _Act(NamedTuple):
    prompt: jax.Array
    decode: jax.Array
    decode_block: int


def _act_specs(x, tm, n_pt):
    d = x.prompt.shape[1]
    return [pl.BlockSpec((tm, d), lambda i, *_: (jnp.minimum(i, n_pt - 1), 0)),
            pl.BlockSpec((tm, d), lambda i, *_: (x.decode_block, 0))]


def _mod_row(ref, i, tps, db):
    return ref[0, 0, pl.ds(db + i // tps, 1), :]


def _mod_tile(ref, db, ts):
    return _tile_rows(ref[0, 0, 0:db, :], ts)


def _mod_kernel(c_ref, w_ref, b_ref, o_ref):
    c = c_ref[...]
    s_hi, s_lo = _split2(c * jax.nn.sigmoid(c))
    w = w_ref[0].astype(_BF16)
    acc = jnp.dot(s_hi, w, preferred_element_type=_F32) + jnp.dot(s_lo, w, preferred_element_type=_F32)
    o_ref[0, 0] = acc + b_ref[0]


def _adaln_mod(c_all, ada_w, ada_b):
    depth, d, six_d = ada_w.shape
    bc = c_all.shape[0]
    tn = 1024
    per = d // tn
    return pl.pallas_call(
        _mod_kernel,
        grid=(depth, six_d // tn),
        in_specs=[pl.BlockSpec((bc, d), lambda l, n: (0, 0)),
                  pl.BlockSpec((1, d, tn), lambda l, n: (l, 0, n)),
                  pl.BlockSpec((1, 1, tn), lambda l, n: (l, 0, n))],
        out_specs=pl.BlockSpec((1, 1, bc, tn), lambda l, n: (l, n // per, 0, n % per)),
        out_shape=jax.ShapeDtypeStruct((depth, 6, bc, d), _F32),
        compiler_params=_params("arbitrary", "arbitrary"),
    )(c_all, ada_w, ada_b.reshape(depth, 1, six_d))


def _norm_lora_kernel(*refs, n_pt, tps, db, has_v):
    n_lora = 4 if has_v else 3
    it = iter(refs)
    xp_ref, xs_ref, sh_ref, sc_ref, g_ref, mu_ref, hl_ref = (next(it) for _ in range(7))
    w_refs = [next(it) for _ in range(n_lora)]
    h_ref = next(it)
    act_refs = [next(it) for _ in range(n_lora)]
    hp_ref, hs_ref, carry_ref, hcur_ref, hprev_ref = (next(it) for _ in range(5))
    i = pl.program_id(0)

    tm = xp_ref.shape[0]
    ts = tm // db
    x = jnp.where(i < n_pt, xp_ref[...], xs_ref[...])
    xn = x * lax.rsqrt(jnp.mean(x * x, axis=-1, keepdims=True) + _RMS_EPS) * g_ref[0, 0]

    @pl.when(i < n_pt)
    def _prompt():
        h = xn * (1.0 + _mod_row(sc_ref, i, tps, db)) + _mod_row(sh_ref, i, tps, db)
        first = (i % tps) == 0
        hprev_ref[...] = _shift_rows_prompt(h, carry_ref[...], first, two=False)
        hcur_ref[...] = h
        carry_ref[...] = h[tm - _SUBLANES:, :]
        hp_ref[0] = h[tm - _SUBLANES:, :]

    @pl.when(i >= n_pt)
    def _decode():
        h = xn * (1.0 + _mod_tile(sc_ref, db, ts)) + _mod_tile(sh_ref, db, ts)
        hprev_ref[...] = jnp.concatenate([hl_ref[0], h[:tm - db, :]], axis=0)
        hcur_ref[...] = h
        hs_ref[...] = h[tm - db:, :]

    h = hcur_ref[...]
    xx = hprev_ref[...] - h
    mu = mu_ref[0]
    h_ref[...] = h.astype(_BF16)
    pre = [_dot(h + xx * mu[g:g + 1], w_refs[g][0]) for g in range(n_lora)]
    act_refs[0][...] = jnp.tanh(pre[0])
    act_refs[1][...] = pre[1]
    act_refs[2][...] = jax.nn.sigmoid(pre[2])
    if has_v:
        act_refs[3][...] = pre[3]


def _norm_lora(l, x, mod, norm_g4, mu_x, state_shift, decay_lora1, iclr_lora1, gate_lora1, vres_lora1,
               *, tm, n_pt, tps, db, bp):
    d = x.prompt.shape[1]
    rows = (n_pt + 1) * tm
    has_v = l > 0
    bc = mod.shape[2]
    ld, la, lg = decay_lora1.shape[2], iclr_lora1.shape[2], gate_lora1.shape[2]
    full = lambda *shape: pl.BlockSpec(shape, lambda i: (l,) + (0,) * (len(shape) - 1))
    in_specs = _act_specs(x, tm, n_pt) + [
                pl.BlockSpec((1, 1, bc, d), lambda i: (l, 0, 0, 0)),
                pl.BlockSpec((1, 1, bc, d), lambda i: (l, 1, 0, 0)),
                pl.BlockSpec((1, 1, 1, d), lambda i: (l, 0, 0, 0)),
                full(1, 4, d), full(1, db, d), full(1, d, ld), full(1, d, la), full(1, d, lg)]
    args = [x.prompt, x.decode, mod, mod, norm_g4, mu_x, state_shift, decay_lora1, iclr_lora1, gate_lora1]
    row_out = lambda n, dt: (pl.BlockSpec((tm, n), lambda i: (i, 0)), jax.ShapeDtypeStruct((rows, n), dt))
    outs = [row_out(d, _BF16), row_out(ld, _F32), row_out(la, _F32), row_out(lg, _F32)]
    if has_v:
        lv = vres_lora1.shape[2]
        in_specs.append(pl.BlockSpec((1, d, lv), lambda i: (l - 1, 0, 0)))
        args.append(vres_lora1)
        outs.append(row_out(lv, _F32))
    outs.append((pl.BlockSpec((1, _SUBLANES, d), lambda i: (jnp.minimum(i // tps, bp - 1), 0, 0)),
                 jax.ShapeDtypeStruct((bp, _SUBLANES, d), _F32)))
    outs.append((pl.BlockSpec((db, d), lambda i: (0, 0)), jax.ShapeDtypeStruct((db, d), _F32)))
    return pl.pallas_call(
        functools.partial(_norm_lora_kernel, n_pt=n_pt, tps=tps, db=db, has_v=has_v),
        grid=(rows // tm,),
        in_specs=in_specs,
        out_specs=[o[0] for o in outs],
        out_shape=[o[1] for o in outs],
        scratch_shapes=[pltpu.VMEM((_SUBLANES, d), _F32), pltpu.VMEM((tm, d), _F32), pltpu.VMEM((tm, d), _F32)],
        compiler_params=_params("arbitrary"),
    )(*args)


def _mix_kernel(*refs, n_pt, tps, db, has_v):
    it = iter(refs)
    h_ref = next(it)
    w_refs = [next(it) for _ in range(6)]
    aw_ref, aa_ref, ag_ref = next(it), next(it), next(it)
    av_ref = next(it) if has_v else None
    d2_ref, i2_ref, g2_ref = next(it), next(it), next(it)
    v2_ref = next(it) if has_v else None
    w0_ref, a0_ref = next(it), next(it)
    v0_ref = next(it) if has_v else None
    mu_ref, kkw_ref, kaw_ref, cw_ref, hl_ref, cs_ref = (next(it) for _ in range(6))
    vf_ref = next(it) if has_v else None
    seq_prompt = [next(it) for _ in range(7)]
    seq_decode = [next(it) for _ in range(7)]
    oc_o = next(it)
    vf_o = None if has_v else next(it)
    cp_o, cso_o = next(it), next(it)
    wc_scr, pcarry, zcarry = (next(it) for _ in range(3))

    i = pl.program_id(1)
    tm = h_ref.shape[0]
    tn = oc_o.shape[1]

    @pl.when(i == 0)
    def _cache_weights():
        for g in range(6):
            wc_scr[g] = w_refs[g][0].astype(_BF16)

    def project(rs):
        h = h_ref[rs, :]
        return [jnp.dot(h, wc_scr[g], preferred_element_type=_F32) for g in range(6)]

    def tail(rs, p, pprev, bg, z, z1, z2, seq_out):
        r_o, lw_o, k_o, v_o, kk_o, kka_o, g_o = seq_out
        mu = mu_ref[0]
        r = p[0] + (pprev[0] - p[0]) * mu[0:1]
        k = p[1] + (pprev[1] - p[1]) * mu[1:2]
        v = p[2] + (pprev[2] - p[2]) * mu[2:3]

        zlog = w0_ref[0] + _dot(aw_ref[rs, :], d2_ref[0])
        softplus = jnp.maximum(-zlog, 0.0) + jnp.log(1.0 + jnp.exp(-jnp.abs(zlog)))
        lw_o[rs, :] = -jnp.exp(-softplus - 0.5)
        a = jax.nn.sigmoid(a0_ref[0] + _dot(aa_ref[rs, :], i2_ref[0]))
        g_o[rs, :] = _dot(ag_ref[rs, :], g2_ref[0])
        if has_v:
            nu = jax.nn.sigmoid(v0_ref[0] + _dot(av_ref[rs, :], v2_ref[0]))
            v = v + (vf_ref[rs, :] - v) * nu
        else:
            vf_o[rs, :] = v

        kk = k * kkw_ref[0]
        norm = jnp.sqrt(_head_sums(kk * kk, _segment_ones(tn)))
        kk = kk / jnp.maximum(norm, 1e-12)
        r_o[rs, :] = r
        k_o[rs, :] = k * (1.0 + (a - 1.0) * kaw_ref[0])
        v_o[rs, :] = v
        kk_o[rs, :] = kk
        kka_o[rs, :] = kk * a

        cw = cw_ref[0]
        zc = z2 * cw[0:1] + z1 * cw[1:2] + z * cw[2:3]
        oc_o[rs, :] = (bg * zc).astype(_BF16)

    @pl.when(i < n_pt)
    def _prompt():
        half = tm // 2
        halves = (slice(0, half), slice(half, tm))
        proj = [project(rs) for rs in halves]
        zs = [q[4] * q[5] for q in proj]
        for idx, rs in enumerate(halves):
            p, z = proj[idx][:3], zs[idx]
            if idx == 0:
                first = (i % tps) == 0
                p_carry = [pcarry[g] for g in range(3)]
                z_carry = zcarry[...]
            else:
                first = False
                p_carry = [q[half - _SUBLANES:, :] for q in proj[0][:3]]
                z_carry = zs[0][half - _SUBLANES:, :]
            pprev = [_shift_rows_prompt(p[g], p_carry[g], first, two=False) for g in range(3)]
            z1, z2 = _shift_rows_prompt(z, z_carry, first)
            tail(rs, p, pprev, proj[idx][3], z, z1, z2, seq_prompt)
        for g in range(3):
            pcarry[g] = proj[1][g][half - _SUBLANES:, :]
        zcarry[...] = zs[1][half - _SUBLANES:, :]
        cp_o[0] = zs[1][half - _SUBLANES:, :]

    @pl.when(i >= n_pt)
    def _decode():
        rs = slice(0, tm)
        proj = project(rs)
        p, z = proj[:3], proj[4] * proj[5]
        hl = hl_ref[0].astype(_BF16)
        pprev = [jnp.concatenate([jnp.dot(hl, wc_scr[g], preferred_element_type=_F32), p[g][:tm - db, :]], axis=0)
                 for g in range(3)]
        z1 = jnp.concatenate([cs_ref[0, 1], z[:tm - db, :]], axis=0)
        z2 = jnp.concatenate([cs_ref[0, 0], cs_ref[0, 1], z[:tm - 2 * db, :]], axis=0)
        cso_o[0] = z[tm - 2 * db:tm - db, :]
        cso_o[1] = z[tm - db:, :]
        tail(rs, p, pprev, proj[3], z, z1, z2, seq_decode)


def _mix(l, h, acts, vf_in, w_in, lora2, vecs, mu_rkv, k_k3, k_a3, conv_w, state_shift, state_conv_t,
         *, tm, tn, n_pt, tps, db, bp):
    rows, d = h.shape
    has_v = l > 0
    g_dim = mu_rkv.shape[2]
    nj = g_dim // tn
    aw, aa, ag, av = acts
    d2, i2, g2, v2 = lora2
    w0, a0, v0 = vecs
    row_in = lambda arr: pl.BlockSpec((tm, arr.shape[1]), lambda j, i: (i, 0))
    col3 = lambda arr, ll: pl.BlockSpec((1, arr.shape[1], tn), lambda j, i: (ll, 0, j))
    in_specs = [row_in(h)] + [pl.BlockSpec((1, d, tn), lambda j, i, g=g: (l, 0, g * nj + j)) for g in range(6)]
    args = [h] + [w_in] * 6
    in_specs += [row_in(aw), row_in(aa), row_in(ag)]
    args += [aw, aa, ag]
    if has_v:
        in_specs.append(row_in(av))
        args.append(av)
    in_specs += [col3(d2, l), col3(i2, l), col3(g2, l)]
    args += [d2, i2, g2]
    if has_v:
        in_specs.append(col3(v2, l - 1))
        args.append(v2)
    in_specs += [col3(w0, l), col3(a0, l)]
    args += [w0, a0]
    if has_v:
        in_specs.append(col3(v0, l - 1))
        args.append(v0)
    in_specs += [col3(mu_rkv, l), col3(k_k3, l), col3(k_a3, l), col3(conv_w, l),
                 pl.BlockSpec((1, db, d), lambda j, i: (l, 0, 0)),
                 pl.BlockSpec((1, 2, db, tn), lambda j, i: (l, 0, 0, j))]
    args += [mu_rkv, k_k3, k_a3, conv_w, state_shift, state_conv_t]
    if has_v:
        in_specs.append(pl.BlockSpec((tm, tn), lambda j, i: (i, j)))
        args.append(vf_in)
    tile = pl.BlockSpec((tm, tn), lambda j, i: (i, j))
    prompt_tile = pl.BlockSpec((tm, tn), lambda j, i: (jnp.minimum(i, n_pt - 1), j))
    decode_tile = pl.BlockSpec((tm, tn), lambda j, i: (0, j), pipeline_mode=pl.Buffered(1))
    outs = [(prompt_tile, jax.ShapeDtypeStruct((n_pt * tm, g_dim), _F32)) for _ in range(7)]
    outs += [(decode_tile, jax.ShapeDtypeStruct((tm, g_dim), _F32)) for _ in range(7)]
    outs.append((tile, jax.ShapeDtypeStruct((rows, g_dim), _BF16)))
    if not has_v:
        outs.append((tile, jax.ShapeDtypeStruct((rows, g_dim), _F32)))
    outs.append((pl.BlockSpec((1, _SUBLANES, tn), lambda j, i: (jnp.minimum(i // tps, bp - 1), 0, j)),
                 jax.ShapeDtypeStruct((bp, _SUBLANES, g_dim), _F32)))
    outs.append((pl.BlockSpec((2, db, tn), lambda j, i: (0, 0, j)), jax.ShapeDtypeStruct((2, db, g_dim), _F32)))
    return pl.pallas_call(
        functools.partial(_mix_kernel, n_pt=n_pt, tps=tps, db=db, has_v=has_v),
        grid=(nj, rows // tm),
        in_specs=in_specs,
        out_specs=[o[0] for o in outs],
        out_shape=[o[1] for o in outs],
        scratch_shapes=[pltpu.VMEM((6, d, tn), _BF16), pltpu.VMEM((3, _SUBLANES, tn), _F32),
                        pltpu.VMEM((_SUBLANES, tn), _F32)],
        compiler_params=_params("arbitrary", "arbitrary"),
    )(*args)


def _unit_lower_inverse_minus_identity(lmats, ri, ci, chunk, mul):
    same8 = (ri >> 3) == (ci >> 3)
    d8 = [jnp.where(same8, m, 0.0) for m in lmats]
    d2 = [mul(a, a) for a in d8]
    d3 = [mul(a, b) for a, b in zip(d8, d2)]
    d4 = [mul(b, b) for b in d2]
    x = [a + b + cc for a, b, cc in zip(d8, d2, d3)]
    xd4 = [mul(a, b) for a, b in zip(x, d4)]
    x = [a + b + cc for a, b, cc in zip(x, d4, xd4)]
    size = 16
    while size <= chunk:
        sh = size.bit_length() - 1
        level = ((ri >> sh) == (ci >> sh)) & ((ri >> (sh - 1)) != (ci >> (sh - 1)))
        e = [jnp.where(level, m, 0.0) for m in lmats]
        y = [b + mul(a, b) for a, b in zip(x, e)]
        x = [a + b + mul(b, a) for a, b in zip(x, y)]
        size *= 2
    return x


def _wkv_kernel(r_ref, lw_ref, k_ref, v_ref, kk_ref, kka_ref, g_ref, rk_ref, lnw_ref, lnb_ref,
                o_ref, so_ref, s_scr, *, n_chunks):
    ci_grid = pl.program_id(1)
    nb, c, g_dim = r_ref.shape
    npair = g_dim // _PAIR
    n2 = 2 * c

    lane = lax.broadcasted_iota(jnp.int32, (1, _PAIR), 1)
    m0 = (lane < _HEAD).astype(_F32)
    m1 = 1.0 - m0

    @pl.when(ci_grid == 0)
    def _init():
        s_scr[...] = jnp.zeros(s_scr.shape, _F32)

    ri = lax.broadcasted_iota(jnp.int32, (n2, n2), 0) & (c - 1)
    ci = lax.broadcasted_iota(jnp.int32, (n2, n2), 1) & (c - 1)
    strict = ri > ci
    incl = ri >= ci
    tri = (lax.broadcasted_iota(jnp.int32, (c, c), 0) >= lax.broadcasted_iota(jnp.int32, (c, c), 1)).astype(_BF16)
    seg_ones = _segment_ones(_SEG)

    def stack(x):
        return jnp.concatenate([x * m0, x * m1], axis=0)

    units = [(u, p) for u in range(nb) for p in range(npair)]
    sl = lambda p: slice(p * _PAIR, (p + 1) * _PAIR)

    seq = []
    for u in range(nb):
        lw = lw_ref[u]
        l_hi, l_mid, l_lo = _split3(lw)
        cl = (jnp.dot(tri, l_hi, preferred_element_type=_F32)
              + jnp.dot(tri, l_mid, preferred_element_type=_F32)
              + jnp.dot(tri, l_lo, preferred_element_type=_F32))
        cl_end = cl[c - 1:c, :]
        r, k, v, kk, kka = r_ref[u], k_ref[u], v_ref[u], kk_ref[u], kka_ref[u]
        e_neg = jnp.exp(-cl)
        e_tail = jnp.exp(cl_end - cl)
        seq.append(dict(r=r, k=k, v=v, p_end=jnp.exp(cl_end), rt=r * jnp.exp(cl), at=-(kk * jnp.exp(cl - lw)),
                        kt=k * e_neg, bt=kka * e_neg, kh=k * e_tail, bh=kka * e_tail))

    ar2 = [jnp.concatenate([stack(seq[u]["at"][:, sl(p)]), stack(seq[u]["rt"][:, sl(p)])], axis=0).astype(_BF16)
           for u, p in units]
    mb = [_dot_nt(a, stack(seq[u]["bt"][:, sl(p)])) for a, (u, p) in zip(ar2, units)]
    mk = [_dot_nt(a, stack(seq[u]["kt"][:, sl(p)])) for a, (u, p) in zip(ar2, units)]
    s_old = [s_scr[u, p] for u, p in units]
    uy0 = [_dot_nt(jnp.concatenate([seq[u]["at"][:, sl(p)], seq[u]["rt"][:, sl(p)]], axis=0), s)
           for (u, p), s in zip(units, s_old)]
    x_inv = _unit_lower_inverse_minus_identity([jnp.where(strict, m[:n2], 0.0) for m in mb], ri, ci, c, _dot)
    vs = [stack(seq[u]["v"][:, sl(p)]).astype(_BF16) for u, p in units]
    ws = [stack(q[:c]) + _dot(jnp.where(strict, m[:n2], 0.0), vv) for q, m, vv in zip(uy0, mk, vs)]
    us = [(w + _dot(xi, w)).astype(_BF16) for xi, w in zip(x_inv, ws)]
    ysd = [_dot(jnp.where(incl, m1_[n2:], 0.0), uu) + _dot(jnp.where(incl, m2_[n2:], 0.0), vv)
           for m1_, m2_, uu, vv in zip(mb, mk, us, vs)]
    ys = [q[c:] + d[:c] + d[c:] for q, d in zip(uy0, ysd)]
    for idx, (u, p) in enumerate(units):
        uv = jnp.concatenate([us[idx], vs[idx]], axis=0)
        bk = jnp.concatenate([stack(seq[u]["bh"][:, sl(p)]), stack(seq[u]["kh"][:, sl(p)])], axis=0)
        s_scr[u, p] = s_old[idx] * seq[u]["p_end"][:, sl(p)] + _dot_tn(uv, bk)
    for u in range(nb):
        y = jnp.concatenate(ys[u * npair:(u + 1) * npair], axis=1)
        mean = _head_sums(y, seg_ones) * (1.0 / _HEAD)
        dy = y - mean
        var = _head_sums(dy * dy, seg_ones) * (1.0 / _HEAD)
        yn = dy * lax.rsqrt(var + _GN_EPS) * lnw_ref[...] + lnb_ref[...]
        bonus = _head_sums(seq[u]["r"] * seq[u]["k"] * rk_ref[...], seg_ones) * seq[u]["v"]
        o_ref[u] = ((yn + bonus) * g_ref[u]).astype(o_ref.dtype)

    @pl.when(ci_grid == n_chunks - 1)
    def _final():
        for u, p in units:
            s = s_scr[u, p]
            so_ref[u, 2 * p] = s[:_HEAD, :_HEAD]
            so_ref[u, 2 * p + 1] = s[_HEAD:, _HEAD:]


def _wkv(l, seqs, chunk, nb, r_k3, ln_w3, ln_b3):
    n_seq, t_len, g_dim = seqs[0].shape
    n_heads = g_dim // _HEAD
    n_chunks = t_len // chunk
    blk = pl.BlockSpec((nb, chunk, g_dim), lambda bi, ci: (bi, ci, 0))
    vec = pl.BlockSpec((None, 1, g_dim), lambda bi, ci: (l, 0, 0))
    return pl.pallas_call(
        functools.partial(_wkv_kernel, n_chunks=n_chunks),
        grid=(n_seq // nb, n_chunks),
        in_specs=[blk] * 7 + [vec] * 3,
        out_specs=[blk, pl.BlockSpec((nb, n_heads, _HEAD, _HEAD), lambda bi, ci: (bi, 0, 0, 0))],
        out_shape=[jax.ShapeDtypeStruct((n_seq, t_len, g_dim), _BF16),
                   jax.ShapeDtypeStruct((n_seq, n_heads, _HEAD, _HEAD), _F32)],
        scratch_shapes=[pltpu.VMEM((nb, g_dim // _PAIR, _PAIR, _PAIR), _F32)],
        compiler_params=_params("arbitrary", "arbitrary"),
    )(*seqs, r_k3, ln_w3, ln_b3)


def _wkv_decode_kernel(*refs, ts, nb, n_earlier):
    (r_ref, lw_ref, k_ref, v_ref, kk_ref, kka_ref, g_ref, rk_ref, lnw_ref, lnb_ref, s0_ref) = refs[:11]
    earlier_refs = refs[11:11 + n_earlier]
    o_ref, so_ref = refs[11 + n_earlier:]
    if n_earlier:
        for idx, e_ref in enumerate(earlier_refs):
            so_ref[idx] = e_ref[...]
        so_ref = so_ref.at[n_earlier]
    g_dim = r_ref.shape[2]
    npair = g_dim // _PAIR
    n = ts * nb
    n2 = 2 * n
    nb_bits = nb.bit_length() - 1

    lane = lax.broadcasted_iota(jnp.int32, (1, _PAIR), 1)
    m0 = (lane < _HEAD).astype(_F32)
    m1 = 1.0 - m0

    def stack(x):
        return jnp.concatenate([x * m0, x * m1], axis=0)

    def rows_tb(ref):
        return jnp.concatenate([ref[t] for t in range(ts)], axis=0)

    lw_t = [lw_ref[t] for t in range(ts)]
    cl_t = [lw_t[0]]
    for t in range(1, ts):
        cl_t.append(cl_t[-1] + lw_t[t])
    cl = jnp.concatenate(cl_t, axis=0)
    lw = jnp.concatenate(lw_t, axis=0)
    cl_end_b = cl_t[-1]
    cl_end = _tile_rows(cl_end_b, ts)
    p_end = jnp.exp(cl_end_b)
    r, k, v, kk, kka = (rows_tb(ref) for ref in (r_ref, k_ref, v_ref, kk_ref, kka_ref))
    e_neg = jnp.exp(-cl)
    e_tail = jnp.exp(cl_end - cl)
    rt = r * jnp.exp(cl)
    at = -(kk * jnp.exp(cl - lw))
    kt = k * e_neg
    bt = kka * e_neg
    kh = k * e_tail
    bh = kka * e_tail

    ri = lax.broadcasted_iota(jnp.int32, (n2, n2), 0)
    ci = lax.broadcasted_iota(jnp.int32, (n2, n2), 1)
    same_seq = (ri & (nb - 1)) == (ci & (nb - 1))
    t_r = (ri & (n - 1)) >> nb_bits
    t_c = (ci & (n - 1)) >> nb_bits
    strict = same_seq & (t_r > t_c)
    incl = same_seq & (t_r >= t_c)
    seq_of_row2 = lax.broadcasted_iota(jnp.int32, (n2, 1), 0) & (nb - 1)
    seq_of_row4 = lax.broadcasted_iota(jnp.int32, (2 * n2, 1), 0) & (nb - 1)
    zero = jnp.zeros((_HEAD, _HEAD), _F32)

    pairs = range(npair)
    sls = [slice(p * _PAIR, (p + 1) * _PAIR) for p in pairs]
    ar2 = [jnp.concatenate([stack(at[:, sl]), stack(rt[:, sl])], axis=0).astype(_BF16) for sl in sls]
    mb = [_dot_nt(a, stack(bt[:, sl])) for a, sl in zip(ar2, sls)]
    mk = [_dot_nt(a, stack(kt[:, sl])) for a, sl in zip(ar2, sls)]
    lab = [jnp.where(strict, m[:n2], 0.0) for m in mb]
    d2 = [_dot(a, a) for a in lab]
    d3 = [_dot(a, b) for a, b in zip(lab, d2)]
    x_inv = [a + b + cc for a, b, cc in zip(lab, d2, d3)]

    def block_diag(b, p):
        top = jnp.concatenate([s0_ref[b, 2 * p], zero], axis=1)
        bot = jnp.concatenate([zero, s0_ref[b, 2 * p + 1]], axis=1)
        return jnp.concatenate([top, bot], axis=0)

    ys = []
    for p, sl in zip(pairs, sls):
        s_b = [block_diag(b, p) for b in range(nb)]
        ar = jnp.concatenate([at[:, sl], rt[:, sl]], axis=0)
        ar_cat = jnp.concatenate([jnp.where(seq_of_row2 == b, ar, 0.0).astype(_BF16) for b in range(nb)], axis=1)
        s_cat = jnp.concatenate([s.astype(_BF16) for s in s_b], axis=1)
        uy0 = _dot_nt(ar_cat, s_cat)
        vs = stack(v[:, sl]).astype(_BF16)
        ws = stack(uy0[:n]) + _dot(jnp.where(strict, mk[p][:n2], 0.0), vs)
        us = (ws + _dot(x_inv[p], ws)).astype(_BF16)
        ysd = _dot(jnp.where(incl, mb[p][n2:], 0.0), us) + _dot(jnp.where(incl, mk[p][n2:], 0.0), vs)
        ys.append(uy0[n:] + ysd[:n] + ysd[n:])
        uv = jnp.concatenate([us, vs], axis=0)
        uv_cat = jnp.concatenate([jnp.where(seq_of_row4 == b, uv, jnp.zeros_like(uv)) for b in range(nb)], axis=1)
        bk = jnp.concatenate([stack(bh[:, sl]), stack(kh[:, sl])], axis=0)
        upd = _dot_tn(uv_cat, bk)
        for b in range(nb):
            s_new = (s_b[b] * p_end[b:b + 1, sl] + upd[b * _PAIR:(b + 1) * _PAIR, :])
            so_ref[b, 2 * p] = s_new[:_HEAD, :_HEAD]
            so_ref[b, 2 * p + 1] = s_new[_HEAD:, _HEAD:]

    y = jnp.concatenate(ys, axis=1)
    seg_ones = _segment_ones(_SEG)
    mean = _head_sums(y, seg_ones) * (1.0 / _HEAD)
    dy = y - mean
    var = _head_sums(dy * dy, seg_ones) * (1.0 / _HEAD)
    yn = dy * lax.rsqrt(var + _GN_EPS) * lnw_ref[...] + lnb_ref[...]
    bonus = _head_sums(r * k * rk_ref[...], seg_ones) * v
    out = ((yn + bonus) * rows_tb(g_ref)).astype(o_ref.dtype)
    for t in range(ts):
        o_ref[t] = out[t * nb:(t + 1) * nb, :]


def _wkv_decode(l, seqs, state0, earlier_states, r_k3, ln_w3, ln_b3, *, ts, db, nb):
    g_dim = seqs[0].shape[2]
    n_heads = g_dim // _HEAD
    n_earlier = len(earlier_states)
    blk = pl.BlockSpec((ts, nb, g_dim), lambda bi: (0, bi, 0))
    vec = pl.BlockSpec((None, 1, g_dim), lambda bi: (l, 0, 0))
    st = pl.BlockSpec((nb, n_heads, _HEAD, _HEAD), lambda bi: (bi, 0, 0, 0))
    if n_earlier:
        st_out = pl.BlockSpec((n_earlier + 1, nb, n_heads, _HEAD, _HEAD), lambda bi: (0, bi, 0, 0, 0))
        st_shape = (n_earlier + 1, db, n_heads, _HEAD, _HEAD)
    else:
        st_out, st_shape = st, (db, n_heads, _HEAD, _HEAD)
    return pl.pallas_call(
        functools.partial(_wkv_decode_kernel, ts=ts, nb=nb, n_earlier=n_earlier),
        grid=(db // nb,),
        in_specs=[blk] * 7 + [vec] * 3 + [st] * (1 + n_earlier),
        out_specs=[pl.BlockSpec((ts, nb, g_dim), lambda bi: (0, bi, 0)), st_out],
        out_shape=[jax.ShapeDtypeStruct((ts, db, g_dim), _BF16), jax.ShapeDtypeStruct(st_shape, _F32)],
        compiler_params=_params("arbitrary"),
    )(*seqs, r_k3, ln_w3, ln_b3, state0, *earlier_states)


def _cast_kernel(w_ref, o_ref):
    o_ref[...] = w_ref[...].astype(o_ref.dtype)


def _cast_bf16(w, rows_per_block):
    depth, k, n = w.shape
    spec = pl.BlockSpec((1, rows_per_block, n), lambda l, i: (l, i, 0))
    return pl.pallas_call(
        _cast_kernel, grid=(depth, k // rows_per_block), in_specs=[spec], out_specs=spec,
        out_shape=jax.ShapeDtypeStruct(w.shape, _BF16),
        compiler_params=_params("arbitrary", "arbitrary"),
    )(w)


def _outproj_kernel(op_ref, os_ref, oc_ref, xp_ref, xs_ref, ga_ref, sh_ref, sc_ref, g_ref, w_ref,
                    x1_ref, h2_ref, *, n_pt, tps, db):
    i = pl.program_id(0)
    tm = xp_ref.shape[0]
    ts = tm // db
    g_dim = op_ref.shape[1]

    def rows_out(rs, o_ref, x_ref, ga, sc, sh):
        acc = (jnp.dot(o_ref[rs, :], w_ref[0, :g_dim, :], preferred_element_type=_F32)
               + jnp.dot(oc_ref[rs, :], w_ref[0, g_dim:, :], preferred_element_type=_F32))
        x1 = x_ref[rs, :] + ga * acc
        x1_ref[rs, :] = x1
        xn = x1 * lax.rsqrt(jnp.mean(x1 * x1, axis=-1, keepdims=True) + _RMS_EPS) * g_ref[0, 0]
        h2_ref[rs, :] = (xn * (1.0 + sc) + sh).astype(_BF16)

    half = tm // 2
    halves = (slice(0, half), slice(half, tm))

    @pl.when(i < n_pt)
    def _prompt():
        ga, sc, sh = (_mod_row(ref, i, tps, db) for ref in (ga_ref, sc_ref, sh_ref))
        for rs in halves:
            rows_out(rs, op_ref, xp_ref, ga, sc, sh)

    @pl.when(i >= n_pt)
    def _decode():
        ga, sc, sh = (_mod_tile(ref, db, ts) for ref in (ga_ref, sc_ref, sh_ref))
        for rs in halves:
            rows_out(rs, os_ref, xs_ref, ga[rs, :], sc[rs, :], sh[rs, :])


def _outproj(l, o_p, o_s, o_conv, x, mod, norm_g4, w_out, *, tm, n_pt, tps, db):
    d = x.prompt.shape[1]
    rows = (n_pt + 1) * tm
    g_dim = o_p.shape[1]
    bc = mod.shape[2]
    modspec = lambda comp: pl.BlockSpec((1, 1, bc, d), lambda i: (l, comp, 0, 0))
    return pl.pallas_call(
        functools.partial(_outproj_kernel, n_pt=n_pt, tps=tps, db=db),
        grid=(rows // tm,),
        in_specs=[pl.BlockSpec((tm, g_dim), lambda i: (jnp.minimum(i, n_pt - 1), 0)),
                  pl.BlockSpec((tm, g_dim), lambda i: (0, 0)),
                  pl.BlockSpec((tm, o_conv.shape[1]), lambda i: (i, 0))] + _act_specs(x, tm, n_pt) + [
                  modspec(2), modspec(3), modspec(4),
                  pl.BlockSpec((1, 1, 1, d), lambda i: (l, 1, 0, 0)),
                  pl.BlockSpec((1, d, d), lambda i: (l, 0, 0), pipeline_mode=pl.Buffered(1))],
        out_specs=[pl.BlockSpec((tm, d), lambda i: (i, 0)), pl.BlockSpec((tm, d), lambda i: (i, 0))],
        out_shape=[jax.ShapeDtypeStruct((rows, d), _F32), jax.ShapeDtypeStruct((rows, d), _BF16)],
        compiler_params=_params("arbitrary"),
    )(o_p, o_s, o_conv, x.prompt, x.decode, mod, mod, mod, norm_g4, w_out)


def _ffn_kernel(*refs, n_pt, tps, db, nj, final_norm):
    it = iter(refs)
    (h2_ref, wa_ref, wb_ref, cwa_ref, cwb_ref, wd_ref, x1_ref, ga_ref, sfa_ref, sfb_ref) = (next(it) for _ in range(10))
    fg_ref = next(it) if final_norm else None
    x2_ref = next(it)
    x2s_ref = next(it) if final_norm else x2_ref
    fpa_o, fpb_o, fsa_o, fsb_o = (next(it) for _ in range(4))
    acc_scr, ca_scr, cb_scr = (next(it) for _ in range(3))
    i = pl.program_id(0)
    j = pl.program_id(1)
    tm = h2_ref.shape[0]
    ts = tm // db

    @pl.when(j == 0)
    def _zero():
        acc_scr[...] = jnp.zeros(acc_scr.shape, _F32)

    def prompt_body():
        half = tm // 2
        halves = (slice(0, half), slice(half, tm))
        us = [[jnp.dot(h2_ref[rs, :], w_ref[0], preferred_element_type=_F32) for w_ref in (wa_ref, wb_ref)]
              for rs in halves]
        wd = wd_ref[0].astype(_BF16)
        for r, rs in enumerate(halves):
            conv = []
            for idx, (cw_ref, c_scr, fp_o) in enumerate(((cwa_ref, ca_scr, fpa_o), (cwb_ref, cb_scr, fpb_o))):
                u = us[r][idx]
                if r == 0:
                    u1, u2 = _shift_rows_prompt(u, c_scr[j], (i % tps) == 0)
                else:
                    u1, u2 = _shift_rows_prompt(u, us[0][idx][half - _SUBLANES:, :], False)
                    c_scr[j] = u[half - _SUBLANES:, :]
                    fp_o[0] = u[half - _SUBLANES:, :]
                cw = cw_ref[0]
                conv.append(u2 * cw[0:1] + u1 * cw[1:2] + u * cw[2:3])
            gact = ((conv[0] * jax.nn.sigmoid(conv[0])) * conv[1]).astype(_BF16)
            acc_scr[rs, :] += jnp.dot(gact, wd, preferred_element_type=_F32)

    def decode_body():
        h2 = h2_ref[...]
        conv = []
        for w_ref, cw_ref, sf, fp_o, fs_o in ((wa_ref, cwa_ref, sfa_ref, fpa_o, fsa_o),
                                              (wb_ref, cwb_ref, sfb_ref, fpb_o, fsb_o)):
            u = jnp.dot(h2, w_ref[0], preferred_element_type=_F32)
            u1 = jnp.concatenate([sf[0, 1], u[:tm - db, :]], axis=0)
            u2 = jnp.concatenate([sf[0, 0], sf[0, 1], u[:tm - 2 * db, :]], axis=0)
            fp_o[0] = jnp.zeros(fp_o.shape[1:], _F32)
            fs_o[0] = u[tm - 2 * db:tm - db, :]
            fs_o[1] = u[tm - db:, :]
            cw = cw_ref[0]
            conv.append(u2 * cw[0:1] + u1 * cw[1:2] + u * cw[2:3])
        gact = ((conv[0] * jax.nn.sigmoid(conv[0])) * conv[1]).astype(_BF16)
        acc_scr[...] += jnp.dot(gact, wd_ref[0].astype(_BF16), preferred_element_type=_F32)

    pl.when(i < n_pt)(prompt_body)
    pl.when(i >= n_pt)(decode_body)

    def finish(ga, out_ref):
        x2 = x1_ref[...] + ga * acc_scr[...]
        if final_norm:
            x2 = x2 * lax.rsqrt(jnp.mean(x2 * x2, axis=-1, keepdims=True) + _RMS_EPS) * fg_ref[...]
        out_ref[...] = x2

    @pl.when((j == nj - 1) & (i < n_pt))
    def _finish_prompt():
        finish(_mod_row(ga_ref, i, tps, db), x2_ref)

    @pl.when((j == nj - 1) & (i >= n_pt))
    def _finish_decode():
        finish(_mod_tile(ga_ref, db, ts), x2s_ref)


def _ffn(l, h2, x1, mod, ffn_up, ffn_conv, ffn_down, state_ffn_t, final_g, *, tm, tn, n_pt, tps, db, bp):
    rows, d = x1.shape
    f = ffn_down.shape[1]
    nj = f // tn
    bc = mod.shape[2]
    final_norm = final_g is not None
    in_specs = [pl.BlockSpec((tm, d), lambda i, j: (i, 0)),
                pl.BlockSpec((1, d, tn), lambda i, j: (l, 0, j)),
                pl.BlockSpec((1, d, tn), lambda i, j: (l, 0, nj + j)),
                pl.BlockSpec((1, 3, tn), lambda i, j: (l, 0, j)),
                pl.BlockSpec((1, 3, tn), lambda i, j: (l, 0, nj + j)),
                pl.BlockSpec((1, tn, d), lambda i, j: (l, j, 0)),
                pl.BlockSpec((tm, d), lambda i, j: (i, 0)),
                pl.BlockSpec((1, 1, bc, d), lambda i, j: (l, 5, 0, 0)),
                pl.BlockSpec((1, 2, db, tn), lambda i, j: (l, 0, 0, jnp.where(i >= n_pt, j, 0))),
                pl.BlockSpec((1, 2, db, tn), lambda i, j: (l, 0, 0, nj + jnp.where(i >= n_pt, j, 0)))]
    args = [h2, ffn_up, ffn_up, ffn_conv, ffn_conv, ffn_down, x1, mod, state_ffn_t, state_ffn_t]
    if final_norm:
        in_specs.append(pl.BlockSpec((1, d), lambda i, j: (0, 0)))
        args.append(final_g)
    n_tiles = rows // tm
    pstate = pl.BlockSpec((1, _SUBLANES, tn), lambda i, j: (i, 0, j))
    sstate = pl.BlockSpec((2, db, tn), lambda i, j: (0, 0, jnp.where(i >= n_pt, j, 0)))
    if final_norm:
        x_specs = [pl.BlockSpec((tm, d), lambda i, j: (jnp.minimum(i, n_pt - 1), 0)),
                   pl.BlockSpec((tm, d), lambda i, j: (0, 0))]
        x_shapes = [jax.ShapeDtypeStruct((n_pt * tm, d), _F32), jax.ShapeDtypeStruct((tm, d), _F32)]
    else:
        x_specs = [pl.BlockSpec((tm, d), lambda i, j: (i, 0))]
        x_shapes = [jax.ShapeDtypeStruct((rows, d), _F32)]
    return pl.pallas_call(
        functools.partial(_ffn_kernel, n_pt=n_pt, tps=tps, db=db, nj=nj, final_norm=final_norm),
        grid=(rows // tm, nj),
        in_specs=in_specs,
        out_specs=x_specs + [pstate, pstate, sstate, sstate],
        out_shape=x_shapes + [
                   jax.ShapeDtypeStruct((n_tiles, _SUBLANES, f), _F32),
                   jax.ShapeDtypeStruct((n_tiles, _SUBLANES, f), _F32),
                   jax.ShapeDtypeStruct((2, db, f), _F32), jax.ShapeDtypeStruct((2, db, f), _F32)],
        scratch_shapes=[pltpu.VMEM((tm, d), _F32), pltpu.VMEM((nj, _SUBLANES, tn), _F32),
                        pltpu.VMEM((nj, _SUBLANES, tn), _F32)],
        compiler_params=_params("arbitrary", "arbitrary"),
    )(*args)


def _forward(x_prompt, x_sample, c_prompt, c_sample, state_wkv, state_shift, state_conv, state_ffn,
             ada_w, ada_b, norm_g, final_norm_g, w_in, mu_x, mu_rkv, decay_w0, decay_lora1, decay_lora2,
             iclr_a0, iclr_lora1, iclr_lora2, gate_lora1, gate_lora2, vres_v0, vres_lora1, vres_lora2,
             k_k, k_a, r_k, ln_x_w, ln_x_b, conv_w, w_out, ffn_up, ffn_conv, ffn_down,
             *, chunk=_WKV_CHUNK, mix_tn=_MIX_COLS, ffn_tn=_FFN_COLS, wkv_decode_nb=_WKV_DECODE_SEQS,
             wkv_prompt_nb=_WKV_PROMPT_SEQS):
    bp, t_len, d = x_prompt.shape
    db, ts, _ = x_sample.shape
    depth = ada_w.shape[0]
    g_dim = mu_rkv.shape[2]
    tm = ts * db
    assert t_len % tm == 0 and t_len % chunk == 0 and db % _SUBLANES == 0 and bp % wkv_prompt_nb == 0
    assert 2 <= ts <= 4 and ts & (ts - 1) == 0 and db % wkv_decode_nb == 0 and wkv_decode_nb & (wkv_decode_nb - 1) == 0
    tps = t_len // tm
    n_pt = bp * tps
    n_prompt_rows = bp * t_len

    x = _Act(x_prompt.reshape(n_prompt_rows, d), x_sample.transpose(1, 0, 2).reshape(tm, d), 0)
    pad = (-(db + bp)) % _SUBLANES
    c_all = jnp.concatenate([c_sample, c_prompt, jnp.zeros((pad, d), _F32)], axis=0)
    mod = _adaln_mod(c_all, ada_w, ada_b)

    norm_g4 = norm_g.reshape(depth, 2, 1, d)
    vec3 = lambda a: a.reshape(a.shape[0], 1, a.shape[1])
    w0_3, a0_3, v0_3, kk_3, ka_3 = vec3(decay_w0), vec3(iclr_a0), vec3(vres_v0), vec3(k_k), vec3(k_a)
    rk_3, lnw_3, lnb_3 = r_k.reshape(depth, 1, g_dim), vec3(ln_x_w), vec3(ln_x_b)
    state_conv_t = state_conv.transpose(0, 2, 1, 3)
    state_ffn_t = state_ffn.transpose(0, 2, 1, 3)
    tiles = dict(tm=tm, n_pt=n_pt, tps=tps, db=db)
    w_out = _cast_bf16(w_out, 512)
    ffn_up = _cast_bf16(ffn_up, 128)

    new = {k: [] for k in ("wkv_p", "shift_p", "conv_p", "ffn_p", "wkv_s", "shift_s", "conv_s", "ffn_s")}
    v_first = None
    for l in range(depth):
        outs = _norm_lora(l, x, mod, norm_g4, mu_x, state_shift, decay_lora1, iclr_lora1, gate_lora1,
                          vres_lora1, bp=bp, **tiles)
        if l > 0:
            h, aw, aa, ag, av, hp_last, hs_last = outs
        else:
            h, aw, aa, ag, hp_last, hs_last = outs
            av = None
        mouts = _mix(l, h, (aw, aa, ag, av), v_first, w_in,
                     (decay_lora2, iclr_lora2, gate_lora2, vres_lora2), (w0_3, a0_3, v0_3),
                     mu_rkv, kk_3, ka_3, conv_w, state_shift, state_conv_t, tn=mix_tn, bp=bp, **tiles)
        seqs_p, seqs_s, rest = mouts[:7], mouts[7:14], mouts[14:]
        if l > 0:
            o_conv, conv_p8, conv_s = rest
        else:
            o_conv, v_first, conv_p8, conv_s = rest

        o_p, wkv_p = _wkv(l, tuple(a.reshape(bp, t_len, g_dim) for a in seqs_p), chunk, wkv_prompt_nb,
                          rk_3, lnw_3, lnb_3)
        o_p = o_p.reshape(n_prompt_rows, g_dim)
        last = l == depth - 1 and depth > 1
        o_s, wkv_s = _wkv_decode(l, tuple(a.reshape(ts, db, g_dim) for a in seqs_s), state_wkv[l],
                                 new["wkv_s"] if last else [], rk_3, lnw_3, lnb_3, ts=ts, db=db,
                                 nb=wkv_decode_nb // depth if last else wkv_decode_nb)
        o_s = o_s.reshape(tm, g_dim)

        x1, h2 = _outproj(l, o_p, o_s, o_conv, x, mod, norm_g4, w_out, **tiles)
        final_g = final_norm_g.reshape(1, d) if l == depth - 1 else None
        *x_out, fpa, fpb, fsa, fsb = _ffn(l, h2, x1, mod, ffn_up, ffn_conv, ffn_down, state_ffn_t, final_g,
                                          tn=ffn_tn, bp=bp, **tiles)
        x = _Act(x_out[0], x_out[0], n_pt) if len(x_out) == 1 else _Act(x_out[0], x_out[1], 0)

        new["wkv_p"].append(wkv_p)
        if last:
            wkv_s_stacked = wkv_s
        else:
            new["wkv_s"].append(wkv_s)
        new["shift_p"].append(hp_last[:, _SUBLANES - 1])
        new["shift_s"].append(hs_last)
        new["conv_p"].append(conv_p8[:, _SUBLANES - 2:])
        new["conv_s"].append(conv_s.transpose(1, 0, 2))
        last_tiles = slice(tps - 1, n_pt, tps)
        new["ffn_p"].append(jnp.concatenate([fpa[last_tiles, _SUBLANES - 2:], fpb[last_tiles, _SUBLANES - 2:]],
                                            axis=-1))
        new["ffn_s"].append(jnp.concatenate([fsa, fsb], axis=-1).transpose(1, 0, 2))

    y_prompt = x.prompt.reshape(bp, t_len, d)
    y_sample = x.decode.reshape(ts, db, d).transpose(1, 0, 2)
    st = {k: jnp.stack(vs) for k, vs in new.items() if k != "wkv_s" or depth == 1}
    if depth > 1:
        st["wkv_s"] = wkv_s_stacked
    return (y_prompt, y_sample, st["wkv_p"], st["shift_p"], st["conv_p"], st["ffn_p"],
            st["wkv_s"], st["shift_s"], st["conv_s"], st["ffn_s"])


def kernel(x_prompt, x_sample, c_prompt, c_sample, state_wkv, state_shift, state_conv, state_ffn, ada_w, ada_b, norm_g, final_norm_g, w_in, mu_x, mu_rkv, decay_w0, decay_lora1, decay_lora2, iclr_a0, iclr_lora1, iclr_lora2, gate_lora1, gate_lora2, vres_v0, vres_lora1, vres_lora2, k_k, k_a, r_k, ln_x_w, ln_x_b, conv_w, w_out, ffn_up, ffn_conv, ffn_down):
    return _forward(x_prompt, x_sample, c_prompt, c_sample, state_wkv, state_shift, state_conv, state_ffn,
                    ada_w, ada_b, norm_g, final_norm_g, w_in, mu_x, mu_rkv, decay_w0, decay_lora1, decay_lora2,
                    iclr_a0, iclr_lora1, iclr_lora2, gate_lora1, gate_lora2, vres_v0, vres_lora1, vres_lora2,
                    k_k, k_a, r_k, ln_x_w, ln_x_b, conv_w, w_out, ffn_up, ffn_conv, ffn_down)
```

```python
import functools
from typing import NamedTuple

import jax
import jax.numpy as jnp
from jax import lax
from jax.experimental import pallas as pl
from jax.experimental.pallas import tpu as pltpu

_F32 = jnp.float32
_BF16 = jnp.bfloat16

_HEAD = 64
_PAIR = 2 * _HEAD
_SEG = 256
_RMS_EPS = 1e-6
_GN_EPS = 64e-5
_VMEM_LIMIT_BYTES = 56 * 1024 * 1024
_SUBLANES = 8

_WKV_CHUNK = 64
_WKV_PROMPT_SEQS = 2
_WKV_DECODE_SEQS = 16
_MIX_COLS = 256
_FFN_COLS = 512


def _params(*sem):
    return pltpu.CompilerParams(dimension_semantics=sem, vmem_limit_bytes=_VMEM_LIMIT_BYTES)


def _dot(a, b):
    return jnp.dot(a.astype(_BF16), b.astype(_BF16), preferred_element_type=_F32)


def _dot_nt(a, b):
    return lax.dot_general(a.astype(_BF16), b.astype(_BF16), (((1,), (1,)), ((), ())),
                           preferred_element_type=_F32)


def _dot_tn(a, b):
    return lax.dot_general(a.astype(_BF16), b.astype(_BF16), (((0,), (0,)), ((), ())),
                           preferred_element_type=_F32)


def _split2(x):
    hi = x.astype(_BF16)
    lo = (x - hi.astype(_F32)).astype(_BF16)
    return hi, lo


def _split3(x):
    hi = x.astype(_BF16)
    r1 = x - hi.astype(_F32)
    mid = r1.astype(_BF16)
    lo = (r1 - mid.astype(_F32)).astype(_BF16)
    return hi, mid, lo


def _dot_exact_rhs(x, m_bf16):
    hi, lo = _split2(x)
    return (jnp.dot(hi, m_bf16, preferred_element_type=_F32)
            + jnp.dot(lo, m_bf16, preferred_element_type=_F32))


def _segment_ones(n):
    r = lax.broadcasted_iota(jnp.int32, (n, n), 0) // _HEAD
    c = lax.broadcasted_iota(jnp.int32, (n, n), 1) // _HEAD
    return (r == c).astype(_BF16)


def _head_sums(x, seg_ones):
    n = seg_ones.shape[0]
    parts = [_dot_exact_rhs(x[:, q * n:(q + 1) * n], seg_ones) for q in range(x.shape[1] // n)]
    return parts[0] if len(parts) == 1 else jnp.concatenate(parts, axis=1)


def _tile_rows(m, reps):
    return m if reps == 1 else jnp.concatenate([m] * reps, axis=0)


def _shift_rows_prompt(x, carry8, first, two=True):
    r8 = lax.broadcasted_iota(jnp.int32, (_SUBLANES, 1), 0)
    l1 = jnp.where(first, 0.0, carry8[_SUBLANES - 1:_SUBLANES, :])
    x1 = pltpu.roll(x, 1, 0)
    x1 = jnp.concatenate([jnp.where(r8 == 0, l1, x1[:_SUBLANES, :]), x1[_SUBLANES:, :]], axis=0)
    if not two:
        return x1
    l2 = jnp.where(first, 0.0, carry8[_SUBLANES - 2:_SUBLANES - 1, :])
    x2 = pltpu.roll(x, 2, 0)
    top2 = jnp.where(r8 == 0, l2, jnp.where(r8 == 1, l1, x2[:_SUBLANES, :]))
    return x1, jnp.concatenate([top2, x2[_SUBLANES:, :]], axis=0)


class _Act(NamedTuple):
    prompt: jax.Array
    decode: jax.Array
    decode_block: int


def _act_specs(x, tm, n_pt):
    d = x.prompt.shape[1]
    return [pl.BlockSpec((tm, d), lambda i, *_: (jnp.minimum(i, n_pt - 1), 0)),
            pl.BlockSpec((tm, d), lambda i, *_: (x.decode_block, 0))]


def _mod_row(ref, i, tps, db):
    return ref[0, 0, pl.ds(db + i // tps, 1), :]


def _mod_tile(ref, db, ts):
    return _tile_rows(ref[0, 0, 0:db, :], ts)


def _mod_kernel(c_ref, w_ref, b_ref, o_ref):
    c = c_ref[...]
    s_hi, s_lo = _split2(c * jax.nn.sigmoid(c))
    w = w_ref[0].astype(_BF16)
    acc = jnp.dot(s_hi, w, preferred_element_type=_F32) + jnp.dot(s_lo, w, preferred_element_type=_F32)
    o_ref[0, 0] = acc + b_ref[0]


def _adaln_mod(c_all, ada_w, ada_b):
    depth, d, six_d = ada_w.shape
    bc = c_all.shape[0]
    tn = 1024
    per = d // tn
    return pl.pallas_call(
        _mod_kernel,
        grid=(depth, six_d // tn),
        in_specs=[pl.BlockSpec((bc, d), lambda l, n: (0, 0)),
                  pl.BlockSpec((1, d, tn), lambda l, n: (l, 0, n)),
                  pl.BlockSpec((1, 1, tn), lambda l, n: (l, 0, n))],
        out_specs=pl.BlockSpec((1, 1, bc, tn), lambda l, n: (l, n // per, 0, n % per)),
        out_shape=jax.ShapeDtypeStruct((depth, 6, bc, d), _F32),
        compiler_params=_params("arbitrary", "arbitrary"),
    )(c_all, ada_w, ada_b.reshape(depth, 1, six_d))


def _norm_lora_kernel(*refs, n_pt, tps, db, has_v):
    n_lora = 4 if has_v else 3
    it = iter(refs)
    xp_ref, xs_ref, sh_ref, sc_ref, g_ref, mu_ref, hl_ref = (next(it) for _ in range(7))
    w_refs = [next(it) for _ in range(n_lora)]
    h_ref = next(it)
    act_refs = [next(it) for _ in range(n_lora)]
    hp_ref, hs_ref, carry_ref, hcur_ref, hprev_ref = (next(it) for _ in range(5))
    i = pl.program_id(0)

    tm = xp_ref.shape[0]
    ts = tm // db
    x = jnp.where(i < n_pt, xp_ref[...], xs_ref[...])
    xn = x * lax.rsqrt(jnp.mean(x * x, axis=-1, keepdims=True) + _RMS_EPS) * g_ref[0, 0]

    @pl.when(i < n_pt)
    def _prompt():
        h = xn * (1.0 + _mod_row(sc_ref, i, tps, db)) + _mod_row(sh_ref, i, tps, db)
        first = (i % tps) == 0
        hprev_ref[...] = _shift_rows_prompt(h, carry_ref[...], first, two=False)
        hcur_ref[...] = h
        carry_ref[...] = h[tm - _SUBLANES:, :]
        hp_ref[0] = h[tm - _SUBLANES:, :]

    @pl.when(i >= n_pt)
    def _decode():
        h = xn * (1.0 + _mod_tile(sc_ref, db, ts)) + _mod_tile(sh_ref, db, ts)
        hprev_ref[...] = jnp.concatenate([hl_ref[0], h[:tm - db, :]], axis=0)
        hcur_ref[...] = h
        hs_ref[...] = h[tm - db:, :]

    h = hcur_ref[...]
    xx = hprev_ref[...] - h
    mu = mu_ref[0]
    h_ref[...] = h.astype(_BF16)
    pre = [_dot(h + xx * mu[g:g + 1], w_refs[g][0]) for g in range(n_lora)]
    act_refs[0][...] = jnp.tanh(pre[0])
    act_refs[1][...] = pre[1]
    act_refs[2][...] = jax.nn.sigmoid(pre[2])
    if has_v:
        act_refs[3][...] = pre[3]


def _norm_lora(l, x, mod, norm_g4, mu_x, state_shift, decay_lora1, iclr_lora1, gate_lora1, vres_lora1,
               *, tm, n_pt, tps, db, bp):
    d = x.prompt.shape[1]
    rows = (n_pt + 1) * tm
    has_v = l > 0
    bc = mod.shape[2]
    ld, la, lg = decay_lora1.shape[2], iclr_lora1.shape[2], gate_lora1.shape[2]
    full = lambda *shape: pl.BlockSpec(shape, lambda i: (l,) + (0,) * (len(shape) - 1))
    in_specs = _act_specs(x, tm, n_pt) + [
                pl.BlockSpec((1, 1, bc, d), lambda i: (l, 0, 0, 0)),
                pl.BlockSpec((1, 1, bc, d), lambda i: (l, 1, 0, 0)),
                pl.BlockSpec((1, 1, 1, d), lambda i: (l, 0, 0, 0)),
                full(1, 4, d), full(1, db, d), full(1, d, ld), full(1, d, la), full(1, d, lg)]
    args = [x.prompt, x.decode, mod, mod, norm_g4, mu_x, state_shift, decay_lora1, iclr_lora1, gate_lora1]
    row_out = lambda n, dt: (pl.BlockSpec((tm, n), lambda i: (i, 0)), jax.ShapeDtypeStruct((rows, n), dt))
    outs = [row_out(d, _BF16), row_out(ld, _F32), row_out(la, _F32), row_out(lg, _F32)]
    if has_v:
        lv = vres_lora1.shape[2]
        in_specs.append(pl.BlockSpec((1, d, lv), lambda i: (l - 1, 0, 0)))
        args.append(vres_lora1)
        outs.append(row_out(lv, _F32))
    outs.append((pl.BlockSpec((1, _SUBLANES, d), lambda i: (jnp.minimum(i // tps, bp - 1), 0, 0)),
                 jax.ShapeDtypeStruct((bp, _SUBLANES, d), _F32)))
    outs.append((pl.BlockSpec((db, d), lambda i: (0, 0)), jax.ShapeDtypeStruct((db, d), _F32)))
    return pl.pallas_call(
        functools.partial(_norm_lora_kernel, n_pt=n_pt, tps=tps, db=db, has_v=has_v),
        grid=(rows // tm,),
        in_specs=in_specs,
        out_specs=[o[0] for o in outs],
        out_shape=[o[1] for o in outs],
        scratch_shapes=[pltpu.VMEM((_SUBLANES, d), _F32), pltpu.VMEM((tm, d), _F32), pltpu.VMEM((tm, d), _F32)],
        compiler_params=_params("arbitrary"),
    )(*args)


def _mix_kernel(*refs, n_pt, tps, db, has_v):
    it = iter(refs)
    h_ref = next(it)
    w_refs = [next(it) for _ in range(6)]
    aw_ref, aa_ref, ag_ref = next(it), next(it), next(it)
    av_ref = next(it) if has_v else None
    d2_ref, i2_ref, g2_ref = next(it), next(it), next(it)
    v2_ref = next(it) if has_v else None
    w0_ref, a0_ref = next(it), next(it)
    v0_ref = next(it) if has_v else None
    mu_ref, kkw_ref, kaw_ref, cw_ref, hl_ref, cs_ref = (next(it) for _ in range(6))
    vf_ref = next(it) if has_v else None
    seq_prompt = [next(it) for _ in range(7)]
    seq_decode = [next(it) for _ in range(7)]
    oc_o = next(it)
    vf_o = None if has_v else next(it)
    cp_o, cso_o = next(it), next(it)
    wc_scr, pcarry, zcarry = (next(it) for _ in range(3))

    i = pl.program_id(1)
    tm = h_ref.shape[0]
    tn = oc_o.shape[1]

    @pl.when(i == 0)
    def _cache_weights():
        for g in range(6):
            wc_scr[g] = w_refs[g][0].astype(_BF16)

    def project(rs):
        h = h_ref[rs, :]
        return [jnp.dot(h, wc_scr[g], preferred_element_type=_F32) for g in range(6)]

    def tail(rs, p, pprev, bg, z, z1, z2, seq_out):
        r_o, lw_o, k_o, v_o, kk_o, kka_o, g_o = seq_out
        mu = mu_ref[0]
        r = p[0] + (pprev[0] - p[0]) * mu[0:1]
        k = p[1] + (pprev[1] - p[1]) * mu[1:2]
        v = p[2] + (pprev[2] - p[2]) * mu[2:3]

        zlog = w0_ref[0] + _dot(aw_ref[rs, :], d2_ref[0])
        softplus = jnp.maximum(-zlog, 0.0) + jnp.log(1.0 + jnp.exp(-jnp.abs(zlog)))
        lw_o[rs, :] = -jnp.exp(-softplus - 0.5)
        a = jax.nn.sigmoid(a0_ref[0] + _dot(aa_ref[rs, :], i2_ref[0]))
        g_o[rs, :] = _dot(ag_ref[rs, :], g2_ref[0])
        if has_v:
            nu = jax.nn.sigmoid(v0_ref[0] + _dot(av_ref[rs, :], v2_ref[0]))
            v = v + (vf_ref[rs, :] - v) * nu
        else:
            vf_o[rs, :] = v

        kk = k * kkw_ref[0]
        norm = jnp.sqrt(_head_sums(kk * kk, _segment_ones(tn)))
        kk = kk / jnp.maximum(norm, 1e-12)
        r_o[rs, :] = r
        k_o[rs, :] = k * (1.0 + (a - 1.0) * kaw_ref[0])
        v_o[rs, :] = v
        kk_o[rs, :] = kk
        kka_o[rs, :] = kk * a

        cw = cw_ref[0]
        zc = z2 * cw[0:1] + z1 * cw[1:2] + z * cw[2:3]
        oc_o[rs, :] = (bg * zc).astype(_BF16)

    @pl.when(i < n_pt)
    def _prompt():
        half = tm // 2
        halves = (slice(0, half), slice(half, tm))
        proj = [project(rs) for rs in halves]
        zs = [q[4] * q[5] for q in proj]
        for idx, rs in enumerate(halves):
            p, z = proj[idx][:3], zs[idx]
            if idx == 0:
                first = (i % tps) == 0
                p_carry = [pcarry[g] for g in range(3)]
                z_carry = zcarry[...]
            else:
                first = False
                p_carry = [q[half - _SUBLANES:, :] for q in proj[0][:3]]
                z_carry = zs[0][half - _SUBLANES:, :]
            pprev = [_shift_rows_prompt(p[g], p_carry[g], first, two=False) for g in range(3)]
            z1, z2 = _shift_rows_prompt(z, z_carry, first)
            tail(rs, p, pprev, proj[idx][3], z, z1, z2, seq_prompt)
        for g in range(3):
            pcarry[g] = proj[1][g][half - _SUBLANES:, :]
        zcarry[...] = zs[1][half - _SUBLANES:, :]
        cp_o[0] = zs[1][half - _SUBLANES:, :]

    @pl.when(i >= n_pt)
    def _decode():
        rs = slice(0, tm)
        proj = project(rs)
        p, z = proj[:3], proj[4] * proj[5]
        hl = hl_ref[0].astype(_BF16)
        pprev = [jnp.concatenate([jnp.dot(hl, wc_scr[g], preferred_element_type=_F32), p[g][:tm - db, :]], axis=0)
                 for g in range(3)]
        z1 = jnp.concatenate([cs_ref[0, 1], z[:tm - db, :]], axis=0)
        z2 = jnp.concatenate([cs_ref[0, 0], cs_ref[0, 1], z[:tm - 2 * db, :]], axis=0)
        cso_o[0] = z[tm - 2 * db:tm - db, :]
        cso_o[1] = z[tm - db:, :]
        tail(rs, p, pprev, proj[3], z, z1, z2, seq_decode)


def _mix(l, h, acts, vf_in, w_in, lora2, vecs, mu_rkv, k_k3, k_a3, conv_w, state_shift, state_conv_t,
         *, tm, tn, n_pt, tps, db, bp):
    rows, d = h.shape
    has_v = l > 0
    g_dim = mu_rkv.shape[2]
    nj = g_dim // tn
    aw, aa, ag, av = acts
    d2, i2, g2, v2 = lora2
    w0, a0, v0 = vecs
    row_in = lambda arr: pl.BlockSpec((tm, arr.shape[1]), lambda j, i: (i, 0))
    col3 = lambda arr, ll: pl.BlockSpec((1, arr.shape[1], tn), lambda j, i: (ll, 0, j))
    in_specs = [row_in(h)] + [pl.BlockSpec((1, d, tn), lambda j, i, g=g: (l, 0, g * nj + j)) for g in range(6)]
    args = [h] + [w_in] * 6
    in_specs += [row_in(aw), row_in(aa), row_in(ag)]
    args += [aw, aa, ag]
    if has_v:
        in_specs.append(row_in(av))
        args.append(av)
    in_specs += [col3(d2, l), col3(i2, l), col3(g2, l)]
    args += [d2, i2, g2]
    if has_v:
        in_specs.append(col3(v2, l - 1))
        args.append(v2)
    in_specs += [col3(w0, l), col3(a0, l)]
    args += [w0, a0]
    if has_v:
        in_specs.append(col3(v0, l - 1))
        args.append(v0)
    in_specs += [col3(mu_rkv, l), col3(k_k3, l), col3(k_a3, l), col3(conv_w, l),
                 pl.BlockSpec((1, db, d), lambda j, i: (l, 0, 0)),
                 pl.BlockSpec((1, 2, db, tn), lambda j, i: (l, 0, 0, j))]
    args += [mu_rkv, k_k3, k_a3, conv_w, state_shift, state_conv_t]
    if has_v:
        in_specs.append(pl.BlockSpec((tm, tn), lambda j, i: (i, j)))
        args.append(vf_in)
    tile = pl.BlockSpec((tm, tn), lambda j, i: (i, j))
    prompt_tile = pl.BlockSpec((tm, tn), lambda j, i: (jnp.minimum(i, n_pt - 1), j))
    decode_tile = pl.BlockSpec((tm, tn), lambda j, i: (0, j), pipeline_mode=pl.Buffered(1))
    outs = [(prompt_tile, jax.ShapeDtypeStruct((n_pt * tm, g_dim), _F32)) for _ in range(7)]
    outs += [(decode_tile, jax.ShapeDtypeStruct((tm, g_dim), _F32)) for _ in range(7)]
    outs.append((tile, jax.ShapeDtypeStruct((rows, g_dim), _BF16)))
    if not has_v:
        outs.append((tile, jax.ShapeDtypeStruct((rows, g_dim), _F32)))
    outs.append((pl.BlockSpec((1, _SUBLANES, tn), lambda j, i: (jnp.minimum(i // tps, bp - 1), 0, j)),
                 jax.ShapeDtypeStruct((bp, _SUBLANES, g_dim), _F32)))
    outs.append((pl.BlockSpec((2, db, tn), lambda j, i: (0, 0, j)), jax.ShapeDtypeStruct((2, db, g_dim), _F32)))
    return pl.pallas_call(
        functools.partial(_mix_kernel, n_pt=n_pt, tps=tps, db=db, has_v=has_v),
        grid=(nj, rows // tm),
        in_specs=in_specs,
        out_specs=[o[0] for o in outs],
        out_shape=[o[1] for o in outs],
        scratch_shapes=[pltpu.VMEM((6, d, tn), _BF16), pltpu.VMEM((3, _SUBLANES, tn), _F32),
                        pltpu.VMEM((_SUBLANES, tn), _F32)],
        compiler_params=_params("arbitrary", "arbitrary"),
    )(*args)


def _unit_lower_inverse_minus_identity(lmats, ri, ci, chunk, mul):
    same8 = (ri >> 3) == (ci >> 3)
    d8 = [jnp.where(same8, m, 0.0) for m in lmats]
    d2 = [mul(a, a) for a in d8]
    d3 = [mul(a, b) for a, b in zip(d8, d2)]
    d4 = [mul(b, b) for b in d2]
    x = [a + b + cc for a, b, cc in zip(d8, d2, d3)]
    xd4 = [mul(a, b) for a, b in zip(x, d4)]
    x = [a + b + cc for a, b, cc in zip(x, d4, xd4)]
    size = 16
    while size <= chunk:
        sh = size.bit_length() - 1
        level = ((ri >> sh) == (ci >> sh)) & ((ri >> (sh - 1)) != (ci >> (sh - 1)))
        e = [jnp.where(level, m, 0.0) for m in lmats]
        y = [b + mul(a, b) for a, b in zip(x, e)]
        x = [a + b + mul(b, a) for a, b in zip(x, y)]
        size *= 2
    return x


def _wkv_kernel(r_ref, lw_ref, k_ref, v_ref, kk_ref, kka_ref, g_ref, rk_ref, lnw_ref, lnb_ref,
                o_ref, so_ref, s_scr, *, n_chunks):
    ci_grid = pl.program_id(1)
    nb, c, g_dim = r_ref.shape
    npair = g_dim // _PAIR
    n2 = 2 * c

    lane = lax.broadcasted_iota(jnp.int32, (1, _PAIR), 1)
    m0 = (lane < _HEAD).astype(_F32)
    m1 = 1.0 - m0

    @pl.when(ci_grid == 0)
    def _init():
        s_scr[...] = jnp.zeros(s_scr.shape, _F32)

    ri = lax.broadcasted_iota(jnp.int32, (n2, n2), 0) & (c - 1)
    ci = lax.broadcasted_iota(jnp.int32, (n2, n2), 1) & (c - 1)
    strict = ri > ci
    incl = ri >= ci
    tri = (lax.broadcasted_iota(jnp.int32, (c, c), 0) >= lax.broadcasted_iota(jnp.int32, (c, c), 1)).astype(_BF16)
    seg_ones = _segment_ones(_SEG)

    def stack(x):
        return jnp.concatenate([x * m0, x * m1], axis=0)

    units = [(u, p) for u in range(nb) for p in range(npair)]
    sl = lambda p: slice(p * _PAIR, (p + 1) * _PAIR)

    seq = []
    for u in range(nb):
        lw = lw_ref[u]
        l_hi, l_mid, l_lo = _split3(lw)
        cl = (jnp.dot(tri, l_hi, preferred_element_type=_F32)
              + jnp.dot(tri, l_mid, preferred_element_type=_F32)
              + jnp.dot(tri, l_lo, preferred_element_type=_F32))
        cl_end = cl[c - 1:c, :]
        r, k, v, kk, kka = r_ref[u], k_ref[u], v_ref[u], kk_ref[u], kka_ref[u]
        e_neg = jnp.exp(-cl)
        e_tail = jnp.exp(cl_end - cl)
        seq.append(dict(r=r, k=k, v=v, p_end=jnp.exp(cl_end), rt=r * jnp.exp(cl), at=-(kk * jnp.exp(cl - lw)),
                        kt=k * e_neg, bt=kka * e_neg, kh=k * e_tail, bh=kka * e_tail))

    ar2 = [jnp.concatenate([stack(seq[u]["at"][:, sl(p)]), stack(seq[u]["rt"][:, sl(p)])], axis=0).astype(_BF16)
           for u, p in units]
    mb = [_dot_nt(a, stack(seq[u]["bt"][:, sl(p)])) for a, (u, p) in zip(ar2, units)]
    mk = [_dot_nt(a, stack(seq[u]["kt"][:, sl(p)])) for a, (u, p) in zip(ar2, units)]
    s_old = [s_scr[u, p] for u, p in units]
    uy0 = [_dot_nt(jnp.concatenate([seq[u]["at"][:, sl(p)], seq[u]["rt"][:, sl(p)]], axis=0), s)
           for (u, p), s in zip(units, s_old)]
    x_inv = _unit_lower_inverse_minus_identity([jnp.where(strict, m[:n2], 0.0) for m in mb], ri, ci, c, _dot)
    vs = [stack(seq[u]["v"][:, sl(p)]).astype(_BF16) for u, p in units]
    ws = [stack(q[:c]) + _dot(jnp.where(strict, m[:n2], 0.0), vv) for q, m, vv in zip(uy0, mk, vs)]
    us = [(w + _dot(xi, w)).astype(_BF16) for xi, w in zip(x_inv, ws)]
    ysd = [_dot(jnp.where(incl, m1_[n2:], 0.0), uu) + _dot(jnp.where(incl, m2_[n2:], 0.0), vv)
           for m1_, m2_, uu, vv in zip(mb, mk, us, vs)]
    ys = [q[c:] + d[:c] + d[c:] for q, d in zip(uy0, ysd)]
    for idx, (u, p) in enumerate(units):
        uv = jnp.concatenate([us[idx], vs[idx]], axis=0)
        bk = jnp.concatenate([stack(seq[u]["bh"][:, sl(p)]), stack(seq[u]["kh"][:, sl(p)])], axis=0)
        s_scr[u, p] = s_old[idx] * seq[u]["p_end"][:, sl(p)] + _dot_tn(uv, bk)
    for u in range(nb):
        y = jnp.concatenate(ys[u * npair:(u + 1) * npair], axis=1)
        mean = _head_sums(y, seg_ones) * (1.0 / _HEAD)
        dy = y - mean
        var = _head_sums(dy * dy, seg_ones) * (1.0 / _HEAD)
        yn = dy * lax.rsqrt(var + _GN_EPS) * lnw_ref[...] + lnb_ref[...]
        bonus = _head_sums(seq[u]["r"] * seq[u]["k"] * rk_ref[...], seg_ones) * seq[u]["v"]
        o_ref[u] = ((yn + bonus) * g_ref[u]).astype(o_ref.dtype)

    @pl.when(ci_grid == n_chunks - 1)
    def _final():
        for u, p in units:
            s = s_scr[u, p]
            so_ref[u, 2 * p] = s[:_HEAD, :_HEAD]
            so_ref[u, 2 * p + 1] = s[_HEAD:, _HEAD:]


def _wkv(l, seqs, chunk, nb, r_k3, ln_w3, ln_b3):
    n_seq, t_len, g_dim = seqs[0].shape
    n_heads = g_dim // _HEAD
    n_chunks = t_len // chunk
    blk = pl.BlockSpec((nb, chunk, g_dim), lambda bi, ci: (bi, ci, 0))
    vec = pl.BlockSpec((None, 1, g_dim), lambda bi, ci: (l, 0, 0))
    return pl.pallas_call(
        functools.partial(_wkv_kernel, n_chunks=n_chunks),
        grid=(n_seq // nb, n_chunks),
        in_specs=[blk] * 7 + [vec] * 3,
        out_specs=[blk, pl.BlockSpec((nb, n_heads, _HEAD, _HEAD), lambda bi, ci: (bi, 0, 0, 0))],
        out_shape=[jax.ShapeDtypeStruct((n_seq, t_len, g_dim), _BF16),
                   jax.ShapeDtypeStruct((n_seq, n_heads, _HEAD, _HEAD), _F32)],
        scratch_shapes=[pltpu.VMEM((nb, g_dim // _PAIR, _PAIR, _PAIR), _F32)],
        compiler_params=_params("arbitrary", "arbitrary"),
    )(*seqs, r_k3, ln_w3, ln_b3)


def _wkv_decode_kernel(*refs, ts, nb, n_earlier):
    (r_ref, lw_ref, k_ref, v_ref, kk_ref, kka_ref, g_ref, rk_ref, lnw_ref, lnb_ref, s0_ref) = refs[:11]
    earlier_refs = refs[11:11 + n_earlier]
    o_ref, so_ref = refs[11 + n_earlier:]
    if n_earlier:
        for idx, e_ref in enumerate(earlier_refs):
            so_ref[idx] = e_ref[...]
        so_ref = so_ref.at[n_earlier]
    g_dim = r_ref.shape[2]
    npair = g_dim // _PAIR
    n = ts * nb
    n2 = 2 * n
    nb_bits = nb.bit_length() - 1

    lane = lax.broadcasted_iota(jnp.int32, (1, _PAIR), 1)
    m0 = (lane < _HEAD).astype(_F32)
    m1 = 1.0 - m0

    def stack(x):
        return jnp.concatenate([x * m0, x * m1], axis=0)

    def rows_tb(ref):
        return jnp.concatenate([ref[t] for t in range(ts)], axis=0)

    lw_t = [lw_ref[t] for t in range(ts)]
    cl_t = [lw_t[0]]
    for t in range(1, ts):
        cl_t.append(cl_t[-1] + lw_t[t])
    cl = jnp.concatenate(cl_t, axis=0)
    lw = jnp.concatenate(lw_t, axis=0)
    cl_end_b = cl_t[-1]
    cl_end = _tile_rows(cl_end_b, ts)
    p_end = jnp.exp(cl_end_b)
    r, k, v, kk, kka = (rows_tb(ref) for ref in (r_ref, k_ref, v_ref, kk_ref, kka_ref))
    e_neg = jnp.exp(-cl)
    e_tail = jnp.exp(cl_end - cl)
    rt = r * jnp.exp(cl)
    at = -(kk * jnp.exp(cl - lw))
    kt = k * e_neg
    bt = kka * e_neg
    kh = k * e_tail
    bh = kka * e_tail

    ri = lax.broadcasted_iota(jnp.int32, (n2, n2), 0)
    ci = lax.broadcasted_iota(jnp.int32, (n2, n2), 1)
    same_seq = (ri & (nb - 1)) == (ci & (nb - 1))
    t_r = (ri & (n - 1)) >> nb_bits
    t_c = (ci & (n - 1)) >> nb_bits
    strict = same_seq & (t_r > t_c)
    incl = same_seq & (t_r >= t_c)
    seq_of_row2 = lax.broadcasted_iota(jnp.int32, (n2, 1), 0) & (nb - 1)
    seq_of_row4 = lax.broadcasted_iota(jnp.int32, (2 * n2, 1), 0) & (nb - 1)
    zero = jnp.zeros((_HEAD, _HEAD), _F32)

    pairs = range(npair)
    sls = [slice(p * _PAIR, (p + 1) * _PAIR) for p in pairs]
    ar2 = [jnp.concatenate([stack(at[:, sl]), stack(rt[:, sl])], axis=0).astype(_BF16) for sl in sls]
    mb = [_dot_nt(a, stack(bt[:, sl])) for a, sl in zip(ar2, sls)]
    mk = [_dot_nt(a, stack(kt[:, sl])) for a, sl in zip(ar2, sls)]
    lab = [jnp.where(strict, m[:n2], 0.0) for m in mb]
    d2 = [_dot(a, a) for a in lab]
    d3 = [_dot(a, b) for a, b in zip(lab, d2)]
    x_inv = [a + b + cc for a, b, cc in zip(lab, d2, d3)]

    def block_diag(b, p):
        top = jnp.concatenate([s0_ref[b, 2 * p], zero], axis=1)
        bot = jnp.concatenate([zero, s0_ref[b, 2 * p + 1]], axis=1)
        return jnp.concatenate([top, bot], axis=0)

    ys = []
    for p, sl in zip(pairs, sls):
        s_b = [block_diag(b, p) for b in range(nb)]
        ar = jnp.concatenate([at[:, sl], rt[:, sl]], axis=0)
        ar_cat = jnp.concatenate([jnp.where(seq_of_row2 == b, ar, 0.0).astype(_BF16) for b in range(nb)], axis=1)
        s_cat = jnp.concatenate([s.astype(_BF16) for s in s_b], axis=1)
        uy0 = _dot_nt(ar_cat, s_cat)
        vs = stack(v[:, sl]).astype(_BF16)
        ws = stack(uy0[:n]) + _dot(jnp.where(strict, mk[p][:n2], 0.0), vs)
        us = (ws + _dot(x_inv[p], ws)).astype(_BF16)
        ysd = _dot(jnp.where(incl, mb[p][n2:], 0.0), us) + _dot(jnp.where(incl, mk[p][n2:], 0.0), vs)
        ys.append(uy0[n:] + ysd[:n] + ysd[n:])
        uv = jnp.concatenate([us, vs], axis=0)
        uv_cat = jnp.concatenate([jnp.where(seq_of_row4 == b, uv, jnp.zeros_like(uv)) for b in range(nb)], axis=1)
        bk = jnp.concatenate([stack(bh[:, sl]), stack(kh[:, sl])], axis=0)
        upd = _dot_tn(uv_cat, bk)
        for b in range(nb):
            s_new = (s_b[b] * p_end[b:b + 1, sl] + upd[b * _PAIR:(b + 1) * _PAIR, :])
            so_ref[b, 2 * p] = s_new[:_HEAD, :_HEAD]
            so_ref[b, 2 * p + 1] = s_new[_HEAD:, _HEAD:]

    y = jnp.concatenate(ys, axis=1)
    seg_ones = _segment_ones(_SEG)
    mean = _head_sums(y, seg_ones) * (1.0 / _HEAD)
    dy = y - mean
    var = _head_sums(dy * dy, seg_ones) * (1.0 / _HEAD)
    yn = dy * lax.rsqrt(var + _GN_EPS) * lnw_ref[...] + lnb_ref[...]
    bonus = _head_sums(r * k * rk_ref[...], seg_ones) * v
    out = ((yn + bonus) * rows_tb(g_ref)).astype(o_ref.dtype)
    for t in range(ts):
        o_ref[t] = out[t * nb:(t + 1) * nb, :]


def _wkv_decode(l, seqs, state0, earlier_states, r_k3, ln_w3, ln_b3, *, ts, db, nb):
    g_dim = seqs[0].shape[2]
    n_heads = g_dim // _HEAD
    n_earlier = len(earlier_states)
    head_groups = 2 if n_earlier else 1
    gw, hw = g_dim // head_groups, n_heads // head_groups
    blk = pl.BlockSpec((ts, nb, gw), lambda bi, hg: (0, bi, hg))
    vec = pl.BlockSpec((None, 1, gw), lambda bi, hg: (l, 0, hg))
    st = pl.BlockSpec((nb, hw, _HEAD, _HEAD), lambda bi, hg: (bi, hg, 0, 0))
    if n_earlier:
        st_out = pl.BlockSpec((n_earlier + 1, nb, hw, _HEAD, _HEAD), lambda bi, hg: (0, bi, hg, 0, 0))
        st_shape = (n_earlier + 1, db, n_heads, _HEAD, _HEAD)
    else:
        st_out, st_shape = st, (db, n_heads, _HEAD, _HEAD)
    return pl.pallas_call(
        functools.partial(_wkv_decode_kernel, ts=ts, nb=nb, n_earlier=n_earlier),
        grid=(db // nb, head_groups),
        in_specs=[blk] * 7 + [vec] * 3 + [st] * (1 + n_earlier),
        out_specs=[blk, st_out],
        out_shape=[jax.ShapeDtypeStruct((ts, db, g_dim), _BF16), jax.ShapeDtypeStruct(st_shape, _F32)],
        compiler_params=_params("arbitrary", "arbitrary"),
    )(*seqs, r_k3, ln_w3, ln_b3, state0, *earlier_states)


def _cast_kernel(w_ref, o_ref):
    o_ref[...] = w_ref[...].astype(o_ref.dtype)


def _cast_bf16(w, rows_per_block):
    depth, k, n = w.shape
    spec = pl.BlockSpec((1, rows_per_block, n), lambda l, i: (l, i, 0))
    return pl.pallas_call(
        _cast_kernel, grid=(depth, k // rows_per_block), in_specs=[spec], out_specs=spec,
        out_shape=jax.ShapeDtypeStruct(w.shape, _BF16),
        compiler_params=_params("arbitrary", "arbitrary"),
    )(w)


def _outproj_kernel(op_ref, os_ref, oc_ref, xp_ref, xs_ref, ga_ref, sh_ref, sc_ref, g_ref, w_ref,
                    x1_ref, h2_ref, *, n_pt, tps, db):
    i = pl.program_id(0)
    tm = xp_ref.shape[0]
    ts = tm // db
    g_dim = op_ref.shape[1]

    def rows_out(rs, o_ref, x_ref, ga, sc, sh):
        acc = (jnp.dot(o_ref[rs, :], w_ref[0, :g_dim, :], preferred_element_type=_F32)
               + jnp.dot(oc_ref[rs, :], w_ref[0, g_dim:, :], preferred_element_type=_F32))
        x1 = x_ref[rs, :] + ga * acc
        x1_ref[rs, :] = x1
        xn = x1 * lax.rsqrt(jnp.mean(x1 * x1, axis=-1, keepdims=True) + _RMS_EPS) * g_ref[0, 0]
        h2_ref[rs, :] = (xn * (1.0 + sc) + sh).astype(_BF16)

    half = tm // 2
    halves = (slice(0, half), slice(half, tm))

    @pl.when(i < n_pt)
    def _prompt():
        ga, sc, sh = (_mod_row(ref, i, tps, db) for ref in (ga_ref, sc_ref, sh_ref))
        for rs in halves:
            rows_out(rs, op_ref, xp_ref, ga, sc, sh)

    @pl.when(i >= n_pt)
    def _decode():
        ga, sc, sh = (_mod_tile(ref, db, ts) for ref in (ga_ref, sc_ref, sh_ref))
        for rs in halves:
            rows_out(rs, os_ref, xs_ref, ga[rs, :], sc[rs, :], sh[rs, :])


def _outproj(l, o_p, o_s, o_conv, x, mod, norm_g4, w_out, *, tm, n_pt, tps, db):
    d = x.prompt.shape[1]
    rows = (n_pt + 1) * tm
    g_dim = o_p.shape[1]
    bc = mod.shape[2]
    modspec = lambda comp: pl.BlockSpec((1, 1, bc, d), lambda i: (l, comp, 0, 0))
    return pl.pallas_call(
        functools.partial(_outproj_kernel, n_pt=n_pt, tps=tps, db=db),
        grid=(rows // tm,),
        in_specs=[pl.BlockSpec((tm, g_dim), lambda i: (jnp.minimum(i, n_pt - 1), 0)),
                  pl.BlockSpec((tm, g_dim), lambda i: (0, 0)),
                  pl.BlockSpec((tm, o_conv.shape[1]), lambda i: (i, 0))] + _act_specs(x, tm, n_pt) + [
                  modspec(2), modspec(3), modspec(4),
                  pl.BlockSpec((1, 1, 1, d), lambda i: (l, 1, 0, 0)),
                  pl.BlockSpec((1, d, d), lambda i: (l, 0, 0), pipeline_mode=pl.Buffered(1))],
        out_specs=[pl.BlockSpec((tm, d), lambda i: (i, 0)), pl.BlockSpec((tm, d), lambda i: (i, 0))],
        out_shape=[jax.ShapeDtypeStruct((rows, d), _F32), jax.ShapeDtypeStruct((rows, d), _BF16)],
        compiler_params=_params("arbitrary"),
    )(o_p, o_s, o_conv, x.prompt, x.decode, mod, mod, mod, norm_g4, w_out)


def _ffn_kernel(*refs, n_pt, tps, db, nj, final_norm):
    it = iter(refs)
    (h2_ref, wa_ref, wb_ref, cwa_ref, cwb_ref, wd_ref, x1_ref, ga_ref, sfa_ref, sfb_ref) = (next(it) for _ in range(10))
    fg_ref = next(it) if final_norm else None
    x2_ref = next(it)
    x2s_ref = next(it) if final_norm else x2_ref
    fpa_o, fpb_o, fsa_o, fsb_o = (next(it) for _ in range(4))
    acc_scr, ca_scr, cb_scr = (next(it) for _ in range(3))
    i = pl.program_id(0)
    j = pl.program_id(1)
    tm = h2_ref.shape[0]
    ts = tm // db

    @pl.when(j == 0)
    def _zero():
        acc_scr[...] = jnp.zeros(acc_scr.shape, _F32)

    def prompt_body():
        half = tm // 2
        halves = (slice(0, half), slice(half, tm))
        us = [[jnp.dot(h2_ref[rs, :], w_ref[0], preferred_element_type=_F32) for w_ref in (wa_ref, wb_ref)]
              for rs in halves]
        wd = wd_ref[0].astype(_BF16)
        for r, rs in enumerate(halves):
            conv = []
            for idx, (cw_ref, c_scr, fp_o) in enumerate(((cwa_ref, ca_scr, fpa_o), (cwb_ref, cb_scr, fpb_o))):
                u = us[r][idx]
                if r == 0:
                    u1, u2 = _shift_rows_prompt(u, c_scr[j], (i % tps) == 0)
                else:
                    u1, u2 = _shift_rows_prompt(u, us[0][idx][half - _SUBLANES:, :], False)
                    c_scr[j] = u[half - _SUBLANES:, :]
                    fp_o[0] = u[half - _SUBLANES:, :]
                cw = cw_ref[0]
                conv.append(u2 * cw[0:1] + u1 * cw[1:2] + u * cw[2:3])
            gact = ((conv[0] * jax.nn.sigmoid(conv[0])) * conv[1]).astype(_BF16)
            acc_scr[rs, :] += jnp.dot(gact, wd, preferred_element_type=_F32)

    def decode_body():
        h2 = h2_ref[...]
        conv = []
        for w_ref, cw_ref, sf, fp_o, fs_o in ((wa_ref, cwa_ref, sfa_ref, fpa_o, fsa_o),
                                              (wb_ref, cwb_ref, sfb_ref, fpb_o, fsb_o)):
            u = jnp.dot(h2, w_ref[0], preferred_element_type=_F32)
            u1 = jnp.concatenate([sf[0, 1], u[:tm - db, :]], axis=0)
            u2 = jnp.concatenate([sf[0, 0], sf[0, 1], u[:tm - 2 * db, :]], axis=0)
            fp_o[0] = jnp.zeros(fp_o.shape[1:], _F32)
            fs_o[0] = u[tm - 2 * db:tm - db, :]
            fs_o[1] = u[tm - db:, :]
            cw = cw_ref[0]
            conv.append(u2 * cw[0:1] + u1 * cw[1:2] + u * cw[2:3])
        gact = ((conv[0] * jax.nn.sigmoid(conv[0])) * conv[1]).astype(_BF16)
        acc_scr[...] += jnp.dot(gact, wd_ref[0].astype(_BF16), preferred_element_type=_F32)

    pl.when(i < n_pt)(prompt_body)
    pl.when(i >= n_pt)(decode_body)

    def finish(ga, out_ref):
        x2 = x1_ref[...] + ga * acc_scr[...]
        if final_norm:
            x2 = x2 * lax.rsqrt(jnp.mean(x2 * x2, axis=-1, keepdims=True) + _RMS_EPS) * fg_ref[...]
        out_ref[...] = x2

    @pl.when((j == nj - 1) & (i < n_pt))
    def _finish_prompt():
        finish(_mod_row(ga_ref, i, tps, db), x2_ref)

    @pl.when((j == nj - 1) & (i >= n_pt))
    def _finish_decode():
        finish(_mod_tile(ga_ref, db, ts), x2s_ref)


def _ffn(l, h2, x1, mod, ffn_up, ffn_conv, ffn_down, state_ffn_t, final_g, *, tm, tn, n_pt, tps, db, bp):
    rows, d = x1.shape
    f = ffn_down.shape[1]
    nj = f // tn
    bc = mod.shape[2]
    final_norm = final_g is not None
    in_specs = [pl.BlockSpec((tm, d), lambda i, j: (i, 0)),
                pl.BlockSpec((1, d, tn), lambda i, j: (l, 0, j)),
                pl.BlockSpec((1, d, tn), lambda i, j: (l, 0, nj + j)),
                pl.BlockSpec((1, 3, tn), lambda i, j: (l, 0, j)),
                pl.BlockSpec((1, 3, tn), lambda i, j: (l, 0, nj + j)),
                pl.BlockSpec((1, tn, d), lambda i, j: (l, j, 0)),
                pl.BlockSpec((tm, d), lambda i, j: (i, 0)),
                pl.BlockSpec((1, 1, bc, d), lambda i, j: (l, 5, 0, 0)),
                pl.BlockSpec((1, 2, db, tn), lambda i, j: (l, 0, 0, jnp.where(i >= n_pt, j, 0))),
                pl.BlockSpec((1, 2, db, tn), lambda i, j: (l, 0, 0, nj + jnp.where(i >= n_pt, j, 0)))]
    args = [h2, ffn_up, ffn_up, ffn_conv, ffn_conv, ffn_down, x1, mod, state_ffn_t, state_ffn_t]
    if final_norm:
        in_specs.append(pl.BlockSpec((1, d), lambda i, j: (0, 0)))
        args.append(final_g)
    n_tiles = rows // tm
    pstate = pl.BlockSpec((1, _SUBLANES, tn), lambda i, j: (i, 0, j))
    sstate = pl.BlockSpec((2, db, tn), lambda i, j: (0, 0, jnp.where(i >= n_pt, j, 0)))
    if final_norm:
        x_specs = [pl.BlockSpec((tm, d), lambda i, j: (jnp.minimum(i, n_pt - 1), 0)),
                   pl.BlockSpec((tm, d), lambda i, j: (0, 0))]
        x_shapes = [jax.ShapeDtypeStruct((n_pt * tm, d), _F32), jax.ShapeDtypeStruct((tm, d), _F32)]
    else:
        x_specs = [pl.BlockSpec((tm, d), lambda i, j: (i, 0))]
        x_shapes = [jax.ShapeDtypeStruct((rows, d), _F32)]
    return pl.pallas_call(
        functools.partial(_ffn_kernel, n_pt=n_pt, tps=tps, db=db, nj=nj, final_norm=final_norm),
        grid=(rows // tm, nj),
        in_specs=in_specs,
        out_specs=x_specs + [pstate, pstate, sstate, sstate],
        out_shape=x_shapes + [
                   jax.ShapeDtypeStruct((n_tiles, _SUBLANES, f), _F32),
                   jax.ShapeDtypeStruct((n_tiles, _SUBLANES, f), _F32),
                   jax.ShapeDtypeStruct((2, db, f), _F32), jax.ShapeDtypeStruct((2, db, f), _F32)],
        scratch_shapes=[pltpu.VMEM((tm, d), _F32), pltpu.VMEM((nj, _SUBLANES, tn), _F32),
                        pltpu.VMEM((nj, _SUBLANES, tn), _F32)],
        compiler_params=_params("arbitrary", "arbitrary"),
    )(*args)


def _forward(x_prompt, x_sample, c_prompt, c_sample, state_wkv, state_shift, state_conv, state_ffn,
             ada_w, ada_b, norm_g, final_norm_g, w_in, mu_x, mu_rkv, decay_w0, decay_lora1, decay_lora2,
             iclr_a0, iclr_lora1, iclr_lora2, gate_lora1, gate_lora2, vres_v0, vres_lora1, vres_lora2,
             k_k, k_a, r_k, ln_x_w, ln_x_b, conv_w, w_out, ffn_up, ffn_conv, ffn_down,
             *, chunk=_WKV_CHUNK, mix_tn=_MIX_COLS, ffn_tn=_FFN_COLS, wkv_decode_nb=_WKV_DECODE_SEQS,
             wkv_prompt_nb=_WKV_PROMPT_SEQS):
    bp, t_len, d = x_prompt.shape
    db, ts, _ = x_sample.shape
    depth = ada_w.shape[0]
    g_dim = mu_rkv.shape[2]
    tm = ts * db
    assert t_len % tm == 0 and t_len % chunk == 0 and db % _SUBLANES == 0 and bp % wkv_prompt_nb == 0
    assert 2 <= ts <= 4 and ts & (ts - 1) == 0 and db % wkv_decode_nb == 0 and wkv_decode_nb & (wkv_decode_nb - 1) == 0
    tps = t_len // tm
    n_pt = bp * tps
    n_prompt_rows = bp * t_len

    x = _Act(x_prompt.reshape(n_prompt_rows, d), x_sample.transpose(1, 0, 2).reshape(tm, d), 0)
    pad = (-(db + bp)) % _SUBLANES
    c_all = jnp.concatenate([c_sample, c_prompt, jnp.zeros((pad, d), _F32)], axis=0)
    mod = _adaln_mod(c_all, ada_w, ada_b)

    norm_g4 = norm_g.reshape(depth, 2, 1, d)
    vec3 = lambda a: a.reshape(a.shape[0], 1, a.shape[1])
    w0_3, a0_3, v0_3, kk_3, ka_3 = vec3(decay_w0), vec3(iclr_a0), vec3(vres_v0), vec3(k_k), vec3(k_a)
    rk_3, lnw_3, lnb_3 = r_k.reshape(depth, 1, g_dim), vec3(ln_x_w), vec3(ln_x_b)
    state_conv_t = state_conv.transpose(0, 2, 1, 3)
    state_ffn_t = state_ffn.transpose(0, 2, 1, 3)
    tiles = dict(tm=tm, n_pt=n_pt, tps=tps, db=db)
    w_out = _cast_bf16(w_out, 512)
    ffn_up = _cast_bf16(ffn_up, 128)

    new = {k: [] for k in ("wkv_p", "shift_p", "conv_p", "ffn_p", "wkv_s", "shift_s", "conv_s", "ffn_s")}
    v_first = None
    for l in range(depth):
        outs = _norm_lora(l, x, mod, norm_g4, mu_x, state_shift, decay_lora1, iclr_lora1, gate_lora1,
                          vres_lora1, bp=bp, **tiles)
        if l > 0:
            h, aw, aa, ag, av, hp_last, hs_last = outs
        else:
            h, aw, aa, ag, hp_last, hs_last = outs
            av = None
        mouts = _mix(l, h, (aw, aa, ag, av), v_first, w_in,
                     (decay_lora2, iclr_lora2, gate_lora2, vres_lora2), (w0_3, a0_3, v0_3),
                     mu_rkv, kk_3, ka_3, conv_w, state_shift, state_conv_t, tn=mix_tn, bp=bp, **tiles)
        seqs_p, seqs_s, rest = mouts[:7], mouts[7:14], mouts[14:]
        if l > 0:
            o_conv, conv_p8, conv_s = rest
        else:
            o_conv, v_first, conv_p8, conv_s = rest

        o_p, wkv_p = _wkv(l, tuple(a.reshape(bp, t_len, g_dim) for a in seqs_p), chunk, wkv_prompt_nb,
                          rk_3, lnw_3, lnb_3)
        o_p = o_p.reshape(n_prompt_rows, g_dim)
        last = l == depth - 1 and depth > 1
        o_s, wkv_s = _wkv_decode(l, tuple(a.reshape(ts, db, g_dim) for a in seqs_s), state_wkv[l],
                                 new["wkv_s"] if last else [], rk_3, lnw_3, lnb_3, ts=ts, db=db, nb=wkv_decode_nb)
        o_s = o_s.reshape(tm, g_dim)

        x1, h2 = _outproj(l, o_p, o_s, o_conv, x, mod, norm_g4, w_out, **tiles)
        final_g = final_norm_g.reshape(1, d) if l == depth - 1 else None
        *x_out, fpa, fpb, fsa, fsb = _ffn(l, h2, x1, mod, ffn_up, ffn_conv, ffn_down, state_ffn_t, final_g,
                                          tn=ffn_tn, bp=bp, **tiles)
        x = _Act(x_out[0], x_out[0], n_pt) if len(x_out) == 1 else _Act(x_out[0], x_out[1], 0)

        new["wkv_p"].append(wkv_p)
        if last:
            wkv_s_stacked = wkv_s
        else:
            new["wkv_s"].append(wkv_s)
        new["shift_p"].append(hp_last[:, _SUBLANES - 1])
        new["shift_s"].append(hs_last)
        new["conv_p"].append(conv_p8[:, _SUBLANES - 2:])
        new["conv_s"].append(conv_s.transpose(1, 0, 2))
        last_tiles = slice(tps - 1, n_pt, tps)
        new["ffn_p"].append(jnp.concatenate([fpa[last_tiles, _SUBLANES - 2:], fpb[last_tiles, _SUBLANES - 2:]],
                                            axis=-1))
        new["ffn_s"].append(jnp.concatenate([fsa, fsb], axis=-1).transpose(1, 0, 2))

    y_prompt = x.prompt.reshape(bp, t_len, d)
    y_sample = x.decode.reshape(ts, db, d).transpose(1, 0, 2)
    st = {k: jnp.stack(vs) for k, vs in new.items() if k != "wkv_s" or depth == 1}
    if depth > 1:
        st["wkv_s"] = wkv_s_stacked
    return (y_prompt, y_sample, st["wkv_p"], st["shift_p"], st["conv_p"], st["ffn_p"],
            st["wkv_s"], st["shift_s"], st["conv_s"], st["ffn_s"])


def kernel(x_prompt, x_sample, c_prompt, c_sample, state_wkv, state_shift, state_conv, state_ffn, ada_w, ada_b, norm_g, final_norm_g, w_in, mu_x, mu_rkv, decay_w0, decay_lora1, decay_lora2, iclr_a0, iclr_lora1, iclr_lora2, gate_lora1, gate_lora2, vres_v0, vres_lora1, vres_lora2, k_k, k_a, r_k, ln_x_w, ln_x_b, conv_w, w_out, ffn_up, ffn_conv, ffn_down):
    return _forward(x_prompt, x_sample, c_prompt, c_sample, state_wkv, state_shift, state_conv, state_ffn,
                    ada_w, ada_b, norm_g, final_norm_g, w_in, mu_x, mu_rkv, decay_w0, decay_lora1, decay_lora2,
                    iclr_a0, iclr_lora1, iclr_lora2, gate_lora1, gate_lora2, vres_v0, vres_lora1, vres_lora2,
                    k_k, k_a, r_k, ln_x_w, ln_x_b, conv_w, w_out, ffn_up, ffn_conv, ffn_down)
```

```python
import functools
from typing import NamedTuple

import jax
import jax.numpy as jnp
from jax import lax
from jax.experimental import pallas as pl
from jax.experimental.pallas import tpu as pltpu

_F32 = jnp.float32
_BF16 = jnp.bfloat16

_HEAD = 64
_PAIR = 2 * _HEAD
_SEG = 256
_RMS_EPS = 1e-6
_GN_EPS = 64e-5
_VMEM_LIMIT_BYTES = 56 * 1024 * 1024
_SUBLANES = 8

_WKV_CHUNK = 64
_WKV_PROMPT_SEQS = 2
_WKV_DECODE_SEQS = 16
_MIX_COLS = 256
_FFN_COLS = 512


def _params(*sem):
    return pltpu.CompilerParams(dimension_semantics=sem, vmem_limit_bytes=_VMEM_LIMIT_BYTES)


def _dot(a, b):
    return jnp.dot(a.astype(_BF16), b.astype(_BF16), preferred_element_type=_F32)


def _dot_nt(a, b):
    return lax.dot_general(a.astype(_BF16), b.astype(_BF16), (((1,), (1,)), ((), ())),
                           preferred_element_type=_F32)


def _dot_tn(a, b):
    return lax.dot_general(a.astype(_BF16), b.astype(_BF16), (((0,), (0,)), ((), ())),
                           preferred_element_type=_F32)


def _split2(x):
    hi = x.astype(_BF16)
    lo = (x - hi.astype(_F32)).astype(_BF16)
    return hi, lo


def _dot_exact_rhs(x, m_bf16):
    hi, lo = _split2(x)
    return (jnp.dot(hi, m_bf16, preferred_element_type=_F32)
            + jnp.dot(lo, m_bf16, preferred_element_type=_F32))


def _segment_ones(n):
    r = lax.broadcasted_iota(jnp.int32, (n, n), 0) // _HEAD
    c = lax.broadcasted_iota(jnp.int32, (n, n), 1) // _HEAD
    return (r == c).astype(_BF16)


def _head_sums(x, seg_ones):
    n = seg_ones.shape[0]
    parts = [_dot_exact_rhs(x[:, q * n:(q + 1) * n], seg_ones) for q in range(x.shape[1] // n)]
    return parts[0] if len(parts) == 1 else jnp.concatenate(parts, axis=1)


def _tile_rows(m, reps):
    return m if reps == 1 else jnp.concatenate([m] * reps, axis=0)


def _shift_rows_prompt(x, carry8, first, two=True):
    r8 = lax.broadcasted_iota(jnp.int32, (_SUBLANES, 1), 0)
    l1 = jnp.where(first, 0.0, carry8[_SUBLANES - 1:_SUBLANES, :])
    x1 = pltpu.roll(x, 1, 0)
    x1 = jnp.concatenate([jnp.where(r8 == 0, l1, x1[:_SUBLANES, :]), x1[_SUBLANES:, :]], axis=0)
    if not two:
        return x1
    l2 = jnp.where(first, 0.0, carry8[_SUBLANES - 2:_SUBLANES - 1, :])
    x2 = pltpu.roll(x, 2, 0)
    top2 = jnp.where(r8 == 0, l2, jnp.where(r8 == 1, l1, x2[:_SUBLANES, :]))
    return x1, jnp.concatenate([top2, x2[_SUBLANES:, :]], axis=0)


class _Act(NamedTuple):
    prompt: jax.Array
    decode: jax.Array
    decode_block: int


def _act_specs(x, tm, n_pt):
    d = x.prompt.shape[1]
    return [pl.BlockSpec((tm, d), lambda i, *_: (jnp.minimum(i, n_pt - 1), 0)),
            pl.BlockSpec((tm, d), lambda i, *_: (x.decode_block, 0))]


def _mod_row(ref, i, tps, db):
    return ref[0, 0, pl.ds(db + i // tps, 1), :]


def _mod_tile(ref, db, ts):
    return _tile_rows(ref[0, 0, 0:db, :], ts)


def _mod_kernel(c_ref, w_ref, b_ref, o_ref):
    c = c_ref[...]
    s_hi, s_lo = _split2(c * jax.nn.sigmoid(c))
    w = w_ref[0].astype(_BF16)
    acc = jnp.dot(s_hi, w, preferred_element_type=_F32) + jnp.dot(s_lo, w, preferred_element_type=_F32)
    o_ref[0, 0] = acc + b_ref[0]


def _adaln_mod(c_all, ada_w, ada_b):
    depth, d, six_d = ada_w.shape
    bc = c_all.shape[0]
    tn = d
    per = d // tn
    return pl.pallas_call(
        _mod_kernel,
        grid=(depth, six_d // tn),
        in_specs=[pl.BlockSpec((bc, d), lambda l, n: (0, 0)),
                  pl.BlockSpec((1, d, tn), lambda l, n: (l, 0, n)),
                  pl.BlockSpec((1, 1, tn), lambda l, n: (l, 0, n))],
        out_specs=pl.BlockSpec((1, 1, bc, tn), lambda l, n: (l, n // per, 0, n % per)),
        out_shape=jax.ShapeDtypeStruct((depth, 6, bc, d), _F32),
        compiler_params=_params("arbitrary", "arbitrary"),
    )(c_all, ada_w, ada_b.reshape(depth, 1, six_d))


def _norm_lora_kernel(*refs, n_pt, tps, db, has_v):
    n_lora = 4 if has_v else 3
    it = iter(refs)
    xp_ref, xs_ref, sh_ref, sc_ref, g_ref, mu_ref, hl_ref = (next(it) for _ in range(7))
    w_refs = [next(it) for _ in range(n_lora)]
    h_ref = next(it)
    act_refs = [next(it) for _ in range(n_lora)]
    hp_ref, hs_ref, carry_ref, hcur_ref, hprev_ref = (next(it) for _ in range(5))
    i = pl.program_id(0)

    tm = xp_ref.shape[0]
    ts = tm // db
    x = jnp.where(i < n_pt, xp_ref[...], xs_ref[...])
    xn = x * lax.rsqrt(jnp.mean(x * x, axis=-1, keepdims=True) + _RMS_EPS) * g_ref[0, 0]

    @pl.when(i < n_pt)
    def _prompt():
        h = xn * (1.0 + _mod_row(sc_ref, i, tps, db)) + _mod_row(sh_ref, i, tps, db)
        first = (i % tps) == 0
        hprev_ref[...] = _shift_rows_prompt(h, carry_ref[...], first, two=False)
        hcur_ref[...] = h
        carry_ref[...] = h[tm - _SUBLANES:, :]
        hp_ref[0] = h[tm - _SUBLANES:, :]

    @pl.when(i >= n_pt)
    def _decode():
        h = xn * (1.0 + _mod_tile(sc_ref, db, ts)) + _mod_tile(sh_ref, db, ts)
        hprev_ref[...] = jnp.concatenate([hl_ref[0], h[:tm - db, :]], axis=0)
        hcur_ref[...] = h
        hs_ref[...] = h[tm - db:, :]

    h = hcur_ref[...]
    xx = hprev_ref[...] - h
    mu = mu_ref[0]
    h_ref[...] = h.astype(_BF16)
    pre = [_dot(h + xx * mu[g:g + 1], w_refs[g][0]) for g in range(n_lora)]
    act_refs[0][...] = jnp.tanh(pre[0])
    act_refs[1][...] = pre[1]
    act_refs[2][...] = jax.nn.sigmoid(pre[2])
    if has_v:
        act_refs[3][...] = pre[3]


def _norm_lora(l, x, mod, norm_g4, mu_x, state_shift, decay_lora1, iclr_lora1, gate_lora1, vres_lora1,
               *, tm, n_pt, tps, db, bp):
    d = x.prompt.shape[1]
    rows = (n_pt + 1) * tm
    has_v = l > 0
    bc = mod.shape[2]
    ld, la, lg = decay_lora1.shape[2], iclr_lora1.shape[2], gate_lora1.shape[2]
    full = lambda *shape: pl.BlockSpec(shape, lambda i: (l,) + (0,) * (len(shape) - 1))
    in_specs = _act_specs(x, tm, n_pt) + [
                pl.BlockSpec((1, 1, bc, d), lambda i: (l, 0, 0, 0)),
                pl.BlockSpec((1, 1, bc, d), lambda i: (l, 1, 0, 0)),
                pl.BlockSpec((1, 1, 1, d), lambda i: (l, 0, 0, 0)),
                full(1, 4, d), full(1, db, d), full(1, d, ld), full(1, d, la), full(1, d, lg)]
    args = [x.prompt, x.decode, mod, mod, norm_g4, mu_x, state_shift, decay_lora1, iclr_lora1, gate_lora1]
    row_out = lambda n, dt: (pl.BlockSpec((tm, n), lambda i: (i, 0)), jax.ShapeDtypeStruct((rows, n), dt))
    outs = [row_out(d, _BF16), row_out(ld, _F32), row_out(la, _F32), row_out(lg, _F32)]
    if has_v:
        lv = vres_lora1.shape[2]
        in_specs.append(pl.BlockSpec((1, d, lv), lambda i: (l - 1, 0, 0)))
        args.append(vres_lora1)
        outs.append(row_out(lv, _F32))
    outs.append((pl.BlockSpec((1, _SUBLANES, d), lambda i: (jnp.minimum(i // tps, bp - 1), 0, 0)),
                 jax.ShapeDtypeStruct((bp, _SUBLANES, d), _F32)))
    outs.append((pl.BlockSpec((db, d), lambda i: (0, 0)), jax.ShapeDtypeStruct((db, d), _F32)))
    return pl.pallas_call(
        functools.partial(_norm_lora_kernel, n_pt=n_pt, tps=tps, db=db, has_v=has_v),
        grid=(rows // tm,),
        in_specs=in_specs,
        out_specs=[o[0] for o in outs],
        out_shape=[o[1] for o in outs],
        scratch_shapes=[pltpu.VMEM((_SUBLANES, d), _F32), pltpu.VMEM((tm, d), _F32), pltpu.VMEM((tm, d), _F32)],
        compiler_params=_params("arbitrary"),
    )(*args)


def _mix_kernel(*refs, n_pt, tps, db, has_v):
    it = iter(refs)
    h_ref = next(it)
    w_refs = [next(it) for _ in range(6)]
    aw_ref, aa_ref, ag_ref = next(it), next(it), next(it)
    av_ref = next(it) if has_v else None
    d2_ref, i2_ref, g2_ref = next(it), next(it), next(it)
    v2_ref = next(it) if has_v else None
    w0_ref, a0_ref = next(it), next(it)
    v0_ref = next(it) if has_v else None
    mu_ref, kkw_ref, kaw_ref, cw_ref, hl_ref, cs_ref = (next(it) for _ in range(6))
    vf_ref = next(it) if has_v else None
    seq_prompt = [next(it) for _ in range(7)]
    seq_decode = [next(it) for _ in range(7)]
    oc_o = next(it)
    vf_o = None if has_v else next(it)
    cp_o, cso_o = next(it), next(it)
    wc_scr, pcarry, zcarry = (next(it) for _ in range(3))

    i = pl.program_id(1)
    tm = h_ref.shape[0]
    tn = oc_o.shape[1]

    @pl.when(i == 0)
    def _cache_weights():
        for g in range(6):
            wc_scr[g] = w_refs[g][0].astype(_BF16)

    def project(rs):
        h = h_ref[rs, :]
        return [jnp.dot(h, wc_scr[g], preferred_element_type=_F32) for g in range(6)]

    def tail(rs, p, pprev, bg, z, z1, z2, seq_out):
        r_o, lw_o, k_o, v_o, kk_o, kka_o, g_o = seq_out
        mu = mu_ref[0]
        r = p[0] + (pprev[0] - p[0]) * mu[0:1]
        k = p[1] + (pprev[1] - p[1]) * mu[1:2]
        v = p[2] + (pprev[2] - p[2]) * mu[2:3]

        zlog = w0_ref[0] + _dot(aw_ref[rs, :], d2_ref[0])
        softplus = jnp.maximum(-zlog, 0.0) + jnp.log(1.0 + jnp.exp(-jnp.abs(zlog)))
        lw_o[rs, :] = -jnp.exp(-softplus - 0.5)
        a = jax.nn.sigmoid(a0_ref[0] + _dot(aa_ref[rs, :], i2_ref[0]))
        g_o[rs, :] = _dot(ag_ref[rs, :], g2_ref[0])
        if has_v:
            nu = jax.nn.sigmoid(v0_ref[0] + _dot(av_ref[rs, :], v2_ref[0]))
            v = v + (vf_ref[rs, :] - v) * nu
        else:
            vf_o[rs, :] = v

        kk = k * kkw_ref[0]
        norm = jnp.sqrt(_head_sums(kk * kk, _segment_ones(tn)))
        kk = kk / jnp.maximum(norm, 1e-12)
        r_o[rs, :] = r
        k_o[rs, :] = k * (1.0 + (a - 1.0) * kaw_ref[0])
        v_o[rs, :] = v
        kk_o[rs, :] = kk
        kka_o[rs, :] = kk * a

        cw = cw_ref[0]
        zc = z2 * cw[0:1] + z1 * cw[1:2] + z * cw[2:3]
        oc_o[rs, :] = (bg * zc).astype(_BF16)

    @pl.when(i < n_pt)
    def _prompt():
        half = tm // 2
        halves = (slice(0, half), slice(half, tm))
        proj = [project(rs) for rs in halves]
        zs = [q[4] * q[5] for q in proj]
        for idx, rs in enumerate(halves):
            p, z = proj[idx][:3], zs[idx]
            if idx == 0:
                first = (i % tps) == 0
                p_carry = [pcarry[g] for g in range(3)]
                z_carry = zcarry[...]
            else:
                first = False
                p_carry = [q[half - _SUBLANES:, :] for q in proj[0][:3]]
                z_carry = zs[0][half - _SUBLANES:, :]
            pprev = [_shift_rows_prompt(p[g], p_carry[g], first, two=False) for g in range(3)]
            z1, z2 = _shift_rows_prompt(z, z_carry, first)
            tail(rs, p, pprev, proj[idx][3], z, z1, z2, seq_prompt)
        for g in range(3):
            pcarry[g] = proj[1][g][half - _SUBLANES:, :]
        zcarry[...] = zs[1][half - _SUBLANES:, :]
        cp_o[0] = zs[1][half - _SUBLANES:, :]

    @pl.when(i >= n_pt)
    def _decode():
        rs = slice(0, tm)
        proj = project(rs)
        p, z = proj[:3], proj[4] * proj[5]
        hl = hl_ref[0].astype(_BF16)
        pprev = [jnp.concatenate([jnp.dot(hl, wc_scr[g], preferred_element_type=_F32), p[g][:tm - db, :]], axis=0)
                 for g in range(3)]
        z1 = jnp.concatenate([cs_ref[0, 1], z[:tm - db, :]], axis=0)
        z2 = jnp.concatenate([cs_ref[0, 0], cs_ref[0, 1], z[:tm - 2 * db, :]], axis=0)
        cso_o[0] = z[tm - 2 * db:tm - db, :]
        cso_o[1] = z[tm - db:, :]
        tail(rs, p, pprev, proj[3], z, z1, z2, seq_decode)


def _mix(l, h, acts, vf_in, w_in, lora2, vecs, mu_rkv, k_k3, k_a3, conv_w, state_shift, state_conv_t,
         *, tm, tn, n_pt, tps, db, bp):
    rows, d = h.shape
    has_v = l > 0
    g_dim = mu_rkv.shape[2]
    nj = g_dim // tn
    aw, aa, ag, av = acts
    d2, i2, g2, v2 = lora2
    w0, a0, v0 = vecs
    row_in = lambda arr: pl.BlockSpec((tm, arr.shape[1]), lambda j, i: (i, 0))
    col3 = lambda arr, ll: pl.BlockSpec((1, arr.shape[1], tn), lambda j, i: (ll, 0, j))
    in_specs = [row_in(h)] + [pl.BlockSpec((1, d, tn), lambda j, i, g=g: (l, 0, g * nj + j)) for g in range(6)]
    args = [h] + [w_in] * 6
    in_specs += [row_in(aw), row_in(aa), row_in(ag)]
    args += [aw, aa, ag]
    if has_v:
        in_specs.append(row_in(av))
        args.append(av)
    in_specs += [col3(d2, l), col3(i2, l), col3(g2, l)]
    args += [d2, i2, g2]
    if has_v:
        in_specs.append(col3(v2, l - 1))
        args.append(v2)
    in_specs += [col3(w0, l), col3(a0, l)]
    args += [w0, a0]
    if has_v:
        in_specs.append(col3(v0, l - 1))
        args.append(v0)
    in_specs += [col3(mu_rkv, l), col3(k_k3, l), col3(k_a3, l), col3(conv_w, l),
                 pl.BlockSpec((1, db, d), lambda j, i: (l, 0, 0)),
                 pl.BlockSpec((1, 2, db, tn), lambda j, i: (l, 0, 0, j))]
    args += [mu_rkv, k_k3, k_a3, conv_w, state_shift, state_conv_t]
    if has_v:
        in_specs.append(pl.BlockSpec((tm, tn), lambda j, i: (i, j)))
        args.append(vf_in)
    tile = pl.BlockSpec((tm, tn), lambda j, i: (i, j))
    prompt_tile = pl.BlockSpec((tm, tn), lambda j, i: (jnp.minimum(i, n_pt - 1), j))
    decode_tile = pl.BlockSpec((tm, tn), lambda j, i: (0, j), pipeline_mode=pl.Buffered(1))
    outs = [(prompt_tile, jax.ShapeDtypeStruct((n_pt * tm, g_dim), _F32)) for _ in range(7)]
    outs += [(decode_tile, jax.ShapeDtypeStruct((tm, g_dim), _F32)) for _ in range(7)]
    outs.append((tile, jax.ShapeDtypeStruct((rows, g_dim), _BF16)))
    if not has_v:
        outs.append((tile, jax.ShapeDtypeStruct((rows, g_dim), _F32)))
    outs.append((pl.BlockSpec((1, _SUBLANES, tn), lambda j, i: (jnp.minimum(i // tps, bp - 1), 0, j)),
                 jax.ShapeDtypeStruct((bp, _SUBLANES, g_dim), _F32)))
    outs.append((pl.BlockSpec((2, db, tn), lambda j, i: (0, 0, j)), jax.ShapeDtypeStruct((2, db, g_dim), _F32)))
    return pl.pallas_call(
        functools.partial(_mix_kernel, n_pt=n_pt, tps=tps, db=db, has_v=has_v),
        grid=(nj, rows // tm),
        in_specs=in_specs,
        out_specs=[o[0] for o in outs],
        out_shape=[o[1] for o in outs],
        scratch_shapes=[pltpu.VMEM((6, d, tn), _BF16), pltpu.VMEM((3, _SUBLANES, tn), _F32),
                        pltpu.VMEM((_SUBLANES, tn), _F32)],
        compiler_params=_params("arbitrary", "arbitrary"),
    )(*args)


def _unit_lower_inverse_minus_identity(lmats, ri, ci, chunk, mul):
    same8 = (ri >> 3) == (ci >> 3)
    d8 = [jnp.where(same8, m, 0.0) for m in lmats]
    d2 = [mul(a, a) for a in d8]
    d3 = [mul(a, b) for a, b in zip(d8, d2)]
    d4 = [mul(b, b) for b in d2]
    x = [a + b + cc for a, b, cc in zip(d8, d2, d3)]
    xd4 = [mul(a, b) for a, b in zip(x, d4)]
    x = [a + b + cc for a, b, cc in zip(x, d4, xd4)]
    size = 16
    while size <= chunk:
        sh = size.bit_length() - 1
        level = ((ri >> sh) == (ci >> sh)) & ((ri >> (sh - 1)) != (ci >> (sh - 1)))
        e = [jnp.where(level, m, 0.0) for m in lmats]
        y = [b + mul(a, b) for a, b in zip(x, e)]
        x = [a + b + mul(b, a) for a, b in zip(x, y)]
        size *= 2
    return x


def _wkv_kernel(r_ref, lw_ref, k_ref, v_ref, kk_ref, kka_ref, g_ref, rk_ref, lnw_ref, lnb_ref,
                o_ref, so_ref, s_scr, *, n_chunks):
    ci_grid = pl.program_id(1)
    nb, c, g_dim = r_ref.shape
    npair = g_dim // _PAIR
    n2 = 2 * c

    lane = lax.broadcasted_iota(jnp.int32, (1, _PAIR), 1)
    m0 = (lane < _HEAD).astype(_F32)
    m1 = 1.0 - m0

    @pl.when(ci_grid == 0)
    def _init():
        s_scr[...] = jnp.zeros(s_scr.shape, _F32)

    ri = lax.broadcasted_iota(jnp.int32, (n2, n2), 0) & (c - 1)
    ci = lax.broadcasted_iota(jnp.int32, (n2, n2), 1) & (c - 1)
    strict = ri > ci
    incl = ri >= ci
    row_id = lax.broadcasted_iota(jnp.int32, (c, 1), 0)
    seg_ones = _segment_ones(_SEG)

    def stack(x):
        return jnp.concatenate([x * m0, x * m1], axis=0)

    units =[(u, p) for u in range(nb) for p in range(npair)]
    sl = lambda p: slice(p * _PAIR, (p + 1) * _PAIR)

    seq = []
    for u in range(nb):
        lw = lw_ref[u]
        cl = lw
        step = 1
        while step < c:
            cl = cl + jnp.where(row_id >= step, pltpu.roll(cl, step, 0), 0.0)
            step *= 2
        cl_end = cl[c - 1:c, :]
        r, k, v, kk, kka = r_ref[u], k_ref[u], v_ref[u], kk_ref[u], kka_ref[u]
        e_neg = jnp.exp(-cl)
        e_tail = jnp.exp(cl_end - cl)
        seq.append(dict(r=r, k=k, v=v, p_end=jnp.exp(cl_end), rt=r * jnp.exp(cl), at=-(kk * jnp.exp(cl - lw)),
                        kt=k * e_neg, bt=kka * e_neg, kh=k * e_tail, bh=kka * e_tail))

    ar2 = [jnp.concatenate([stack(seq[u]["at"][:, sl(p)]), stack(seq[u]["rt"][:, sl(p)])], axis=0).astype(_BF16)
           for u, p in units]
    mb = [_dot_nt(a, stack(seq[u]["bt"][:, sl(p)])) for a, (u, p) in zip(ar2, units)]
    mk = [_dot_nt(a, stack(seq[u]["kt"][:, sl(p)])) for a, (u, p) in zip(ar2, units)]
    s_old = [s_scr[u, p] for u, p in units]
    uy0 = [_dot_nt(jnp.concatenate([seq[u]["at"][:, sl(p)], seq[u]["rt"][:, sl(p)]], axis=0), s)
           for (u, p), s in zip(units, s_old)]
    x_inv = _unit_lower_inverse_minus_identity([jnp.where(strict, m[:n2], 0.0) for m in mb], ri, ci, c, _dot)
    vs = [stack(seq[u]["v"][:, sl(p)]).astype(_BF16) for u, p in units]
    ws = [stack(q[:c]) + _dot(jnp.where(strict, m[:n2], 0.0), vv) for q, m, vv in zip(uy0, mk, vs)]
    us = [(w + _dot(xi, w)).astype(_BF16) for xi, w in zip(x_inv, ws)]
    ysd = [_dot(jnp.where(incl, m1_[n2:], 0.0), uu) + _dot(jnp.where(incl, m2_[n2:], 0.0), vv)
           for m1_, m2_, uu, vv in zip(mb, mk, us, vs)]
    ys = [q[c:] + d[:c] + d[c:] for q, d in zip(uy0, ysd)]
    for idx, (u, p) in enumerate(units):
        uv = jnp.concatenate([us[idx], vs[idx]], axis=0)
        bk = jnp.concatenate([stack(seq[u]["bh"][:, sl(p)]), stack(seq[u]["kh"][:, sl(p)])], axis=0)
        s_scr[u, p] = s_old[idx] * seq[u]["p_end"][:, sl(p)] + _dot_tn(uv, bk)
    for u in range(nb):
        y = jnp.concatenate(ys[u * npair:(u + 1) * npair], axis=1)
        mean = _head_sums(y, seg_ones) * (1.0 / _HEAD)
        dy = y - mean
        var = _head_sums(dy * dy, seg_ones) * (1.0 / _HEAD)
        yn = dy * lax.rsqrt(var + _GN_EPS) * lnw_ref[...] + lnb_ref[...]
        bonus = _head_sums(seq[u]["r"] * seq[u]["k"] * rk_ref[...], seg_ones) * seq[u]["v"]
        o_ref[u] = ((yn + bonus) * g_ref[u]).astype(o_ref.dtype)

    @pl.when(ci_grid == n_chunks - 1)
    def _final():
        for u, p in units:
            s = s_scr[u, p]
            so_ref[u, 2 * p] = s[:_HEAD, :_HEAD]
            so_ref[u, 2 * p + 1] = s[_HEAD:, _HEAD:]


def _wkv(l, seqs, chunk, nb, r_k3, ln_w3, ln_b3):
    n_seq, t_len, g_dim = seqs[0].shape
    n_heads = g_dim // _HEAD
    n_chunks = t_len // chunk
    blk = pl.BlockSpec((nb, chunk, g_dim), lambda bi, ci: (bi, ci, 0))
    vec = pl.BlockSpec((None, 1, g_dim), lambda bi, ci: (l, 0, 0))
    return pl.pallas_call(
        functools.partial(_wkv_kernel, n_chunks=n_chunks),
        grid=(n_seq // nb, n_chunks),
        in_specs=[blk] * 7 + [vec] * 3,
        out_specs=[blk, pl.BlockSpec((nb, n_heads, _HEAD, _HEAD), lambda bi, ci: (bi, 0, 0, 0))],
        out_shape=[jax.ShapeDtypeStruct((n_seq, t_len, g_dim), _BF16),
                   jax.ShapeDtypeStruct((n_seq, n_heads, _HEAD, _HEAD), _F32)],
        scratch_shapes=[pltpu.VMEM((nb, g_dim // _PAIR, _PAIR, _PAIR), _F32)],
        compiler_params=_params("arbitrary", "arbitrary"),
    )(*seqs, r_k3, ln_w3, ln_b3)


def _wkv_decode_kernel(*refs, ts, nb, n_earlier):
    (r_ref, lw_ref, k_ref, v_ref, kk_ref, kka_ref, g_ref, rk_ref, lnw_ref, lnb_ref, s0_ref) = refs[:11]
    earlier_refs = refs[11:11 + n_earlier]
    o_ref, so_ref = refs[11 + n_earlier:]
    if n_earlier:
        for idx, e_ref in enumerate(earlier_refs):
            so_ref[idx] = e_ref[...]
        so_ref = so_ref.at[n_earlier]
    g_dim = r_ref.shape[2]
    npair = g_dim // _PAIR
    n = ts * nb
    n2 = 2 * n
    nb_bits = nb.bit_length() - 1

    lane = lax.broadcasted_iota(jnp.int32, (1, _PAIR), 1)
    m0 = (lane < _HEAD).astype(_F32)
    m1 = 1.0 - m0

    def stack(x):
        return jnp.concatenate([x * m0, x * m1], axis=0)

    def rows_tb(ref):
        return jnp.concatenate([ref[t] for t in range(ts)], axis=0)

    lw_t = [lw_ref[t] for t in range(ts)]
    cl_t = [lw_t[0]]
    for t in range(1, ts):
        cl_t.append(cl_t[-1] + lw_t[t])
    cl = jnp.concatenate(cl_t, axis=0)
    lw = jnp.concatenate(lw_t, axis=0)
    cl_end_b = cl_t[-1]
    cl_end = _tile_rows(cl_end_b, ts)
    p_end = jnp.exp(cl_end_b)
    r, k, v, kk, kka = (rows_tb(ref) for ref in (r_ref, k_ref, v_ref, kk_ref, kka_ref))
    e_neg = jnp.exp(-cl)
    e_tail = jnp.exp(cl_end - cl)
    rt = r * jnp.exp(cl)
    at = -(kk * jnp.exp(cl - lw))
    kt = k * e_neg
    bt = kka * e_neg
    kh = k * e_tail
    bh = kka * e_tail

    ri = lax.broadcasted_iota(jnp.int32, (n2, n2), 0)
    ci = lax.broadcasted_iota(jnp.int32, (n2, n2), 1)
    same_seq = (ri & (nb - 1)) == (ci & (nb - 1))
    t_r = (ri & (n - 1)) >> nb_bits
    t_c = (ci & (n - 1)) >> nb_bits
    strict = same_seq & (t_r > t_c)
    incl = same_seq & (t_r >= t_c)
    seq_of_row2 = lax.broadcasted_iota(jnp.int32, (n2, 1), 0) & (nb - 1)
    seq_of_row4 = lax.broadcasted_iota(jnp.int32, (2 * n2, 1), 0) & (nb - 1)
    zero = jnp.zeros((_HEAD, _HEAD), _F32)

    pairs = range(npair)
    sls = [slice(p * _PAIR, (p + 1) * _PAIR) for p in pairs]
    ar2 = [jnp.concatenate([stack(at[:, sl]), stack(rt[:, sl])], axis=0).astype(_BF16) for sl in sls]
    mb = [_dot_nt(a, stack(bt[:, sl])) for a, sl in zip(ar2, sls)]
    mk = [_dot_nt(a, stack(kt[:, sl])) for a, sl in zip(ar2, sls)]
    lab = [jnp.where(strict, m[:n2], 0.0) for m in mb]
    d2 = [_dot(a, a) for a in lab]
    d3 = [_dot(a, b) for a, b in zip(lab, d2)]
    x_inv = [a + b + cc for a, b, cc in zip(lab, d2, d3)]

    def block_diag(b, p):
        top = jnp.concatenate([s0_ref[b, 2 * p], zero], axis=1)
        bot = jnp.concatenate([zero, s0_ref[b, 2 * p + 1]], axis=1)
        return jnp.concatenate([top, bot], axis=0)

    ys = []
    for p, sl in zip(pairs, sls):
        s_b = [block_diag(b, p) for b in range(nb)]
        ar = jnp.concatenate([at[:, sl], rt[:, sl]], axis=0)
        ar_cat = jnp.concatenate([jnp.where(seq_of_row2 == b, ar, 0.0).astype(_BF16) for b in range(nb)], axis=1)
        s_cat = jnp.concatenate([s.astype(_BF16) for s in s_b], axis=1)
        uy0 = _dot_nt(ar_cat, s_cat)
        vs = stack(v[:, sl]).astype(_BF16)
        ws = stack(uy0[:n]) + _dot(jnp.where(strict, mk[p][:n2], 0.0), vs)
        us = (ws + _dot(x_inv[p], ws)).astype(_BF16)
        ysd = _dot(jnp.where(incl, mb[p][n2:], 0.0), us) + _dot(jnp.where(incl, mk[p][n2:], 0.0), vs)
        ys.append(uy0[n:] + ysd[:n] + ysd[n:])
        uv = jnp.concatenate([us, vs], axis=0)
        uv_cat = jnp.concatenate([jnp.where(seq_of_row4 == b, uv, jnp.zeros_like(uv)) for b in range(nb)], axis=1)
        bk = jnp.concatenate([stack(bh[:, sl]), stack(kh[:, sl])], axis=0)
        upd = _dot_tn(uv_cat, bk)
        for b in range(nb):
            s_new = (s_b[b] * p_end[b:b + 1, sl] + upd[b * _PAIR:(b + 1) * _PAIR, :])
            so_ref[b, 2 * p] = s_new[:_HEAD, :_HEAD]
            so_ref[b, 2 * p + 1] = s_new[_HEAD:, _HEAD:]

    y = jnp.concatenate(ys, axis=1)
    seg_ones = _segment_ones(_SEG)
    mean = _head_sums(y, seg_ones) * (1.0 / _HEAD)
    dy = y - mean
    var = _head_sums(dy * dy, seg_ones) * (1.0 / _HEAD)
    yn = dy * lax.rsqrt(var + _GN_EPS) * lnw_ref[...] + lnb_ref[...]
    bonus = _head_sums(r * k * rk_ref[...], seg_ones) * v
    out = ((yn + bonus) * rows_tb(g_ref)).astype(o_ref.dtype)
    for t in range(ts):
        o_ref[t] = out[t * nb:(t + 1) * nb, :]


def _wkv_decode(l, seqs, state0, earlier_states, r_k3, ln_w3, ln_b3, *, ts, db, nb):
    g_dim = seqs[0].shape[2]
    n_heads = g_dim // _HEAD
    n_earlier = len(earlier_states)
    head_groups = 2 if n_earlier else 1
    gw, hw = g_dim // head_groups, n_heads // head_groups
    blk = pl.BlockSpec((ts, nb, gw), lambda bi, hg: (0, bi, hg))
    vec = pl.BlockSpec((None, 1, gw), lambda bi, hg: (l, 0, hg))
    st = pl.BlockSpec((nb, hw, _HEAD, _HEAD), lambda bi, hg: (bi, hg, 0, 0))
    if n_earlier:
        st_out = pl.BlockSpec((n_earlier + 1, nb, hw, _HEAD, _HEAD), lambda bi, hg: (0, bi, hg, 0, 0))
        st_shape = (n_earlier + 1, db, n_heads, _HEAD, _HEAD)
    else:
        st_out, st_shape = st, (db, n_heads, _HEAD, _HEAD)
    return pl.pallas_call(
        functools.partial(_wkv_decode_kernel, ts=ts, nb=nb, n_earlier=n_earlier),
        grid=(db // nb, head_groups),
        in_specs=[blk] * 7 + [vec] * 3 + [st] * (1 + n_earlier),
        out_specs=[blk, st_out],
        out_shape=[jax.ShapeDtypeStruct((ts, db, g_dim), _BF16), jax.ShapeDtypeStruct(st_shape, _F32)],
        compiler_params=_params("arbitrary", "arbitrary"),
    )(*seqs, r_k3, ln_w3, ln_b3, state0, *earlier_states)


def _cast_kernel(w_ref, o_ref):
    o_ref[...] = w_ref[...].astype(o_ref.dtype)


def _cast_bf16(w, rows_per_block):
    depth, k, n = w.shape
    spec = pl.BlockSpec((1, rows_per_block, n), lambda l, i: (l, i, 0))
    return pl.pallas_call(
        _cast_kernel, grid=(depth, k // rows_per_block), in_specs=[spec], out_specs=spec,
        out_shape=jax.ShapeDtypeStruct(w.shape, _BF16),
        compiler_params=_params("arbitrary", "arbitrary"),
    )(w)


def _outproj_kernel(op_ref, os_ref, oc_ref, xp_ref, xs_ref, ga_ref, sh_ref, sc_ref, g_ref, w_ref,
                    x1_ref, h2_ref, *, n_pt, tps, db):
    i = pl.program_id(0)
    tm = xp_ref.shape[0]
    ts = tm // db
    g_dim = op_ref.shape[1]

    def rows_out(rs, o_ref, x_ref, ga, sc, sh):
        acc = (jnp.dot(o_ref[rs, :], w_ref[0, :g_dim, :], preferred_element_type=_F32)
               + jnp.dot(oc_ref[rs, :], w_ref[0, g_dim:, :], preferred_element_type=_F32))
        x1 = x_ref[rs, :] + ga * acc
        x1_ref[rs, :] = x1
        xn = x1 * lax.rsqrt(jnp.mean(x1 * x1, axis=-1, keepdims=True) + _RMS_EPS) * g_ref[0, 0]
        h2_ref[rs, :] = (xn * (1.0 + sc) + sh).astype(_BF16)

    half = tm // 2
    halves = (slice(0, half), slice(half, tm))

    @pl.when(i < n_pt)
    def _prompt():
        ga, sc, sh = (_mod_row(ref, i, tps, db) for ref in (ga_ref, sc_ref, sh_ref))
        for rs in halves:
            rows_out(rs, op_ref, xp_ref, ga, sc, sh)

    @pl.when(i >= n_pt)
    def _decode():
        ga, sc, sh = (_mod_tile(ref, db, ts) for ref in (ga_ref, sc_ref, sh_ref))
        for rs in halves:
            rows_out(rs, os_ref, xs_ref, ga[rs, :], sc[rs, :], sh[rs, :])


def _outproj(l, o_p, o_s, o_conv, x, mod, norm_g4, w_out, *, tm, n_pt, tps, db):
    d = x.prompt.shape[1]
    rows = (n_pt + 1) * tm
    g_dim = o_p.shape[1]
    bc = mod.shape[2]
    modspec = lambda comp: pl.BlockSpec((1, 1, bc, d), lambda i: (l, comp, 0, 0))
    return pl.pallas_call(
        functools.partial(_outproj_kernel, n_pt=n_pt, tps=tps, db=db),
        grid=(rows // tm,),
        in_specs=[pl.BlockSpec((tm, g_dim), lambda i: (jnp.minimum(i, n_pt - 1), 0)),
                  pl.BlockSpec((tm, g_dim), lambda i: (0, 0)),
                  pl.BlockSpec((tm, o_conv.shape[1]), lambda i: (i, 0))] + _act_specs(x, tm, n_pt) + [
                  modspec(2), modspec(3), modspec(4),
                  pl.BlockSpec((1, 1, 1, d), lambda i: (l, 1, 0, 0)),
                  pl.BlockSpec((1, d, d), lambda i: (l, 0, 0), pipeline_mode=pl.Buffered(1))],
        out_specs=[pl.BlockSpec((tm, d), lambda i: (i, 0)), pl.BlockSpec((tm, d), lambda i: (i, 0))],
        out_shape=[jax.ShapeDtypeStruct((rows, d), _F32), jax.ShapeDtypeStruct((rows, d), _BF16)],
        compiler_params=_params("arbitrary"),
    )(o_p, o_s, o_conv, x.prompt, x.decode, mod, mod, mod, norm_g4, w_out)


def _ffn_kernel(*refs, n_pt, tps, db, nj, final_norm):
    it = iter(refs)
    (h2_ref, wa_ref, wb_ref, cwa_ref, cwb_ref, wd_ref, x1_ref, ga_ref, sfa_ref, sfb_ref) = (next(it) for _ in range(10))
    fg_ref = next(it) if final_norm else None
    x2_ref = next(it)
    x2s_ref = next(it) if final_norm else x2_ref
    fpa_o, fpb_o, fsa_o, fsb_o = (next(it) for _ in range(4))
    acc_scr, ca_scr, cb_scr = (next(it) for _ in range(3))
    i = pl.program_id(0)
    j = pl.program_id(1)
    tm = h2_ref.shape[0]
    ts = tm // db

    @pl.when(j == 0)
    def _zero():
        acc_scr[...] = jnp.zeros(acc_scr.shape, _F32)

    def prompt_body():
        half = tm // 2
        halves = (slice(0, half), slice(half, tm))
        us = [[jnp.dot(h2_ref[rs, :], w_ref[0], preferred_element_type=_F32) for w_ref in (wa_ref, wb_ref)]
              for rs in halves]
        wd = wd_ref[0].astype(_BF16)
        for r, rs in enumerate(halves):
            conv = []
            for idx, (cw_ref, c_scr, fp_o) in enumerate(((cwa_ref, ca_scr, fpa_o), (cwb_ref, cb_scr, fpb_o))):
                u = us[r][idx]
                if r == 0:
                    u1, u2 = _shift_rows_prompt(u, c_scr[j], (i % tps) == 0)
                else:
                    u1, u2 = _shift_rows_prompt(u, us[0][idx][half - _SUBLANES:, :], False)
                    c_scr[j] = u[half - _SUBLANES:, :]
                    fp_o[0] = u[half - _SUBLANES:, :]
                cw = cw_ref[0]
                conv.append(u2 * cw[0:1] + u1 * cw[1:2] + u * cw[2:3])
            gact = ((conv[0] * jax.nn.sigmoid(conv[0])) * conv[1]).astype(_BF16)
            acc_scr[rs, :] += jnp.dot(gact, wd, preferred_element_type=_F32)

    def decode_body():
        h2 = h2_ref[...]
        conv = []
        for w_ref, cw_ref, sf, fp_o, fs_o in ((wa_ref, cwa_ref, sfa_ref, fpa_o, fsa_o),
                                              (wb_ref, cwb_ref, sfb_ref, fpb_o, fsb_o)):
            u = jnp.dot(h2, w_ref[0], preferred_element_type=_F32)
            u1 = jnp.concatenate([sf[0, 1], u[:tm - db, :]], axis=0)
            u2 = jnp.concatenate([sf[0, 0], sf[0, 1], u[:tm - 2 * db, :]], axis=0)
            fp_o[0] = jnp.zeros(fp_o.shape[1:], _F32)
            fs_o[0] = u[tm - 2 * db:tm - db, :]
            fs_o[1] = u[tm - db:, :]
            cw = cw_ref[0]
            conv.append(u2 * cw[0:1] + u1 * cw[1:2] + u * cw[2:3])
        gact = ((conv[0] * jax.nn.sigmoid(conv[0])) * conv[1]).astype(_BF16)
        acc_scr[...] += jnp.dot(gact, wd_ref[0].astype(_BF16), preferred_element_type=_F32)

    pl.when(i < n_pt)(prompt_body)
    pl.when(i >= n_pt)(decode_body)

    def finish(ga, out_ref):
        x2 = x1_ref[...] + ga * acc_scr[...]
        if final_norm:
            x2 = x2 * lax.rsqrt(jnp.mean(x2 * x2, axis=-1, keepdims=True) + _RMS_EPS) * fg_ref[...]
        out_ref[...] = x2

    @pl.when((j == nj - 1) & (i < n_pt))
    def _finish_prompt():
        finish(_mod_row(ga_ref, i, tps, db), x2_ref)

    @pl.when((j == nj - 1) & (i >= n_pt))
    def _finish_decode():
        finish(_mod_tile(ga_ref, db, ts), x2s_ref)


def _ffn(l, h2, x1, mod, ffn_up, ffn_conv, ffn_down, state_ffn_t, final_g, *, tm, tn, n_pt, tps, db, bp):
    rows, d = x1.shape
    f = ffn_down.shape[1]
    nj = f // tn
    bc = mod.shape[2]
    final_norm = final_g is not None
    in_specs = [pl.BlockSpec((tm, d), lambda i, j: (i, 0)),
                pl.BlockSpec((1, d, tn), lambda i, j: (l, 0, j)),
                pl.BlockSpec((1, d, tn), lambda i, j: (l, 0, nj + j)),
                pl.BlockSpec((1, 3, tn), lambda i, j: (l, 0, j)),
                pl.BlockSpec((1, 3, tn), lambda i, j: (l, 0, nj + j)),
                pl.BlockSpec((1, tn, d), lambda i, j: (l, j, 0)),
                pl.BlockSpec((tm, d), lambda i, j: (i, 0)),
                pl.BlockSpec((1, 1, bc, d), lambda i, j: (l, 5, 0, 0)),
                pl.BlockSpec((1, 2, db, tn), lambda i, j: (l, 0, 0, jnp.where(i >= n_pt, j, 0))),
                pl.BlockSpec((1, 2, db, tn), lambda i, j: (l, 0, 0, nj + jnp.where(i >= n_pt, j, 0)))]
    args = [h2, ffn_up, ffn_up, ffn_conv, ffn_conv, ffn_down, x1, mod, state_ffn_t, state_ffn_t]
    if final_norm:
        in_specs.append(pl.BlockSpec((1, d), lambda i, j: (0, 0)))
        args.append(final_g)
    n_tiles = rows // tm
    pstate = pl.BlockSpec((1, _SUBLANES, tn), lambda i, j: (i, 0, j))
    sstate = pl.BlockSpec((2, db, tn), lambda i, j: (0, 0, jnp.where(i >= n_pt, j, 0)))
    if final_norm:
        x_specs = [pl.BlockSpec((tm, d), lambda i, j: (jnp.minimum(i, n_pt - 1), 0)),
                   pl.BlockSpec((tm, d), lambda i, j: (0, 0))]
        x_shapes = [jax.ShapeDtypeStruct((n_pt * tm, d), _F32), jax.ShapeDtypeStruct((tm, d), _F32)]
    else:
        x_specs = [pl.BlockSpec((tm, d), lambda i, j: (i, 0))]
        x_shapes = [jax.ShapeDtypeStruct((rows, d), _F32)]
    return pl.pallas_call(
        functools.partial(_ffn_kernel, n_pt=n_pt, tps=tps, db=db, nj=nj, final_norm=final_norm),
        grid=(rows // tm, nj),
        in_specs=in_specs,
        out_specs=x_specs + [pstate, pstate, sstate, sstate],
        out_shape=x_shapes + [
                   jax.ShapeDtypeStruct((n_tiles, _SUBLANES, f), _F32),
                   jax.ShapeDtypeStruct((n_tiles, _SUBLANES, f), _F32),
                   jax.ShapeDtypeStruct((2, db, f), _F32), jax.ShapeDtypeStruct((2, db, f), _F32)],
        scratch_shapes=[pltpu.VMEM((tm, d), _F32), pltpu.VMEM((nj, _SUBLANES, tn), _F32),
                        pltpu.VMEM((nj, _SUBLANES, tn), _F32)],
        compiler_params=_params("arbitrary", "arbitrary"),
    )(*args)


def _forward(x_prompt, x_sample, c_prompt, c_sample, state_wkv, state_shift, state_conv, state_ffn,
             ada_w, ada_b, norm_g, final_norm_g, w_in, mu_x, mu_rkv, decay_w0, decay_lora1, decay_lora2,
             iclr_a0, iclr_lora1, iclr_lora2, gate_lora1, gate_lora2, vres_v0, vres_lora1, vres_lora2,
             k_k, k_a, r_k, ln_x_w, ln_x_b, conv_w, w_out, ffn_up, ffn_conv, ffn_down,
             *, chunk=_WKV_CHUNK, mix_tn=_MIX_COLS, ffn_tn=_FFN_COLS, wkv_decode_nb=_WKV_DECODE_SEQS,
             wkv_prompt_nb=_WKV_PROMPT_SEQS):
    bp, t_len, d = x_prompt.shape
    db, ts, _ = x_sample.shape
    depth = ada_w.shape[0]
    g_dim = mu_rkv.shape[2]
    tm = ts * db
    assert t_len % tm == 0 and t_len % chunk == 0 and db % _SUBLANES == 0 and bp % wkv_prompt_nb == 0
    assert 2 <= ts <= 4 and ts & (ts - 1) == 0 and db % wkv_decode_nb == 0 and wkv_decode_nb & (wkv_decode_nb - 1) == 0
    tps = t_len // tm
    n_pt = bp * tps
    n_prompt_rows = bp * t_len

    x = _Act(x_prompt.reshape(n_prompt_rows, d), x_sample.transpose(1, 0, 2).reshape(tm, d), 0)
    pad = (-(db + bp)) % _SUBLANES
    c_all = jnp.concatenate([c_sample, c_prompt, jnp.zeros((pad, d), _F32)], axis=0)
    mod = _adaln_mod(c_all, ada_w, ada_b)

    norm_g4 = norm_g.reshape(depth, 2, 1, d)
    vec3 = lambda a: a.reshape(a.shape[0], 1, a.shape[1])
    w0_3, a0_3, v0_3, kk_3, ka_3 = vec3(decay_w0), vec3(iclr_a0), vec3(vres_v0), vec3(k_k), vec3(k_a)
    rk_3, lnw_3, lnb_3 = r_k.reshape(depth, 1, g_dim), vec3(ln_x_w), vec3(ln_x_b)
    state_conv_t = state_conv.transpose(0, 2, 1, 3)
    state_ffn_t = state_ffn.transpose(0, 2, 1, 3)
    tiles = dict(tm=tm, n_pt=n_pt, tps=tps, db=db)
    w_out = _cast_bf16(w_out, 512)
    ffn_up = _cast_bf16(ffn_up, 256)

    new = {k: [] for k in ("wkv_p", "shift_p", "conv_p", "ffn_p", "wkv_s", "shift_s", "conv_s", "ffn_s")}
    v_first = None
    for l in range(depth):
        outs = _norm_lora(l, x, mod, norm_g4, mu_x, state_shift, decay_lora1, iclr_lora1, gate_lora1,
                          vres_lora1, bp=bp, **tiles)
        if l > 0:
            h, aw, aa, ag, av, hp_last, hs_last = outs
        else:
            h, aw, aa, ag, hp_last, hs_last = outs
            av = None
        mouts = _mix(l, h, (aw, aa, ag, av), v_first, w_in,
                     (decay_lora2, iclr_lora2, gate_lora2, vres_lora2), (w0_3, a0_3, v0_3),
                     mu_rkv, kk_3, ka_3, conv_w, state_shift, state_conv_t, tn=mix_tn, bp=bp, **tiles)
        seqs_p, seqs_s, rest = mouts[:7], mouts[7:14], mouts[14:]
        if l > 0:
            o_conv, conv_p8, conv_s = rest
        else:
            o_conv, v_first, conv_p8, conv_s = rest

        o_p, wkv_p = _wkv(l, tuple(a.reshape(bp, t_len, g_dim) for a in seqs_p), chunk, wkv_prompt_nb,
                          rk_3, lnw_3, lnb_3)
        o_p = o_p.reshape(n_prompt_rows, g_dim)
        last = l == depth - 1 and depth > 1
        o_s, wkv_s = _wkv_decode(l, tuple(a.reshape(ts, db, g_dim) for a in seqs_s), state_wkv[l],
                                 new["wkv_s"] if last else [], rk_3, lnw_3, lnb_3, ts=ts, db=db, nb=wkv_decode_nb)
        o_s = o_s.reshape(tm, g_dim)

        x1, h2 = _outproj(l, o_p, o_s, o_conv, x, mod, norm_g4, w_out, **tiles)
        final_g = final_norm_g.reshape(1, d) if l == depth - 1 else None
        *x_out, fpa, fpb, fsa, fsb = _ffn(l, h2, x1, mod, ffn_up, ffn_conv, ffn_down, state_ffn_t, final_g,
                                          tn=ffn_tn, bp=bp, **tiles)
        x = _Act(x_out[0], x_out[0], n_pt) if len(x_out) == 1 else _Act(x_out[0], x_out[1], 0)

        new["wkv_p"].append(wkv_p)
        if last:
            wkv_s_stacked = wkv_s
        else:
            new["wkv_s"].append(wkv_s)
        new["shift_p"].append(hp_last[:, _SUBLANES - 1])
        new["shift_s"].append(hs_last)
        new["conv_p"].append(conv_p8[:, _SUBLANES - 2:])
        new["conv_s"].append(conv_s.transpose(1, 0, 2))
        last_tiles = slice(tps - 1, n_pt, tps)
        new["ffn_p"].append(jnp.concatenate([fpa[last_tiles, _SUBLANES - 2:], fpb[last_tiles, _SUBLANES - 2:]],
                                            axis=-1))
        new["ffn_s"].append(jnp.concatenate([fsa, fsb], axis=-1).transpose(1, 0, 2))

    y_prompt = x.prompt.reshape(bp, t_len, d)
    y_sample = x.decode.reshape(ts, db, d).transpose(1, 0, 2)
    st = {k: jnp.stack(vs) for k, vs in new.items() if k != "wkv_s" or depth == 1}
    if depth > 1:
        st["wkv_s"] = wkv_s_stacked
    return (y_prompt, y_sample, st["wkv_p"], st["shift_p"], st["conv_p"], st["ffn_p"],
            st["wkv_s"], st["shift_s"], st["conv_s"], st["ffn_s"])


def kernel(x_prompt, x_sample, c_prompt, c_sample, state_wkv, state_shift, state_conv, state_ffn, ada_w, ada_b, norm_g, final_norm_g, w_in, mu_x, mu_rkv, decay_w0, decay_lora1, decay_lora2, iclr_a0, iclr_lora1, iclr_lora2, gate_lora1, gate_lora2, vres_v0, vres_lora1, vres_lora2, k_k, k_a, r_k, ln_x_w, ln_x_b, conv_w, w_out, ffn_up, ffn_conv, ffn_down):
    return _forward(x_prompt, x_sample, c_prompt, c_sample, state_wkv, state_shift, state_conv, state_ffn,
                    ada_w, ada_b, norm_g, final_norm_g, w_in, mu_x, mu_rkv, decay_w0, decay_lora1, decay_lora2,
                    iclr_a0, iclr_lora1, iclr_lora2, gate_lora1, gate_lora2, vres_v0, vres_lora1, vres_lora2,
                    k_k, k_a, r_k, ln_x_w, ln_x_b, conv_w, w_out, ffn_up, ffn_conv, ffn_down)
```

```python
import functools
from typing import NamedTuple

import jax
import jax.numpy as jnp
from jax import lax
from jax.experimental import pallas as pl
from jax.experimental.pallas import tpu as pltpu

_F32 = jnp.float32
_BF16 = jnp.bfloat16

_HEAD = 64
_PAIR = 2 * _HEAD
_SEG = 256
_RMS_EPS = 1e-6
_GN_EPS = 64e-5
_VMEM_LIMIT_BYTES = 56 * 1024 * 1024
_SUBLANES = 8

_WKV_CHUNK = 64
_WKV_PROMPT_SEQS = 2
_WKV_DECODE_SEQS = 16
_MIX_COLS = 256
_FFN_COLS = 512


def _params(*sem):
    return pltpu.CompilerParams(dimension_semantics=sem, vmem_limit_bytes=_VMEM_LIMIT_BYTES)


def _dot(a, b):
    return jnp.dot(a.astype(_BF16), b.astype(_BF16), preferred_element_type=_F32)


def _dot_nt(a, b):
    return lax.dot_general(a.astype(_BF16), b.astype(_BF16), (((1,), (1,)), ((), ())),
                           preferred_element_type=_F32)


def _dot_tn(a, b):
    return lax.dot_general(a.astype(_BF16), b.astype(_BF16), (((0,), (0,)), ((), ())),
                           preferred_element_type=_F32)


def _split2(x):
    hi = x.astype(_BF16)
    lo = (x - hi.astype(_F32)).astype(_BF16)
    return hi, lo


def _dot_exact_rhs(x, m_bf16):
    hi, lo = _split2(x)
    return (jnp.dot(hi, m_bf16, preferred_element_type=_F32)
            + jnp.dot(lo, m_bf16, preferred_element_type=_F32))


def _segment_ones(n):
    r = lax.broadcasted_iota(jnp.int32, (n, n), 0) // _HEAD
    c = lax.broadcasted_iota(jnp.int32, (n, n), 1) // _HEAD
    return (r == c).astype(_BF16)


def _head_sums(x, seg_ones):
    n = seg_ones.shape[0]
    parts = [_dot_exact_rhs(x[:, q * n:(q + 1) * n], seg_ones) for q in range(x.shape[1] // n)]
    return parts[0] if len(parts) == 1 else jnp.concatenate(parts, axis=1)


def _tile_rows(m, reps):
    return m if reps == 1 else jnp.concatenate([m] * reps, axis=0)


def _shift_rows_prompt(x, carry8, first, two=True):
    r8 = lax.broadcasted_iota(jnp.int32, (_SUBLANES, 1), 0)
    l1 = jnp.where(first, 0.0, carry8[_SUBLANES - 1:_SUBLANES, :])
    x1 = pltpu.roll(x, 1, 0)
    x1 = jnp.concatenate([jnp.where(r8 == 0, l1, x1[:_SUBLANES, :]), x1[_SUBLANES:, :]], axis=0)
    if not two:
        return x1
    l2 = jnp.where(first, 0.0, carry8[_SUBLANES - 2:_SUBLANES - 1, :])
    x2 = pltpu.roll(x, 2, 0)
    top2 = jnp.where(r8 == 0, l2, jnp.where(r8 == 1, l1, x2[:_SUBLANES, :]))
    return x1, jnp.concatenate([top2, x2[_SUBLANES:, :]], axis=0)


class _Act(NamedTuple):
    prompt: jax.Array
    decode: jax.Array
    decode_block: int


def _act_specs(x, tm, n_pt):
    d = x.prompt.shape[1]
    return [pl.BlockSpec((tm, d), lambda i, *_: (jnp.minimum(i, n_pt - 1), 0)),
            pl.BlockSpec((tm, d), lambda i, *_: (x.decode_block, 0))]


def _mod_row(ref, i, tps, db):
    return ref[0, 0, pl.ds(db + i // tps, 1), :]


def _mod_tile(ref, db, ts):
    return _tile_rows(ref[0, 0, 0:db, :], ts)


def _mod_kernel(c_ref, w_ref, b_ref, o_ref):
    c = c_ref[...]
    s_hi, s_lo = _split2(c * jax.nn.sigmoid(c))
    w = w_ref[0].astype(_BF16)
    acc = jnp.dot(s_hi, w, preferred_element_type=_F32) + jnp.dot(s_lo, w, preferred_element_type=_F32)
    o_ref[0, 0] = acc + b_ref[0]


def _adaln_mod(c_all, ada_w, ada_b):
    depth, d, six_d = ada_w.shape
    bc = c_all.shape[0]
    tn = d
    per = d // tn
    return pl.pallas_call(
        _mod_kernel,
        grid=(depth, six_d // tn),
        in_specs=[pl.BlockSpec((bc, d), lambda l, n: (0, 0)),
                  pl.BlockSpec((1, d, tn), lambda l, n: (l, 0, n)),
                  pl.BlockSpec((1, 1, tn), lambda l, n: (l, 0, n))],
        out_specs=pl.BlockSpec((1, 1, bc, tn), lambda l, n: (l, n // per, 0, n % per)),
        out_shape=jax.ShapeDtypeStruct((depth, 6, bc, d), _F32),
        compiler_params=_params("arbitrary", "arbitrary"),
    )(c_all, ada_w, ada_b.reshape(depth, 1, six_d))


def _norm_lora_kernel(*refs, n_pt, tps, db, has_v):
    n_lora = 4 if has_v else 3
    it = iter(refs)
    xp_ref, xs_ref, sh_ref, sc_ref, g_ref, mu_ref, hl_ref = (next(it) for _ in range(7))
    w_refs = [next(it) for _ in range(n_lora)]
    h_ref = next(it)
    act_refs = [next(it) for _ in range(n_lora)]
    hp_ref, hs_ref, carry_ref, hcur_ref, hprev_ref = (next(it) for _ in range(5))
    i = pl.program_id(0)

    tm = xp_ref.shape[0]
    ts = tm // db
    x = jnp.where(i < n_pt, xp_ref[...], xs_ref[...])
    xn = x * lax.rsqrt(jnp.mean(x * x, axis=-1, keepdims=True) + _RMS_EPS) * g_ref[0, 0]

    @pl.when(i < n_pt)
    def _prompt():
        h = xn * (1.0 + _mod_row(sc_ref, i, tps, db)) + _mod_row(sh_ref, i, tps, db)
        first = (i % tps) == 0
        hprev_ref[...] = _shift_rows_prompt(h, carry_ref[...], first, two=False)
        hcur_ref[...] = h
        carry_ref[...] = h[tm - _SUBLANES:, :]
        hp_ref[0] = h[tm - _SUBLANES:, :]

    @pl.when(i >= n_pt)
    def _decode():
        h = xn * (1.0 + _mod_tile(sc_ref, db, ts)) + _mod_tile(sh_ref, db, ts)
        hprev_ref[...] = jnp.concatenate([hl_ref[0], h[:tm - db, :]], axis=0)
        hcur_ref[...] = h
        hs_ref[...] = h[tm - db:, :]

    h = hcur_ref[...]
    xx = hprev_ref[...] - h
    mu = mu_ref[0]
    h_ref[...] = h.astype(_BF16)
    pre = [_dot(h + xx * mu[g:g + 1], w_refs[g][0]) for g in range(n_lora)]
    act_refs[0][...] = jnp.tanh(pre[0])
    act_refs[1][...] = pre[1]
    act_refs[2][...] = jax.nn.sigmoid(pre[2])
    if has_v:
        act_refs[3][...] = pre[3]


def _norm_lora(l, x, mod, norm_g4, mu_x, state_shift, decay_lora1, iclr_lora1, gate_lora1, vres_lora1,
               *, tm, n_pt, tps, db, bp):
    d = x.prompt.shape[1]
    rows = (n_pt + 1) * tm
    has_v = l > 0
    bc = mod.shape[2]
    ld, la, lg = decay_lora1.shape[2], iclr_lora1.shape[2], gate_lora1.shape[2]
    full = lambda *shape: pl.BlockSpec(shape, lambda i: (l,) + (0,) * (len(shape) - 1))
    in_specs = _act_specs(x, tm, n_pt) + [
                pl.BlockSpec((1, 1, bc, d), lambda i: (l, 0, 0, 0)),
                pl.BlockSpec((1, 1, bc, d), lambda i: (l, 1, 0, 0)),
                pl.BlockSpec((1, 1, 1, d), lambda i: (l, 0, 0, 0)),
                full(1, 4, d), full(1, db, d), full(1, d, ld), full(1, d, la), full(1, d, lg)]
    args = [x.prompt, x.decode, mod, mod, norm_g4, mu_x, state_shift, decay_lora1, iclr_lora1, gate_lora1]
    row_out = lambda n, dt: (pl.BlockSpec((tm, n), lambda i: (i, 0)), jax.ShapeDtypeStruct((rows, n), dt))
    outs = [row_out(d, _BF16), row_out(ld, _F32), row_out(la, _F32), row_out(lg, _F32)]
    if has_v:
        lv = vres_lora1.shape[2]
        in_specs.append(pl.BlockSpec((1, d, lv), lambda i: (l - 1, 0, 0)))
        args.append(vres_lora1)
        outs.append(row_out(lv, _F32))
    outs.append((pl.BlockSpec((1, _SUBLANES, d), lambda i: (jnp.minimum(i // tps, bp - 1), 0, 0)),
                 jax.ShapeDtypeStruct((bp, _SUBLANES, d), _F32)))
    outs.append((pl.BlockSpec((db, d), lambda i: (0, 0)), jax.ShapeDtypeStruct((db, d), _F32)))
    return pl.pallas_call(
        functools.partial(_norm_lora_kernel, n_pt=n_pt, tps=tps, db=db, has_v=has_v),
        grid=(rows // tm,),
        in_specs=in_specs,
        out_specs=[o[0] for o in outs],
        out_shape=[o[1] for o in outs],
        scratch_shapes=[pltpu.VMEM((_SUBLANES, d), _F32), pltpu.VMEM((tm, d), _F32), pltpu.VMEM((tm, d), _F32)],
        compiler_params=_params("arbitrary"),
    )(*args)


def _mix_kernel(*refs, n_pt, tps, db, has_v):
    it = iter(refs)
    h_ref = next(it)
    w_refs = [next(it) for _ in range(6)]
    aw_ref, aa_ref, ag_ref = next(it), next(it), next(it)
    av_ref = next(it) if has_v else None
    d2_ref, i2_ref, g2_ref = next(it), next(it), next(it)
    v2_ref = next(it) if has_v else None
    w0_ref, a0_ref = next(it), next(it)
    v0_ref = next(it) if has_v else None
    mu_ref, kkw_ref, kaw_ref, cw_ref, hl_ref, cs_ref = (next(it) for _ in range(6))
    vf_ref = next(it) if has_v else None
    seq_prompt = [next(it) for _ in range(7)]
    seq_decode = [next(it) for _ in range(7)]
    oc_o = next(it)
    vf_o = None if has_v else next(it)
    cp_o, cso_o = next(it), next(it)
    wc_scr, pcarry, zcarry = (next(it) for _ in range(3))

    i = pl.program_id(1)
    tm = h_ref.shape[0]
    tn = oc_o.shape[1]

    @pl.when(i == 0)
    def _cache_weights():
        for g in range(6):
            wc_scr[g] = w_refs[g][0].astype(_BF16)

    def project(rs):
        h = h_ref[rs, :]
        return [jnp.dot(h, wc_scr[g], preferred_element_type=_F32) for g in range(6)]

    def tail(rs, p, pprev, bg, z, z1, z2, seq_out):
        r_o, lw_o, k_o, v_o, kk_o, kka_o, g_o = seq_out
        mu = mu_ref[0]
        r = p[0] + (pprev[0] - p[0]) * mu[0:1]
        k = p[1] + (pprev[1] - p[1]) * mu[1:2]
        v = p[2] + (pprev[2] - p[2]) * mu[2:3]

        zlog = w0_ref[0] + _dot(aw_ref[rs, :], d2_ref[0])
        softplus = jnp.maximum(-zlog, 0.0) + jnp.log(1.0 + jnp.exp(-jnp.abs(zlog)))
        lw_o[rs, :] = -jnp.exp(-softplus - 0.5)
        a = jax.nn.sigmoid(a0_ref[0] + _dot(aa_ref[rs, :], i2_ref[0]))
        g_o[rs, :] = _dot(ag_ref[rs, :], g2_ref[0])
        if has_v:
            nu = jax.nn.sigmoid(v0_ref[0] + _dot(av_ref[rs, :], v2_ref[0]))
            v = v + (vf_ref[rs, :] - v) * nu
        else:
            vf_o[rs, :] = v

        kk = k * kkw_ref[0]
        norm = jnp.sqrt(_head_sums(kk * kk, _segment_ones(tn)))
        kk = kk / jnp.maximum(norm, 1e-12)
        r_o[rs, :] = r
        k_o[rs, :] = k * (1.0 + (a - 1.0) * kaw_ref[0])
        v_o[rs, :] = v
        kk_o[rs, :] = kk
        kka_o[rs, :] = kk * a

        cw = cw_ref[0]
        zc = z2 * cw[0:1] + z1 * cw[1:2] + z * cw[2:3]
        oc_o[rs, :] = (bg * zc).astype(_BF16)

    @pl.when(i < n_pt)
    def _prompt():
        half = tm // 2
        halves = (slice(0, half), slice(half, tm))
        proj = [project(rs) for rs in halves]
        zs = [q[4] * q[5] for q in proj]
        for idx, rs in enumerate(halves):
            p, z = proj[idx][:3], zs[idx]
            if idx == 0:
                first = (i % tps) == 0
                p_carry = [pcarry[g] for g in range(3)]
                z_carry = zcarry[...]
            else:
                first = False
                p_carry = [q[half - _SUBLANES:, :] for q in proj[0][:3]]
                z_carry = zs[0][half - _SUBLANES:, :]
            pprev = [_shift_rows_prompt(p[g], p_carry[g], first, two=False) for g in range(3)]
            z1, z2 = _shift_rows_prompt(z, z_carry, first)
            tail(rs, p, pprev, proj[idx][3], z, z1, z2, seq_prompt)
        for g in range(3):
            pcarry[g] = proj[1][g][half - _SUBLANES:, :]
        zcarry[...] = zs[1][half - _SUBLANES:, :]
        cp_o[0] = zs[1][half - _SUBLANES:, :]

    @pl.when(i >= n_pt)
    def _decode():
        rs = slice(0, tm)
        proj = project(rs)
        p, z = proj[:3], proj[4] * proj[5]
        hl = hl_ref[0].astype(_BF16)
        pprev = [jnp.concatenate([jnp.dot(hl, wc_scr[g], preferred_element_type=_F32), p[g][:tm - db, :]], axis=0)
                 for g in range(3)]
        z1 = jnp.concatenate([cs_ref[0, 1], z[:tm - db, :]], axis=0)
        z2 = jnp.concatenate([cs_ref[0, 0], cs_ref[0, 1], z[:tm - 2 * db, :]], axis=0)
        cso_o[0] = z[tm - 2 * db:tm - db, :]
        cso_o[1] = z[tm - db:, :]
        tail(rs, p, pprev, proj[3], z, z1, z2, seq_decode)


def _mix(l, h, acts, vf_in, w_in, lora2, vecs, mu_rkv, k_k3, k_a3, conv_w, state_shift, state_conv_t,
         *, tm, tn, n_pt, tps, db, bp):
    rows, d = h.shape
    has_v = l > 0
    g_dim = mu_rkv.shape[2]
    nj = g_dim // tn
    aw, aa, ag, av = acts
    d2, i2, g2, v2 = lora2
    w0, a0, v0 = vecs
    row_in = lambda arr: pl.BlockSpec((tm, arr.shape[1]), lambda j, i: (i, 0))
    col3 = lambda arr, ll: pl.BlockSpec((1, arr.shape[1], tn), lambda j, i: (ll, 0, j))
    in_specs = [row_in(h)] + [pl.BlockSpec((1, d, tn), lambda j, i, g=g: (l, 0, g * nj + j)) for g in range(6)]
    args = [h] + [w_in] * 6
    in_specs += [row_in(aw), row_in(aa), row_in(ag)]
    args += [aw, aa, ag]
    if has_v:
        in_specs.append(row_in(av))
        args.append(av)
    in_specs += [col3(d2, l), col3(i2, l), col3(g2, l)]
    args += [d2, i2, g2]
    if has_v:
        in_specs.append(col3(v2, l - 1))
        args.append(v2)
    in_specs += [col3(w0, l), col3(a0, l)]
    args += [w0, a0]
    if has_v:
        in_specs.append(col3(v0, l - 1))
        args.append(v0)
    in_specs += [col3(mu_rkv, l), col3(k_k3, l), col3(k_a3, l), col3(conv_w, l),
                 pl.BlockSpec((1, db, d), lambda j, i: (l, 0, 0)),
                 pl.BlockSpec((1, 2, db, tn), lambda j, i: (l, 0, 0, j))]
    args += [mu_rkv, k_k3, k_a3, conv_w, state_shift, state_conv_t]
    if has_v:
        in_specs.append(pl.BlockSpec((tm, tn), lambda j, i: (i, j)))
        args.append(vf_in)
    tile = pl.BlockSpec((tm, tn), lambda j, i: (i, j))
    prompt_tile = pl.BlockSpec((tm, tn), lambda j, i: (jnp.minimum(i, n_pt - 1), j))
    decode_tile = pl.BlockSpec((tm, tn), lambda j, i: (0, j), pipeline_mode=pl.Buffered(1))
    outs = [(prompt_tile, jax.ShapeDtypeStruct((n_pt * tm, g_dim), _F32)) for _ in range(7)]
    outs += [(decode_tile, jax.ShapeDtypeStruct((tm, g_dim), _F32)) for _ in range(7)]
    outs.append((tile, jax.ShapeDtypeStruct((rows, g_dim), _BF16)))
    if not has_v:
        outs.append((tile, jax.ShapeDtypeStruct((rows, g_dim), _F32)))
    outs.append((pl.BlockSpec((1, _SUBLANES, tn), lambda j, i: (jnp.minimum(i // tps, bp - 1), 0, j)),
                 jax.ShapeDtypeStruct((bp, _SUBLANES, g_dim), _F32)))
    outs.append((pl.BlockSpec((2, db, tn), lambda j, i: (0, 0, j)), jax.ShapeDtypeStruct((2, db, g_dim), _F32)))
    return pl.pallas_call(
        functools.partial(_mix_kernel, n_pt=n_pt, tps=tps, db=db, has_v=has_v),
        grid=(nj, rows // tm),
        in_specs=in_specs,
        out_specs=[o[0] for o in outs],
        out_shape=[o[1] for o in outs],
        scratch_shapes=[pltpu.VMEM((6, d, tn), _BF16), pltpu.VMEM((3, _SUBLANES, tn), _F32),
                        pltpu.VMEM((_SUBLANES, tn), _F32)],
        compiler_params=_params("arbitrary", "arbitrary"),
    )(*args)


def _unit_lower_inverse_minus_identity(lmats, ri, ci, chunk, mul):
    same8 = (ri >> 3) == (ci >> 3)
    d8 = [jnp.where(same8, m, 0.0) for m in lmats]
    d2 = [mul(a, a) for a in d8]
    d3 = [mul(a, b) for a, b in zip(d8, d2)]
    d4 = [mul(b, b) for b in d2]
    x = [a + b + cc for a, b, cc in zip(d8, d2, d3)]
    xd4 = [mul(a, b) for a, b in zip(x, d4)]
    x = [a + b + cc for a, b, cc in zip(x, d4, xd4)]
    size = 16
    while size <= chunk:
        sh = size.bit_length() - 1
        level = ((ri >> sh) == (ci >> sh)) & ((ri >> (sh - 1)) != (ci >> (sh - 1)))
        e = [jnp.where(level, m, 0.0) for m in lmats]
        y = [b + mul(a, b) for a, b in zip(x, e)]
        x = [a + b + mul(b, a) for a, b in zip(x, y)]
        size *= 2
    return x


def _wkv_kernel(r_ref, lw_ref, k_ref, v_ref, kk_ref, kka_ref, g_ref, rk_ref, lnw_ref, lnb_ref,
                o_ref, so_ref, s_scr, *, n_chunks):
    ci_grid = pl.program_id(1)
    nb, c, g_dim = r_ref.shape
    npair = g_dim // _PAIR
    n2 = 2 * c

    lane = lax.broadcasted_iota(jnp.int32, (1, _PAIR), 1)
    m0 = (lane < _HEAD).astype(_F32)
    m1 = 1.0 - m0

    @pl.when(ci_grid == 0)
    def _init():
        s_scr[...] = jnp.zeros(s_scr.shape, _F32)

    ri = lax.broadcasted_iota(jnp.int32, (n2, n2), 0) & (c - 1)
    ci = lax.broadcasted_iota(jnp.int32, (n2, n2), 1) & (c - 1)
    strict = ri > ci
    incl = ri >= ci
    row_id = lax.broadcasted_iota(jnp.int32, (c, 1), 0)
    seg_ones = _segment_ones(_SEG)

    def stack(x):
        return jnp.concatenate([x * m0, x * m1], axis=0)

    units =[(u, p) for u in range(nb) for p in range(npair)]
    sl = lambda p: slice(p * _PAIR, (p + 1) * _PAIR)

    seq = []
    for u in range(nb):
        lw = lw_ref[u]
        cl = lw
        step = 1
        while step < c:
            cl = cl + jnp.where(row_id >= step, pltpu.roll(cl, step, 0), 0.0)
            step *= 2
        cl_end = cl[c - 1:c, :]
        r, k, v, kk, kka = r_ref[u], k_ref[u], v_ref[u], kk_ref[u], kka_ref[u]
        e_neg = jnp.exp(-cl)
        e_tail = jnp.exp(cl_end - cl)
        seq.append(dict(r=r, k=k, v=v, p_end=jnp.exp(cl_end), rt=r * jnp.exp(cl), at=-(kk * jnp.exp(cl - lw)),
                        kt=k * e_neg, bt=kka * e_neg, kh=k * e_tail, bh=kka * e_tail))

    ar2 = [jnp.concatenate([stack(seq[u]["at"][:, sl(p)]), stack(seq[u]["rt"][:, sl(p)])], axis=0).astype(_BF16)
           for u, p in units]
    mb = [_dot_nt(a, stack(seq[u]["bt"][:, sl(p)])) for a, (u, p) in zip(ar2, units)]
    mk = [_dot_nt(a, stack(seq[u]["kt"][:, sl(p)])) for a, (u, p) in zip(ar2, units)]
    s_old = [s_scr[u, p] for u, p in units]
    uy0 = [_dot_nt(jnp.concatenate([seq[u]["at"][:, sl(p)], seq[u]["rt"][:, sl(p)]], axis=0), s)
           for (u, p), s in zip(units, s_old)]
    x_inv = _unit_lower_inverse_minus_identity([jnp.where(strict, m[:n2], 0.0) for m in mb], ri, ci, c, _dot)
    vs = [stack(seq[u]["v"][:, sl(p)]).astype(_BF16) for u, p in units]
    ws = [stack(q[:c]) + _dot(jnp.where(strict, m[:n2], 0.0), vv) for q, m, vv in zip(uy0, mk, vs)]
    us = [(w + _dot(xi, w)).astype(_BF16) for xi, w in zip(x_inv, ws)]
    ysd = [_dot(jnp.where(incl, m1_[n2:], 0.0), uu) + _dot(jnp.where(incl, m2_[n2:], 0.0), vv)
           for m1_, m2_, uu, vv in zip(mb, mk, us, vs)]
    ys = [q[c:] + d[:c] + d[c:] for q, d in zip(uy0, ysd)]
    for idx, (u, p) in enumerate(units):
        uv = jnp.concatenate([us[idx], vs[idx]], axis=0)
        bk = jnp.concatenate([stack(seq[u]["bh"][:, sl(p)]), stack(seq[u]["kh"][:, sl(p)])], axis=0)
        s_scr[u, p] = s_old[idx] * seq[u]["p_end"][:, sl(p)] + _dot_tn(uv, bk)
    for u in range(nb):
        y = jnp.concatenate(ys[u * npair:(u + 1) * npair], axis=1)
        mean = _head_sums(y, seg_ones) * (1.0 / _HEAD)
        dy = y - mean
        var = _head_sums(dy * dy, seg_ones) * (1.0 / _HEAD)
        yn = dy * lax.rsqrt(var + _GN_EPS) * lnw_ref[...] + lnb_ref[...]
        bonus = _head_sums(seq[u]["r"] * seq[u]["k"] * rk_ref[...], seg_ones) * seq[u]["v"]
        o_ref[u] = ((yn + bonus) * g_ref[u]).astype(o_ref.dtype)

    @pl.when(ci_grid == n_chunks - 1)
    def _final():
        for u, p in units:
            s = s_scr[u, p]
            so_ref[u, 2 * p] = s[:_HEAD, :_HEAD]
            so_ref[u, 2 * p + 1] = s[_HEAD:, _HEAD:]


def _wkv(l, seqs, chunk, nb, r_k3, ln_w3, ln_b3):
    n_seq, t_len, g_dim = seqs[0].shape
    n_heads = g_dim // _HEAD
    n_chunks = t_len // chunk
    blk = pl.BlockSpec((nb, chunk, g_dim), lambda bi, ci: (bi, ci, 0))
    vec = pl.BlockSpec((None, 1, g_dim), lambda bi, ci: (l, 0, 0))
    return pl.pallas_call(
        functools.partial(_wkv_kernel, n_chunks=n_chunks),
        grid=(n_seq // nb, n_chunks),
        in_specs=[blk] * 7 + [vec] * 3,
        out_specs=[blk, pl.BlockSpec((nb, n_heads, _HEAD, _HEAD), lambda bi, ci: (bi, 0, 0, 0))],
        out_shape=[jax.ShapeDtypeStruct((n_seq, t_len, g_dim), _BF16),
                   jax.ShapeDtypeStruct((n_seq, n_heads, _HEAD, _HEAD), _F32)],
        scratch_shapes=[pltpu.VMEM((nb, g_dim // _PAIR, _PAIR, _PAIR), _F32)],
        compiler_params=_params("arbitrary", "arbitrary"),
    )(*seqs, r_k3, ln_w3, ln_b3)


def _wkv_decode_kernel(*refs, ts, nb, n_earlier):
    (r_ref, lw_ref, k_ref, v_ref, kk_ref, kka_ref, g_ref, rk_ref, lnw_ref, lnb_ref, s0_ref) = refs[:11]
    earlier_refs = refs[11:11 + n_earlier]
    o_ref, so_ref = refs[11 + n_earlier:]
    if n_earlier:
        for idx, e_ref in enumerate(earlier_refs):
            so_ref[idx] = e_ref[...]
        so_ref = so_ref.at[n_earlier]
    g_dim = r_ref.shape[2]
    npair = g_dim // _PAIR
    n = ts * nb
    n2 = 2 * n
    nb_bits = nb.bit_length() - 1

    lane = lax.broadcasted_iota(jnp.int32, (1, _PAIR), 1)
    m0 = (lane < _HEAD).astype(_F32)
    m1 = 1.0 - m0

    def stack(x):
        return jnp.concatenate([x * m0, x * m1], axis=0)

    def rows_tb(ref):
        return jnp.concatenate([ref[t] for t in range(ts)], axis=0)

    lw_t = [lw_ref[t] for t in range(ts)]
    cl_t = [lw_t[0]]
    for t in range(1, ts):
        cl_t.append(cl_t[-1] + lw_t[t])
    cl = jnp.concatenate(cl_t, axis=0)
    lw = jnp.concatenate(lw_t, axis=0)
    cl_end_b = cl_t[-1]
    cl_end = _tile_rows(cl_end_b, ts)
    p_end = jnp.exp(cl_end_b)
    r, k, v, kk, kka = (rows_tb(ref) for ref in (r_ref, k_ref, v_ref, kk_ref, kka_ref))
    e_neg = jnp.exp(-cl)
    e_tail = jnp.exp(cl_end - cl)
    rt = r * jnp.exp(cl)
    at = -(kk * jnp.exp(cl - lw))
    kt = k * e_neg
    bt = kka * e_neg
    kh = k * e_tail
    bh = kka * e_tail

    ri = lax.broadcasted_iota(jnp.int32, (n2, n2), 0)
    ci = lax.broadcasted_iota(jnp.int32, (n2, n2), 1)
    same_seq = (ri & (nb - 1)) == (ci & (nb - 1))
    t_r = (ri & (n - 1)) >> nb_bits
    t_c = (ci & (n - 1)) >> nb_bits
    strict = same_seq & (t_r > t_c)
    incl = same_seq & (t_r >= t_c)
    seq_of_row2 = lax.broadcasted_iota(jnp.int32, (n2, 1), 0) & (nb - 1)
    seq_of_row4 = lax.broadcasted_iota(jnp.int32, (2 * n2, 1), 0) & (nb - 1)
    zero = jnp.zeros((_HEAD, _HEAD), _F32)

    pairs = range(npair)
    sls = [slice(p * _PAIR, (p + 1) * _PAIR) for p in pairs]
    ar2 = [jnp.concatenate([stack(at[:, sl]), stack(rt[:, sl])], axis=0).astype(_BF16) for sl in sls]
    mb = [_dot_nt(a, stack(bt[:, sl])) for a, sl in zip(ar2, sls)]
    mk = [_dot_nt(a, stack(kt[:, sl])) for a, sl in zip(ar2, sls)]
    lab = [jnp.where(strict, m[:n2], 0.0) for m in mb]
    d2 = [_dot(a, a) for a in lab]
    d3 = [_dot(a, b) for a, b in zip(lab, d2)]
    x_inv = [a + b + cc for a, b, cc in zip(lab, d2, d3)]

    def block_diag(b, p):
        top = jnp.concatenate([s0_ref[b, 2 * p], zero], axis=1)
        bot = jnp.concatenate([zero, s0_ref[b, 2 * p + 1]], axis=1)
        return jnp.concatenate([top, bot], axis=0)

    ys = []
    for p, sl in zip(pairs, sls):
        s_b = [block_diag(b, p) for b in range(nb)]
        ar = jnp.concatenate([at[:, sl], rt[:, sl]], axis=0)
        ar_cat = jnp.concatenate([jnp.where(seq_of_row2 == b, ar, 0.0).astype(_BF16) for b in range(nb)], axis=1)
        s_cat = jnp.concatenate([s.astype(_BF16) for s in s_b], axis=1)
        uy0 = _dot_nt(ar_cat, s_cat)
        vs = stack(v[:, sl]).astype(_BF16)
        ws = stack(uy0[:n]) + _dot(jnp.where(strict, mk[p][:n2], 0.0), vs)
        us = (ws + _dot(x_inv[p], ws)).astype(_BF16)
        ysd = _dot(jnp.where(incl, mb[p][n2:], 0.0), us) + _dot(jnp.where(incl, mk[p][n2:], 0.0), vs)
        ys.append(uy0[n:] + ysd[:n] + ysd[n:])
        uv = jnp.concatenate([us, vs], axis=0)
        uv_cat = jnp.concatenate([jnp.where(seq_of_row4 == b, uv, jnp.zeros_like(uv)) for b in range(nb)], axis=1)
        bk = jnp.concatenate([stack(bh[:, sl]), stack(kh[:, sl])], axis=0)
        upd = _dot_tn(uv_cat, bk)
        for b in range(nb):
            s_new = (s_b[b] * p_end[b:b + 1, sl] + upd[b * _PAIR:(b + 1) * _PAIR, :])
            so_ref[b, 2 * p] = s_new[:_HEAD, :_HEAD]
            so_ref[b, 2 * p + 1] = s_new[_HEAD:, _HEAD:]

    y = jnp.concatenate(ys, axis=1)
    seg_ones = _segment_ones(_SEG)
    mean = _head_sums(y, seg_ones) * (1.0 / _HEAD)
    dy = y - mean
    var = _head_sums(dy * dy, seg_ones) * (1.0 / _HEAD)
    yn = dy * lax.rsqrt(var + _GN_EPS) * lnw_ref[...] + lnb_ref[...]
    bonus = _head_sums(r * k * rk_ref[...], seg_ones) * v
    out = ((yn + bonus) * rows_tb(g_ref)).astype(o_ref.dtype)
    for t in range(ts):
        o_ref[t] = out[t * nb:(t + 1) * nb, :]


def _wkv_decode(l, seqs, state0, earlier_states, r_k3, ln_w3, ln_b3, *, ts, db, nb):
    g_dim = seqs[0].shape[2]
    n_heads = g_dim // _HEAD
    n_earlier = len(earlier_states)
    head_groups = 2 if n_earlier else 1
    gw, hw = g_dim // head_groups, n_heads // head_groups
    blk = pl.BlockSpec((ts, nb, gw), lambda bi, hg: (0, bi, hg))
    vec = pl.BlockSpec((None, 1, gw), lambda bi, hg: (l, 0, hg))
    st = pl.BlockSpec((nb, hw, _HEAD, _HEAD), lambda bi, hg: (bi, hg, 0, 0))
    if n_earlier:
        st_out = pl.BlockSpec((n_earlier + 1, nb, hw, _HEAD, _HEAD), lambda bi, hg: (0, bi, hg, 0, 0))
        st_shape = (n_earlier + 1, db, n_heads, _HEAD, _HEAD)
    else:
        st_out, st_shape = st, (db, n_heads, _HEAD, _HEAD)
    return pl.pallas_call(
        functools.partial(_wkv_decode_kernel, ts=ts, nb=nb, n_earlier=n_earlier),
        grid=(db // nb, head_groups),
        in_specs=[blk] * 7 + [vec] * 3 + [st] * (1 + n_earlier),
        out_specs=[blk, st_out],
        out_shape=[jax.ShapeDtypeStruct((ts, db, g_dim), _BF16), jax.ShapeDtypeStruct(st_shape, _F32)],
        compiler_params=_params("arbitrary", "arbitrary"),
    )(*seqs, r_k3, ln_w3, ln_b3, state0, *earlier_states)


def _cast_kernel(w_ref, o_ref):
    o_ref[...] = w_ref[...].astype(o_ref.dtype)


def _cast_bf16(w, rows_per_block):
    depth, k, n = w.shape
    spec = pl.BlockSpec((1, rows_per_block, n), lambda l, i: (l, i, 0))
    return pl.pallas_call(
        _cast_kernel, grid=(depth, k // rows_per_block), in_specs=[spec], out_specs=spec,
        out_shape=jax.ShapeDtypeStruct(w.shape, _BF16),
        compiler_params=_params("arbitrary", "arbitrary"),
    )(w)


def _outproj_kernel(op_ref, os_ref, oc_ref, xp_ref, xs_ref, ga_ref, sh_ref, sc_ref, g_ref, w_ref,
                    x1_ref, h2_ref, *, n_pt, tps, db):
    i = pl.program_id(0)
    tm = xp_ref.shape[0]
    ts = tm // db
    g_dim = op_ref.shape[1]

    def rows_out(rs, o_ref, x_ref, ga, sc, sh):
        acc = (jnp.dot(o_ref[rs, :], w_ref[0, :g_dim, :], preferred_element_type=_F32)
               + jnp.dot(oc_ref[rs, :], w_ref[0, g_dim:, :], preferred_element_type=_F32))
        x1 = x_ref[rs, :] + ga * acc
        x1_ref[rs, :] = x1
        xn = x1 * lax.rsqrt(jnp.mean(x1 * x1, axis=-1, keepdims=True) + _RMS_EPS) * g_ref[0, 0]
        h2_ref[rs, :] = (xn * (1.0 + sc) + sh).astype(_BF16)

    half = tm // 2
    halves = (slice(0, half), slice(half, tm))

    @pl.when(i < n_pt)
    def _prompt():
        ga, sc, sh = (_mod_row(ref, i, tps, db) for ref in (ga_ref, sc_ref, sh_ref))
        for rs in halves:
            rows_out(rs, op_ref, xp_ref, ga, sc, sh)

    @pl.when(i >= n_pt)
    def _decode():
        ga, sc, sh = (_mod_tile(ref, db, ts) for ref in (ga_ref, sc_ref, sh_ref))
        for rs in halves:
            rows_out(rs, os_ref, xs_ref, ga[rs, :], sc[rs, :], sh[rs, :])


def _outproj(l, o_p, o_s, o_conv, x, mod, norm_g4, w_out, *, tm, n_pt, tps, db):
    d = x.prompt.shape[1]
    rows = (n_pt + 1) * tm
    g_dim = o_p.shape[1]
    bc = mod.shape[2]
    modspec = lambda comp: pl.BlockSpec((1, 1, bc, d), lambda i: (l, comp, 0, 0))
    return pl.pallas_call(
        functools.partial(_outproj_kernel, n_pt=n_pt, tps=tps, db=db),
        grid=(rows // tm,),
        in_specs=[pl.BlockSpec((tm, g_dim), lambda i: (jnp.minimum(i, n_pt - 1), 0)),
                  pl.BlockSpec((tm, g_dim), lambda i: (0, 0)),
                  pl.BlockSpec((tm, o_conv.shape[1]), lambda i: (i, 0))] + _act_specs(x, tm, n_pt) + [
                  modspec(2), modspec(3), modspec(4),
                  pl.BlockSpec((1, 1, 1, d), lambda i: (l, 1, 0, 0)),
                  pl.BlockSpec((1, d, d), lambda i: (l, 0, 0), pipeline_mode=pl.Buffered(1))],
        out_specs=[pl.BlockSpec((tm, d), lambda i: (i, 0)), pl.BlockSpec((tm, d), lambda i: (i, 0))],
        out_shape=[jax.ShapeDtypeStruct((rows, d), _F32), jax.ShapeDtypeStruct((rows, d), _BF16)],
        compiler_params=_params("arbitrary"),
    )(o_p, o_s, o_conv, x.prompt, x.decode, mod, mod, mod, norm_g4, w_out)


def _ffn_kernel(*refs, n_pt, tps, db, nj, final_norm):
    it = iter(refs)
    (h2_ref, wa_ref, wb_ref, cwa_ref, cwb_ref, wd_ref, x1_ref, ga_ref, sfa_ref, sfb_ref) = (next(it) for _ in range(10))
    fg_ref = next(it) if final_norm else None
    x2_ref = next(it)
    x2s_ref = next(it) if final_norm else x2_ref
    fpa_o, fpb_o, fsa_o, fsb_o = (next(it) for _ in range(4))
    acc_scr, ca_scr, cb_scr = (next(it) for _ in range(3))
    i = pl.program_id(0)
    j = pl.program_id(1)
    tm = h2_ref.shape[0]
    ts = tm // db

    @pl.when(j == 0)
    def _zero():
        acc_scr[...] = jnp.zeros(acc_scr.shape, _F32)

    def prompt_body():
        half = tm // 2
        halves = (slice(0, half), slice(half, tm))
        us = [[jnp.dot(h2_ref[rs, :], w_ref[0], preferred_element_type=_F32) for w_ref in (wa_ref, wb_ref)]
              for rs in halves]
        wd = wd_ref[0].astype(_BF16)
        for r, rs in enumerate(halves):
            conv = []
            for idx, (cw_ref, c_scr, fp_o) in enumerate(((cwa_ref, ca_scr, fpa_o), (cwb_ref, cb_scr, fpb_o))):
                u = us[r][idx]
                if r == 0:
                    u1, u2 = _shift_rows_prompt(u, c_scr[j], (i % tps) == 0)
                else:
                    u1, u2 = _shift_rows_prompt(u, us[0][idx][half - _SUBLANES:, :], False)
                    c_scr[j] = u[half - _SUBLANES:, :]
                    fp_o[0] = u[half - _SUBLANES:, :]
                cw = cw_ref[0]
                conv.append(u2 * cw[0:1] + u1 * cw[1:2] + u * cw[2:3])
            gact = ((conv[0] * jax.nn.sigmoid(conv[0])) * conv[1]).astype(_BF16)
            acc_scr[rs, :] += jnp.dot(gact, wd, preferred_element_type=_F32)

    def decode_body():
        h2 = h2_ref[...]
        conv = []
        for w_ref, cw_ref, sf, fp_o, fs_o in ((wa_ref, cwa_ref, sfa_ref, fpa_o, fsa_o),
                                              (wb_ref, cwb_ref, sfb_ref, fpb_o, fsb_o)):
            u = jnp.dot(h2, w_ref[0], preferred_element_type=_F32)
            prev2, prev1 = sf[0, :, 0, :], sf[0, :, 1, :]
            u1 = jnp.concatenate([prev1, u[:tm - db, :]], axis=0)
            u2 = jnp.concatenate([prev2, prev1, u[:tm - 2 * db, :]], axis=0)
            fp_o[0] = jnp.zeros(fp_o.shape[1:], _F32)
            fs_o[:, 0, :] = u[tm - 2 * db:tm - db, :]
            fs_o[:, 1, :] = u[tm - db:, :]
            cw = cw_ref[0]
            conv.append(u2 * cw[0:1] + u1 * cw[1:2] + u * cw[2:3])
        gact = ((conv[0] * jax.nn.sigmoid(conv[0])) * conv[1]).astype(_BF16)
        acc_scr[...] += jnp.dot(gact, wd_ref[0].astype(_BF16), preferred_element_type=_F32)

    pl.when(i < n_pt)(prompt_body)
    pl.when(i >= n_pt)(decode_body)

    def finish(ga, out_ref):
        x2 = x1_ref[...] + ga * acc_scr[...]
        if final_norm:
            x2 = x2 * lax.rsqrt(jnp.mean(x2 * x2, axis=-1, keepdims=True) + _RMS_EPS) * fg_ref[...]
        out_ref[...] = x2

    @pl.when((j == nj - 1) & (i < n_pt))
    def _finish_prompt():
        finish(_mod_row(ga_ref, i, tps, db), x2_ref)

    @pl.when((j == nj - 1) & (i >= n_pt))
    def _finish_decode():
        finish(_mod_tile(ga_ref, db, ts), x2s_ref)


def _ffn(l, h2, x1, mod, ffn_up, ffn_conv, ffn_down, state_ffn, final_g, *, tm, tn, n_pt, tps, db, bp):
    rows, d = x1.shape
    f = ffn_down.shape[1]
    nj = f // tn
    bc = mod.shape[2]
    final_norm = final_g is not None
    in_specs = [pl.BlockSpec((tm, d), lambda i, j: (i, 0)),
                pl.BlockSpec((1, d, tn), lambda i, j: (l, 0, j)),
                pl.BlockSpec((1, d, tn), lambda i, j: (l, 0, nj + j)),
                pl.BlockSpec((1, 3, tn), lambda i, j: (l, 0, j)),
                pl.BlockSpec((1, 3, tn), lambda i, j: (l, 0, nj + j)),
                pl.BlockSpec((1, tn, d), lambda i, j: (l, j, 0)),
                pl.BlockSpec((tm, d), lambda i, j: (i, 0)),
                pl.BlockSpec((1, 1, bc, d), lambda i, j: (l, 5, 0, 0)),
                pl.BlockSpec((1, db, 2, tn), lambda i, j: (l, 0, 0, jnp.where(i >= n_pt, j, 0))),
                pl.BlockSpec((1, db, 2, tn), lambda i, j: (l, 0, 0, nj + jnp.where(i >= n_pt, j, 0)))]
    args = [h2, ffn_up, ffn_up, ffn_conv, ffn_conv, ffn_down, x1, mod, state_ffn, state_ffn]
    if final_norm:
        in_specs.append(pl.BlockSpec((1, d), lambda i, j: (0, 0)))
        args.append(final_g)
    n_tiles = rows // tm
    pstate = pl.BlockSpec((1, _SUBLANES, tn), lambda i, j: (i, 0, j))
    sstate = pl.BlockSpec((db, 2, tn), lambda i, j: (0, 0, jnp.where(i >= n_pt, j, 0)))
    if final_norm:
        x_specs = [pl.BlockSpec((tm, d), lambda i, j: (jnp.minimum(i, n_pt - 1), 0)),
                   pl.BlockSpec((tm, d), lambda i, j: (0, 0))]
        x_shapes = [jax.ShapeDtypeStruct((n_pt * tm, d), _F32), jax.ShapeDtypeStruct((tm, d), _F32)]
    else:
        x_specs = [pl.BlockSpec((tm, d), lambda i, j: (i, 0))]
        x_shapes = [jax.ShapeDtypeStruct((rows, d), _F32)]
    return pl.pallas_call(
        functools.partial(_ffn_kernel, n_pt=n_pt, tps=tps, db=db, nj=nj, final_norm=final_norm),
        grid=(rows // tm, nj),
        in_specs=in_specs,
        out_specs=x_specs + [pstate, pstate, sstate, sstate],
        out_shape=x_shapes + [
                   jax.ShapeDtypeStruct((n_tiles, _SUBLANES, f), _F32),
                   jax.ShapeDtypeStruct((n_tiles, _SUBLANES, f), _F32),
                   jax.ShapeDtypeStruct((db, 2, f), _F32), jax.ShapeDtypeStruct((db, 2, f), _F32)],
        scratch_shapes=[pltpu.VMEM((tm, d), _F32), pltpu.VMEM((nj, _SUBLANES, tn), _F32),
                        pltpu.VMEM((nj, _SUBLANES, tn), _F32)],
        compiler_params=_params("arbitrary", "arbitrary"),
    )(*args)


def _forward(x_prompt, x_sample, c_prompt, c_sample, state_wkv, state_shift, state_conv, state_ffn,
             ada_w, ada_b, norm_g, final_norm_g, w_in, mu_x, mu_rkv, decay_w0, decay_lora1, decay_lora2,
             iclr_a0, iclr_lora1, iclr_lora2, gate_lora1, gate_lora2, vres_v0, vres_lora1, vres_lora2,
             k_k, k_a, r_k, ln_x_w, ln_x_b, conv_w, w_out, ffn_up, ffn_conv, ffn_down,
             *, chunk=_WKV_CHUNK, mix_tn=_MIX_COLS, ffn_tn=_FFN_COLS, wkv_decode_nb=_WKV_DECODE_SEQS,
             wkv_prompt_nb=_WKV_PROMPT_SEQS):
    bp, t_len, d = x_prompt.shape
    db, ts, _ = x_sample.shape
    depth = ada_w.shape[0]
    g_dim = mu_rkv.shape[2]
    tm = ts * db
    assert t_len % tm == 0 and t_len % chunk == 0 and db % _SUBLANES == 0 and bp % wkv_prompt_nb == 0
    assert 2 <= ts <= 4 and ts & (ts - 1) == 0 and db % wkv_decode_nb == 0 and wkv_decode_nb & (wkv_decode_nb - 1) == 0
    tps = t_len // tm
    n_pt = bp * tps
    n_prompt_rows = bp * t_len

    x = _Act(x_prompt.reshape(n_prompt_rows, d), x_sample.transpose(1, 0, 2).reshape(tm, d), 0)
    pad = (-(db + bp)) % _SUBLANES
    c_all = jnp.concatenate([c_sample, c_prompt, jnp.zeros((pad, d), _F32)], axis=0)
    mod = _adaln_mod(c_all, ada_w, ada_b)

    norm_g4 = norm_g.reshape(depth, 2, 1, d)
    vec3 = lambda a: a.reshape(a.shape[0], 1, a.shape[1])
    w0_3, a0_3, v0_3, kk_3, ka_3 = vec3(decay_w0), vec3(iclr_a0), vec3(vres_v0), vec3(k_k), vec3(k_a)
    rk_3, lnw_3, lnb_3 = r_k.reshape(depth, 1, g_dim), vec3(ln_x_w), vec3(ln_x_b)
    state_conv_t = state_conv.transpose(0, 2, 1, 3)
    tiles = dict(tm=tm, n_pt=n_pt, tps=tps, db=db)
    w_out = _cast_bf16(w_out, 512)
    ffn_up = _cast_bf16(ffn_up, 256)

    new = {k: [] for k in ("wkv_p", "shift_p", "conv_p", "ffn_p", "wkv_s", "shift_s", "conv_s", "ffn_s")}
    v_first = None
    for l in range(depth):
        outs = _norm_lora(l, x, mod, norm_g4, mu_x, state_shift, decay_lora1, iclr_lora1, gate_lora1,
                          vres_lora1, bp=bp, **tiles)
        if l > 0:
            h, aw, aa, ag, av, hp_last, hs_last = outs
        else:
            h, aw, aa, ag, hp_last, hs_last = outs
            av = None
        mouts = _mix(l, h, (aw, aa, ag, av), v_first, w_in,
                     (decay_lora2, iclr_lora2, gate_lora2, vres_lora2), (w0_3, a0_3, v0_3),
                     mu_rkv, kk_3, ka_3, conv_w, state_shift, state_conv_t, tn=mix_tn, bp=bp, **tiles)
        seqs_p, seqs_s, rest = mouts[:7], mouts[7:14], mouts[14:]
        if l > 0:
            o_conv, conv_p8, conv_s = rest
        else:
            o_conv, v_first, conv_p8, conv_s = rest

        o_p, wkv_p = _wkv(l, tuple(a.reshape(bp, t_len, g_dim) for a in seqs_p), chunk, wkv_prompt_nb,
                          rk_3, lnw_3, lnb_3)
        o_p = o_p.reshape(n_prompt_rows, g_dim)
        last = l == depth - 1 and depth > 1
        o_s, wkv_s = _wkv_decode(l, tuple(a.reshape(ts, db, g_dim) for a in seqs_s), state_wkv[l],
                                 new["wkv_s"] if last else [], rk_3, lnw_3, lnb_3, ts=ts, db=db, nb=wkv_decode_nb)
        o_s = o_s.reshape(tm, g_dim)

        x1, h2 = _outproj(l, o_p, o_s, o_conv, x, mod, norm_g4, w_out, **tiles)
        final_g = final_norm_g.reshape(1, d) if l == depth - 1 else None
        *x_out, fpa, fpb, fsa, fsb = _ffn(l, h2, x1, mod, ffn_up, ffn_conv, ffn_down, state_ffn, final_g,
                                          tn=ffn_tn, bp=bp, **tiles)
        x = _Act(x_out[0], x_out[0], n_pt) if len(x_out) == 1 else _Act(x_out[0], x_out[1], 0)

        new["wkv_p"].append(wkv_p)
        if last:
            wkv_s_stacked = wkv_s
        else:
            new["wkv_s"].append(wkv_s)
        new["shift_p"].append(hp_last[:, _SUBLANES - 1])
        new["shift_s"].append(hs_last)
        new["conv_p"].append(conv_p8[:, _SUBLANES - 2:])
        new["conv_s"].append(conv_s.transpose(1, 0, 2))
        last_tiles = slice(tps - 1, n_pt, tps)
        new["ffn_p"].append(jnp.concatenate([fpa[last_tiles, _SUBLANES - 2:], fpb[last_tiles, _SUBLANES - 2:]],
                                            axis=-1))
        new["ffn_s"].append(jnp.concatenate([fsa, fsb], axis=-1))

    y_prompt = x.prompt.reshape(bp, t_len, d)
    y_sample = x.decode.reshape(ts, db, d).transpose(1, 0, 2)
    st = {k: jnp.stack(vs) for k, vs in new.items() if k != "wkv_s" or depth == 1}
    if depth > 1:
        st["wkv_s"] = wkv_s_stacked
    return (y_prompt, y_sample, st["wkv_p"], st["shift_p"], st["conv_p"], st["ffn_p"],
            st["wkv_s"], st["shift_s"], st["conv_s"], st["ffn_s"])


def kernel(x_prompt, x_sample, c_prompt, c_sample, state_wkv, state_shift, state_conv, state_ffn, ada_w, ada_b, norm_g, final_norm_g, w_in, mu_x, mu_rkv, decay_w0, decay_lora1, decay_lora2, iclr_a0, iclr_lora1, iclr_lora2, gate_lora1, gate_lora2, vres_v0, vres_lora1, vres_lora2, k_k, k_a, r_k, ln_x_w, ln_x_b, conv_w, w_out, ffn_up, ffn_conv, ffn_down):
    return _forward(x_prompt, x_sample, c_prompt, c_sample, state_wkv, state_shift, state_conv, state_ffn,
                    ada_w, ada_b, norm_g, final_norm_g, w_in, mu_x, mu_rkv, decay_w0, decay_lora1, decay_lora2,
                    iclr_a0, iclr_lora1, iclr_lora2, gate_lora1, gate_lora2, vres_v0, vres_lora1, vres_lora2,
                    k_k, k_a, r_k, ln_x_w, ln_x_b, conv_w, w_out, ffn_up, ffn_conv, ffn_down)
```
